```python
import math
import jax, jax.numpy as jnp
from jax import lax
import numpy as np

D_MODEL = 1024
BATCH = 4
SEQ = 4096
DEPTH = 1

MEM_TOKENS = 256
EPS = 1e-6
MLA_HEADS = 8
MLA_NOPE = 64
MLA_ROPE = 32
MLA_V = 64
MLA_Q_RANK = 256
MLA_KV_RANK = 128
ROPE_BASE = 10000.0
Q_BLOCK = 128
HY_WIDTH = D_MODEL - MLA_HEADS * MLA_V
HY_ORDER = 2
HY_DIRS = 2
HY_BANDS = 16
HY_EMB = 1 + 2 * HY_BANDS
HY_FFN = 64
HY_TARGET = 1e-2
HY_FAST_PCT = 0.3
HY_SLOW_PCT = 1.5
OFF_Q = 0
OFF_KV = OFF_Q + MLA_Q_RANK
OFF_KR = OFF_KV + MLA_KV_RANK
OFF_HY = OFF_KR + MLA_ROPE
IN_COLS = OFF_HY + 3 * HY_WIDTH
MEM_HEADS = 4
MEM_HEAD_DIM = D_MODEL // MEM_HEADS
N_GROUPS = 4
EXPERTS_PER_GROUP = 8
TOP_K_IN_GROUP = 2
D_EXPERT = 256

kernel_name = "hybrid_mla_hyena_hmoe_encoder"


def _rms(x, g):
    xf = x.astype(jnp.float32)
    y = xf * lax.rsqrt(jnp.mean(xf * xf, axis=-1, keepdims=True) + EPS)
    return (y * g.astype(jnp.float32)).astype(x.dtype)


def _rope(x, cos, sin):
    x1, x2 = jnp.split(x.astype(jnp.float32), 2, axis=-1)
    return jnp.concatenate([x1 * cos - x2 * sin, x1 * sin + x2 * cos], axis=-1).astype(x.dtype)


def _mla(h_q, h_kv, h_kr, q_norm_g, kv_norm_g, w_uq, w_ukv):
    B, S, _ = h_q.shape
    q = (_rms(h_q, q_norm_g) @ w_uq).reshape(B, S, MLA_HEADS, MLA_NOPE + MLA_ROPE)
    q_nope, q_rot = q[..., :MLA_NOPE], q[..., MLA_NOPE:]
    kv = (_rms(h_kv, kv_norm_g) @ w_ukv).reshape(B, S, MLA_HEADS, MLA_NOPE + MLA_V)
    k_nope, v = kv[..., :MLA_NOPE], kv[..., MLA_NOPE:]
    pos = jnp.arange(S, dtype=jnp.float32)
    half = MLA_ROPE // 2
    inv = ROPE_BASE ** (-jnp.arange(half, dtype=jnp.float32) / half)
    ang = pos[:, None] * inv[None, :]
    cos, sin = jnp.cos(ang), jnp.sin(ang)
    q_rot = _rope(q_rot, cos[:, None, :], sin[:, None, :])
    k_rot = _rope(h_kr, cos, sin)
    scale = (MLA_NOPE + MLA_ROPE) ** -0.5
    nb = S // Q_BLOCK

    def to_blocks(t):
        return jnp.moveaxis(t.reshape(B, nb, Q_BLOCK, *t.shape[2:]), 1, 0)

    def attend(blk):
        qn, qr = blk
        s = (jnp.einsum('bqhd,bkhd->bhqk', qn, k_nope)
             + jnp.einsum('bqhr,bkr->bhqk', qr, k_rot)).astype(jnp.float32) * scale
        p = jax.nn.softmax(s, axis=-1).astype(v.dtype)
        return jnp.einsum('bhqk,bkhd->bqhd', p, v)

    o = lax.map(attend, (to_blocks(q_nope), to_blocks(q_rot)))
    return jnp.moveaxis(o, 0, 1).reshape(B, S, MLA_HEADS * MLA_V)


def _hyena_filters(L, w1, b1, freq, w2, b2, w3, b3, decay):
    f32 = jnp.float32
    t = jnp.arange(L, dtype=f32)
    t01 = t / L
    bands = jnp.linspace(1e-4, HY_BANDS - 1, HY_BANDS, dtype=f32)
    ang = 2.0 * math.pi * t[:, None] * bands[None, :] / L
    z = jnp.concatenate([t01[:, None], jnp.cos(ang), -jnp.sin(ang)], axis=-1)
    fr = freq.astype(f32)
    h = jnp.sin(fr[0] * (z @ w1.astype(f32) + b1.astype(f32)))
    h = jnp.sin(fr[1] * (h @ w2.astype(f32) + b2.astype(f32)))
    h = h @ w3.astype(f32) + b3.astype(f32)
    h = h * jnp.exp(-t01[:, None] * jnp.abs(decay.astype(f32))[None, :])
    h = h.reshape(L, HY_ORDER, HY_DIRS, HY_WIDTH)
    fwd, bwd = h[:, :, 0], h[:, :, 1]
    k = jnp.concatenate([fwd, jnp.zeros((1, HY_ORDER, HY_WIDTH), f32), bwd[:0:-1]], axis=0)
    return jnp.fft.rfft(k, axis=0)


def _hyena(u, conv_w, conv_b, kf, skip):
    B, S, _ = u.shape
    up = jnp.pad(u, ((0, 0), (1, 1), (0, 0)))
    u = up[:, :-2] * conv_w[0] + up[:, 1:-1] * conv_w[1] + up[:, 2:] * conv_w[2] + conv_b
    x1, x2, v = jnp.split(u, 3, axis=-1)
    gates = (x1, x2)
    z = v.astype(jnp.float32)
    for n in range(HY_ORDER):
        zf = jnp.fft.rfft(z, n=2 * S, axis=1)
        conv = jnp.fft.irfft(zf * kf[None, :, n], n=2 * S, axis=1)[:, :S]
        z = gates[n].astype(jnp.float32) * (conv + skip[n].astype(jnp.float32) * z)
    return z.astype(u.dtype)


def _mem_xattn(hx, hm, w_mq, w_mkv, w_mo):
    B, S, _ = hx.shape
    M = hm.shape[1]
    q = (hx @ w_mq).reshape(B, S, MEM_HEADS, MEM_HEAD_DIM)
    k, v = jnp.split(hm @ w_mkv, 2, axis=-1)
    k = k.reshape(B, M, MEM_HEADS, MEM_HEAD_DIM)
    v = v.reshape(B, M, MEM_HEADS, MEM_HEAD_DIM)
    s = jnp.einsum('bshd,bmhd->bhsm', q, k).astype(jnp.float32) * MEM_HEAD_DIM ** -0.5
    p = jax.nn.softmax(s, axis=-1).astype(v.dtype)
    o = jnp.einsum('bhsm,bmhd->bshd', p, v).reshape(B, S, MEM_HEADS * MEM_HEAD_DIM)
    return o @ w_mo


def _hier_moe(h, w_rg, b_rg, w_re, b_re, w_gate, w_up, w_down):
    B, S, D = h.shape
    t = h.reshape(B * S, D)
    g_logits = (t @ w_rg).astype(jnp.float32) + b_rg.astype(jnp.float32)
    g_prob = jax.nn.softmax(g_logits, axis=-1)
    _, g_idx = lax.top_k(g_logits, 1)
    p_group = jnp.take_along_axis(g_prob, g_idx, axis=1)
    e_logits = ((t @ w_re).astype(jnp.float32) + b_re.astype(jnp.float32)).reshape(-1, N_GROUPS, EXPERTS_PER_GROUP)
    e_in = jnp.take_along_axis(e_logits, g_idx[:, :, None], axis=1)[:, 0]
    top_v, top_i = lax.top_k(e_in, TOP_K_IN_GROUP)
    p_exp = jax.nn.softmax(top_v, axis=-1)
    w_e = jnp.sum(jax.nn.one_hot(top_i, EXPERTS_PER_GROUP, dtype=jnp.float32) * p_exp[..., None], axis=1)
    combine = ((p_group * w_e)[:, None, :]
               * jax.nn.one_hot(g_idx[:, 0], N_GROUPS, dtype=jnp.float32)[:, :, None]).astype(t.dtype)
    y = jnp.zeros_like(t)
    for g in range(N_GROUPS):
        a = jnp.einsum('td,edf->tef', t, w_gate[g])
        b = jnp.einsum('td,edf->tef', t, w_up[g])
        m = jax.nn.silu(a) * b * combine[:, g, :, None]
        y = y + jnp.einsum('tef,efd->td', m, w_down[g])
    return y.reshape(B, S, D)


def setup_inputs(seed: int = 0) -> dict:
    key = jax.random.key(seed)
    ks = iter(jax.random.split(key, 40))
    f32 = jnp.float32
    L_ = DEPTH

    def nrm(shape, fan_in, scale=1.0):
        return jax.random.normal(next(ks), shape, f32) * (scale * fan_in ** -0.5)

    def gain(shape):
        return 1.0 + 0.02 * jax.random.normal(next(ks), shape, f32)

    def small(shape, s=0.02):
        return s * jax.random.normal(next(ks), shape, f32)

    G, E, F = N_GROUPS, EXPERTS_PER_GROUP, D_EXPERT
    d_min = math.log(1.0 / HY_TARGET) / HY_SLOW_PCT
    d_max = math.log(1.0 / HY_TARGET) / HY_FAST_PCT
    return {
        "x": jax.random.normal(next(ks), (BATCH, SEQ, D_MODEL), f32),
        "mem": jax.random.normal(next(ks), (BATCH, MEM_TOKENS, D_MODEL), f32),
        "mix_norm_g": gain((L_, D_MODEL)),
        "w_in": nrm((L_, D_MODEL, IN_COLS), D_MODEL),
        "q_norm_g": gain((L_, MLA_Q_RANK)),
        "kv_norm_g": gain((L_, MLA_KV_RANK)),
        "w_uq": nrm((L_, MLA_Q_RANK, MLA_HEADS * (MLA_NOPE + MLA_ROPE)), MLA_Q_RANK),
        "w_ukv": nrm((L_, MLA_KV_RANK, MLA_HEADS * (MLA_NOPE + MLA_V)), MLA_KV_RANK),
        "hy_conv_w": nrm((L_, 3, 3 * HY_WIDTH), 3),
        "hy_conv_b": small((L_, 3 * HY_WIDTH)),
        "hy_w1": nrm((L_, HY_EMB, HY_FFN), HY_EMB),
        "hy_b1": small((L_, HY_FFN), 0.1),
        "hy_freq": gain((L_, 2, HY_FFN)),
        "hy_w2": nrm((L_, HY_FFN, HY_FFN), HY_FFN),
        "hy_b2": small((L_, HY_FFN), 0.1),
        "hy_w3": nrm((L_, HY_FFN, HY_ORDER * HY_DIRS * HY_WIDTH), HY_FFN, 0.1),
        "hy_b3": small((L_, HY_ORDER * HY_DIRS * HY_WIDTH), 0.01),
        "hy_decay": jax.random.uniform(next(ks), (L_, HY_ORDER * HY_DIRS * HY_WIDTH), f32, d_min, d_max),
        "hy_skip": jax.random.normal(next(ks), (L_, HY_ORDER, HY_WIDTH), f32),
        "attn_out_g": gain((L_, MLA_HEADS * MLA_V)),
        "hy_out_g": gain((L_, HY_WIDTH)),
        "w_out": nrm((L_, D_MODEL, D_MODEL), D_MODEL),
        "cross_norm_g": gain((L_, D_MODEL)),
        "mem_norm_g": gain((L_, D_MODEL)),
        "w_mq": nrm((L_, D_MODEL, MEM_HEADS * MEM_HEAD_DIM), D_MODEL),
        "w_mkv": nrm((L_, D_MODEL, 2 * MEM_HEADS * MEM_HEAD_DIM), D_MODEL),
        "w_mo": nrm((L_, MEM_HEADS * MEM_HEAD_DIM, D_MODEL), MEM_HEADS * MEM_HEAD_DIM),
        "ffn_norm_g": gain((L_, D_MODEL)),
        "w_route_group": nrm((L_, D_MODEL, G), D_MODEL),
        "b_route_group": small((L_, G), 0.01),
        "w_route_expert": nrm((L_, D_MODEL, G * E), D_MODEL),
        "b_route_expert": small((L_, G * E), 0.01),
        "w_gate": nrm((L_, G, E, D_MODEL, F), D_MODEL),
        "w_up": nrm((L_, G, E, D_MODEL, F), D_MODEL),
        "w_down": nrm((L_, G, E, F, D_MODEL), F),
        "final_norm_g": gain((D_MODEL,)),
    }


def reference(x, mem, mix_norm_g, w_in, q_norm_g, kv_norm_g, w_uq, w_ukv, hy_conv_w, hy_conv_b,
              hy_w1, hy_b1, hy_freq, hy_w2, hy_b2, hy_w3, hy_b3, hy_decay, hy_skip, attn_out_g, hy_out_g,
              w_out, cross_norm_g, mem_norm_g, w_mq, w_mkv, w_mo, ffn_norm_g, w_route_group, b_route_group,
              w_route_expert, b_route_expert, w_gate, w_up, w_down, final_norm_g):
    S = x.shape[1]
    for l in range(DEPTH):
        h = _rms(x, mix_norm_g[l])
        p = h @ w_in[l]
        a_out = _mla(p[..., OFF_Q:OFF_KV], p[..., OFF_KV:OFF_KR], p[..., OFF_KR:OFF_HY],
                     q_norm_g[l], kv_norm_g[l], w_uq[l], w_ukv[l])
        kf = _hyena_filters(S, hy_w1[l], hy_b1[l], hy_freq[l], hy_w2[l], hy_b2[l], hy_w3[l], hy_b3[l], hy_decay[l])
        h_out = _hyena(p[..., OFF_HY:], hy_conv_w[l], hy_conv_b[l], kf, hy_skip[l])
        mixed = jnp.concatenate([_rms(a_out, attn_out_g[l]), _rms(h_out, hy_out_g[l])], axis=-1)
        x = x + mixed @ w_out[l]
        x = x + _mem_xattn(_rms(x, cross_norm_g[l]), _rms(mem, mem_norm_g[l]), w_mq[l], w_mkv[l], w_mo[l])
        x = x + _hier_moe(_rms(x, ffn_norm_g[l]), w_route_group[l], b_route_group[l], w_route_expert[l],
                          b_route_expert[l], w_gate[l], w_up[l], w_down[l])
    return _rms(x, final_norm_g)
```

```python
import functools
import math

import numpy as np
import jax
import jax.numpy as jnp
from jax import lax
from jax.experimental import pallas as pl
from jax.experimental.pallas import tpu as pltpu

F32 = jnp.float32
BF16 = jnp.bfloat16

EPS = 1e-6
MLA_HEADS = 8
MLA_NOPE = 64
MLA_ROPE = 32
MLA_V = 64
ROPE_BASE = 10000.0
HEAD_PAD = 128
HY_ORDER = 2
HY_DIRS = 2
HY_BANDS = 16
MEM_HEADS = 4
N_GROUPS = 4
EXPERTS_PER_GROUP = 8
N_EXPERTS = N_GROUPS * EXPERTS_PER_GROUP
ROUTE_LANES = 128

FFT_N1 = 64
FFT_N2 = 128

VMEM_LIMIT = 56 * 1024 * 1024


def _cparams(*sem):
    return pltpu.CompilerParams(dimension_semantics=sem, vmem_limit_bytes=VMEM_LIMIT)


def _rms(x, g):
    return x * lax.rsqrt(jnp.mean(x * x, axis=-1, keepdims=True) + EPS) * g


def _dot(a, b):
    return jnp.dot(a, b, preferred_element_type=F32)


def _inproj_kernel(x_ref, g_ref, wq_ref, wkv_ref, wkra_ref, wkrb_ref, why_ref, qg_ref, kvg_ref,
                   wqa_ref, wqb_ref, wka_ref, wv_ref, tab_ref,
                   q_out, k_out, v_out, x1_out, x2_out, hv_out):
    h = _rms(x_ref[...], g_ref[...]).astype(BF16)
    qn = _rms(_dot(h, wq_ref[...]), qg_ref[...]).astype(BF16)
    kvn = _rms(_dot(h, wkv_ref[...]), kvg_ref[...]).astype(BF16)
    tab = tab_ref[...]
    cq, sq, ck, sk = (tab[:, i * HEAD_PAD:(i + 1) * HEAD_PAD] for i in range(4))
    tile = lambda t: jnp.concatenate([t] * MLA_HEADS, axis=1)
    q = _dot(qn, wqa_ref[...]) * tile(cq) + _dot(qn, wqb_ref[...]) * tile(sq)
    q_out[...] = q.astype(BF16)
    kr = _dot(h, wkra_ref[...]) * ck + _dot(h, wkrb_ref[...]) * sk
    k_out[...] = (_dot(kvn, wka_ref[...]) + tile(kr)).astype(BF16)
    v_out[...] = _dot(kvn, wv_ref[...]).astype(BF16)
    hy = _dot(h, why_ref[...])
    c = x1_out.shape[1]
    x1_out[...] = hy[:, :c]
    x2_out[...] = hy[:, c:2 * c]
    hv_out[...] = hy[:, 2 * c:]


def _inproj(x2d, seq, mix_g, w_in, q_g, kv_g, w_uq, w_ukv, tm=512):
    T, D = x2d.shape
    q_rank, kv_rank = q_g.shape[0], kv_g.shape[0]
    off_kv = q_rank
    off_kr = off_kv + kv_rank
    off_hy = off_kr + MLA_ROPE
    C = (w_in.shape[1] - off_hy) // 3
    H = MLA_HEADS
    half = MLA_ROPE // 2
    wq = w_in[:, :off_kv].astype(BF16)
    wkv = w_in[:, off_kv:off_kr].astype(BF16)
    wkr = w_in[:, off_kr:off_hy]
    wkr_sw = jnp.concatenate([wkr[:, half:], wkr[:, :half]], axis=1)
    zpad = lambda n: jnp.zeros((D, n), F32)
    wkra = jnp.concatenate([zpad(MLA_NOPE), wkr, zpad(HEAD_PAD - MLA_NOPE - MLA_ROPE)], 1).astype(BF16)
    wkrb = jnp.concatenate([zpad(MLA_NOPE), wkr_sw, zpad(HEAD_PAD - MLA_NOPE - MLA_ROPE)], 1).astype(BF16)
    why = w_in[:, off_hy:].astype(BF16)

    uq = w_uq.reshape(q_rank, H, MLA_NOPE + MLA_ROPE)
    uq_n, uq_r = uq[..., :MLA_NOPE], uq[..., MLA_NOPE:]
    uq_rs = jnp.concatenate([uq_r[..., half:], uq_r[..., :half]], axis=-1)
    zq = lambda n: jnp.zeros((q_rank, H, n), F32)
    wqa = jnp.concatenate([uq_n, uq_r, zq(HEAD_PAD - MLA_NOPE - MLA_ROPE)], -1).reshape(q_rank, H * HEAD_PAD).astype(BF16)
    wqb = jnp.concatenate([zq(MLA_NOPE), uq_rs, zq(HEAD_PAD - MLA_NOPE - MLA_ROPE)], -1).reshape(q_rank, H * HEAD_PAD).astype(BF16)
    ukv = w_ukv.reshape(kv_rank, H, MLA_NOPE + MLA_V)
    zk = lambda n: jnp.zeros((kv_rank, H, n), F32)
    wka = jnp.concatenate([ukv[..., :MLA_NOPE], zk(HEAD_PAD - MLA_NOPE)], -1).reshape(kv_rank, H * HEAD_PAD).astype(BF16)
    wv = jnp.concatenate([ukv[..., MLA_NOPE:], zk(HEAD_PAD - MLA_V)], -1).reshape(kv_rank, H * HEAD_PAD).astype(BF16)

    pos = jnp.arange(seq, dtype=F32)
    inv = ROPE_BASE ** (-jnp.arange(half, dtype=F32) / half)
    ang = pos[:, None] * inv[None, :]
    cos2 = jnp.concatenate([jnp.cos(ang), jnp.cos(ang)], 1)
    sin2 = jnp.concatenate([-jnp.sin(ang), jnp.sin(ang)], 1)
    zs = lambda n: jnp.zeros((seq, n), F32)
    scale = (MLA_NOPE + MLA_ROPE) ** -0.5
    rest = HEAD_PAD - MLA_NOPE - MLA_ROPE
    cq = scale * jnp.concatenate([jnp.ones((seq, MLA_NOPE), F32), cos2, zs(rest)], 1)
    sq = scale * jnp.concatenate([zs(MLA_NOPE), sin2, zs(rest)], 1)
    ck = jnp.concatenate([zs(MLA_NOPE), cos2, zs(rest)], 1)
    sk = jnp.concatenate([zs(MLA_NOPE), sin2, zs(rest)], 1)
    tab = jnp.concatenate([cq, sq, ck, sk], 1)

    nseq = seq // tm
    full = lambda a: pl.BlockSpec(a.shape, lambda i: (0,) * a.ndim)
    row = lambda n: pl.BlockSpec((tm, n), lambda i: (i, 0))
    consts = [mix_g.reshape(1, D), wq, wkv, wkra, wkrb, why, q_g.reshape(1, -1), kv_g.reshape(1, -1),
              wqa, wqb, wka, wv]
    HP = H * HEAD_PAD
    return pl.pallas_call(
        _inproj_kernel,
        grid=(T // tm,),
        in_specs=[row(D)] + [full(a) for a in consts]
        + [pl.BlockSpec((tm, 4 * HEAD_PAD), lambda i: (i % nseq, 0))],
        out_specs=[row(HP), row(HP), row(HP), row(C), row(C), row(C)],
        out_shape=[jax.ShapeDtypeStruct((T, HP), BF16)] * 3 + [jax.ShapeDtypeStruct((T, C), F32)] * 3,
        compiler_params=_cparams("parallel"),
        name="inproj",
    )(x2d, *consts, tab)


def _attn_kernel(q_ref, k_ref, v_ref, o_ref):
    outs = []
    for h in range(MLA_HEADS):
        sl = slice(h * HEAD_PAD, (h + 1) * HEAD_PAD)
        s = lax.dot_general(q_ref[0, :, sl], k_ref[0, :, sl], (((1,), (1,)), ((), ())),
                            preferred_element_type=F32)
        p = jnp.exp(s - jnp.max(s, axis=-1, keepdims=True))
        l = jnp.sum(p, axis=-1, keepdims=True)
        o = _dot(p.astype(BF16), v_ref[0, :, sl]) / l
        outs.append(o[:, :MLA_V])
    o_ref[0] = jnp.concatenate(outs, axis=1)


def _attention(q, k, v, tq=256):
    B, S, HP = q.shape
    return pl.pallas_call(
        _attn_kernel,
        grid=(B, S // tq),
        in_specs=[pl.BlockSpec((1, tq, HP), lambda b, i: (b, i, 0)),
                  pl.BlockSpec((1, S, HP), lambda b, i: (b, 0, 0)),
                  pl.BlockSpec((1, S, HP), lambda b, i: (b, 0, 0))],
        out_specs=pl.BlockSpec((1, tq, MLA_HEADS * MLA_V), lambda b, i: (b, i, 0)),
        out_shape=jax.ShapeDtypeStruct((B, S, MLA_HEADS * MLA_V), F32),
        compiler_params=_cparams("parallel", "arbitrary"),
        name="mla_attention",
    )(q, k, v)


def _dft_constants(seq):
    n = 2 * seq
    n1, n2 = FFT_N1, FFT_N2
    assert n1 * n2 == n
    r1 = np.arange(n1)
    r2 = np.arange(n2)
    blk = lambda z: np.block([[z.real, -z.imag], [z.imag, z.real]])
    w1 = np.exp(-2j * np.pi * np.outer(r1, r1) / n1)
    fa_data = blk(w1[:, :n1 // 2])
    fa_filt = np.concatenate([w1.real, w1.imag], axis=0)
    fc = blk(np.conj(w1).T[:n1 // 2, :])
    w2 = np.exp(-2j * np.pi * np.outer(r2, r2) / n2)
    tw = np.exp(-2j * np.pi * np.outer(r1, r2) / n)
    fb = np.stack([blk(w2 * tw[k][None, :]) for k in range(n1)])
    fbi = np.stack([blk(np.conj(w2).T * np.conj(tw[k])[:, None] / n) for k in range(n1)])
    as_bf = lambda a: jnp.asarray(a, dtype=F32).astype(BF16)
    return as_bf(fa_data), as_bf(fa_filt), as_bf(fc), as_bf(fb), as_bf(fbi)


def _filter_kernel(z_ref, w1_ref, b1_ref, fr_ref, w2_ref, b2_ref, w3_ref, b3_ref, dec_ref, o_ref, *, seq, tr):
    hp = lax.Precision.HIGHEST
    z = z_ref[...]
    fr = fr_ref[...]
    h = jnp.sin(fr[0:1] * (jnp.dot(z, w1_ref[...], precision=hp, preferred_element_type=F32) + b1_ref[...]))
    h = jnp.sin(fr[1:2] * (jnp.dot(h, w2_ref[...], precision=hp, preferred_element_type=F32) + b2_ref[...]))
    h = jnp.dot(h, w3_ref[0], precision=hp, preferred_element_type=F32) + b3_ref[0]
    h = h * jnp.exp(-z[:, 0:1] * jnp.abs(dec_ref[0]))
    n = pl.program_id(0) * tr + lax.broadcasted_iota(jnp.int32, h.shape, 0)
    o_ref[...] = jnp.where(n == seq, 0.0, h)


def _hyena_filter_time(seq, w1, b1, freq, w2, b2, w3, b3, decay, tr=512):
    n = 2 * seq
    emb, ffn = w1.shape
    C = w3.shape[1] // (HY_ORDER * HY_DIRS)
    off = jnp.arange(n)
    t = jnp.where(off < seq, off, n - off).astype(F32)
    bands = jnp.linspace(1e-4, HY_BANDS - 1, HY_BANDS, dtype=F32)
    ang = 2.0 * math.pi * t[:, None] * bands[None, :] / seq
    z = jnp.concatenate([(t / seq)[:, None], jnp.cos(ang), -jnp.sin(ang)], axis=-1)
    zl = 128
    z = jnp.pad(z, ((0, 0), (0, zl - emb)))
    w1p = jnp.pad(w1, ((0, zl - emb), (0, 0)))
    by_dir = lambda a: jnp.moveaxis(a.reshape(a.shape[0], HY_ORDER, HY_DIRS, C), 2, 0).reshape(
        HY_DIRS, a.shape[0], HY_ORDER * C)
    w3d, b3d, decd = by_dir(w3), by_dir(b3.reshape(1, -1)), by_dir(decay.reshape(1, -1))
    full = lambda a: pl.BlockSpec(a.shape, lambda i: (0,) * a.ndim)
    ndir = lambda a: pl.BlockSpec((1,) + a.shape[1:], lambda i: ((i * tr) // seq, 0, 0))
    consts = [w1p, b1.reshape(1, -1), freq, w2, b2.reshape(1, -1)]
    return pl.pallas_call(
        functools.partial(_filter_kernel, seq=seq, tr=tr),
        grid=(n // tr,),
        in_specs=[pl.BlockSpec((tr, zl), lambda i: (i, 0))] + [full(a) for a in consts]
        + [ndir(w3d), ndir(b3d), ndir(decd)],
        out_specs=pl.BlockSpec((tr, HY_ORDER * C), lambda i: (i, 0)),
        out_shape=jax.ShapeDtypeStruct((n, HY_ORDER * C), F32),
        compiler_params=_cparams("parallel"),
        name="hyena_filter_mlp",
    )(z, *consts, w3d, b3d, decd)


def _stage_a_kernel(x_ref, fa_ref, o_ref):
    a = _dot(fa_ref[...], x_ref[...].astype(BF16))
    o_ref[...] = a.reshape(o_ref.shape).astype(o_ref.dtype)


def _filter_stage_b_kernel(x_ref, fb_ref, o_ref):
    x = jnp.concatenate([x_ref[0, 0], x_ref[1, 0]], axis=0)
    o_ref[0] = _dot(fb_ref[0], x)


def _hyena_filter_spectrum(filt, fa_filt, fb, lb=4096):
    n, oc = filt.shape
    lanes = FFT_N2 * oc
    xa = filt.reshape(FFT_N1, lanes)
    a = pl.pallas_call(
        _stage_a_kernel,
        grid=(lanes // lb,),
        in_specs=[pl.BlockSpec((FFT_N1, lb), lambda j: (0, j)),
                  pl.BlockSpec(fa_filt.shape, lambda j: (0, 0))],
        out_specs=pl.BlockSpec((2, FFT_N1, lb), lambda j: (0, 0, j)),
        out_shape=jax.ShapeDtypeStruct((2, FFT_N1, lanes), BF16),
        compiler_params=_cparams("parallel"),
        name="hyena_filter_dft_a",
    )(xa, fa_filt)
    a = a.reshape(2, FFT_N1, FFT_N2, oc)
    return pl.pallas_call(
        _filter_stage_b_kernel,
        grid=(FFT_N1,),
        in_specs=[pl.BlockSpec((2, 1, FFT_N2, oc), lambda k: (0, k, 0, 0)),
                  pl.BlockSpec((1, 2 * FFT_N2, 2 * FFT_N2), lambda k: (k, 0, 0))],
        out_specs=pl.BlockSpec((1, 2 * FFT_N2, oc), lambda k: (k, 0, 0)),
        out_shape=jax.ShapeDtypeStruct((FFT_N1, 2 * FFT_N2, oc), F32),
        compiler_params=_cparams("parallel"),
        name="hyena_filter_dft_b",
    )(a, fb)


def _short_conv_kernel(x_ref, xp_ref, xn_ref, w_ref, b_ref, *rest, stage_a):
    j = pl.program_id(1)
    nb, r, lb = x_ref.shape
    c = xp_ref.shape[2]
    x = x_ref[...].reshape(nb * r, lb)
    xp = xp_ref[...].reshape(nb * r, c)
    xn = xn_ref[...].reshape(nb * r, c)
    row = lax.broadcasted_iota(jnp.int32, xp.shape, 0) % r
    xp = jnp.where(j == 0, jnp.where(row == 0, 0.0, pltpu.roll(xp, 1, axis=0)), xp)
    xn = jnp.where(j == pl.num_programs(1) - 1,
                   jnp.where(row == r - 1, 0.0, pltpu.roll(xn, nb * r - 1, axis=0)), xn)
    prev = jnp.concatenate([xp, x[:, :lb - c]], axis=1)
    nxt = jnp.concatenate([x[:, c:], xn], axis=1)
    w = w_ref[...]
    u = prev * w[0:1] + x * w[1:2] + nxt * w[2:3] + b_ref[...]
    if stage_a:
        fa_ref, u_out, a_out = rest
        a_out[0] = _dot(fa_ref[...], u.astype(BF16)).reshape(a_out.shape[1:]).astype(a_out.dtype)
    else:
        (u_out,) = rest
    u_out[...] = u.reshape(nb, r, lb)


def _short_conv(x, w3c, bc, fa=None, lb=4096):
    B, S, C = x.shape
    r = S // FFT_N2
    lanes = FFT_N2 * C
    per = lb // C
    nlb = FFT_N2 // per
    xv = x.reshape(B, r, lanes)
    wt = jnp.tile(w3c, (1, per))
    bt = jnp.tile(bc.reshape(1, C), (1, per))
    nb = 2 if fa is not None else 1
    in_specs = [pl.BlockSpec((nb, r, lb), lambda p, j: (p, 0, j)),
                pl.BlockSpec((nb, r, C), lambda p, j: (p, 0, (j * per + FFT_N2 - 1) % FFT_N2)),
                pl.BlockSpec((nb, r, C), lambda p, j: (p, 0, (j * per + per) % FFT_N2)),
                pl.BlockSpec(wt.shape, lambda p, j: (0, 0)),
                pl.BlockSpec(bt.shape, lambda p, j: (0, 0))]
    out_specs = [pl.BlockSpec((nb, r, lb), lambda p, j: (p, 0, j))]
    out_shape = [jax.ShapeDtypeStruct((B, r, lanes), F32)]
    args = [xv, xv, xv, wt, bt]
    if fa is not None:
        in_specs.append(pl.BlockSpec(fa.shape, lambda p, j: (0, 0)))
        out_specs.append(pl.BlockSpec((1, 2, FFT_N1, lb), lambda p, j: (p, 0, 0, j)))
        out_shape.append(jax.ShapeDtypeStruct((B // 2, 2, FFT_N1, lanes), BF16))
        args.append(fa)
    return pl.pallas_call(
        functools.partial(_short_conv_kernel, stage_a=fa is not None),
        grid=(B // nb, nlb),
        in_specs=in_specs, out_specs=out_specs, out_shape=out_shape,
        compiler_params=_cparams("parallel", "arbitrary"),
        name="hyena_short_conv",
    )(*args)


def _stage_b_kernel(x_ref, fb_ref, kf_ref, fbi_ref, o_ref):
    npair = x_ref.shape[0]
    n2 = x_ref.shape[3]
    x = jnp.concatenate(
        [jnp.concatenate([x_ref[p, 0, 0], x_ref[p, 1, 0]], axis=0) for p in range(npair)], axis=1)
    g = _dot(fb_ref[0], x)
    gr, gi = g[:n2], g[n2:]
    kf = kf_ref[0]
    kr = jnp.concatenate([kf[:n2]] * npair, axis=1)
    ki = jnp.concatenate([kf[n2:]] * npair, axis=1)
    hcat = jnp.concatenate([gr * kr - gi * ki, gr * ki + gi * kr], axis=0).astype(BF16)
    y = _dot(fbi_ref[0], hcat)
    c = x_ref.shape[4]
    for p in range(npair):
        o_ref[p, 0, 0] = y[:n2, p * c:(p + 1) * c].astype(o_ref.dtype)
        o_ref[p, 1, 0] = y[n2:, p * c:(p + 1) * c].astype(o_ref.dtype)


def _stage_b(spec, fb, kf, fbi, order, C):
    npair = spec.shape[0]
    sv = spec.reshape(npair, 2, FFT_N1, FFT_N2, C)
    blk = pl.BlockSpec((npair, 2, 1, FFT_N2, C), lambda k: (0, 0, k, 0, 0))
    mat = pl.BlockSpec((1, 2 * FFT_N2, 2 * FFT_N2), lambda k: (k, 0, 0))
    out = pl.pallas_call(
        _stage_b_kernel,
        grid=(FFT_N1,),
        in_specs=[blk, mat, pl.BlockSpec((1, 2 * FFT_N2, C), lambda k: (k, 0, order)), mat],
        out_specs=blk,
        out_shape=jax.ShapeDtypeStruct(sv.shape, BF16),
        compiler_params=_cparams("parallel"),
        name="hyena_dft_b",
    )(sv, fb, kf, fbi)
    return out.reshape(spec.shape)


def _stage_c_kernel(y_ref, fc_ref, gate_ref, z_ref, skip_ref, *rest, stage_a):
    nb, r, lb = gate_ref.shape
    conv = _dot(fc_ref[...], y_ref[0].reshape(2 * FFT_N1, lb))
    zin = z_ref[...].reshape(nb * r, lb)
    z = gate_ref[...].reshape(nb * r, lb) * (conv + skip_ref[...] * zin)
    if stage_a:
        fa_ref, z_out, a_out = rest
        a_out[0] = _dot(fa_ref[...], z.astype(BF16)).reshape(a_out.shape[1:]).astype(a_out.dtype)
    else:
        (z_out,) = rest
    z_out[...] = z.reshape(nb, r, lb)


def _stage_c(yspec, fc, gate, zin, skip, fa=None, lb=4096):
    B, r, lanes = gate.shape
    C = skip.shape[0]
    st = jnp.tile(skip.reshape(1, C), (1, lb // C))
    dat = pl.BlockSpec((2, r, lb), lambda p, j: (p, 0, j))
    spc = pl.BlockSpec((1, 2, FFT_N1, lb), lambda p, j: (p, 0, 0, j))
    in_specs = [spc, pl.BlockSpec(fc.shape, lambda p, j: (0, 0)), dat, dat,
                pl.BlockSpec(st.shape, lambda p, j: (0, 0))]
    out_specs = [dat]
    out_shape = [jax.ShapeDtypeStruct(gate.shape, F32)]
    args = [yspec, fc, gate, zin, st]
    if fa is not None:
        in_specs.append(pl.BlockSpec(fa.shape, lambda p, j: (0, 0)))
        out_specs.append(spc)
        out_shape.append(jax.ShapeDtypeStruct(yspec.shape, BF16))
        args.append(fa)
    return pl.pallas_call(
        functools.partial(_stage_c_kernel, stage_a=fa is not None),
        grid=(B // 2, lanes // lb),
        in_specs=in_specs, out_specs=out_specs, out_shape=out_shape,
        compiler_params=_cparams("parallel", "arbitrary"),
        name="hyena_dft_c",
    )(*args)


def _hyena(x1p, x2p, vp, conv_w, conv_b, skip, kf, consts):
    fa_data, _, fc, fb, fbi = consts
    B, S, C = vp.shape
    cw = lambda i: conv_w[:, i * C:(i + 1) * C]
    cb = lambda i: conv_b[i * C:(i + 1) * C]
    (x1,) = _short_conv(x1p, cw(0), cb(0))
    (x2,) = _short_conv(x2p, cw(1), cb(1))
    v, a0 = _short_conv(vp, cw(2), cb(2), fa=fa_data)
    y0 = _stage_b(a0, fb, kf, fbi, 0, C)
    z1, a1 = _stage_c(y0, fc, x1, v, skip[0], fa=fa_data)
    y1 = _stage_b(a1, fb, kf, fbi, 1, C)
    (out,) = _stage_c(y1, fc, x2, z1, skip[1])
    return out.reshape(B, S, C)


def _memkv_kernel(m_ref, g_ref, w_ref, k_out, v_out):
    hm = _rms(m_ref[0], g_ref[...]).astype(BF16)
    kv = _dot(hm, w_ref[...])
    d = k_out.shape[2]
    k_out[0] = kv[:, :d].astype(BF16)
    v_out[0] = kv[:, d:].astype(BF16)


def _memkv(mem, g, w_mkv):
    B, M, D = mem.shape
    dk = w_mkv.shape[1] // 2
    w = w_mkv.astype(BF16)
    return pl.pallas_call(
        _memkv_kernel,
        grid=(B,),
        in_specs=[pl.BlockSpec((1, M, D), lambda b: (b, 0, 0)),
                  pl.BlockSpec((1, D), lambda b: (0, 0)),
                  pl.BlockSpec(w.shape, lambda b: (0, 0))],
        out_specs=[pl.BlockSpec((1, M, dk), lambda b: (b, 0, 0))] * 2,
        out_shape=[jax.ShapeDtypeStruct((B, M, dk), BF16)] * 2,
        compiler_params=_cparams("parallel"),
        name="mem_kv",
    )(mem, g.reshape(1, D), w)


def _route(logits):
    lane = lax.broadcasted_iota(jnp.int32, logits.shape, 1)
    ninf = -jnp.inf
    big = ROUTE_LANES
    first = lambda mask: jnp.min(jnp.where(mask, lane, big), axis=-1, keepdims=True)
    is_g = (lane >= N_EXPERTS) & (lane < N_EXPERTS + N_GROUPS)
    gl = jnp.where(is_g, logits, ninf)
    gmax = jnp.max(gl, axis=-1, keepdims=True)
    g_idx = first(gl == gmax) - N_EXPERTS
    p_group = 1.0 / jnp.sum(jnp.exp(gl - gmax), axis=-1, keepdims=True)
    in_g = (lane < N_EXPERTS) & ((lane // EXPERTS_PER_GROUP) == g_idx)
    el = jnp.where(in_g, logits, ninf)
    v1 = jnp.max(el, axis=-1, keepdims=True)
    i1 = first(el == v1)
    el2 = jnp.where(lane == i1, ninf, el)
    v2 = jnp.max(el2, axis=-1, keepdims=True)
    i2 = first(el2 == v2)
    e2 = jnp.exp(v2 - v1)
    p1 = 1.0 / (1.0 + e2)
    p2 = e2 / (1.0 + e2)
    return p_group * (jnp.where(lane == i1, p1, 0.0) + jnp.where(lane == i2, p2, 0.0))


def _postmix_kernel(x_ref, a_ref, hy_ref, ag_ref, hg_ref, woa_ref, woh_ref, cg_ref, wmq_ref,
                    mk_ref, mv_ref, wmo_ref, fg_ref, wr_ref, br_ref, x_out, hn_out, comb_out):
    ra = _rms(a_ref[...], ag_ref[...]).astype(BF16)
    rh = _rms(hy_ref[...], hg_ref[...]).astype(BF16)
    x = x_ref[...] + _dot(ra, woa_ref[...]) + _dot(rh, woh_ref[...])
    q = _dot(_rms(x, cg_ref[...]).astype(BF16), wmq_ref[...])
    dh = q.shape[1] // MEM_HEADS
    outs = []
    for h in range(MEM_HEADS):
        sl = slice(h * dh, (h + 1) * dh)
        s = lax.dot_general(q[:, sl].astype(BF16), mk_ref[0, :, sl], (((1,), (1,)), ((), ())),
                            preferred_element_type=F32) * dh ** -0.5
        p = jnp.exp(s - jnp.max(s, axis=-1, keepdims=True))
        l = jnp.sum(p, axis=-1, keepdims=True)
        outs.append(_dot(p.astype(BF16), mv_ref[0, :, sl]) / l)
    o = jnp.concatenate(outs, axis=1).astype(BF16)
    x = x + _dot(o, wmo_ref[...])
    x_out[...] = x
    hn = _rms(x, fg_ref[...]).astype(BF16)
    hn_out[...] = hn
    comb_out[...] = _route(_dot(hn, wr_ref[...]) + br_ref[...])


def _postmix(x2d, a2d, hy2d, seq, ag, hg, w_out, cg, w_mq, mk, mv, w_mo, fg, w_rg, b_rg, w_re, b_re, tm=512):
    T, D = x2d.shape
    ca = a2d.shape[1]
    woa = w_out[:ca].astype(BF16)
    woh = w_out[ca:].astype(BF16)
    pad = ROUTE_LANES - N_EXPERTS - N_GROUPS
    wr = jnp.concatenate([w_re, w_rg, jnp.zeros((D, pad), F32)], 1).astype(BF16)
    br = jnp.concatenate([b_re, b_rg, jnp.zeros((pad,), F32)]).reshape(1, ROUTE_LANES)
    nseq = seq // tm
    full = lambda a: pl.BlockSpec(a.shape, lambda i: (0,) * a.ndim)
    row = lambda n: pl.BlockSpec((tm, n), lambda i: (i, 0))
    memb = pl.BlockSpec((1,) + mk.shape[1:], lambda i: (i // nseq, 0, 0))
    args = [x2d, a2d, hy2d, ag.reshape(1, -1), hg.reshape(1, -1), woa, woh, cg.reshape(1, D),
            w_mq.astype(BF16), mk, mv, w_mo.astype(BF16), fg.reshape(1, D), wr, br]
    in_specs = [row(D), row(ca), row(hy2d.shape[1])] + [full(a) for a in args[3:9]] + [memb, memb] \
        + [full(a) for a in args[11:]]
    return pl.pallas_call(
        _postmix_kernel,
        grid=(T // tm,),
        in_specs=in_specs,
        out_specs=[row(D), row(D), row(ROUTE_LANES)],
        out_shape=[jax.ShapeDtypeStruct((T, D), F32), jax.ShapeDtypeStruct((T, D), BF16),
                   jax.ShapeDtypeStruct((T, ROUTE_LANES), F32)],
        compiler_params=_cparams("parallel"),
        name="postmix",
    )(*args)


def _moe_kernel(hn_ref, comb_ref, wg_ref, wu_ref, wd_ref, x_ref, fg_ref, o_ref, acc_ref):
    e = pl.program_id(1)

    @pl.when(e == 0)
    def _():
        acc_ref[...] = x_ref[...]

    hn = hn_ref[...]
    a = _dot(hn, wg_ref[0])
    b = _dot(hn, wu_ref[0])
    comb = comb_ref[...]
    lane = lax.broadcasted_iota(jnp.int32, comb.shape, 1)
    c = jnp.sum(jnp.where(lane == e, comb, 0.0), axis=-1, keepdims=True)
    m = (a * jax.nn.sigmoid(a)) * b * c
    acc_ref[...] += _dot(m.astype(BF16), wd_ref[0])

    @pl.when(e == pl.num_programs(1) - 1)
    def _():
        o_ref[...] = _rms(acc_ref[...], fg_ref[...])


def _moe(hn, comb, x2d, w_gate, w_up, w_down, fg, tm=1024):
    T, D = x2d.shape
    E = N_EXPERTS
    F = w_gate.shape[-1]
    wg = w_gate.reshape(E, D, F).astype(BF16)
    wu = w_up.reshape(E, D, F).astype(BF16)
    wd = w_down.reshape(E, F, D).astype(BF16)
    row = lambda n: pl.BlockSpec((tm, n), lambda i, e: (i, 0))
    return pl.pallas_call(
        _moe_kernel,
        grid=(T // tm, E),
        in_specs=[row(D), row(ROUTE_LANES),
                  pl.BlockSpec((1, D, F), lambda i, e: (e, 0, 0)),
                  pl.BlockSpec((1, D, F), lambda i, e: (e, 0, 0)),
                  pl.BlockSpec((1, F, D), lambda i, e: (e, 0, 0)),
                  row(D), pl.BlockSpec((1, D), lambda i, e: (0, 0))],
        out_specs=row(D),
        out_shape=jax.ShapeDtypeStruct((T, D), F32),
        scratch_shapes=[pltpu.VMEM((tm, D), F32)],
        compiler_params=_cparams("parallel", "arbitrary"),
        name="moe",
    )(hn, comb, wg, wu, wd, x2d, fg.reshape(1, D))


def kernel(x, mem, mix_norm_g, w_in, q_norm_g, kv_norm_g, w_uq, w_ukv, hy_conv_w, hy_conv_b, hy_w1, hy_b1, hy_freq, hy_w2, hy_b2, hy_w3, hy_b3, hy_decay, hy_skip, attn_out_g, hy_out_g, w_out, cross_norm_g, mem_norm_g, w_mq, w_mkv, w_mo, ffn_norm_g, w_route_group, b_route_group, w_route_expert, b_route_expert, w_gate, w_up, w_down, final_norm_g):
    B, S, D = x.shape
    depth = w_in.shape[0]
    consts = _dft_constants(S)
    xf = x.reshape(B * S, D)
    for l in range(depth):
        q, k, v, x1p, x2p, vp = _inproj(xf, S, mix_norm_g[l], w_in[l], q_norm_g[l], kv_norm_g[l],
                                        w_uq[l], w_ukv[l])
        HP = q.shape[1]
        a_out = _attention(q.reshape(B, S, HP), k.reshape(B, S, HP), v.reshape(B, S, HP))
        filt = _hyena_filter_time(S, hy_w1[l], hy_b1[l], hy_freq[l], hy_w2[l], hy_b2[l], hy_w3[l],
                                  hy_b3[l], hy_decay[l])
        kf = _hyena_filter_spectrum(filt, consts[1], consts[3])
        C = vp.shape[1]
        h_out = _hyena(x1p.reshape(B, S, C), x2p.reshape(B, S, C), vp.reshape(B, S, C),
                       hy_conv_w[l], hy_conv_b[l], hy_skip[l], kf, consts)
        mk, mv = _memkv(mem, mem_norm_g[l], w_mkv[l])
        x2, hn, comb = _postmix(xf, a_out.reshape(B * S, -1), h_out.reshape(B * S, C), S,
                                attn_out_g[l], hy_out_g[l], w_out[l], cross_norm_g[l], w_mq[l], mk, mv,
                                w_mo[l], ffn_norm_g[l], w_route_group[l], b_route_group[l],
                                w_route_expert[l], b_route_expert[l])
        assert depth == 1
        xf = _moe(hn, comb, x2, w_gate[l], w_up[l], w_down[l], final_norm_g)
    return xf.reshape(B, S, D)
```

```python
import functools
import math

import numpy as np
import jax
import jax.numpy as jnp
from jax import lax
from jax.experimental import pallas as pl
from jax.experimental.pallas import tpu as pltpu

F32 = jnp.float32
BF16 = jnp.bfloat16

EPS = 1e-6
MLA_HEADS = 8
MLA_NOPE = 64
MLA_ROPE = 32
MLA_V = 64
ROPE_BASE = 10000.0
HEAD_PAD = 128
HY_ORDER = 2
HY_DIRS = 2
HY_BANDS = 16
MEM_HEADS = 4
N_GROUPS = 4
EXPERTS_PER_GROUP = 8
N_EXPERTS = N_GROUPS * EXPERTS_PER_GROUP
ROUTE_LANES = 128
ROUTE_ID0 = 0
ROUTE_W0 = 2
MOE_ROW_TILE = 256

FFT_N1 = 64
FFT_N2 = 128

VMEM_LIMIT = 56 * 1024 * 1024


def _cparams(*sem):
    return pltpu.CompilerParams(dimension_semantics=sem, vmem_limit_bytes=VMEM_LIMIT)


def _rms(x, g):
    return x * lax.rsqrt(jnp.mean(x * x, axis=-1, keepdims=True) + EPS) * g


def _dot(a, b):
    return jnp.dot(a, b, preferred_element_type=F32)


SUBLANES = 8
LANES = 128


def _load_row_tiles(ref):
    return jnp.concatenate([ref[:, j, :] for j in range(SUBLANES)], axis=1)


def _store_row_tiles(ref, val):
    for j in range(SUBLANES):
        ref[:, j, :] = val[:, j * LANES:(j + 1) * LANES]


def _inproj_kernel(x_ref, g_ref, wq_ref, wkv_ref, wkra_ref, wkrb_ref, why_ref, qg_ref, kvg_ref,
                   wqa_ref, wqb_ref, wka_ref, wv_ref, tab_ref,
                   q_out, k_out, v_out, x1_out, x2_out, hv_out):
    h = _rms(x_ref[...], g_ref[...]).astype(BF16)
    qn = _rms(_dot(h, wq_ref[...]), qg_ref[...]).astype(BF16)
    kvn = _rms(_dot(h, wkv_ref[...]), kvg_ref[...]).astype(BF16)
    tab = tab_ref[...]
    cq, sq, ck, sk = (tab[:, i * HEAD_PAD:(i + 1) * HEAD_PAD] for i in range(4))
    tile = lambda t: jnp.concatenate([t] * MLA_HEADS, axis=1)
    q = _dot(qn, wqa_ref[...]) * tile(cq) + _dot(qn, wqb_ref[...]) * tile(sq)
    q_out[...] = q.astype(BF16)
    kr = _dot(h, wkra_ref[...]) * ck + _dot(h, wkrb_ref[...]) * sk
    k_out[...] = (_dot(kvn, wka_ref[...]) + tile(kr)).astype(BF16)
    v_out[...] = _dot(kvn, wv_ref[...]).astype(BF16)
    hy = _dot(h, why_ref[...])
    c = x1_out.shape[1]
    x1_out[...] = hy[:, :c]
    x2_out[...] = hy[:, c:2 * c]
    hv_out[...] = hy[:, 2 * c:]


def _inproj(x2d, seq, mix_g, w_in, q_g, kv_g, w_uq, w_ukv, tm=512):
    T, D = x2d.shape
    q_rank, kv_rank = q_g.shape[0], kv_g.shape[0]
    off_kv = q_rank
    off_kr = off_kv + kv_rank
    off_hy = off_kr + MLA_ROPE
    C = (w_in.shape[1] - off_hy) // 3
    H = MLA_HEADS
    half = MLA_ROPE // 2
    wq = w_in[:, :off_kv].astype(BF16)
    wkv = w_in[:, off_kv:off_kr].astype(BF16)
    wkr = w_in[:, off_kr:off_hy]
    wkr_sw = jnp.concatenate([wkr[:, half:], wkr[:, :half]], axis=1)
    zpad = lambda n: jnp.zeros((D, n), F32)
    wkra = jnp.concatenate([zpad(MLA_NOPE), wkr, zpad(HEAD_PAD - MLA_NOPE - MLA_ROPE)], 1).astype(BF16)
    wkrb = jnp.concatenate([zpad(MLA_NOPE), wkr_sw, zpad(HEAD_PAD - MLA_NOPE - MLA_ROPE)], 1).astype(BF16)
    why = w_in[:, off_hy:].astype(BF16)

    uq = w_uq.reshape(q_rank, H, MLA_NOPE + MLA_ROPE)
    uq_n, uq_r = uq[..., :MLA_NOPE], uq[..., MLA_NOPE:]
    uq_rs = jnp.concatenate([uq_r[..., half:], uq_r[..., :half]], axis=-1)
    zq = lambda n: jnp.zeros((q_rank, H, n), F32)
    wqa = jnp.concatenate([uq_n, uq_r, zq(HEAD_PAD - MLA_NOPE - MLA_ROPE)], -1).reshape(q_rank, H * HEAD_PAD).astype(BF16)
    wqb = jnp.concatenate([zq(MLA_NOPE), uq_rs, zq(HEAD_PAD - MLA_NOPE - MLA_ROPE)], -1).reshape(q_rank, H * HEAD_PAD).astype(BF16)
    ukv = w_ukv.reshape(kv_rank, H, MLA_NOPE + MLA_V)
    zk = lambda n: jnp.zeros((kv_rank, H, n), F32)
    wka = jnp.concatenate([ukv[..., :MLA_NOPE], zk(HEAD_PAD - MLA_NOPE)], -1).reshape(kv_rank, H * HEAD_PAD).astype(BF16)
    wv = jnp.concatenate([ukv[..., MLA_NOPE:], zk(HEAD_PAD - MLA_V)], -1).reshape(kv_rank, H * HEAD_PAD).astype(BF16)

    pos = jnp.arange(seq, dtype=F32)
    inv = ROPE_BASE ** (-jnp.arange(half, dtype=F32) / half)
    ang = pos[:, None] * inv[None, :]
    cos2 = jnp.concatenate([jnp.cos(ang), jnp.cos(ang)], 1)
    sin2 = jnp.concatenate([-jnp.sin(ang), jnp.sin(ang)], 1)
    zs = lambda n: jnp.zeros((seq, n), F32)
    scale = (MLA_NOPE + MLA_ROPE) ** -0.5
    rest = HEAD_PAD - MLA_NOPE - MLA_ROPE
    cq = scale * jnp.concatenate([jnp.ones((seq, MLA_NOPE), F32), cos2, zs(rest)], 1)
    sq = scale * jnp.concatenate([zs(MLA_NOPE), sin2, zs(rest)], 1)
    ck = jnp.concatenate([zs(MLA_NOPE), cos2, zs(rest)], 1)
    sk = jnp.concatenate([zs(MLA_NOPE), sin2, zs(rest)], 1)
    tab = jnp.concatenate([cq, sq, ck, sk], 1)

    nseq = seq // tm
    full = lambda a: pl.BlockSpec(a.shape, lambda i: (0,) * a.ndim)
    row = lambda n: pl.BlockSpec((tm, n), lambda i: (i, 0))
    consts = [mix_g.reshape(1, D), wq, wkv, wkra, wkrb, why, q_g.reshape(1, -1), kv_g.reshape(1, -1),
              wqa, wqb, wka, wv]
    HP = H * HEAD_PAD
    return pl.pallas_call(
        _inproj_kernel,
        grid=(T // tm,),
        in_specs=[row(D)] + [full(a) for a in consts]
        + [pl.BlockSpec((tm, 4 * HEAD_PAD), lambda i: (i % nseq, 0))],
        out_specs=[row(HP), row(HP), row(HP), row(C), row(C), row(C)],
        out_shape=[jax.ShapeDtypeStruct((T, HP), BF16)] * 3 + [jax.ShapeDtypeStruct((T, C), F32)] * 3,
        compiler_params=_cparams("parallel"),
        name="inproj",
    )(x2d, *consts, tab)


def _attn_kernel(q_ref, k_ref, v_ref, o_ref):
    outs = []
    for h in range(MLA_HEADS):
        sl = slice(h * HEAD_PAD, (h + 1) * HEAD_PAD)
        s = lax.dot_general(q_ref[0, :, sl], k_ref[0, :, sl], (((1,), (1,)), ((), ())),
                            preferred_element_type=F32)
        p = jnp.exp(s - jnp.max(s, axis=-1, keepdims=True))
        l = jnp.sum(p, axis=-1, keepdims=True)
        o = _dot(p.astype(BF16), v_ref[0, :, sl]) / l
        outs.append(o[:, :MLA_V])
    o_ref[0] = jnp.concatenate(outs, axis=1)


def _attention(q, k, v, tq=256):
    B, S, HP = q.shape
    return pl.pallas_call(
        _attn_kernel,
        grid=(B, S // tq),
        in_specs=[pl.BlockSpec((1, tq, HP), lambda b, i: (b, i, 0)),
                  pl.BlockSpec((1, S, HP), lambda b, i: (b, 0, 0)),
                  pl.BlockSpec((1, S, HP), lambda b, i: (b, 0, 0))],
        out_specs=pl.BlockSpec((1, tq, MLA_HEADS * MLA_V), lambda b, i: (b, i, 0)),
        out_shape=jax.ShapeDtypeStruct((B, S, MLA_HEADS * MLA_V), F32),
        compiler_params=_cparams("parallel", "arbitrary"),
        name="mla_attention",
    )(q, k, v)


def _dft_constants(seq):
    n = 2 * seq
    n1, n2 = FFT_N1, FFT_N2
    assert n1 * n2 == n
    r1 = np.arange(n1)
    r2 = np.arange(n2)
    blk = lambda z: np.block([[z.real, -z.imag], [z.imag, z.real]])
    w1 = np.exp(-2j * np.pi * np.outer(r1, r1) / n1)
    fa_data = blk(w1[:, :n1 // 2])
    fa_filt = np.concatenate([w1.real, w1.imag], axis=0)
    fc = blk(np.conj(w1).T[:n1 // 2, :])
    w2 = np.exp(-2j * np.pi * np.outer(r2, r2) / n2)
    tw = np.exp(-2j * np.pi * np.outer(r1, r2) / n)
    fb = np.stack([blk(w2 * tw[k][None, :]) for k in range(n1)])
    fbi = np.stack([blk(np.conj(w2).T * np.conj(tw[k])[:, None] / n) for k in range(n1)])
    as_bf = lambda a: jnp.asarray(a, dtype=F32).astype(BF16)
    return as_bf(fa_data), as_bf(fa_filt), as_bf(fc), as_bf(fb), as_bf(fbi)


def _filter_kernel(z_ref, w1_ref, b1_ref, fr_ref, w2_ref, b2_ref, w3_ref, b3_ref, dec_ref, o_ref, *, seq, tr):
    hp = lax.Precision.HIGHEST
    z = z_ref[...]
    fr = fr_ref[...]
    h = jnp.sin(fr[0:1] * (jnp.dot(z, w1_ref[...], precision=hp, preferred_element_type=F32) + b1_ref[...]))
    h = jnp.sin(fr[1:2] * (jnp.dot(h, w2_ref[...], precision=hp, preferred_element_type=F32) + b2_ref[...]))
    h = jnp.dot(h, w3_ref[0], precision=hp, preferred_element_type=F32) + b3_ref[0]
    h = h * jnp.exp(-z[:, 0:1] * jnp.abs(dec_ref[0]))
    n = pl.program_id(0) * tr + lax.broadcasted_iota(jnp.int32, h.shape, 0)
    o_ref[...] = jnp.where(n == seq, 0.0, h)


def _hyena_filter_time(seq, w1, b1, freq, w2, b2, w3, b3, decay, tr=512):
    n = 2 * seq
    emb, ffn = w1.shape
    C = w3.shape[1] // (HY_ORDER * HY_DIRS)
    off = jnp.arange(n)
    t = jnp.where(off < seq, off, n - off).astype(F32)
    bands = jnp.linspace(1e-4, HY_BANDS - 1, HY_BANDS, dtype=F32)
    ang = 2.0 * math.pi * t[:, None] * bands[None, :] / seq
    z = jnp.concatenate([(t / seq)[:, None], jnp.cos(ang), -jnp.sin(ang)], axis=-1)
    zl = 128
    z = jnp.pad(z, ((0, 0), (0, zl - emb)))
    w1p = jnp.pad(w1, ((0, zl - emb), (0, 0)))
    by_dir = lambda a: jnp.moveaxis(a.reshape(a.shape[0], HY_ORDER, HY_DIRS, C), 2, 0).reshape(
        HY_DIRS, a.shape[0], HY_ORDER * C)
    w3d, b3d, decd = by_dir(w3), by_dir(b3.reshape(1, -1)), by_dir(decay.reshape(1, -1))
    full = lambda a: pl.BlockSpec(a.shape, lambda i: (0,) * a.ndim)
    ndir = lambda a: pl.BlockSpec((1,) + a.shape[1:], lambda i: ((i * tr) // seq, 0, 0))
    consts = [w1p, b1.reshape(1, -1), freq, w2, b2.reshape(1, -1)]
    return pl.pallas_call(
        functools.partial(_filter_kernel, seq=seq, tr=tr),
        grid=(n // tr,),
        in_specs=[pl.BlockSpec((tr, zl), lambda i: (i, 0))] + [full(a) for a in consts]
        + [ndir(w3d), ndir(b3d), ndir(decd)],
        out_specs=pl.BlockSpec((tr, HY_ORDER * C), lambda i: (i, 0)),
        out_shape=jax.ShapeDtypeStruct((n, HY_ORDER * C), F32),
        compiler_params=_cparams("parallel"),
        name="hyena_filter_mlp",
    )(z, *consts, w3d, b3d, decd)


def _stage_a_kernel(x_ref, fa_ref, o_ref):
    a = _dot(fa_ref[...], x_ref[...].astype(BF16))
    o_ref[...] = a.reshape(o_ref.shape).astype(o_ref.dtype)


def _filter_stage_b_kernel(x_ref, fb_ref, o_ref):
    x = jnp.concatenate([x_ref[0, 0], x_ref[1, 0]], axis=0)
    o_ref[0] = _dot(fb_ref[0], x)


def _hyena_filter_spectrum(filt, fa_filt, fb, lb=4096):
    n, oc = filt.shape
    lanes = FFT_N2 * oc
    xa = filt.reshape(FFT_N1, lanes)
    a = pl.pallas_call(
        _stage_a_kernel,
        grid=(lanes // lb,),
        in_specs=[pl.BlockSpec((FFT_N1, lb), lambda j: (0, j)),
                  pl.BlockSpec(fa_filt.shape, lambda j: (0, 0))],
        out_specs=pl.BlockSpec((2, FFT_N1, lb), lambda j: (0, 0, j)),
        out_shape=jax.ShapeDtypeStruct((2, FFT_N1, lanes), BF16),
        compiler_params=_cparams("parallel"),
        name="hyena_filter_dft_a",
    )(xa, fa_filt)
    a = a.reshape(2, FFT_N1, FFT_N2, oc)
    return pl.pallas_call(
        _filter_stage_b_kernel,
        grid=(FFT_N1,),
        in_specs=[pl.BlockSpec((2, 1, FFT_N2, oc), lambda k: (0, k, 0, 0)),
                  pl.BlockSpec((1, 2 * FFT_N2, 2 * FFT_N2), lambda k: (k, 0, 0))],
        out_specs=pl.BlockSpec((1, 2 * FFT_N2, oc), lambda k: (k, 0, 0)),
        out_shape=jax.ShapeDtypeStruct((FFT_N1, 2 * FFT_N2, oc), F32),
        compiler_params=_cparams("parallel"),
        name="hyena_filter_dft_b",
    )(a, fb)


def _short_conv_kernel(x_ref, xp_ref, xn_ref, w_ref, b_ref, *rest, stage_a):
    j = pl.program_id(1)
    nb, r, lb = x_ref.shape
    c = xp_ref.shape[2]
    x = x_ref[...].reshape(nb * r, lb)
    xp = xp_ref[...].reshape(nb * r, c)
    xn = xn_ref[...].reshape(nb * r, c)
    row = lax.broadcasted_iota(jnp.int32, xp.shape, 0) % r
    xp = jnp.where(j == 0, jnp.where(row == 0, 0.0, pltpu.roll(xp, 1, axis=0)), xp)
    xn = jnp.where(j == pl.num_programs(1) - 1,
                   jnp.where(row == r - 1, 0.0, pltpu.roll(xn, nb * r - 1, axis=0)), xn)
    prev = jnp.concatenate([xp, x[:, :lb - c]], axis=1)
    nxt = jnp.concatenate([x[:, c:], xn], axis=1)
    w = w_ref[...]
    u = prev * w[0:1] + x * w[1:2] + nxt * w[2:3] + b_ref[...]
    if stage_a:
        fa_ref, u_out, a_out = rest
        a_out[0] = _dot(fa_ref[...], u.astype(BF16)).reshape(a_out.shape[1:]).astype(a_out.dtype)
    else:
        (u_out,) = rest
    u_out[...] = u.reshape(nb, r, lb)


def _short_conv(x, w3c, bc, fa=None, lb=4096):
    B, S, C = x.shape
    r = S // FFT_N2
    lanes = FFT_N2 * C
    per = lb // C
    nlb = FFT_N2 // per
    xv = x.reshape(B, r, lanes)
    wt = jnp.tile(w3c, (1, per))
    bt = jnp.tile(bc.reshape(1, C), (1, per))
    nb = 2 if fa is not None else 1
    in_specs = [pl.BlockSpec((nb, r, lb), lambda p, j: (p, 0, j)),
                pl.BlockSpec((nb, r, C), lambda p, j: (p, 0, (j * per + FFT_N2 - 1) % FFT_N2)),
                pl.BlockSpec((nb, r, C), lambda p, j: (p, 0, (j * per + per) % FFT_N2)),
                pl.BlockSpec(wt.shape, lambda p, j: (0, 0)),
                pl.BlockSpec(bt.shape, lambda p, j: (0, 0))]
    out_specs = [pl.BlockSpec((nb, r, lb), lambda p, j: (p, 0, j))]
    out_shape = [jax.ShapeDtypeStruct((B, r, lanes), F32)]
    args = [xv, xv, xv, wt, bt]
    if fa is not None:
        in_specs.append(pl.BlockSpec(fa.shape, lambda p, j: (0, 0)))
        out_specs.append(pl.BlockSpec((1, 2, FFT_N1, lb), lambda p, j: (p, 0, 0, j)))
        out_shape.append(jax.ShapeDtypeStruct((B // 2, 2, FFT_N1, lanes), BF16))
        args.append(fa)
    return pl.pallas_call(
        functools.partial(_short_conv_kernel, stage_a=fa is not None),
        grid=(B // nb, nlb),
        in_specs=in_specs, out_specs=out_specs, out_shape=out_shape,
        compiler_params=_cparams("parallel", "arbitrary"),
        name="hyena_short_conv",
    )(*args)


def _stage_b_kernel(x_ref, fb_ref, kf_ref, fbi_ref, o_ref):
    npair = x_ref.shape[0]
    n2 = x_ref.shape[3]
    x = jnp.concatenate(
        [jnp.concatenate([x_ref[p, 0, 0], x_ref[p, 1, 0]], axis=0) for p in range(npair)], axis=1)
    g = _dot(fb_ref[0], x)
    gr, gi = g[:n2], g[n2:]
    kf = kf_ref[0]
    kr = jnp.concatenate([kf[:n2]] * npair, axis=1)
    ki = jnp.concatenate([kf[n2:]] * npair, axis=1)
    hcat = jnp.concatenate([gr * kr - gi * ki, gr * ki + gi * kr], axis=0).astype(BF16)
    y = _dot(fbi_ref[0], hcat)
    c = x_ref.shape[4]
    for p in range(npair):
        o_ref[p, 0, 0] = y[:n2, p * c:(p + 1) * c].astype(o_ref.dtype)
        o_ref[p, 1, 0] = y[n2:, p * c:(p + 1) * c].astype(o_ref.dtype)


def _stage_b(spec, fb, kf, fbi, order, C):
    npair = spec.shape[0]
    sv = spec.reshape(npair, 2, FFT_N1, FFT_N2, C)
    blk = pl.BlockSpec((npair, 2, 1, FFT_N2, C), lambda k: (0, 0, k, 0, 0))
    mat = pl.BlockSpec((1, 2 * FFT_N2, 2 * FFT_N2), lambda k: (k, 0, 0))
    out = pl.pallas_call(
        _stage_b_kernel,
        grid=(FFT_N1,),
        in_specs=[blk, mat, pl.BlockSpec((1, 2 * FFT_N2, C), lambda k: (k, 0, order)), mat],
        out_specs=blk,
        out_shape=jax.ShapeDtypeStruct(sv.shape, BF16),
        compiler_params=_cparams("parallel"),
        name="hyena_dft_b",
    )(sv, fb, kf, fbi)
    return out.reshape(spec.shape)


def _stage_c_kernel(y_ref, fc_ref, gate_ref, z_ref, skip_ref, *rest, stage_a):
    nb, r, lb = gate_ref.shape
    conv = _dot(fc_ref[...], y_ref[0].reshape(2 * FFT_N1, lb))
    zin = z_ref[...].reshape(nb * r, lb)
    z = gate_ref[...].reshape(nb * r, lb) * (conv + skip_ref[...] * zin)
    if stage_a:
        fa_ref, z_out, a_out = rest
        a_out[0] = _dot(fa_ref[...], z.astype(BF16)).reshape(a_out.shape[1:]).astype(a_out.dtype)
    else:
        (z_out,) = rest
    z_out[...] = z.reshape(nb, r, lb)


def _stage_c(yspec, fc, gate, zin, skip, fa=None, lb=4096):
    B, r, lanes = gate.shape
    C = skip.shape[0]
    st = jnp.tile(skip.reshape(1, C), (1, lb // C))
    dat = pl.BlockSpec((2, r, lb), lambda p, j: (p, 0, j))
    spc = pl.BlockSpec((1, 2, FFT_N1, lb), lambda p, j: (p, 0, 0, j))
    in_specs = [spc, pl.BlockSpec(fc.shape, lambda p, j: (0, 0)), dat, dat,
                pl.BlockSpec(st.shape, lambda p, j: (0, 0))]
    out_specs = [dat]
    out_shape = [jax.ShapeDtypeStruct(gate.shape, F32)]
    args = [yspec, fc, gate, zin, st]
    if fa is not None:
        in_specs.append(pl.BlockSpec(fa.shape, lambda p, j: (0, 0)))
        out_specs.append(spc)
        out_shape.append(jax.ShapeDtypeStruct(yspec.shape, BF16))
        args.append(fa)
    return pl.pallas_call(
        functools.partial(_stage_c_kernel, stage_a=fa is not None),
        grid=(B // 2, lanes // lb),
        in_specs=in_specs, out_specs=out_specs, out_shape=out_shape,
        compiler_params=_cparams("parallel", "arbitrary"),
        name="hyena_dft_c",
    )(*args)


def _hyena(x1p, x2p, vp, conv_w, conv_b, skip, kf, consts):
    fa_data, _, fc, fb, fbi = consts
    B, S, C = vp.shape
    cw = lambda i: conv_w[:, i * C:(i + 1) * C]
    cb = lambda i: conv_b[i * C:(i + 1) * C]
    (x1,) = _short_conv(x1p, cw(0), cb(0))
    (x2,) = _short_conv(x2p, cw(1), cb(1))
    v, a0 = _short_conv(vp, cw(2), cb(2), fa=fa_data)
    y0 = _stage_b(a0, fb, kf, fbi, 0, C)
    z1, a1 = _stage_c(y0, fc, x1, v, skip[0], fa=fa_data)
    y1 = _stage_b(a1, fb, kf, fbi, 1, C)
    (out,) = _stage_c(y1, fc, x2, z1, skip[1])
    return out.reshape(B, S, C)


def _memkv_kernel(m_ref, g_ref, w_ref, k_out, v_out):
    hm = _rms(m_ref[0], g_ref[...]).astype(BF16)
    kv = _dot(hm, w_ref[...])
    d = k_out.shape[2]
    k_out[0] = kv[:, :d].astype(BF16)
    v_out[0] = kv[:, d:].astype(BF16)


def _memkv(mem, g, w_mkv):
    B, M, D = mem.shape
    dk = w_mkv.shape[1] // 2
    w = w_mkv.astype(BF16)
    return pl.pallas_call(
        _memkv_kernel,
        grid=(B,),
        in_specs=[pl.BlockSpec((1, M, D), lambda b: (b, 0, 0)),
                  pl.BlockSpec((1, D), lambda b: (0, 0)),
                  pl.BlockSpec(w.shape, lambda b: (0, 0))],
        out_specs=[pl.BlockSpec((1, M, dk), lambda b: (b, 0, 0))] * 2,
        out_shape=[jax.ShapeDtypeStruct((B, M, dk), BF16)] * 2,
        compiler_params=_cparams("parallel"),
        name="mem_kv",
    )(mem, g.reshape(1, D), w)


def _route(logits):
    lane = lax.broadcasted_iota(jnp.int32, logits.shape, 1)
    ninf = -jnp.inf
    big = ROUTE_LANES
    first = lambda mask: jnp.min(jnp.where(mask, lane, big), axis=-1, keepdims=True)
    is_g = (lane >= N_EXPERTS) & (lane < N_EXPERTS + N_GROUPS)
    gl = jnp.where(is_g, logits, ninf)
    gmax = jnp.max(gl, axis=-1, keepdims=True)
    g_idx = first(gl == gmax) - N_EXPERTS
    p_group = 1.0 / jnp.sum(jnp.exp(gl - gmax), axis=-1, keepdims=True)
    in_g = (lane < N_EXPERTS) & ((lane // EXPERTS_PER_GROUP) == g_idx)
    el = jnp.where(in_g, logits, ninf)
    v1 = jnp.max(el, axis=-1, keepdims=True)
    i1 = first(el == v1)
    el2 = jnp.where(lane == i1, ninf, el)
    v2 = jnp.max(el2, axis=-1, keepdims=True)
    i2 = first(el2 == v2)
    e2 = jnp.exp(v2 - v1)
    p1 = 1.0 / (1.0 + e2)
    p2 = e2 / (1.0 + e2)
    sel = lambda n, val: jnp.where(lane == n, val, 0.0)
    return (sel(ROUTE_ID0, i1.astype(F32)) + sel(ROUTE_ID0 + 1, i2.astype(F32))
            + sel(ROUTE_W0, p_group * p1) + sel(ROUTE_W0 + 1, p_group * p2))


def _postmix_kernel(x_ref, a_ref, hy_ref, ag_ref, hg_ref, woa_ref, woh_ref, cg_ref, wmq_ref,
                    mk_ref, mv_ref, wmo_ref, fg_ref, wr_ref, br_ref, x_out, hn_out, route_out):
    ra = _rms(a_ref[...], ag_ref[...]).astype(BF16)
    rh = _rms(hy_ref[...], hg_ref[...]).astype(BF16)
    x = x_ref[...] + _dot(ra, woa_ref[...]) + _dot(rh, woh_ref[...])
    q = _dot(_rms(x, cg_ref[...]).astype(BF16), wmq_ref[...])
    dh = q.shape[1] // MEM_HEADS
    outs = []
    for h in range(MEM_HEADS):
        sl = slice(h * dh, (h + 1) * dh)
        s = lax.dot_general(q[:, sl].astype(BF16), mk_ref[0, :, sl], (((1,), (1,)), ((), ())),
                            preferred_element_type=F32) * dh ** -0.5
        p = jnp.exp(s - jnp.max(s, axis=-1, keepdims=True))
        l = jnp.sum(p, axis=-1, keepdims=True)
        outs.append(_dot(p.astype(BF16), mv_ref[0, :, sl]) / l)
    o = jnp.concatenate(outs, axis=1).astype(BF16)
    x = x + _dot(o, wmo_ref[...])
    x_out[...] = x
    hn = _rms(x, fg_ref[...])
    _store_row_tiles(hn_out, hn)
    route_out[...] = _route(_dot(hn.astype(BF16), wr_ref[...]) + br_ref[...])


def _postmix(x2d, a2d, hy2d, seq, ag, hg, w_out, cg, w_mq, mk, mv, w_mo, fg, w_rg, b_rg, w_re, b_re, tm=512):
    T, D = x2d.shape
    ca = a2d.shape[1]
    woa = w_out[:ca].astype(BF16)
    woh = w_out[ca:].astype(BF16)
    pad = ROUTE_LANES - N_EXPERTS - N_GROUPS
    wr = jnp.concatenate([w_re, w_rg, jnp.zeros((D, pad), F32)], 1).astype(BF16)
    br = jnp.concatenate([b_re, b_rg, jnp.zeros((pad,), F32)]).reshape(1, ROUTE_LANES)
    nseq = seq // tm
    full = lambda a: pl.BlockSpec(a.shape, lambda i: (0,) * a.ndim)
    row = lambda n: pl.BlockSpec((tm, n), lambda i: (i, 0))
    memb = pl.BlockSpec((1,) + mk.shape[1:], lambda i: (i // nseq, 0, 0))
    args = [x2d, a2d, hy2d, ag.reshape(1, -1), hg.reshape(1, -1), woa, woh, cg.reshape(1, D),
            w_mq.astype(BF16), mk, mv, w_mo.astype(BF16), fg.reshape(1, D), wr, br]
    in_specs = [row(D), row(ca), row(hy2d.shape[1])] + [full(a) for a in args[3:9]] + [memb, memb] \
        + [full(a) for a in args[11:]]
    return pl.pallas_call(
        _postmix_kernel,
        grid=(T // tm,),
        in_specs=in_specs,
        out_specs=[row(D), pl.BlockSpec((tm, SUBLANES, LANES), lambda i: (i, 0, 0)), row(ROUTE_LANES)],
        out_shape=[jax.ShapeDtypeStruct((T, D), F32), jax.ShapeDtypeStruct((T, SUBLANES, LANES), F32),
                   jax.ShapeDtypeStruct((T, ROUTE_LANES), F32)],
        compiler_params=_cparams("parallel"),
        name="postmix",
    )(*args)


def _rank_kernel(route_ref, rank_out, cnt_out, carry_ref):
    @pl.when(pl.program_id(0) == 0)
    def _():
        carry_ref[...] = jnp.zeros_like(carry_ref)

    route = route_ref[...]
    tr = route.shape[0]
    lane = lax.broadcasted_iota(jnp.int32, route.shape, 1).astype(F32)
    oh1 = lane == route[:, ROUTE_ID0:ROUTE_ID0 + 1]
    oh2 = lane == route[:, ROUTE_ID0 + 1:ROUTE_ID0 + 2]
    cnt = jnp.where(oh1 | oh2, 1.0, 0.0)
    r = lax.broadcasted_iota(jnp.int32, (tr, tr), 0)
    c = lax.broadcasted_iota(jnp.int32, (tr, tr), 1)
    below = jnp.where(c < r, 1.0, 0.0).astype(BF16)
    cum = _dot(below, cnt.astype(BF16)) + carry_ref[...]
    r1 = jnp.sum(jnp.where(oh1, cum, 0.0), axis=-1, keepdims=True)
    r2 = jnp.sum(jnp.where(oh2, cum, 0.0), axis=-1, keepdims=True)
    rank_out[...] = jnp.where(lane == 0, r1, 0.0) + jnp.where(lane == 1, r2, 0.0)
    carry_ref[...] += jnp.sum(cnt, axis=0, keepdims=True)
    cnt_out[...] = carry_ref[...]


def _pos_kernel(route_ref, rank_ref, base_ref, pos_out):
    route = route_ref[...]
    rank = rank_ref[...]
    lane = lax.broadcasted_iota(jnp.int32, route.shape, 1).astype(F32)
    base = base_ref[...]
    pick = lambda k: jnp.sum(jnp.where(lane == route[:, ROUTE_ID0 + k:ROUTE_ID0 + k + 1], base, 0.0),
                             axis=-1, keepdims=True) + rank[:, k:k + 1]
    pos_out[...] = (jnp.where(lane == 0, pick(0), 0.0) + jnp.where(lane == 1, pick(1), 0.0)).astype(jnp.int32)


def _dispatch_kernel(base_ref, cp_ref, pos_ref, hn_hbm, xs_hbm, zero_ref, sem, *, tt, tmm):
    i = pl.program_id(0)
    row_copy = lambda src, dst: pltpu.make_async_copy(
        hn_hbm.at[pl.ds(src, 1)], xs_hbm.at[pl.ds(dst, 1)], sem)

    @pl.when(i == 0)
    def _():
        zero_ref[...] = jnp.zeros_like(zero_ref)
        pad_copy = lambda e: pltpu.make_async_copy(
            zero_ref, xs_hbm.at[pl.ds(base_ref[e] + cp_ref[e] - tmm, tmm)], sem)
        for e in range(N_EXPERTS):
            @pl.when(cp_ref[e] > 0)
            def _():
                pad_copy(e).start()
        for e in range(N_EXPERTS):
            @pl.when(cp_ref[e] > 0)
            def _():
                pad_copy(e).wait()
        last = N_EXPERTS - 1
        tail_copy = lambda r: pltpu.make_async_copy(zero_ref, xs_hbm.at[pl.ds(r * tmm, tmm)], sem)
        first_free = (base_ref[last] + cp_ref[last]) // tmm
        n_tiles = xs_hbm.shape[0] // tmm
        lax.fori_loop(first_free, n_tiles, lambda r, c: (tail_copy(r).start(), c)[1], 0)
        lax.fori_loop(first_free, n_tiles, lambda r, c: (tail_copy(r).wait(), c)[1], 0)

    def start(t, carry):
        row_copy(i * tt + t, pos_ref[2 * t]).start()
        row_copy(i * tt + t, pos_ref[2 * t + 1]).start()
        return carry

    lax.fori_loop(0, tt, start, 0)

    def wait(t, carry):
        row_copy(0, 0).wait()
        row_copy(0, 0).wait()
        return carry

    lax.fori_loop(0, tt, wait, 0)


def _ffn_kernel(te_ref, nu_ref, xs_ref, wg_ref, wu_ref, wd_ref, ys_ref):
    used = pl.program_id(0) < nu_ref[0]

    @pl.when(used)
    def _():
        x = _load_row_tiles(xs_ref).astype(BF16)
        a = _dot(x, wg_ref[0].astype(BF16))
        b = _dot(x, wu_ref[0].astype(BF16))
        m = (a * jax.nn.sigmoid(a)) * b
        _store_row_tiles(ys_ref, _dot(m.astype(BF16), wd_ref[0].astype(BF16)))

    @pl.when(jnp.logical_not(used))
    def _():
        ys_ref[...] = jnp.zeros_like(ys_ref)


def _combine_kernel(pos_ref, ys_hbm, x_ref, route_ref, fg_ref, o_ref, buf_ref, sem, *, tc):
    row_copy = lambda k, t, p: pltpu.make_async_copy(
        ys_hbm.at[pl.ds(p, 1)], buf_ref.at[k, pl.ds(t, 1)], sem)

    def start(t, carry):
        row_copy(0, t, pos_ref[2 * t]).start()
        row_copy(1, t, pos_ref[2 * t + 1]).start()
        return carry

    lax.fori_loop(0, tc, start, 0)

    def wait(t, carry):
        row_copy(0, 0, 0).wait()
        row_copy(1, 0, 0).wait()
        return carry

    lax.fori_loop(0, tc, wait, 0)
    route = route_ref[...]
    y = (x_ref[...] + route[:, ROUTE_W0:ROUTE_W0 + 1] * _load_row_tiles(buf_ref.at[0])
         + route[:, ROUTE_W0 + 1:ROUTE_W0 + 2] * _load_row_tiles(buf_ref.at[1]))
    o_ref[...] = _rms(y, fg_ref[...])


def _moe(hn, route, x2d, w_gate, w_up, w_down, fg, tr=512, tt=512, tc=256):
    T, D = x2d.shape
    E = N_EXPERTS
    F = w_gate.shape[-1]
    tmm = MOE_ROW_TILE
    row = lambda tm, n: pl.BlockSpec((tm, n), lambda i: (i, 0))
    rank, counts = pl.pallas_call(
        _rank_kernel,
        grid=(T // tr,),
        in_specs=[row(tr, ROUTE_LANES)],
        out_specs=[row(tr, ROUTE_LANES), pl.BlockSpec((1, ROUTE_LANES), lambda i: (0, 0))],
        out_shape=[jax.ShapeDtypeStruct((T, ROUTE_LANES), F32), jax.ShapeDtypeStruct((1, ROUTE_LANES), F32)],
        scratch_shapes=[pltpu.VMEM((1, ROUTE_LANES), F32)],
        compiler_params=_cparams("arbitrary"),
        name="moe_rank",
    )(route)

    cnt = counts[0, :E].astype(jnp.int32)
    cp = ((cnt + tmm - 1) // tmm) * tmm
    ends = jnp.cumsum(cp)
    base = ends - cp
    n_used = ends[-1] // tmm
    n_tiles = (2 * T) // tmm + E
    tile_start = jnp.minimum(jnp.arange(n_tiles, dtype=jnp.int32), n_used - 1) * tmm
    tile_expert = jnp.minimum(jnp.searchsorted(ends, tile_start, side="right"), E - 1).astype(jnp.int32)
    base_lanes = jnp.zeros((1, ROUTE_LANES), F32).at[0, :E].set(base.astype(F32))

    pos = pl.pallas_call(
        _pos_kernel,
        grid=(T // tr,),
        in_specs=[row(tr, ROUTE_LANES), row(tr, ROUTE_LANES), pl.BlockSpec((1, ROUTE_LANES), lambda i: (0, 0))],
        out_specs=row(tr, ROUTE_LANES),
        out_shape=jax.ShapeDtypeStruct((T, ROUTE_LANES), jnp.int32),
        compiler_params=_cparams("parallel"),
        name="moe_pos",
    )(route, rank, base_lanes)
    pos = pos[:, :2].reshape(2 * T)

    xs = pl.pallas_call(
        functools.partial(_dispatch_kernel, tt=tt, tmm=tmm),
        grid_spec=pltpu.PrefetchScalarGridSpec(
            num_scalar_prefetch=2,
            grid=(T // tt,),
            in_specs=[pl.BlockSpec((2 * tt,), lambda i, b, c: (i,), memory_space=pltpu.SMEM),
                      pl.BlockSpec(memory_space=pl.ANY)],
            out_specs=pl.BlockSpec(memory_space=pl.ANY),
            scratch_shapes=[pltpu.VMEM((tmm, SUBLANES, LANES), F32), pltpu.SemaphoreType.DMA(())]),
        out_shape=jax.ShapeDtypeStruct((n_tiles * tmm, SUBLANES, LANES), F32),
        compiler_params=_cparams("arbitrary"),
        name="moe_dispatch",
    )(base, cp, pos, hn)

    tile = lambda r, te, nu: (jnp.minimum(r, nu[0] - 1), 0, 0)
    ys = pl.pallas_call(
        _ffn_kernel,
        grid_spec=pltpu.PrefetchScalarGridSpec(
            num_scalar_prefetch=2,
            grid=(n_tiles,),
            in_specs=[pl.BlockSpec((tmm, SUBLANES, LANES), tile),
                      pl.BlockSpec((1, D, F), lambda r, te, nu: (te[r], 0, 0)),
                      pl.BlockSpec((1, D, F), lambda r, te, nu: (te[r], 0, 0)),
                      pl.BlockSpec((1, F, D), lambda r, te, nu: (te[r], 0, 0))],
            out_specs=pl.BlockSpec((tmm, SUBLANES, LANES), lambda r, te, nu: (r, 0, 0))),
        out_shape=jax.ShapeDtypeStruct((n_tiles * tmm, SUBLANES, LANES), F32),
        compiler_params=_cparams("arbitrary"),
        name="moe_ffn",
    )(tile_expert, n_used.reshape(1), xs, w_gate.reshape(E, D, F), w_up.reshape(E, D, F),
      w_down.reshape(E, F, D))

    return pl.pallas_call(
        functools.partial(_combine_kernel, tc=tc),
        grid=(T // tc,),
        in_specs=[pl.BlockSpec((2 * tc,), lambda i: (i,), memory_space=pltpu.SMEM),
                  pl.BlockSpec(memory_space=pl.ANY),
                  row(tc, D), row(tc, ROUTE_LANES), pl.BlockSpec((1, D), lambda i: (0, 0))],
        out_specs=row(tc, D),
        out_shape=jax.ShapeDtypeStruct((T, D), F32),
        scratch_shapes=[pltpu.VMEM((2, tc, SUBLANES, LANES), F32), pltpu.SemaphoreType.DMA(())],
        compiler_params=_cparams("arbitrary"),
        name="moe_combine",
    )(pos, ys, x2d, route, fg.reshape(1, D))


def kernel(x, mem, mix_norm_g, w_in, q_norm_g, kv_norm_g, w_uq, w_ukv, hy_conv_w, hy_conv_b, hy_w1, hy_b1, hy_freq, hy_w2, hy_b2, hy_w3, hy_b3, hy_decay, hy_skip, attn_out_g, hy_out_g, w_out, cross_norm_g, mem_norm_g, w_mq, w_mkv, w_mo, ffn_norm_g, w_route_group, b_route_group, w_route_expert, b_route_expert, w_gate, w_up, w_down, final_norm_g):
    B, S, D = x.shape
    depth = w_in.shape[0]
    consts = _dft_constants(S)
    xf = x.reshape(B * S, D)
    for l in range(depth):
        q, k, v, x1p, x2p, vp = _inproj(xf, S, mix_norm_g[l], w_in[l], q_norm_g[l], kv_norm_g[l],
                                        w_uq[l], w_ukv[l])
        HP = q.shape[1]
        a_out = _attention(q.reshape(B, S, HP), k.reshape(B, S, HP), v.reshape(B, S, HP))
        filt = _hyena_filter_time(S, hy_w1[l], hy_b1[l], hy_freq[l], hy_w2[l], hy_b2[l], hy_w3[l],
                                  hy_b3[l], hy_decay[l])
        kf = _hyena_filter_spectrum(filt, consts[1], consts[3])
        C = vp.shape[1]
        h_out = _hyena(x1p.reshape(B, S, C), x2p.reshape(B, S, C), vp.reshape(B, S, C),
                       hy_conv_w[l], hy_conv_b[l], hy_skip[l], kf, consts)
        mk, mv = _memkv(mem, mem_norm_g[l], w_mkv[l])
        x2, hn, route = _postmix(xf, a_out.reshape(B * S, -1), h_out.reshape(B * S, C), S,
                                attn_out_g[l], hy_out_g[l], w_out[l], cross_norm_g[l], w_mq[l], mk, mv,
                                w_mo[l], ffn_norm_g[l], w_route_group[l], b_route_group[l],
                                w_route_expert[l], b_route_expert[l])
        assert depth == 1
        xf = _moe(hn, route, x2, w_gate[l], w_up[l], w_down[l], final_norm_g)
    return xf.reshape(B, S, D)
```

```python
import functools
import math

import numpy as np
import jax
import jax.numpy as jnp
from jax import lax
from jax.experimental import pallas as pl
from jax.experimental.pallas import tpu as pltpu

F32 = jnp.float32
BF16 = jnp.bfloat16

EPS = 1e-6
MLA_HEADS = 8
MLA_NOPE = 64
MLA_ROPE = 32
MLA_V = 64
ROPE_BASE = 10000.0
HEAD_PAD = 128
HY_ORDER = 2
HY_DIRS = 2
HY_BANDS = 16
MEM_HEADS = 4
N_GROUPS = 4
EXPERTS_PER_GROUP = 8
N_EXPERTS = N_GROUPS * EXPERTS_PER_GROUP
ROUTE_LANES = 128
ROUTE_ID0 = 0
ROUTE_W0 = 2
MOE_ROW_TILE = 256
DMA_UNROLL = 8

FFT_N1 = 64
FFT_N2 = 128

VMEM_LIMIT = 56 * 1024 * 1024


def _cparams(*sem):
    return pltpu.CompilerParams(dimension_semantics=sem, vmem_limit_bytes=VMEM_LIMIT)


def _rms(x, g):
    return x * lax.rsqrt(jnp.mean(x * x, axis=-1, keepdims=True) + EPS) * g


def _dot(a, b):
    return jnp.dot(a, b, preferred_element_type=F32)


SUBLANES = 8
LANES = 128


def _load_row_tiles(ref):
    return jnp.concatenate([ref[:, j, :] for j in range(SUBLANES)], axis=1)


def _store_row_tiles(ref, val):
    for j in range(SUBLANES):
        ref[:, j, :] = val[:, j * LANES:(j + 1) * LANES]


def _inproj_kernel(x_ref, xp_ref, xn_ref, g_ref, wq_ref, wkv_ref, wkra_ref, wkrb_ref, why_ref, qg_ref,
                   kvg_ref, wqa_ref, wqb_ref, wka_ref, wv_ref, tab_ref, cw_ref, cb_ref,
                   q_out, k_out, v_out, x1_out, x2_out, hv_out, *, nseq):
    tm = x_ref.shape[0]
    halo = xp_ref.shape[0]
    hf = _rms(jnp.concatenate([xp_ref[...], x_ref[...], xn_ref[...]], axis=0), g_ref[...])
    h = hf[halo:halo + tm].astype(BF16)
    qn = _rms(_dot(h, wq_ref[...]), qg_ref[...]).astype(BF16)
    kvn = _rms(_dot(h, wkv_ref[...]), kvg_ref[...]).astype(BF16)
    tab = tab_ref[...]
    cq, sq, ck, sk = (tab[:, i * HEAD_PAD:(i + 1) * HEAD_PAD] for i in range(4))
    tile = lambda t: jnp.concatenate([t] * MLA_HEADS, axis=1)
    q = _dot(qn, wqa_ref[...]) * tile(cq) + _dot(qn, wqb_ref[...]) * tile(sq)
    q_out[...] = q.astype(BF16)
    kr = _dot(h, wkra_ref[...]) * ck + _dot(h, wkrb_ref[...]) * sk
    k_out[...] = (_dot(kvn, wka_ref[...]) + tile(kr)).astype(BF16)
    v_out[...] = _dot(kvn, wv_ref[...]).astype(BF16)
    hy = _dot(hf.astype(BF16), why_ref[...])
    i = pl.program_id(0) % nseq
    row = lax.broadcasted_iota(jnp.int32, hy.shape, 0)
    outside = ((row == halo - 1) & (i == 0)) | ((row == halo + tm) & (i == nseq - 1))
    hy = jnp.where(outside, 0.0, hy)
    cw = cw_ref[...]
    u = (hy[halo - 1:halo - 1 + tm] * cw[0:1] + hy[halo:halo + tm] * cw[1:2]
         + hy[halo + 1:halo + 1 + tm] * cw[2:3] + cb_ref[...])
    c = x1_out.shape[1]
    x1_out[...] = u[:, :c]
    x2_out[...] = u[:, c:2 * c]
    hv_out[...] = u[:, 2 * c:]


def _inproj(x2d, seq, mix_g, w_in, q_g, kv_g, w_uq, w_ukv, conv_w, conv_b, tm=512):
    T, D = x2d.shape
    per = tm // SUBLANES
    cb = conv_b.reshape(1, -1)
    q_rank, kv_rank = q_g.shape[0], kv_g.shape[0]
    off_kv = q_rank
    off_kr = off_kv + kv_rank
    off_hy = off_kr + MLA_ROPE
    C = (w_in.shape[1] - off_hy) // 3
    H = MLA_HEADS
    half = MLA_ROPE // 2
    wq = w_in[:, :off_kv].astype(BF16)
    wkv = w_in[:, off_kv:off_kr].astype(BF16)
    wkr = w_in[:, off_kr:off_hy]
    wkr_sw = jnp.concatenate([wkr[:, half:], wkr[:, :half]], axis=1)
    zpad = lambda n: jnp.zeros((D, n), F32)
    wkra = jnp.concatenate([zpad(MLA_NOPE), wkr, zpad(HEAD_PAD - MLA_NOPE - MLA_ROPE)], 1).astype(BF16)
    wkrb = jnp.concatenate([zpad(MLA_NOPE), wkr_sw, zpad(HEAD_PAD - MLA_NOPE - MLA_ROPE)], 1).astype(BF16)
    why = w_in[:, off_hy:].astype(BF16)

    uq = w_uq.reshape(q_rank, H, MLA_NOPE + MLA_ROPE)
    uq_n, uq_r = uq[..., :MLA_NOPE], uq[..., MLA_NOPE:]
    uq_rs = jnp.concatenate([uq_r[..., half:], uq_r[..., :half]], axis=-1)
    zq = lambda n: jnp.zeros((q_rank, H, n), F32)
    wqa = jnp.concatenate([uq_n, uq_r, zq(HEAD_PAD - MLA_NOPE - MLA_ROPE)], -1).reshape(q_rank, H * HEAD_PAD).astype(BF16)
    wqb = jnp.concatenate([zq(MLA_NOPE), uq_rs, zq(HEAD_PAD - MLA_NOPE - MLA_ROPE)], -1).reshape(q_rank, H * HEAD_PAD).astype(BF16)
    ukv = w_ukv.reshape(kv_rank, H, MLA_NOPE + MLA_V)
    zk = lambda n: jnp.zeros((kv_rank, H, n), F32)
    wka = jnp.concatenate([ukv[..., :MLA_NOPE], zk(HEAD_PAD - MLA_NOPE)], -1).reshape(kv_rank, H * HEAD_PAD).astype(BF16)
    wv = jnp.concatenate([ukv[..., MLA_NOPE:], zk(HEAD_PAD - MLA_V)], -1).reshape(kv_rank, H * HEAD_PAD).astype(BF16)

    pos = jnp.arange(seq, dtype=F32)
    inv = ROPE_BASE ** (-jnp.arange(half, dtype=F32) / half)
    ang = pos[:, None] * inv[None, :]
    cos2 = jnp.concatenate([jnp.cos(ang), jnp.cos(ang)], 1)
    sin2 = jnp.concatenate([-jnp.sin(ang), jnp.sin(ang)], 1)
    zs = lambda n: jnp.zeros((seq, n), F32)
    scale = (MLA_NOPE + MLA_ROPE) ** -0.5
    rest = HEAD_PAD - MLA_NOPE - MLA_ROPE
    cq = scale * jnp.concatenate([jnp.ones((seq, MLA_NOPE), F32), cos2, zs(rest)], 1)
    sq = scale * jnp.concatenate([zs(MLA_NOPE), sin2, zs(rest)], 1)
    ck = jnp.concatenate([zs(MLA_NOPE), cos2, zs(rest)], 1)
    sk = jnp.concatenate([zs(MLA_NOPE), sin2, zs(rest)], 1)
    tab = jnp.concatenate([cq, sq, ck, sk], 1)

    nseq = seq // tm
    full = lambda a: pl.BlockSpec(a.shape, lambda i: (0,) * a.ndim)
    row = lambda n: pl.BlockSpec((tm, n), lambda i: (i, 0))
    consts = [mix_g.reshape(1, D), wq, wkv, wkra, wkrb, why, q_g.reshape(1, -1), kv_g.reshape(1, -1),
              wqa, wqb, wka, wv]
    HP = H * HEAD_PAD
    return pl.pallas_call(
        functools.partial(_inproj_kernel, nseq=nseq),
        grid=(T // tm,),
        in_specs=[row(D),
                  pl.BlockSpec((SUBLANES, D), lambda i: (jnp.maximum(i * per - 1, 0), 0)),
                  pl.BlockSpec((SUBLANES, D), lambda i: (jnp.minimum((i + 1) * per, T // SUBLANES - 1), 0))]
        + [full(a) for a in consts]
        + [pl.BlockSpec((tm, 4 * HEAD_PAD), lambda i: (i % nseq, 0)), full(conv_w), full(cb)],
        out_specs=[row(HP), row(HP), row(HP), row(C), row(C), row(C)],
        out_shape=[jax.ShapeDtypeStruct((T, HP), BF16)] * 3 + [jax.ShapeDtypeStruct((T, C), F32)] * 3,
        compiler_params=_cparams("parallel"),
        name="inproj",
    )(x2d, x2d, x2d, *consts, tab, conv_w, cb)


def _attn_kernel(q_ref, k_ref, v_ref, o_ref):
    outs = []
    for h in range(MLA_HEADS):
        sl = slice(h * HEAD_PAD, (h + 1) * HEAD_PAD)
        s = lax.dot_general(q_ref[0, :, sl], k_ref[0, :, sl], (((1,), (1,)), ((), ())),
                            preferred_element_type=F32)
        p = jnp.exp(s - jnp.max(s, axis=-1, keepdims=True))
        l = jnp.sum(p, axis=-1, keepdims=True)
        o = _dot(p.astype(BF16), v_ref[0, :, sl]) / l
        outs.append(o[:, :MLA_V])
    o_ref[0] = jnp.concatenate(outs, axis=1)


def _attention(q, k, v, tq=256):
    B, S, HP = q.shape
    return pl.pallas_call(
        _attn_kernel,
        grid=(B, S // tq),
        in_specs=[pl.BlockSpec((1, tq, HP), lambda b, i: (b, i, 0)),
                  pl.BlockSpec((1, S, HP), lambda b, i: (b, 0, 0)),
                  pl.BlockSpec((1, S, HP), lambda b, i: (b, 0, 0))],
        out_specs=pl.BlockSpec((1, tq, MLA_HEADS * MLA_V), lambda b, i: (b, i, 0)),
        out_shape=jax.ShapeDtypeStruct((B, S, MLA_HEADS * MLA_V), F32),
        compiler_params=_cparams("parallel", "arbitrary"),
        name="mla_attention",
    )(q, k, v)


def _dft_constants(seq):
    n = 2 * seq
    n1, n2 = FFT_N1, FFT_N2
    assert n1 * n2 == n
    r1 = np.arange(n1)
    r2 = np.arange(n2)
    blk = lambda z: np.block([[z.real, -z.imag], [z.imag, z.real]])
    w1 = np.exp(-2j * np.pi * np.outer(r1, r1) / n1)
    fa_data = blk(w1[:, :n1 // 2])
    fa_filt = np.concatenate([w1.real, w1.imag], axis=0)
    fc = blk(np.conj(w1).T[:n1 // 2, :])
    w2 = np.exp(-2j * np.pi * np.outer(r2, r2) / n2)
    tw = np.exp(-2j * np.pi * np.outer(r1, r2) / n)
    fb = np.stack([blk(w2 * tw[k][None, :]) for k in range(n1)])
    fbi = np.stack([blk(np.conj(w2).T * np.conj(tw[k])[:, None] / n) for k in range(n1)])
    as_bf = lambda a: jnp.asarray(a, dtype=F32).astype(BF16)
    return as_bf(fa_data), as_bf(fa_filt), as_bf(fc), as_bf(fb), as_bf(fbi)


def _filter_kernel(z_ref, w1_ref, b1_ref, fr_ref, w2_ref, b2_ref, w3_ref, b3_ref, dec_ref, o_ref, *, seq, tr):
    hp = lax.Precision.HIGHEST
    z = z_ref[...]
    fr = fr_ref[...]
    h = jnp.sin(fr[0:1] * (jnp.dot(z, w1_ref[...], precision=hp, preferred_element_type=F32) + b1_ref[...]))
    h = jnp.sin(fr[1:2] * (jnp.dot(h, w2_ref[...], precision=hp, preferred_element_type=F32) + b2_ref[...]))
    h = jnp.dot(h, w3_ref[0], precision=hp, preferred_element_type=F32) + b3_ref[0]
    h = h * jnp.exp(-z[:, 0:1] * jnp.abs(dec_ref[0]))
    n = pl.program_id(0) * tr + lax.broadcasted_iota(jnp.int32, h.shape, 0)
    o_ref[...] = jnp.where(n == seq, 0.0, h)


def _hyena_filter_time(seq, w1, b1, freq, w2, b2, w3, b3, decay, tr=512):
    n = 2 * seq
    emb, ffn = w1.shape
    C = w3.shape[1] // (HY_ORDER * HY_DIRS)
    off = jnp.arange(n)
    t = jnp.where(off < seq, off, n - off).astype(F32)
    bands = jnp.linspace(1e-4, HY_BANDS - 1, HY_BANDS, dtype=F32)
    ang = 2.0 * math.pi * t[:, None] * bands[None, :] / seq
    z = jnp.concatenate([(t / seq)[:, None], jnp.cos(ang), -jnp.sin(ang)], axis=-1)
    zl = 128
    z = jnp.pad(z, ((0, 0), (0, zl - emb)))
    w1p = jnp.pad(w1, ((0, zl - emb), (0, 0)))
    by_dir = lambda a: jnp.moveaxis(a.reshape(a.shape[0], HY_ORDER, HY_DIRS, C), 2, 0).reshape(
        HY_DIRS, a.shape[0], HY_ORDER * C)
    w3d, b3d, decd = by_dir(w3), by_dir(b3.reshape(1, -1)), by_dir(decay.reshape(1, -1))
    full = lambda a: pl.BlockSpec(a.shape, lambda i: (0,) * a.ndim)
    ndir = lambda a: pl.BlockSpec((1,) + a.shape[1:], lambda i: ((i * tr) // seq, 0, 0))
    consts = [w1p, b1.reshape(1, -1), freq, w2, b2.reshape(1, -1)]
    return pl.pallas_call(
        functools.partial(_filter_kernel, seq=seq, tr=tr),
        grid=(n // tr,),
        in_specs=[pl.BlockSpec((tr, zl), lambda i: (i, 0))] + [full(a) for a in consts]
        + [ndir(w3d), ndir(b3d), ndir(decd)],
        out_specs=pl.BlockSpec((tr, HY_ORDER * C), lambda i: (i, 0)),
        out_shape=jax.ShapeDtypeStruct((n, HY_ORDER * C), F32),
        compiler_params=_cparams("parallel"),
        name="hyena_filter_mlp",
    )(z, *consts, w3d, b3d, decd)


def _outer_dft(mat_ref, x_ref, o_ref, lead_in, lead_out):
    rows_in = x_ref.shape[-4] * x_ref.shape[-3]
    c = x_ref.shape[-1]
    for n in range(x_ref.shape[-2]):
        x = x_ref[(*lead_in, slice(None), slice(None), n, slice(None))].reshape(rows_in, c)
        a = _dot(mat_ref[...], x.astype(BF16))
        o_ref[(*lead_out, slice(None), slice(None), n, slice(None))] = a.reshape(
            o_ref.shape[-4], o_ref.shape[-3], c)


def _filter_stage_a_kernel(x_ref, fa_ref, o_ref):
    _outer_dft(fa_ref, x_ref, o_ref, (), ())


def _filter_stage_b_kernel(x_ref, fb_ref, o_ref):
    x = jnp.concatenate([x_ref[0, 0], x_ref[1, 0]], axis=0).astype(BF16)
    o_ref[0] = _dot(fb_ref[0], x)


def _hyena_filter_spectrum(filt, fa_filt, fb, nb=8):
    n, oc = filt.shape
    a = pl.pallas_call(
        _filter_stage_a_kernel,
        grid=(FFT_N2 // nb,),
        in_specs=[pl.BlockSpec((1, FFT_N1, nb, oc), lambda j: (0, 0, j, 0)),
                  pl.BlockSpec(fa_filt.shape, lambda j: (0, 0))],
        out_specs=pl.BlockSpec((2, FFT_N1, nb, oc), lambda j: (0, 0, j, 0)),
        out_shape=jax.ShapeDtypeStruct((2, FFT_N1, FFT_N2, oc), F32),
        compiler_params=_cparams("parallel"),
        name="hyena_filter_dft_a",
    )(filt.reshape(1, FFT_N1, FFT_N2, oc), fa_filt)
    return pl.pallas_call(
        _filter_stage_b_kernel,
        grid=(FFT_N1,),
        in_specs=[pl.BlockSpec((2, 1, FFT_N2, oc), lambda k: (0, k, 0, 0)),
                  pl.BlockSpec((1, 2 * FFT_N2, 2 * FFT_N2), lambda k: (k, 0, 0))],
        out_specs=pl.BlockSpec((1, 2 * FFT_N2, oc), lambda k: (k, 0, 0)),
        out_shape=jax.ShapeDtypeStruct((FFT_N1, 2 * FFT_N2, oc), F32),
        compiler_params=_cparams("parallel"),
        name="hyena_filter_dft_b",
    )(a, fb)


def _stage_a_kernel(x_ref, fa_ref, o_ref):
    _outer_dft(fa_ref, x_ref, o_ref, (), (0,))


def _stage_a(x4, fa, nb):
    B, r, n2, C = x4.shape
    return pl.pallas_call(
        _stage_a_kernel,
        grid=(B // 2, n2 // nb),
        in_specs=[pl.BlockSpec((2, r, nb, C), lambda p, j: (p, 0, j, 0)),
                  pl.BlockSpec(fa.shape, lambda p, j: (0, 0))],
        out_specs=pl.BlockSpec((1, 2, FFT_N1, nb, C), lambda p, j: (p, 0, 0, j, 0)),
        out_shape=jax.ShapeDtypeStruct((B // 2, 2, FFT_N1, n2, C), F32),
        compiler_params=_cparams("parallel", "arbitrary"),
        name="hyena_dft_a",
    )(x4, fa)


def _stage_b_kernel(x_ref, fb_ref, kf_ref, fbi_ref, o_ref):
    npair = x_ref.shape[0]
    n2 = x_ref.shape[3]
    x = jnp.concatenate(
        [jnp.concatenate([x_ref[p, 0, 0], x_ref[p, 1, 0]], axis=0) for p in range(npair)],
        axis=1).astype(BF16)
    g = _dot(fb_ref[0], x)
    gr, gi = g[:n2], g[n2:]
    kf = kf_ref[0]
    kr = jnp.concatenate([kf[:n2]] * npair, axis=1)
    ki = jnp.concatenate([kf[n2:]] * npair, axis=1)
    hcat = jnp.concatenate([gr * kr - gi * ki, gr * ki + gi * kr], axis=0).astype(BF16)
    y = _dot(fbi_ref[0], hcat)
    c = x_ref.shape[4]
    for p in range(npair):
        o_ref[p, 0, 0] = y[:n2, p * c:(p + 1) * c].astype(o_ref.dtype)
        o_ref[p, 1, 0] = y[n2:, p * c:(p + 1) * c].astype(o_ref.dtype)


def _stage_b(spec, fb, kf, fbi, order):
    npair, _, _, _, C = spec.shape
    blk = pl.BlockSpec((npair, 2, 1, FFT_N2, C), lambda k: (0, 0, k, 0, 0))
    mat = pl.BlockSpec((1, 2 * FFT_N2, 2 * FFT_N2), lambda k: (k, 0, 0))
    return pl.pallas_call(
        _stage_b_kernel,
        grid=(FFT_N1,),
        in_specs=[blk, mat, pl.BlockSpec((1, 2 * FFT_N2, C), lambda k: (k, 0, order)), mat],
        out_specs=blk,
        out_shape=jax.ShapeDtypeStruct(spec.shape, F32),
        compiler_params=_cparams("parallel"),
        name="hyena_dft_b",
    )(spec, fb, kf, fbi)


def _stage_c_kernel(y_ref, fc_ref, gate_ref, z_ref, skip_ref, *rest, stage_a):
    nbat, r, nb, c = gate_ref.shape
    if stage_a:
        fa_ref, z_out, a_out = rest
    else:
        (z_out,) = rest
    skip = skip_ref[...]
    for n in range(nb):
        y = y_ref[0, :, :, n, :].reshape(2 * FFT_N1, c).astype(BF16)
        conv = _dot(fc_ref[...], y)
        zin = z_ref[:, :, n, :].reshape(nbat * r, c)
        z = gate_ref[:, :, n, :].reshape(nbat * r, c) * (conv + skip * zin)
        z_out[:, :, n, :] = z.reshape(nbat, r, c)
        if stage_a:
            a_out[0, :, :, n, :] = _dot(fa_ref[...], z.astype(BF16)).reshape(2, FFT_N1, c)


def _stage_c(yspec, fc, gate, zin, skip, fa=None, nb=16):
    B, r, n2, C = gate.shape
    dat = pl.BlockSpec((2, r, nb, C), lambda p, j: (p, 0, j, 0))
    spc = pl.BlockSpec((1, 2, FFT_N1, nb, C), lambda p, j: (p, 0, 0, j, 0))
    in_specs = [spc, pl.BlockSpec(fc.shape, lambda p, j: (0, 0)), dat, dat,
                pl.BlockSpec((1, C), lambda p, j: (0, 0))]
    out_specs = [dat]
    out_shape = [jax.ShapeDtypeStruct(gate.shape, F32)]
    args = [yspec, fc, gate, zin, skip.reshape(1, C)]
    if fa is not None:
        in_specs.append(pl.BlockSpec(fa.shape, lambda p, j: (0, 0)))
        out_specs.append(spc)
        out_shape.append(jax.ShapeDtypeStruct(yspec.shape, F32))
        args.append(fa)
    return pl.pallas_call(
        functools.partial(_stage_c_kernel, stage_a=fa is not None),
        grid=(B // 2, n2 // nb),
        in_specs=in_specs, out_specs=out_specs, out_shape=out_shape,
        compiler_params=_cparams("parallel", "arbitrary"),
        name="hyena_dft_c",
    )(*args)


def _hyena(x1, x2, v, skip, kf, consts, nb=16):
    fa_data, _, fc, fb, fbi = consts
    B, S, C = v.shape
    split = lambda a: a.reshape(B, S // FFT_N2, FFT_N2, C)
    a0 = _stage_a(split(v), fa_data, nb)
    y0 = _stage_b(a0, fb, kf, fbi, 0)
    z1, a1 = _stage_c(y0, fc, split(x1), split(v), skip[0], fa=fa_data, nb=nb)
    y1 = _stage_b(a1, fb, kf, fbi, 1)
    (out,) = _stage_c(y1, fc, split(x2), z1, skip[1], nb=nb)
    return out.reshape(B, S, C)


def _memkv_kernel(m_ref, g_ref, w_ref, k_out, v_out):
    hm = _rms(m_ref[0], g_ref[...]).astype(BF16)
    kv = _dot(hm, w_ref[...])
    d = k_out.shape[2]
    k_out[0] = kv[:, :d].astype(BF16)
    v_out[0] = kv[:, d:].astype(BF16)


def _memkv(mem, g, w_mkv):
    B, M, D = mem.shape
    dk = w_mkv.shape[1] // 2
    w = w_mkv.astype(BF16)
    return pl.pallas_call(
        _memkv_kernel,
        grid=(B,),
        in_specs=[pl.BlockSpec((1, M, D), lambda b: (b, 0, 0)),
                  pl.BlockSpec((1, D), lambda b: (0, 0)),
                  pl.BlockSpec(w.shape, lambda b: (0, 0))],
        out_specs=[pl.BlockSpec((1, M, dk), lambda b: (b, 0, 0))] * 2,
        out_shape=[jax.ShapeDtypeStruct((B, M, dk), BF16)] * 2,
        compiler_params=_cparams("parallel"),
        name="mem_kv",
    )(mem, g.reshape(1, D), w)


def _route(logits):
    lane = lax.broadcasted_iota(jnp.int32, logits.shape, 1)
    ninf = -jnp.inf
    big = ROUTE_LANES
    first = lambda mask: jnp.min(jnp.where(mask, lane, big), axis=-1, keepdims=True)
    is_g = (lane >= N_EXPERTS) & (lane < N_EXPERTS + N_GROUPS)
    gl = jnp.where(is_g, logits, ninf)
    gmax = jnp.max(gl, axis=-1, keepdims=True)
    g_idx = first(gl == gmax) - N_EXPERTS
    p_group = 1.0 / jnp.sum(jnp.exp(gl - gmax), axis=-1, keepdims=True)
    in_g = (lane < N_EXPERTS) & ((lane // EXPERTS_PER_GROUP) == g_idx)
    el = jnp.where(in_g, logits, ninf)
    v1 = jnp.max(el, axis=-1, keepdims=True)
    i1 = first(el == v1)
    el2 = jnp.where(lane == i1, ninf, el)
    v2 = jnp.max(el2, axis=-1, keepdims=True)
    i2 = first(el2 == v2)
    e2 = jnp.exp(v2 - v1)
    p1 = 1.0 / (1.0 + e2)
    p2 = e2 / (1.0 + e2)
    sel = lambda n, val: jnp.where(lane == n, val, 0.0)
    return (sel(ROUTE_ID0, i1.astype(F32)) + sel(ROUTE_ID0 + 1, i2.astype(F32))
            + sel(ROUTE_W0, p_group * p1) + sel(ROUTE_W0 + 1, p_group * p2))


def _postmix_kernel(x_ref, a_ref, hy_ref, ag_ref, hg_ref, woa_ref, woh_ref, cg_ref, wmq_ref,
                    mk_ref, mv_ref, wmo_ref, fg_ref, wr_ref, br_ref, x_out, hn_out, route_out):
    ra = _rms(a_ref[...], ag_ref[...]).astype(BF16)
    rh = _rms(hy_ref[...], hg_ref[...]).astype(BF16)
    x = x_ref[...] + _dot(ra, woa_ref[...]) + _dot(rh, woh_ref[...])
    q = _dot(_rms(x, cg_ref[...]).astype(BF16), wmq_ref[...])
    dh = q.shape[1] // MEM_HEADS
    outs = []
    for h in range(MEM_HEADS):
        sl = slice(h * dh, (h + 1) * dh)
        s = lax.dot_general(q[:, sl].astype(BF16), mk_ref[0, :, sl], (((1,), (1,)), ((), ())),
                            preferred_element_type=F32) * dh ** -0.5
        p = jnp.exp(s - jnp.max(s, axis=-1, keepdims=True))
        l = jnp.sum(p, axis=-1, keepdims=True)
        outs.append(_dot(p.astype(BF16), mv_ref[0, :, sl]) / l)
    o = jnp.concatenate(outs, axis=1).astype(BF16)
    x = x + _dot(o, wmo_ref[...])
    x_out[...] = x
    hn = _rms(x, fg_ref[...])
    _store_row_tiles(hn_out, hn)
    route_out[...] = _route(_dot(hn.astype(BF16), wr_ref[...]) + br_ref[...])


def _postmix(x2d, a2d, hy2d, seq, ag, hg, w_out, cg, w_mq, mk, mv, w_mo, fg, w_rg, b_rg, w_re, b_re, tm=512):
    T, D = x2d.shape
    ca = a2d.shape[1]
    woa = w_out[:ca].astype(BF16)
    woh = w_out[ca:].astype(BF16)
    pad = ROUTE_LANES - N_EXPERTS - N_GROUPS
    wr = jnp.concatenate([w_re, w_rg, jnp.zeros((D, pad), F32)], 1).astype(BF16)
    br = jnp.concatenate([b_re, b_rg, jnp.zeros((pad,), F32)]).reshape(1, ROUTE_LANES)
    nseq = seq // tm
    full = lambda a: pl.BlockSpec(a.shape, lambda i: (0,) * a.ndim)
    row = lambda n: pl.BlockSpec((tm, n), lambda i: (i, 0))
    memb = pl.BlockSpec((1,) + mk.shape[1:], lambda i: (i // nseq, 0, 0))
    args = [x2d, a2d, hy2d, ag.reshape(1, -1), hg.reshape(1, -1), woa, woh, cg.reshape(1, D),
            w_mq.astype(BF16), mk, mv, w_mo.astype(BF16), fg.reshape(1, D), wr, br]
    in_specs = [row(D), row(ca), row(hy2d.shape[1])] + [full(a) for a in args[3:9]] + [memb, memb] \
        + [full(a) for a in args[11:]]
    return pl.pallas_call(
        _postmix_kernel,
        grid=(T // tm,),
        in_specs=in_specs,
        out_specs=[row(D), pl.BlockSpec((tm, SUBLANES, LANES), lambda i: (i, 0, 0)), row(ROUTE_LANES)],
        out_shape=[jax.ShapeDtypeStruct((T, D), F32), jax.ShapeDtypeStruct((T, SUBLANES, LANES), F32),
                   jax.ShapeDtypeStruct((T, ROUTE_LANES), F32)],
        compiler_params=_cparams("parallel"),
        name="postmix",
    )(*args)


def _rank_kernel(route_ref, rank_out, cnt_out, carry_ref):
    @pl.when(pl.program_id(0) == 0)
    def _():
        carry_ref[...] = jnp.zeros_like(carry_ref)

    route = route_ref[...]
    tr = route.shape[0]
    lane = lax.broadcasted_iota(jnp.int32, route.shape, 1).astype(F32)
    oh1 = lane == route[:, ROUTE_ID0:ROUTE_ID0 + 1]
    oh2 = lane == route[:, ROUTE_ID0 + 1:ROUTE_ID0 + 2]
    cnt = jnp.where(oh1 | oh2, 1.0, 0.0)
    r = lax.broadcasted_iota(jnp.int32, (tr, tr), 0)
    c = lax.broadcasted_iota(jnp.int32, (tr, tr), 1)
    below = jnp.where(c < r, 1.0, 0.0).astype(BF16)
    cum = _dot(below, cnt.astype(BF16)) + carry_ref[...]
    r1 = jnp.sum(jnp.where(oh1, cum, 0.0), axis=-1, keepdims=True)
    r2 = jnp.sum(jnp.where(oh2, cum, 0.0), axis=-1, keepdims=True)
    rank_out[...] = jnp.where(lane == 0, r1, 0.0) + jnp.where(lane == 1, r2, 0.0)
    carry_ref[...] += jnp.sum(cnt, axis=0, keepdims=True)
    cnt_out[...] = carry_ref[...]


def _pos_kernel(route_ref, rank_ref, base_ref, pos_out):
    route = route_ref[...]
    rank = rank_ref[...]
    lane = lax.broadcasted_iota(jnp.int32, route.shape, 1).astype(F32)
    base = base_ref[...]
    pick = lambda k: jnp.sum(jnp.where(lane == route[:, ROUTE_ID0 + k:ROUTE_ID0 + k + 1], base, 0.0),
                             axis=-1, keepdims=True) + rank[:, k:k + 1]
    pos_out[...] = (jnp.where(lane == 0, pick(0), 0.0) + jnp.where(lane == 1, pick(1), 0.0)).astype(jnp.int32)


def _dispatch_kernel(base_ref, cp_ref, pos_ref, hn_ref, xs_hbm, zero_ref, sem, *, tt, tmm):
    i = pl.program_id(0)
    row_copy = lambda src, dst: pltpu.make_async_copy(
        hn_ref.at[pl.ds(src, 1)], xs_hbm.at[pl.ds(dst, 1)], sem)

    @pl.when(i == 0)
    def _():
        zero_ref[...] = jnp.zeros_like(zero_ref)
        pad_copy = lambda e: pltpu.make_async_copy(
            zero_ref, xs_hbm.at[pl.ds(base_ref[e] + cp_ref[e] - tmm, tmm)], sem)
        for e in range(N_EXPERTS):
            @pl.when(cp_ref[e] > 0)
            def _():
                pad_copy(e).start()
        for e in range(N_EXPERTS):
            @pl.when(cp_ref[e] > 0)
            def _():
                pad_copy(e).wait()
        last = N_EXPERTS - 1
        tail_copy = lambda r: pltpu.make_async_copy(zero_ref, xs_hbm.at[pl.ds(r * tmm, tmm)], sem)
        first_free = (base_ref[last] + cp_ref[last]) // tmm
        n_tiles = xs_hbm.shape[0] // tmm
        lax.fori_loop(first_free, n_tiles, lambda r, c: (tail_copy(r).start(), c)[1], 0)
        lax.fori_loop(first_free, n_tiles, lambda r, c: (tail_copy(r).wait(), c)[1], 0)

    def start(t, carry):
        row_copy(t, pos_ref[2 * t]).start(priority=0)
        row_copy(t, pos_ref[2 * t + 1]).start(priority=1)
        return carry

    lax.fori_loop(0, tt, start, 0, unroll=DMA_UNROLL)

    def wait(t, carry):
        row_copy(0, 0).wait()
        row_copy(0, 0).wait()
        return carry

    lax.fori_loop(0, tt, wait, 0, unroll=DMA_UNROLL)


def _ffn_kernel(te_ref, nu_ref, xs_ref, wg_ref, wu_ref, wd_ref, ys_ref):
    used = pl.program_id(0) < nu_ref[0]

    @pl.when(used)
    def _():
        x = _load_row_tiles(xs_ref).astype(BF16)
        a = _dot(x, wg_ref[0].astype(BF16))
        b = _dot(x, wu_ref[0].astype(BF16))
        m = (a * jax.nn.sigmoid(a)) * b
        _store_row_tiles(ys_ref, _dot(m.astype(BF16), wd_ref[0].astype(BF16)))

    @pl.when(jnp.logical_not(used))
    def _():
        ys_ref[...] = jnp.zeros_like(ys_ref)


def _combine_kernel(pos_ref, pos_next_ref, ys_hbm, x_ref, route_ref, fg_ref, o_ref, buf_ref, sem, *, tc):
    i = pl.program_id(0)
    slot = i % 2
    row_copy = lambda s, k, t, p: pltpu.make_async_copy(
        ys_hbm.at[pl.ds(p, 1)], buf_ref.at[s, k, pl.ds(t, 1)], sem.at[s])

    def fetch(p_ref, s):
        def start(t, carry):
            row_copy(s, 0, t, p_ref[2 * t]).start(priority=0)
            row_copy(s, 1, t, p_ref[2 * t + 1]).start(priority=1)
            return carry

        lax.fori_loop(0, tc, start, 0, unroll=DMA_UNROLL)

    @pl.when(i == 0)
    def _():
        fetch(pos_ref, 0)

    @pl.when(i + 1 < pl.num_programs(0))
    def _():
        fetch(pos_next_ref, 1 - slot)

    def wait(t, carry):
        row_copy(slot, 0, 0, 0).wait()
        row_copy(slot, 1, 0, 0).wait()
        return carry

    lax.fori_loop(0, tc, wait, 0, unroll=DMA_UNROLL)
    route = route_ref[...]
    y = (x_ref[...] + route[:, ROUTE_W0:ROUTE_W0 + 1] * _load_row_tiles(buf_ref.at[slot, 0])
         + route[:, ROUTE_W0 + 1:ROUTE_W0 + 2] * _load_row_tiles(buf_ref.at[slot, 1]))
    o_ref[...] = _rms(y, fg_ref[...])


def _moe(hn, route, x2d, w_gate, w_up, w_down, fg, tr=512, tt=512, tc=256):
    T, D = x2d.shape
    E = N_EXPERTS
    F = w_gate.shape[-1]
    tmm = MOE_ROW_TILE
    row = lambda tm, n: pl.BlockSpec((tm, n), lambda i: (i, 0))
    rank, counts = pl.pallas_call(
        _rank_kernel,
        grid=(T // tr,),
        in_specs=[row(tr, ROUTE_LANES)],
        out_specs=[row(tr, ROUTE_LANES), pl.BlockSpec((1, ROUTE_LANES), lambda i: (0, 0))],
        out_shape=[jax.ShapeDtypeStruct((T, ROUTE_LANES), F32), jax.ShapeDtypeStruct((1, ROUTE_LANES), F32)],
        scratch_shapes=[pltpu.VMEM((1, ROUTE_LANES), F32)],
        compiler_params=_cparams("arbitrary"),
        name="moe_rank",
    )(route)

    cnt = counts[0, :E].astype(jnp.int32)
    cp = ((cnt + tmm - 1) // tmm) * tmm
    ends = jnp.cumsum(cp)
    base = ends - cp
    n_used = ends[-1] // tmm
    n_tiles = (2 * T) // tmm + E
    tile_start = jnp.minimum(jnp.arange(n_tiles, dtype=jnp.int32), n_used - 1) * tmm
    tile_expert = jnp.minimum(jnp.sum((tile_start[:, None] >= ends[None, :]).astype(jnp.int32), axis=1), E - 1)
    base_lanes = jnp.zeros((1, ROUTE_LANES), F32).at[0, :E].set(base.astype(F32))

    pos = pl.pallas_call(
        _pos_kernel,
        grid=(T // tr,),
        in_specs=[row(tr, ROUTE_LANES), row(tr, ROUTE_LANES), pl.BlockSpec((1, ROUTE_LANES), lambda i: (0, 0))],
        out_specs=row(tr, ROUTE_LANES),
        out_shape=jax.ShapeDtypeStruct((T, ROUTE_LANES), jnp.int32),
        compiler_params=_cparams("parallel"),
        name="moe_pos",
    )(route, rank, base_lanes)
    pos = pos[:, :2].reshape(2 * T)

    xs = pl.pallas_call(
        functools.partial(_dispatch_kernel, tt=tt, tmm=tmm),
        grid_spec=pltpu.PrefetchScalarGridSpec(
            num_scalar_prefetch=2,
            grid=(T // tt,),
            in_specs=[pl.BlockSpec((2 * tt,), lambda i, b, c: (i,), memory_space=pltpu.SMEM),
                      pl.BlockSpec((tt, SUBLANES, LANES), lambda i, b, c: (i, 0, 0))],
            out_specs=pl.BlockSpec(memory_space=pl.ANY),
            scratch_shapes=[pltpu.VMEM((tmm, SUBLANES, LANES), F32), pltpu.SemaphoreType.DMA(())]),
        out_shape=jax.ShapeDtypeStruct((n_tiles * tmm, SUBLANES, LANES), F32),
        compiler_params=_cparams("arbitrary"),
        name="moe_dispatch",
    )(base, cp, pos, hn)

    tile = lambda r, te, nu: (jnp.minimum(r, nu[0] - 1), 0, 0)
    ys = pl.pallas_call(
        _ffn_kernel,
        grid_spec=pltpu.PrefetchScalarGridSpec(
            num_scalar_prefetch=2,
            grid=(n_tiles,),
            in_specs=[pl.BlockSpec((tmm, SUBLANES, LANES), tile),
                      pl.BlockSpec((1, D, F), lambda r, te, nu: (te[r], 0, 0)),
                      pl.BlockSpec((1, D, F), lambda r, te, nu: (te[r], 0, 0)),
                      pl.BlockSpec((1, F, D), lambda r, te, nu: (te[r], 0, 0))],
            out_specs=pl.BlockSpec((tmm, SUBLANES, LANES), lambda r, te, nu: (r, 0, 0))),
        out_shape=jax.ShapeDtypeStruct((n_tiles * tmm, SUBLANES, LANES), F32),
        compiler_params=_cparams("arbitrary"),
        name="moe_ffn",
    )(tile_expert, n_used.reshape(1), xs, w_gate.reshape(E, D, F), w_up.reshape(E, D, F),
      w_down.reshape(E, F, D))

    return pl.pallas_call(
        functools.partial(_combine_kernel, tc=tc),
        grid=(T // tc,),
        in_specs=[pl.BlockSpec((2 * tc,), lambda i: (i,), memory_space=pltpu.SMEM),
                  pl.BlockSpec((2 * tc,), lambda i: (jnp.minimum(i + 1, T // tc - 1),),
                               memory_space=pltpu.SMEM),
                  pl.BlockSpec(memory_space=pl.ANY),
                  row(tc, D), row(tc, ROUTE_LANES), pl.BlockSpec((1, D), lambda i: (0, 0))],
        out_specs=row(tc, D),
        out_shape=jax.ShapeDtypeStruct((T, D), F32),
        scratch_shapes=[pltpu.VMEM((2, 2, tc, SUBLANES, LANES), F32), pltpu.SemaphoreType.DMA((2,))],
        compiler_params=_cparams("arbitrary"),
        name="moe_combine",
    )(pos, pos, ys, x2d, route, fg.reshape(1, D))


def kernel(x, mem, mix_norm_g, w_in, q_norm_g, kv_norm_g, w_uq, w_ukv, hy_conv_w, hy_conv_b, hy_w1, hy_b1, hy_freq, hy_w2, hy_b2, hy_w3, hy_b3, hy_decay, hy_skip, attn_out_g, hy_out_g, w_out, cross_norm_g, mem_norm_g, w_mq, w_mkv, w_mo, ffn_norm_g, w_route_group, b_route_group, w_route_expert, b_route_expert, w_gate, w_up, w_down, final_norm_g):
    B, S, D = x.shape
    depth = w_in.shape[0]
    consts = _dft_constants(S)
    xf = x.reshape(B * S, D)
    for l in range(depth):
        q, k, v, hx1, hx2, hv = _inproj(xf, S, mix_norm_g[l], w_in[l], q_norm_g[l], kv_norm_g[l],
                                        w_uq[l], w_ukv[l], hy_conv_w[l], hy_conv_b[l])
        HP = q.shape[1]
        a_out = _attention(q.reshape(B, S, HP), k.reshape(B, S, HP), v.reshape(B, S, HP))
        filt = _hyena_filter_time(S, hy_w1[l], hy_b1[l], hy_freq[l], hy_w2[l], hy_b2[l], hy_w3[l],
                                  hy_b3[l], hy_decay[l])
        kf = _hyena_filter_spectrum(filt, consts[1], consts[3])
        C = hv.shape[1]
        h_out = _hyena(hx1.reshape(B, S, C), hx2.reshape(B, S, C), hv.reshape(B, S, C),
                       hy_skip[l], kf, consts)
        mk, mv = _memkv(mem, mem_norm_g[l], w_mkv[l])
        x2, hn, route = _postmix(xf, a_out.reshape(B * S, -1), h_out.reshape(B * S, C), S,
                                attn_out_g[l], hy_out_g[l], w_out[l], cross_norm_g[l], w_mq[l], mk, mv,
                                w_mo[l], ffn_norm_g[l], w_route_group[l], b_route_group[l],
                                w_route_expert[l], b_route_expert[l])
        assert depth == 1
        xf = _moe(hn, route, x2, w_gate[l], w_up[l], w_down[l], final_norm_g)
    return xf.reshape(B, S, D)
```

```python
import functools
import math

import numpy as np
import jax
import jax.numpy as jnp
from jax import lax
from jax.experimental import pallas as pl
from jax.experimental.pallas import tpu as pltpu

F32 = jnp.float32
BF16 = jnp.bfloat16

EPS = 1e-6
MLA_HEADS = 8
MLA_NOPE = 64
MLA_ROPE = 32
MLA_V = 64
ROPE_BASE = 10000.0
HEAD_PAD = 128
HY_ORDER = 2
HY_DIRS = 2
HY_BANDS = 16
MEM_HEADS = 4
N_GROUPS = 4
EXPERTS_PER_GROUP = 8
N_EXPERTS = N_GROUPS * EXPERTS_PER_GROUP
ROUTE_LANES = 128
ROUTE_ID0 = 0
ROUTE_W0 = 2
MOE_ROW_TILE = 256
DMA_UNROLL = 8

FFT_N1 = 64
FFT_N2 = 128
DFT_N2_BLOCK = 8

VMEM_LIMIT = 56 * 1024 * 1024


def _cparams(*sem):
    return pltpu.CompilerParams(dimension_semantics=sem, vmem_limit_bytes=VMEM_LIMIT)


def _rms(x, g):
    return x * lax.rsqrt(jnp.mean(x * x, axis=-1, keepdims=True) + EPS) * g


def _dot(a, b):
    return jnp.dot(a, b, preferred_element_type=F32)


SUBLANES = 8
LANES = 128


def _load_row_tiles(ref, lead=()):
    rows = ref.shape[-3]
    flat = ref.reshape(*ref.shape[:-3], rows * SUBLANES, LANES)
    return jnp.concatenate(
        [flat[(*lead, pl.ds(j, rows, stride=SUBLANES), slice(None))] for j in range(SUBLANES)], axis=1)


def _store_row_tiles(ref, val):
    rows = ref.shape[0]
    flat = ref.reshape(rows * SUBLANES, LANES)
    for j in range(SUBLANES):
        flat[pl.ds(j, rows, stride=SUBLANES), :] = val[:, j * LANES:(j + 1) * LANES]


def _inproj_kernel(x_ref, xp_ref, xn_ref, g_ref, wq_ref, wkv_ref, wkra_ref, wkrb_ref, why_ref, qg_ref,
                   kvg_ref, wqa_ref, wqb_ref, wka_ref, wv_ref, tab_ref, cw_ref, cb_ref,
                   q_out, k_out, v_out, x1_out, x2_out, hv_out, *, nseq):
    tm = x_ref.shape[0]
    halo = xp_ref.shape[0]
    hf = _rms(jnp.concatenate([xp_ref[...], x_ref[...], xn_ref[...]], axis=0), g_ref[...])
    h = hf[halo:halo + tm].astype(BF16)
    qn = _rms(_dot(h, wq_ref[...]), qg_ref[...]).astype(BF16)
    kvn = _rms(_dot(h, wkv_ref[...]), kvg_ref[...]).astype(BF16)
    tab = tab_ref[...]
    cq, sq, ck, sk = (tab[:, i * HEAD_PAD:(i + 1) * HEAD_PAD] for i in range(4))
    tile = lambda t: jnp.concatenate([t] * MLA_HEADS, axis=1)
    q = _dot(qn, wqa_ref[...]) * tile(cq) + _dot(qn, wqb_ref[...]) * tile(sq)
    q_out[...] = q.astype(BF16)
    kr = _dot(h, wkra_ref[...]) * ck + _dot(h, wkrb_ref[...]) * sk
    k_out[...] = (_dot(kvn, wka_ref[...]) + tile(kr)).astype(BF16)
    lane = lax.broadcasted_iota(jnp.int32, (1, v_out.shape[1]), 1) % HEAD_PAD
    v_out[...] = (_dot(kvn, wv_ref[...]) + jnp.where(lane == MLA_V, 1.0, 0.0)).astype(BF16)
    hy = _dot(hf.astype(BF16), why_ref[...])
    i = pl.program_id(0) % nseq
    row = lax.broadcasted_iota(jnp.int32, hy.shape, 0)
    outside = ((row == halo - 1) & (i == 0)) | ((row == halo + tm) & (i == nseq - 1))
    hy = jnp.where(outside, 0.0, hy)
    cw = cw_ref[...]
    u = (hy[halo - 1:halo - 1 + tm] * cw[0:1] + hy[halo:halo + tm] * cw[1:2]
         + hy[halo + 1:halo + 1 + tm] * cw[2:3] + cb_ref[...])
    c = x1_out.shape[1]
    x1_out[...] = u[:, :c]
    x2_out[...] = u[:, c:2 * c]
    hv_out[...] = u[:, 2 * c:]


def _inproj(x2d, seq, mix_g, w_in, q_g, kv_g, w_uq, w_ukv, conv_w, conv_b, tm=512):
    T, D = x2d.shape
    per = tm // SUBLANES
    cb = conv_b.reshape(1, -1)
    q_rank, kv_rank = q_g.shape[0], kv_g.shape[0]
    off_kv = q_rank
    off_kr = off_kv + kv_rank
    off_hy = off_kr + MLA_ROPE
    C = (w_in.shape[1] - off_hy) // 3
    H = MLA_HEADS
    half = MLA_ROPE // 2
    wq = w_in[:, :off_kv].astype(BF16)
    wkv = w_in[:, off_kv:off_kr].astype(BF16)
    wkr = w_in[:, off_kr:off_hy]
    wkr_sw = jnp.concatenate([wkr[:, half:], wkr[:, :half]], axis=1)
    zpad = lambda n: jnp.zeros((D, n), F32)
    wkra = jnp.concatenate([zpad(MLA_NOPE), wkr, zpad(HEAD_PAD - MLA_NOPE - MLA_ROPE)], 1).astype(BF16)
    wkrb = jnp.concatenate([zpad(MLA_NOPE), wkr_sw, zpad(HEAD_PAD - MLA_NOPE - MLA_ROPE)], 1).astype(BF16)
    why = w_in[:, off_hy:].astype(BF16)

    uq = w_uq.reshape(q_rank, H, MLA_NOPE + MLA_ROPE)
    uq_n, uq_r = uq[..., :MLA_NOPE], uq[..., MLA_NOPE:]
    uq_rs = jnp.concatenate([uq_r[..., half:], uq_r[..., :half]], axis=-1)
    zq = lambda n: jnp.zeros((q_rank, H, n), F32)
    wqa = jnp.concatenate([uq_n, uq_r, zq(HEAD_PAD - MLA_NOPE - MLA_ROPE)], -1).reshape(q_rank, H * HEAD_PAD).astype(BF16)
    wqb = jnp.concatenate([zq(MLA_NOPE), uq_rs, zq(HEAD_PAD - MLA_NOPE - MLA_ROPE)], -1).reshape(q_rank, H * HEAD_PAD).astype(BF16)
    ukv = w_ukv.reshape(kv_rank, H, MLA_NOPE + MLA_V)
    zk = lambda n: jnp.zeros((kv_rank, H, n), F32)
    wka = jnp.concatenate([ukv[..., :MLA_NOPE], zk(HEAD_PAD - MLA_NOPE)], -1).reshape(kv_rank, H * HEAD_PAD).astype(BF16)
    wv = jnp.concatenate([ukv[..., MLA_NOPE:], zk(HEAD_PAD - MLA_V)], -1).reshape(kv_rank, H * HEAD_PAD).astype(BF16)

    pos = jnp.arange(seq, dtype=F32)
    inv = ROPE_BASE ** (-jnp.arange(half, dtype=F32) / half)
    ang = pos[:, None] * inv[None, :]
    cos2 = jnp.concatenate([jnp.cos(ang), jnp.cos(ang)], 1)
    sin2 = jnp.concatenate([-jnp.sin(ang), jnp.sin(ang)], 1)
    zs = lambda n: jnp.zeros((seq, n), F32)
    scale = (MLA_NOPE + MLA_ROPE) ** -0.5 * math.log2(math.e)
    rest = HEAD_PAD - MLA_NOPE - MLA_ROPE
    cq = scale * jnp.concatenate([jnp.ones((seq, MLA_NOPE), F32), cos2, zs(rest)], 1)
    sq = scale * jnp.concatenate([zs(MLA_NOPE), sin2, zs(rest)], 1)
    ck = jnp.concatenate([zs(MLA_NOPE), cos2, zs(rest)], 1)
    sk = jnp.concatenate([zs(MLA_NOPE), sin2, zs(rest)], 1)
    tab = jnp.concatenate([cq, sq, ck, sk], 1)

    nseq = seq // tm
    full = lambda a: pl.BlockSpec(a.shape, lambda i: (0,) * a.ndim)
    row = lambda n: pl.BlockSpec((tm, n), lambda i: (i, 0))
    consts = [mix_g.reshape(1, D), wq, wkv, wkra, wkrb, why, q_g.reshape(1, -1), kv_g.reshape(1, -1),
              wqa, wqb, wka, wv]
    HP = H * HEAD_PAD
    return pl.pallas_call(
        functools.partial(_inproj_kernel, nseq=nseq),
        grid=(T // tm,),
        in_specs=[row(D),
                  pl.BlockSpec((SUBLANES, D), lambda i: (jnp.maximum(i * per - 1, 0), 0)),
                  pl.BlockSpec((SUBLANES, D), lambda i: (jnp.minimum((i + 1) * per, T // SUBLANES - 1), 0))]
        + [full(a) for a in consts]
        + [pl.BlockSpec((tm, 4 * HEAD_PAD), lambda i: (i % nseq, 0)), full(conv_w), full(cb)],
        out_specs=[row(HP), row(HP), row(HP), row(C), row(C), row(C)],
        out_shape=[jax.ShapeDtypeStruct((T, HP), BF16)] * 3 + [jax.ShapeDtypeStruct((T, C), F32)] * 3,
        compiler_params=_cparams("parallel"),
        name="inproj",
    )(x2d, x2d, x2d, *consts, tab, conv_w, cb)


def _attn_kernel(q_ref, k_ref, v_ref, o_ref):
    outs = []
    for h in range(MLA_HEADS):
        sl = slice(h * HEAD_PAD, (h + 1) * HEAD_PAD)
        s = lax.dot_general(q_ref[0, :, sl], k_ref[0, :, sl], (((1,), (1,)), ((), ())),
                            preferred_element_type=F32).astype(BF16)
        p = jnp.exp2(s - jnp.max(s, axis=-1, keepdims=True))
        o = _dot(p, v_ref[0, :, sl])
        outs.append(o[:, :MLA_V] / o[:, MLA_V:MLA_V + 1])
    o_ref[0] = jnp.concatenate(outs, axis=1)


def _attention(q, k, v, tq=256):
    B, S, HP = q.shape
    return pl.pallas_call(
        _attn_kernel,
        grid=(B, S // tq),
        in_specs=[pl.BlockSpec((1, tq, HP), lambda b, i: (b, i, 0)),
                  pl.BlockSpec((1, S, HP), lambda b, i: (b, 0, 0)),
                  pl.BlockSpec((1, S, HP), lambda b, i: (b, 0, 0))],
        out_specs=pl.BlockSpec((1, tq, MLA_HEADS * MLA_V), lambda b, i: (b, i, 0)),
        out_shape=jax.ShapeDtypeStruct((B, S, MLA_HEADS * MLA_V), F32),
        compiler_params=_cparams("parallel", "arbitrary"),
        name="mla_attention",
    )(q, k, v)


def _dft_constants(seq):
    n = 2 * seq
    n1, n2 = FFT_N1, FFT_N2
    assert n1 * n2 == n
    r1 = np.arange(n1)
    r2 = np.arange(n2)
    blk = lambda z: np.block([[z.real, -z.imag], [z.imag, z.real]])
    w1 = np.exp(-2j * np.pi * np.outer(r1, r1) / n1)
    fa_data = blk(w1[:, :n1 // 2])
    fa_filt = np.concatenate([w1.real, w1.imag], axis=0)
    fc = blk(np.conj(w1).T[:n1 // 2, :])
    w2 = np.exp(-2j * np.pi * np.outer(r2, r2) / n2)
    tw = np.exp(-2j * np.pi * np.outer(r1, r2) / n)
    fb = np.stack([blk(w2 * tw[k][None, :]) for k in range(n1)])
    fbi = np.stack([blk(np.conj(w2).T * np.conj(tw[k])[:, None] / n) for k in range(n1)])
    as_bf = lambda a: jnp.asarray(a, dtype=F32).astype(BF16)
    return as_bf(fa_data), as_bf(fa_filt), as_bf(fc), as_bf(fb), as_bf(fbi)


def _filter_kernel(z_ref, w1_ref, b1_ref, fr_ref, w2_ref, b2_ref, w3_ref, b3_ref, dec_ref, o_ref, *, seq, tr):
    hp = lax.Precision.HIGHEST
    z = z_ref[...]
    fr = fr_ref[...]
    h = jnp.sin(fr[0:1] * (jnp.dot(z, w1_ref[...], precision=hp, preferred_element_type=F32) + b1_ref[...]))
    h = jnp.sin(fr[1:2] * (jnp.dot(h, w2_ref[...], precision=hp, preferred_element_type=F32) + b2_ref[...]))
    h = jnp.dot(h, w3_ref[0], precision=hp, preferred_element_type=F32) + b3_ref[0]
    h = h * jnp.exp(-z[:, 0:1] * jnp.abs(dec_ref[0]))
    n = pl.program_id(0) * tr + lax.broadcasted_iota(jnp.int32, h.shape, 0)
    o_ref[...] = jnp.where(n == seq, 0.0, h)


def _hyena_filter_time(seq, w1, b1, freq, w2, b2, w3, b3, decay, tr=512):
    n = 2 * seq
    emb, ffn = w1.shape
    C = w3.shape[1] // (HY_ORDER * HY_DIRS)
    off = jnp.arange(n)
    t = jnp.where(off < seq, off, n - off).astype(F32)
    bands = jnp.linspace(1e-4, HY_BANDS - 1, HY_BANDS, dtype=F32)
    ang = 2.0 * math.pi * t[:, None] * bands[None, :] / seq
    z = jnp.concatenate([(t / seq)[:, None], jnp.cos(ang), -jnp.sin(ang)], axis=-1)
    zl = 128
    z = jnp.pad(z, ((0, 0), (0, zl - emb)))
    w1p = jnp.pad(w1, ((0, zl - emb), (0, 0)))
    by_dir = lambda a: jnp.moveaxis(a.reshape(a.shape[0], HY_ORDER, HY_DIRS, C), 2, 0).reshape(
        HY_DIRS, a.shape[0], HY_ORDER * C)
    w3d, b3d, decd = by_dir(w3), by_dir(b3.reshape(1, -1)), by_dir(decay.reshape(1, -1))
    full = lambda a: pl.BlockSpec(a.shape, lambda i: (0,) * a.ndim)
    ndir = lambda a: pl.BlockSpec((1,) + a.shape[1:], lambda i: ((i * tr) // seq, 0, 0))
    consts = [w1p, b1.reshape(1, -1), freq, w2, b2.reshape(1, -1)]
    return pl.pallas_call(
        functools.partial(_filter_kernel, seq=seq, tr=tr),
        grid=(n // tr,),
        in_specs=[pl.BlockSpec((tr, zl), lambda i: (i, 0))] + [full(a) for a in consts]
        + [ndir(w3d), ndir(b3d), ndir(decd)],
        out_specs=pl.BlockSpec((tr, HY_ORDER * C), lambda i: (i, 0)),
        out_shape=jax.ShapeDtypeStruct((n, HY_ORDER * C), F32),
        compiler_params=_cparams("parallel"),
        name="hyena_filter_mlp",
    )(z, *consts, w3d, b3d, decd)


def _n2_rows(ref):
    nb = ref.shape[-2]
    rows = math.prod(ref.shape[:-2])
    flat = ref.reshape(rows * nb, ref.shape[-1])
    return flat, rows, nb


def _at_n2(flat, rows, nb, n):
    return flat.at[pl.ds(n, rows, stride=nb), :]


def _outer_dft(mat_ref, x_ref, o_ref):
    xf, rows_in, nb = _n2_rows(x_ref)
    of, rows_out, _ = _n2_rows(o_ref)
    for n in range(nb):
        x = _at_n2(xf, rows_in, nb, n)[...]
        _at_n2(of, rows_out, nb, n)[...] = _dot(mat_ref[...], x.astype(BF16))


def _filter_stage_a_kernel(x_ref, fa_ref, o_ref):
    _outer_dft(fa_ref, x_ref, o_ref)


def _filter_stage_b_kernel(x_ref, fb_ref, o_ref):
    x = jnp.concatenate([x_ref[0, 0], x_ref[1, 0]], axis=0).astype(BF16)
    o_ref[0] = _dot(fb_ref[0], x)


def _hyena_filter_spectrum(filt, fa_filt, fb, nb=DFT_N2_BLOCK):
    n, oc = filt.shape
    a = pl.pallas_call(
        _filter_stage_a_kernel,
        grid=(FFT_N2 // nb, oc // LANES),
        in_specs=[pl.BlockSpec((FFT_N1, nb, LANES), lambda j, c: (0, j, c)),
                  pl.BlockSpec(fa_filt.shape, lambda j, c: (0, 0))],
        out_specs=pl.BlockSpec((2, FFT_N1, nb, LANES), lambda j, c: (0, 0, j, c)),
        out_shape=jax.ShapeDtypeStruct((2, FFT_N1, FFT_N2, oc), F32),
        compiler_params=_cparams("parallel", "parallel"),
        name="hyena_filter_dft_a",
    )(filt.reshape(FFT_N1, FFT_N2, oc), fa_filt)
    return pl.pallas_call(
        _filter_stage_b_kernel,
        grid=(FFT_N1,),
        in_specs=[pl.BlockSpec((2, 1, FFT_N2, oc), lambda k: (0, k, 0, 0)),
                  pl.BlockSpec((1, 2 * FFT_N2, 2 * FFT_N2), lambda k: (k, 0, 0))],
        out_specs=pl.BlockSpec((1, 2 * FFT_N2, oc), lambda k: (k, 0, 0)),
        out_shape=jax.ShapeDtypeStruct((FFT_N1, 2 * FFT_N2, oc), F32),
        compiler_params=_cparams("parallel"),
        name="hyena_filter_dft_b",
    )(a, fb)


def _stage_a_kernel(x_ref, fa_ref, o_ref):
    _outer_dft(fa_ref, x_ref, o_ref)


def _stage_a(x4, fa, nb):
    B, r, n2, C = x4.shape
    return pl.pallas_call(
        _stage_a_kernel,
        grid=(B // 2, n2 // nb, C // LANES),
        in_specs=[pl.BlockSpec((2, r, nb, LANES), lambda p, j, c: (p, 0, j, c)),
                  pl.BlockSpec(fa.shape, lambda p, j, c: (0, 0))],
        out_specs=pl.BlockSpec((1, 2, FFT_N1, nb, LANES), lambda p, j, c: (p, 0, 0, j, c)),
        out_shape=jax.ShapeDtypeStruct((B // 2, 2, FFT_N1, n2, C), F32),
        compiler_params=_cparams("parallel", "parallel", "parallel"),
        name="hyena_dft_a",
    )(x4, fa)


def _stage_b_kernel(x_ref, fb_ref, kf_ref, fbi_ref, o_ref):
    npair = x_ref.shape[0]
    n2 = x_ref.shape[3]
    x = jnp.concatenate(
        [jnp.concatenate([x_ref[p, 0, 0], x_ref[p, 1, 0]], axis=0) for p in range(npair)],
        axis=1).astype(BF16)
    g = _dot(fb_ref[0], x)
    gr, gi = g[:n2], g[n2:]
    kf = kf_ref[0]
    kr = jnp.concatenate([kf[:n2]] * npair, axis=1)
    ki = jnp.concatenate([kf[n2:]] * npair, axis=1)
    hcat = jnp.concatenate([gr * kr - gi * ki, gr * ki + gi * kr], axis=0).astype(BF16)
    y = _dot(fbi_ref[0], hcat)
    c = x_ref.shape[4]
    for p in range(npair):
        o_ref[p, 0, 0] = y[:n2, p * c:(p + 1) * c].astype(o_ref.dtype)
        o_ref[p, 1, 0] = y[n2:, p * c:(p + 1) * c].astype(o_ref.dtype)


def _stage_b(spec, fb, kf, fbi, order):
    npair, _, _, _, C = spec.shape
    blk = pl.BlockSpec((npair, 2, 1, FFT_N2, C), lambda k: (0, 0, k, 0, 0))
    mat = pl.BlockSpec((1, 2 * FFT_N2, 2 * FFT_N2), lambda k: (k, 0, 0))
    return pl.pallas_call(
        _stage_b_kernel,
        grid=(FFT_N1,),
        in_specs=[blk, mat, pl.BlockSpec((1, 2 * FFT_N2, C), lambda k: (k, 0, order)), mat],
        out_specs=blk,
        out_shape=jax.ShapeDtypeStruct(spec.shape, F32),
        compiler_params=_cparams("parallel"),
        name="hyena_dft_b",
    )(spec, fb, kf, fbi)


def _stage_c_kernel(y_ref, fc_ref, gate_ref, z_ref, skip_ref, *rest, stage_a):
    if stage_a:
        fa_ref, z_out, a_out = rest
        af, rows_a, _ = _n2_rows(a_out)
    else:
        (z_out,) = rest
    yf, rows_y, nb = _n2_rows(y_ref)
    gf, rows, _ = _n2_rows(gate_ref)
    zf, _, _ = _n2_rows(z_ref)
    of, _, _ = _n2_rows(z_out)
    skip = skip_ref[...]
    for n in range(nb):
        conv = _dot(fc_ref[...], _at_n2(yf, rows_y, nb, n)[...].astype(BF16))
        z = _at_n2(gf, rows, nb, n)[...] * (conv + skip * _at_n2(zf, rows, nb, n)[...])
        _at_n2(of, rows, nb, n)[...] = z
        if stage_a:
            _at_n2(af, rows_a, nb, n)[...] = _dot(fa_ref[...], z.astype(BF16))


def _stage_c(yspec, fc, gate, zin, skip, fa=None, nb=DFT_N2_BLOCK):
    B, r, n2, C = gate.shape
    dat = pl.BlockSpec((2, r, nb, LANES), lambda p, j, c: (p, 0, j, c))
    spc = pl.BlockSpec((1, 2, FFT_N1, nb, LANES), lambda p, j, c: (p, 0, 0, j, c))
    in_specs = [spc, pl.BlockSpec(fc.shape, lambda p, j, c: (0, 0)), dat, dat,
                pl.BlockSpec((1, LANES), lambda p, j, c: (0, c))]
    out_specs = [dat]
    out_shape = [jax.ShapeDtypeStruct(gate.shape, F32)]
    args = [yspec, fc, gate, zin, skip.reshape(1, C)]
    if fa is not None:
        in_specs.append(pl.BlockSpec(fa.shape, lambda p, j, c: (0, 0)))
        out_specs.append(spc)
        out_shape.append(jax.ShapeDtypeStruct(yspec.shape, F32))
        args.append(fa)
    return pl.pallas_call(
        functools.partial(_stage_c_kernel, stage_a=fa is not None),
        grid=(B // 2, n2 // nb, C // LANES),
        in_specs=in_specs, out_specs=out_specs, out_shape=out_shape,
        compiler_params=_cparams("parallel", "parallel", "parallel"),
        name="hyena_dft_c",
    )(*args)


def _hyena(x1, x2, v, skip, kf, consts, nb=DFT_N2_BLOCK):
    fa_data, _, fc, fb, fbi = consts
    B, S, C = v.shape
    split = lambda a: a.reshape(B, S // FFT_N2, FFT_N2, C)
    a0 = _stage_a(split(v), fa_data, nb)
    y0 = _stage_b(a0, fb, kf, fbi, 0)
    z1, a1 = _stage_c(y0, fc, split(x1), split(v), skip[0], fa=fa_data, nb=nb)
    y1 = _stage_b(a1, fb, kf, fbi, 1)
    (out,) = _stage_c(y1, fc, split(x2), z1, skip[1], nb=nb)
    return out.reshape(B, S, C)


def _memkv_kernel(m_ref, g_ref, w_ref, k_out, v_out):
    hm = _rms(m_ref[0], g_ref[...]).astype(BF16)
    kv = _dot(hm, w_ref[...])
    d = k_out.shape[2]
    k_out[0] = kv[:, :d].astype(BF16)
    v_out[0] = kv[:, d:].astype(BF16)


def _memkv(mem, g, w_mkv):
    B, M, D = mem.shape
    dk = w_mkv.shape[1] // 2
    w = w_mkv.astype(BF16)
    return pl.pallas_call(
        _memkv_kernel,
        grid=(B,),
        in_specs=[pl.BlockSpec((1, M, D), lambda b: (b, 0, 0)),
                  pl.BlockSpec((1, D), lambda b: (0, 0)),
                  pl.BlockSpec(w.shape, lambda b: (0, 0))],
        out_specs=[pl.BlockSpec((1, M, dk), lambda b: (b, 0, 0))] * 2,
        out_shape=[jax.ShapeDtypeStruct((B, M, dk), BF16)] * 2,
        compiler_params=_cparams("parallel"),
        name="mem_kv",
    )(mem, g.reshape(1, D), w)


def _route(logits):
    lane = lax.broadcasted_iota(jnp.int32, logits.shape, 1)
    ninf = -jnp.inf
    big = ROUTE_LANES
    first = lambda mask: jnp.min(jnp.where(mask, lane, big), axis=-1, keepdims=True)
    is_g = (lane >= N_EXPERTS) & (lane < N_EXPERTS + N_GROUPS)
    gl = jnp.where(is_g, logits, ninf)
    gmax = jnp.max(gl, axis=-1, keepdims=True)
    g_idx = first(gl == gmax) - N_EXPERTS
    p_group = 1.0 / jnp.sum(jnp.exp(gl - gmax), axis=-1, keepdims=True)
    in_g = (lane < N_EXPERTS) & ((lane // EXPERTS_PER_GROUP) == g_idx)
    el = jnp.where(in_g, logits, ninf)
    v1 = jnp.max(el, axis=-1, keepdims=True)
    i1 = first(el == v1)
    el2 = jnp.where(lane == i1, ninf, el)
    v2 = jnp.max(el2, axis=-1, keepdims=True)
    i2 = first(el2 == v2)
    e2 = jnp.exp(v2 - v1)
    p1 = 1.0 / (1.0 + e2)
    p2 = e2 / (1.0 + e2)
    sel = lambda n, val: jnp.where(lane == n, val, 0.0)
    return (sel(ROUTE_ID0, i1.astype(F32)) + sel(ROUTE_ID0 + 1, i2.astype(F32))
            + sel(ROUTE_W0, p_group * p1) + sel(ROUTE_W0 + 1, p_group * p2))


def _postmix_kernel(x_ref, a_ref, hy_ref, ag_ref, hg_ref, woa_ref, woh_ref, cg_ref, wmq_ref,
                    mk_ref, mv_ref, wmo_ref, fg_ref, wr_ref, br_ref, x_out, hn_out, route_out):
    ra = _rms(a_ref[...], ag_ref[...]).astype(BF16)
    rh = _rms(hy_ref[...], hg_ref[...]).astype(BF16)
    x = x_ref[...] + _dot(ra, woa_ref[...]) + _dot(rh, woh_ref[...])
    q = _dot(_rms(x, cg_ref[...]).astype(BF16), wmq_ref[...])
    dh = q.shape[1] // MEM_HEADS
    outs = []
    for h in range(MEM_HEADS):
        sl = slice(h * dh, (h + 1) * dh)
        s = lax.dot_general(q[:, sl].astype(BF16), mk_ref[0, :, sl], (((1,), (1,)), ((), ())),
                            preferred_element_type=F32) * dh ** -0.5
        p = jnp.exp(s - jnp.max(s, axis=-1, keepdims=True))
        l = jnp.sum(p, axis=-1, keepdims=True)
        outs.append(_dot(p.astype(BF16), mv_ref[0, :, sl]) / l)
    o = jnp.concatenate(outs, axis=1).astype(BF16)
    x = x + _dot(o, wmo_ref[...])
    x_out[...] = x
    hn = _rms(x, fg_ref[...])
    _store_row_tiles(hn_out, hn)
    route_out[...] = _route(_dot(hn.astype(BF16), wr_ref[...]) + br_ref[...])


def _postmix(x2d, a2d, hy2d, seq, ag, hg, w_out, cg, w_mq, mk, mv, w_mo, fg, w_rg, b_rg, w_re, b_re, tm=512):
    T, D = x2d.shape
    ca = a2d.shape[1]
    woa = w_out[:ca].astype(BF16)
    woh = w_out[ca:].astype(BF16)
    pad = ROUTE_LANES - N_EXPERTS - N_GROUPS
    wr = jnp.concatenate([w_re, w_rg, jnp.zeros((D, pad), F32)], 1).astype(BF16)
    br = jnp.concatenate([b_re, b_rg, jnp.zeros((pad,), F32)]).reshape(1, ROUTE_LANES)
    nseq = seq // tm
    full = lambda a: pl.BlockSpec(a.shape, lambda i: (0,) * a.ndim)
    row = lambda n: pl.BlockSpec((tm, n), lambda i: (i, 0))
    memb = pl.BlockSpec((1,) + mk.shape[1:], lambda i: (i // nseq, 0, 0))
    args = [x2d, a2d, hy2d, ag.reshape(1, -1), hg.reshape(1, -1), woa, woh, cg.reshape(1, D),
            w_mq.astype(BF16), mk, mv, w_mo.astype(BF16), fg.reshape(1, D), wr, br]
    in_specs = [row(D), row(ca), row(hy2d.shape[1])] + [full(a) for a in args[3:9]] + [memb, memb] \
        + [full(a) for a in args[11:]]
    return pl.pallas_call(
        _postmix_kernel,
        grid=(T // tm,),
        in_specs=in_specs,
        out_specs=[row(D), pl.BlockSpec((tm, SUBLANES, LANES), lambda i: (i, 0, 0)), row(ROUTE_LANES)],
        out_shape=[jax.ShapeDtypeStruct((T, D), F32), jax.ShapeDtypeStruct((T, SUBLANES, LANES), F32),
                   jax.ShapeDtypeStruct((T, ROUTE_LANES), F32)],
        compiler_params=_cparams("parallel"),
        name="postmix",
    )(*args)


def _rank_kernel(route_ref, rank_out, cnt_out, carry_ref):
    @pl.when(pl.program_id(0) == 0)
    def _():
        carry_ref[...] = jnp.zeros_like(carry_ref)

    route = route_ref[...]
    tr = route.shape[0]
    lane = lax.broadcasted_iota(jnp.int32, route.shape, 1).astype(F32)
    oh1 = lane == route[:, ROUTE_ID0:ROUTE_ID0 + 1]
    oh2 = lane == route[:, ROUTE_ID0 + 1:ROUTE_ID0 + 2]
    cnt = jnp.where(oh1 | oh2, 1.0, 0.0)
    r = lax.broadcasted_iota(jnp.int32, (tr, tr), 0)
    c = lax.broadcasted_iota(jnp.int32, (tr, tr), 1)
    below = jnp.where(c < r, 1.0, 0.0).astype(BF16)
    cum = _dot(below, cnt.astype(BF16)) + carry_ref[...]
    r1 = jnp.sum(jnp.where(oh1, cum, 0.0), axis=-1, keepdims=True)
    r2 = jnp.sum(jnp.where(oh2, cum, 0.0), axis=-1, keepdims=True)
    rank_out[...] = jnp.where(lane == 0, r1, 0.0) + jnp.where(lane == 1, r2, 0.0)
    carry_ref[...] += jnp.sum(cnt, axis=0, keepdims=True)
    cnt_out[...] = carry_ref[...]


def _pos_kernel(route_ref, rank_ref, base_ref, pos_out):
    route = route_ref[...]
    rank = rank_ref[...]
    lane = lax.broadcasted_iota(jnp.int32, route.shape, 1).astype(F32)
    base = base_ref[...]
    pick = lambda k: jnp.sum(jnp.where(lane == route[:, ROUTE_ID0 + k:ROUTE_ID0 + k + 1], base, 0.0),
                             axis=-1, keepdims=True) + rank[:, k:k + 1]
    pos_out[...] = (jnp.where(lane == 0, pick(0), 0.0) + jnp.where(lane == 1, pick(1), 0.0)).astype(jnp.int32)


def _dispatch_kernel(base_ref, cp_ref, pos_ref, hn_ref, xs_hbm, zero_ref, sem, *, tt, tmm):
    i = pl.program_id(0)
    row_copy = lambda src, dst: pltpu.make_async_copy(
        hn_ref.at[pl.ds(src, 1)], xs_hbm.at[pl.ds(dst, 1)], sem)

    @pl.when(i == 0)
    def _():
        zero_ref[...] = jnp.zeros_like(zero_ref)
        pad_copy = lambda e: pltpu.make_async_copy(
            zero_ref, xs_hbm.at[pl.ds(base_ref[e] + cp_ref[e] - tmm, tmm)], sem)
        for e in range(N_EXPERTS):
            @pl.when(cp_ref[e] > 0)
            def _():
                pad_copy(e).start()
        for e in range(N_EXPERTS):
            @pl.when(cp_ref[e] > 0)
            def _():
                pad_copy(e).wait()
        last = N_EXPERTS - 1
        tail_copy = lambda r: pltpu.make_async_copy(zero_ref, xs_hbm.at[pl.ds(r * tmm, tmm)], sem)
        first_free = (base_ref[last] + cp_ref[last]) // tmm
        n_tiles = xs_hbm.shape[0] // tmm
        lax.fori_loop(first_free, n_tiles, lambda r, c: (tail_copy(r).start(), c)[1], 0)
        lax.fori_loop(first_free, n_tiles, lambda r, c: (tail_copy(r).wait(), c)[1], 0)

    def start(t, carry):
        row_copy(t, pos_ref[2 * t]).start(priority=0)
        row_copy(t, pos_ref[2 * t + 1]).start(priority=1)
        return carry

    lax.fori_loop(0, tt, start, 0, unroll=DMA_UNROLL)

    def wait(t, carry):
        row_copy(0, 0).wait()
        row_copy(0, 0).wait()
        return carry

    lax.fori_loop(0, tt, wait, 0, unroll=DMA_UNROLL)


def _ffn_kernel(te_ref, nu_ref, xs_ref, wg_ref, wu_ref, wd_ref, ys_ref):
    used = pl.program_id(0) < nu_ref[0]

    @pl.when(used)
    def _():
        x = _load_row_tiles(xs_ref).astype(BF16)
        a = _dot(x, wg_ref[0].astype(BF16))
        b = _dot(x, wu_ref[0].astype(BF16))
        m = (a * jax.nn.sigmoid(a)) * b
        _store_row_tiles(ys_ref, _dot(m.astype(BF16), wd_ref[0].astype(BF16)))

    @pl.when(jnp.logical_not(used))
    def _():
        ys_ref[...] = jnp.zeros_like(ys_ref)


def _combine_kernel(pos_ref, pos_next_ref, ys_hbm, x_ref, route_ref, fg_ref, o_ref, buf_ref, sem, *, tc):
    i = pl.program_id(0)
    slot = i % 2
    row_copy = lambda s, k, t, p: pltpu.make_async_copy(
        ys_hbm.at[pl.ds(p, 1)], buf_ref.at[s, k, pl.ds(t, 1)], sem.at[s])

    def fetch(p_ref, s):
        def start(t, carry):
            row_copy(s, 0, t, p_ref[2 * t]).start(priority=0)
            row_copy(s, 1, t, p_ref[2 * t + 1]).start(priority=1)
            return carry

        lax.fori_loop(0, tc, start, 0, unroll=DMA_UNROLL)

    @pl.when(i == 0)
    def _():
        fetch(pos_ref, 0)

    @pl.when(i + 1 < pl.num_programs(0))
    def _():
        fetch(pos_next_ref, 1 - slot)

    def wait(t, carry):
        row_copy(slot, 0, 0, 0).wait()
        row_copy(slot, 1, 0, 0).wait()
        return carry

    lax.fori_loop(0, tc, wait, 0, unroll=DMA_UNROLL)
    route = route_ref[...]
    y = (x_ref[...] + route[:, ROUTE_W0:ROUTE_W0 + 1] * _load_row_tiles(buf_ref, (slot, 0))
         + route[:, ROUTE_W0 + 1:ROUTE_W0 + 2] * _load_row_tiles(buf_ref, (slot, 1)))
    o_ref[...] = _rms(y, fg_ref[...])


def _moe(hn, route, x2d, w_gate, w_up, w_down, fg, tr=512, tt=512, tc=256):
    T, D = x2d.shape
    E = N_EXPERTS
    F = w_gate.shape[-1]
    tmm = MOE_ROW_TILE
    row = lambda tm, n: pl.BlockSpec((tm, n), lambda i: (i, 0))
    rank, counts = pl.pallas_call(
        _rank_kernel,
        grid=(T // tr,),
        in_specs=[row(tr, ROUTE_LANES)],
        out_specs=[row(tr, ROUTE_LANES), pl.BlockSpec((1, ROUTE_LANES), lambda i: (0, 0))],
        out_shape=[jax.ShapeDtypeStruct((T, ROUTE_LANES), F32), jax.ShapeDtypeStruct((1, ROUTE_LANES), F32)],
        scratch_shapes=[pltpu.VMEM((1, ROUTE_LANES), F32)],
        compiler_params=_cparams("arbitrary"),
        name="moe_rank",
    )(route)

    cnt = counts[0, :E].astype(jnp.int32)
    cp = ((cnt + tmm - 1) // tmm) * tmm
    ends = jnp.cumsum(cp)
    base = ends - cp
    n_used = ends[-1] // tmm
    n_tiles = (2 * T) // tmm + E
    tile_start = jnp.minimum(jnp.arange(n_tiles, dtype=jnp.int32), n_used - 1) * tmm
    tile_expert = jnp.minimum(jnp.sum((tile_start[:, None] >= ends[None, :]).astype(jnp.int32), axis=1), E - 1)
    base_lanes = jnp.zeros((1, ROUTE_LANES), F32).at[0, :E].set(base.astype(F32))

    pos = pl.pallas_call(
        _pos_kernel,
        grid=(T // tr,),
        in_specs=[row(tr, ROUTE_LANES), row(tr, ROUTE_LANES), pl.BlockSpec((1, ROUTE_LANES), lambda i: (0, 0))],
        out_specs=row(tr, ROUTE_LANES),
        out_shape=jax.ShapeDtypeStruct((T, ROUTE_LANES), jnp.int32),
        compiler_params=_cparams("parallel"),
        name="moe_pos",
    )(route, rank, base_lanes)
    pos = pos[:, :2].reshape(2 * T)

    xs = pl.pallas_call(
        functools.partial(_dispatch_kernel, tt=tt, tmm=tmm),
        grid_spec=pltpu.PrefetchScalarGridSpec(
            num_scalar_prefetch=2,
            grid=(T // tt,),
            in_specs=[pl.BlockSpec((2 * tt,), lambda i, b, c: (i,), memory_space=pltpu.SMEM),
                      pl.BlockSpec((tt, SUBLANES, LANES), lambda i, b, c: (i, 0, 0))],
            out_specs=pl.BlockSpec(memory_space=pl.ANY),
            scratch_shapes=[pltpu.VMEM((tmm, SUBLANES, LANES), F32), pltpu.SemaphoreType.DMA(())]),
        out_shape=jax.ShapeDtypeStruct((n_tiles * tmm, SUBLANES, LANES), F32),
        compiler_params=_cparams("arbitrary"),
        name="moe_dispatch",
    )(base, cp, pos, hn)

    tile = lambda r, te, nu: (jnp.minimum(r, nu[0] - 1), 0, 0)
    ys = pl.pallas_call(
        _ffn_kernel,
        grid_spec=pltpu.PrefetchScalarGridSpec(
            num_scalar_prefetch=2,
            grid=(n_tiles,),
            in_specs=[pl.BlockSpec((tmm, SUBLANES, LANES), tile),
                      pl.BlockSpec((1, D, F), lambda r, te, nu: (te[r], 0, 0)),
                      pl.BlockSpec((1, D, F), lambda r, te, nu: (te[r], 0, 0)),
                      pl.BlockSpec((1, F, D), lambda r, te, nu: (te[r], 0, 0))],
            out_specs=pl.BlockSpec((tmm, SUBLANES, LANES), lambda r, te, nu: (r, 0, 0))),
        out_shape=jax.ShapeDtypeStruct((n_tiles * tmm, SUBLANES, LANES), F32),
        compiler_params=_cparams("arbitrary"),
        name="moe_ffn",
    )(tile_expert, n_used.reshape(1), xs, w_gate.reshape(E, D, F), w_up.reshape(E, D, F),
      w_down.reshape(E, F, D))

    return pl.pallas_call(
        functools.partial(_combine_kernel, tc=tc),
        grid=(T // tc,),
        in_specs=[pl.BlockSpec((2 * tc,), lambda i: (i,), memory_space=pltpu.SMEM),
                  pl.BlockSpec((2 * tc,), lambda i: (jnp.minimum(i + 1, T // tc - 1),),
                               memory_space=pltpu.SMEM),
                  pl.BlockSpec(memory_space=pl.ANY),
                  row(tc, D), row(tc, ROUTE_LANES), pl.BlockSpec((1, D), lambda i: (0, 0))],
        out_specs=row(tc, D),
        out_shape=jax.ShapeDtypeStruct((T, D), F32),
        scratch_shapes=[pltpu.VMEM((2, 2, tc, SUBLANES, LANES), F32), pltpu.SemaphoreType.DMA((2,))],
        compiler_params=_cparams("arbitrary"),
        name="moe_combine",
    )(pos, pos, ys, x2d, route, fg.reshape(1, D))


def kernel(x, mem, mix_norm_g, w_in, q_norm_g, kv_norm_g, w_uq, w_ukv, hy_conv_w, hy_conv_b, hy_w1, hy_b1, hy_freq, hy_w2, hy_b2, hy_w3, hy_b3, hy_decay, hy_skip, attn_out_g, hy_out_g, w_out, cross_norm_g, mem_norm_g, w_mq, w_mkv, w_mo, ffn_norm_g, w_route_group, b_route_group, w_route_expert, b_route_expert, w_gate, w_up, w_down, final_norm_g):
    B, S, D = x.shape
    depth = w_in.shape[0]
    consts = _dft_constants(S)
    xf = x.reshape(B * S, D)
    for l in range(depth):
        q, k, v, hx1, hx2, hv = _inproj(xf, S, mix_norm_g[l], w_in[l], q_norm_g[l], kv_norm_g[l],
                                        w_uq[l], w_ukv[l], hy_conv_w[l], hy_conv_b[l])
        HP = q.shape[1]
        a_out = _attention(q.reshape(B, S, HP), k.reshape(B, S, HP), v.reshape(B, S, HP))
        filt = _hyena_filter_time(S, hy_w1[l], hy_b1[l], hy_freq[l], hy_w2[l], hy_b2[l], hy_w3[l],
                                  hy_b3[l], hy_decay[l])
        kf = _hyena_filter_spectrum(filt, consts[1], consts[3])
        C = hv.shape[1]
        h_out = _hyena(hx1.reshape(B, S, C), hx2.reshape(B, S, C), hv.reshape(B, S, C),
                       hy_skip[l], kf, consts)
        mk, mv = _memkv(mem, mem_norm_g[l], w_mkv[l])
        x2, hn, route = _postmix(xf, a_out.reshape(B * S, -1), h_out.reshape(B * S, C), S,
                                attn_out_g[l], hy_out_g[l], w_out[l], cross_norm_g[l], w_mq[l], mk, mv,
                                w_mo[l], ffn_norm_g[l], w_route_group[l], b_route_group[l],
                                w_route_expert[l], b_route_expert[l])
        assert depth == 1
        xf = _moe(hn, route, x2, w_gate[l], w_up[l], w_down[l], final_norm_g)
    return xf.reshape(B, S, D)
```

```python
import functools
import math

import numpy as np
import jax
import jax.numpy as jnp
from jax import lax
from jax.experimental import pallas as pl
from jax.experimental.pallas import tpu as pltpu

F32 = jnp.float32
BF16 = jnp.bfloat16

EPS = 1e-6
MLA_HEADS = 8
MLA_NOPE = 64
MLA_ROPE = 32
MLA_V = 64
ROPE_BASE = 10000.0
HEAD_PAD = 128
HY_ORDER = 2
HY_DIRS = 2
HY_BANDS = 16
MEM_HEADS = 4
N_GROUPS = 4
EXPERTS_PER_GROUP = 8
N_EXPERTS = N_GROUPS * EXPERTS_PER_GROUP
ROUTE_LANES = 128
ROUTE_ID0 = 0
ROUTE_W0 = 2
MOE_ROW_TILE = 256
DMA_UNROLL = 8

FFT_N1 = 64
FFT_N2 = 128
DFT_N2_BLOCK = 32
DFT_PITCH_PAD = 8

VMEM_LIMIT = 56 * 1024 * 1024


def _cparams(*sem):
    return pltpu.CompilerParams(dimension_semantics=sem, vmem_limit_bytes=VMEM_LIMIT)


def _rms(x, g):
    return x * lax.rsqrt(jnp.mean(x * x, axis=-1, keepdims=True) + EPS) * g


def _dot(a, b):
    return jnp.dot(a, b, preferred_element_type=F32)


SUBLANES = 8
LANES = 128


def _load_row_tiles(ref, lead=()):
    rows = ref.shape[-3]
    flat = ref.reshape(*ref.shape[:-3], rows * SUBLANES, LANES)
    return jnp.concatenate(
        [flat[(*lead, pl.ds(j, rows, stride=SUBLANES), slice(None))] for j in range(SUBLANES)], axis=1)


def _store_row_tiles(ref, val):
    rows = ref.shape[0]
    flat = ref.reshape(rows * SUBLANES, LANES)
    for j in range(SUBLANES):
        flat[pl.ds(j, rows, stride=SUBLANES), :] = val[:, j * LANES:(j + 1) * LANES]


def _inproj_kernel(x_ref, xp_ref, xn_ref, g_ref, wq_ref, wkv_ref, wkra_ref, wkrb_ref, why_ref, qg_ref,
                   kvg_ref, wqa_ref, wqb_ref, wka_ref, wv_ref, tab_ref, cw_ref, cb_ref,
                   q_out, k_out, v_out, x1_out, x2_out, hv_out, *, nseq):
    tm = x_ref.shape[0]
    halo = xp_ref.shape[0]
    hf = _rms(jnp.concatenate([xp_ref[...], x_ref[...], xn_ref[...]], axis=0), g_ref[...])
    h = hf[halo:halo + tm].astype(BF16)
    qn = _rms(_dot(h, wq_ref[...]), qg_ref[...]).astype(BF16)
    kvn = _rms(_dot(h, wkv_ref[...]), kvg_ref[...]).astype(BF16)
    tab = tab_ref[...]
    cq, sq, ck, sk = (tab[:, i * HEAD_PAD:(i + 1) * HEAD_PAD] for i in range(4))
    tile = lambda t: jnp.concatenate([t] * MLA_HEADS, axis=1)
    q = _dot(qn, wqa_ref[...]) * tile(cq) + _dot(qn, wqb_ref[...]) * tile(sq)
    q_out[...] = q.astype(BF16)
    kr = _dot(h, wkra_ref[...]) * ck + _dot(h, wkrb_ref[...]) * sk
    k_out[...] = (_dot(kvn, wka_ref[...]) + tile(kr)).astype(BF16)
    lane = lax.broadcasted_iota(jnp.int32, (1, v_out.shape[1]), 1) % HEAD_PAD
    v_out[...] = (_dot(kvn, wv_ref[...]) + jnp.where(lane == MLA_V, 1.0, 0.0)).astype(BF16)
    hy = _dot(hf.astype(BF16), why_ref[...])
    i = pl.program_id(0) % nseq
    row = lax.broadcasted_iota(jnp.int32, hy.shape, 0)
    outside = ((row == halo - 1) & (i == 0)) | ((row == halo + tm) & (i == nseq - 1))
    hy = jnp.where(outside, 0.0, hy)
    cw = cw_ref[...]
    u = (hy[halo - 1:halo - 1 + tm] * cw[0:1] + hy[halo:halo + tm] * cw[1:2]
         + hy[halo + 1:halo + 1 + tm] * cw[2:3] + cb_ref[...])
    c = x1_out.shape[1]
    x1_out[...] = u[:, :c]
    x2_out[...] = u[:, c:2 * c]
    hv_out[...] = u[:, 2 * c:]


def _inproj(x2d, seq, mix_g, w_in, q_g, kv_g, w_uq, w_ukv, conv_w, conv_b, tm=512):
    T, D = x2d.shape
    per = tm // SUBLANES
    cb = conv_b.reshape(1, -1)
    q_rank, kv_rank = q_g.shape[0], kv_g.shape[0]
    off_kv = q_rank
    off_kr = off_kv + kv_rank
    off_hy = off_kr + MLA_ROPE
    C = (w_in.shape[1] - off_hy) // 3
    H = MLA_HEADS
    half = MLA_ROPE // 2
    wq = w_in[:, :off_kv].astype(BF16)
    wkv = w_in[:, off_kv:off_kr].astype(BF16)
    wkr = w_in[:, off_kr:off_hy]
    wkr_sw = jnp.concatenate([wkr[:, half:], wkr[:, :half]], axis=1)
    zpad = lambda n: jnp.zeros((D, n), F32)
    wkra = jnp.concatenate([zpad(MLA_NOPE), wkr, zpad(HEAD_PAD - MLA_NOPE - MLA_ROPE)], 1).astype(BF16)
    wkrb = jnp.concatenate([zpad(MLA_NOPE), wkr_sw, zpad(HEAD_PAD - MLA_NOPE - MLA_ROPE)], 1).astype(BF16)
    why = w_in[:, off_hy:].astype(BF16)

    uq = w_uq.reshape(q_rank, H, MLA_NOPE + MLA_ROPE)
    uq_n, uq_r = uq[..., :MLA_NOPE], uq[..., MLA_NOPE:]
    uq_rs = jnp.concatenate([uq_r[..., half:], uq_r[..., :half]], axis=-1)
    zq = lambda n: jnp.zeros((q_rank, H, n), F32)
    wqa = jnp.concatenate([uq_n, uq_r, zq(HEAD_PAD - MLA_NOPE - MLA_ROPE)], -1).reshape(q_rank, H * HEAD_PAD).astype(BF16)
    wqb = jnp.concatenate([zq(MLA_NOPE), uq_rs, zq(HEAD_PAD - MLA_NOPE - MLA_ROPE)], -1).reshape(q_rank, H * HEAD_PAD).astype(BF16)
    ukv = w_ukv.reshape(kv_rank, H, MLA_NOPE + MLA_V)
    zk = lambda n: jnp.zeros((kv_rank, H, n), F32)
    wka = jnp.concatenate([ukv[..., :MLA_NOPE], zk(HEAD_PAD - MLA_NOPE)], -1).reshape(kv_rank, H * HEAD_PAD).astype(BF16)
    wv = jnp.concatenate([ukv[..., MLA_NOPE:], zk(HEAD_PAD - MLA_V)], -1).reshape(kv_rank, H * HEAD_PAD).astype(BF16)

    pos = jnp.arange(seq, dtype=F32)
    inv = ROPE_BASE ** (-jnp.arange(half, dtype=F32) / half)
    ang = pos[:, None] * inv[None, :]
    cos2 = jnp.concatenate([jnp.cos(ang), jnp.cos(ang)], 1)
    sin2 = jnp.concatenate([-jnp.sin(ang), jnp.sin(ang)], 1)
    zs = lambda n: jnp.zeros((seq, n), F32)
    scale = (MLA_NOPE + MLA_ROPE) ** -0.5 * math.log2(math.e)
    rest = HEAD_PAD - MLA_NOPE - MLA_ROPE
    cq = scale * jnp.concatenate([jnp.ones((seq, MLA_NOPE), F32), cos2, zs(rest)], 1)
    sq = scale * jnp.concatenate([zs(MLA_NOPE), sin2, zs(rest)], 1)
    ck = jnp.concatenate([zs(MLA_NOPE), cos2, zs(rest)], 1)
    sk = jnp.concatenate([zs(MLA_NOPE), sin2, zs(rest)], 1)
    tab = jnp.concatenate([cq, sq, ck, sk], 1)

    nseq = seq // tm
    full = lambda a: pl.BlockSpec(a.shape, lambda i: (0,) * a.ndim)
    row = lambda n: pl.BlockSpec((tm, n), lambda i: (i, 0))
    consts = [mix_g.reshape(1, D), wq, wkv, wkra, wkrb, why, q_g.reshape(1, -1), kv_g.reshape(1, -1),
              wqa, wqb, wka, wv]
    HP = H * HEAD_PAD
    return pl.pallas_call(
        functools.partial(_inproj_kernel, nseq=nseq),
        grid=(T // tm,),
        in_specs=[row(D),
                  pl.BlockSpec((SUBLANES, D), lambda i: (jnp.maximum(i * per - 1, 0), 0)),
                  pl.BlockSpec((SUBLANES, D), lambda i: (jnp.minimum((i + 1) * per, T // SUBLANES - 1), 0))]
        + [full(a) for a in consts]
        + [pl.BlockSpec((tm, 4 * HEAD_PAD), lambda i: (i % nseq, 0)), full(conv_w), full(cb)],
        out_specs=[row(HP), row(HP), row(HP), row(C), row(C), row(C)],
        out_shape=[jax.ShapeDtypeStruct((T, HP), BF16)] * 3 + [jax.ShapeDtypeStruct((T, C), F32)] * 3,
        compiler_params=_cparams("parallel"),
        name="inproj",
    )(x2d, x2d, x2d, *consts, tab, conv_w, cb)


def _attn_kernel(q_ref, k_ref, v_ref, o_ref):
    outs = []
    for h in range(MLA_HEADS):
        sl = slice(h * HEAD_PAD, (h + 1) * HEAD_PAD)
        s = lax.dot_general(q_ref[0, :, sl], k_ref[0, :, sl], (((1,), (1,)), ((), ())),
                            preferred_element_type=F32).astype(BF16)
        p = jnp.exp2(s - jnp.max(s, axis=-1, keepdims=True))
        o = _dot(p, v_ref[0, :, sl])
        outs.append(o[:, :MLA_V] / o[:, MLA_V:MLA_V + 1])
    o_ref[0] = jnp.concatenate(outs, axis=1)


def _attention(q, k, v, tq=256):
    B, S, HP = q.shape
    return pl.pallas_call(
        _attn_kernel,
        grid=(B, S // tq),
        in_specs=[pl.BlockSpec((1, tq, HP), lambda b, i: (b, i, 0)),
                  pl.BlockSpec((1, S, HP), lambda b, i: (b, 0, 0)),
                  pl.BlockSpec((1, S, HP), lambda b, i: (b, 0, 0))],
        out_specs=pl.BlockSpec((1, tq, MLA_HEADS * MLA_V), lambda b, i: (b, i, 0)),
        out_shape=jax.ShapeDtypeStruct((B, S, MLA_HEADS * MLA_V), F32),
        compiler_params=_cparams("parallel", "arbitrary"),
        name="mla_attention",
    )(q, k, v)


def _dft_constants(seq):
    n = 2 * seq
    n1, n2 = FFT_N1, FFT_N2
    assert n1 * n2 == n
    r1 = np.arange(n1)
    r2 = np.arange(n2)
    blk = lambda z: np.block([[z.real, -z.imag], [z.imag, z.real]])
    w1 = np.exp(-2j * np.pi * np.outer(r1, r1) / n1)
    fa_data = blk(w1[:, :n1 // 2])
    fa_filt = np.concatenate([w1.real, w1.imag], axis=0)
    fc = blk(np.conj(w1).T[:n1 // 2, :])
    w2 = np.exp(-2j * np.pi * np.outer(r2, r2) / n2)
    tw = np.exp(-2j * np.pi * np.outer(r1, r2) / n)
    fb = np.stack([blk(w2 * tw[k][None, :]) for k in range(n1)])
    fbi = np.stack([blk(np.conj(w2).T * np.conj(tw[k])[:, None] / n) for k in range(n1)])
    as_bf = lambda a: jnp.asarray(a, dtype=F32).astype(BF16)
    return as_bf(fa_data), as_bf(fa_filt), as_bf(fc), as_bf(fb), as_bf(fbi)


def _filter_kernel(z_ref, w1_ref, b1_ref, fr_ref, w2_ref, b2_ref, w3_ref, b3_ref, dec_ref, o_ref, *, seq, tr):
    hp = lax.Precision.HIGHEST
    z = z_ref[...]
    fr = fr_ref[...]
    h = jnp.sin(fr[0:1] * (jnp.dot(z, w1_ref[...], precision=hp, preferred_element_type=F32) + b1_ref[...]))
    h = jnp.sin(fr[1:2] * (jnp.dot(h, w2_ref[...], precision=hp, preferred_element_type=F32) + b2_ref[...]))
    h = jnp.dot(h, w3_ref[0], precision=hp, preferred_element_type=F32) + b3_ref[0]
    h = h * jnp.exp(-z[:, 0:1] * jnp.abs(dec_ref[0]))
    n = pl.program_id(0) * tr + lax.broadcasted_iota(jnp.int32, h.shape, 0)
    o_ref[...] = jnp.where(n == seq, 0.0, h)


def _hyena_filter_time(seq, w1, b1, freq, w2, b2, w3, b3, decay, tr=512):
    n = 2 * seq
    emb, ffn = w1.shape
    C = w3.shape[1] // (HY_ORDER * HY_DIRS)
    off = jnp.arange(n)
    t = jnp.where(off < seq, off, n - off).astype(F32)
    bands = jnp.linspace(1e-4, HY_BANDS - 1, HY_BANDS, dtype=F32)
    ang = 2.0 * math.pi * t[:, None] * bands[None, :] / seq
    z = jnp.concatenate([(t / seq)[:, None], jnp.cos(ang), -jnp.sin(ang)], axis=-1)
    zl = 128
    z = jnp.pad(z, ((0, 0), (0, zl - emb)))
    w1p = jnp.pad(w1, ((0, zl - emb), (0, 0)))
    by_dir = lambda a: jnp.moveaxis(a.reshape(a.shape[0], HY_ORDER, HY_DIRS, C), 2, 0).reshape(
        HY_DIRS, a.shape[0], HY_ORDER * C)
    w3d, b3d, decd = by_dir(w3), by_dir(b3.reshape(1, -1)), by_dir(decay.reshape(1, -1))
    full = lambda a: pl.BlockSpec(a.shape, lambda i: (0,) * a.ndim)
    ndir = lambda a: pl.BlockSpec((1,) + a.shape[1:], lambda i: ((i * tr) // seq, 0, 0))
    consts = [w1p, b1.reshape(1, -1), freq, w2, b2.reshape(1, -1)]
    return pl.pallas_call(
        functools.partial(_filter_kernel, seq=seq, tr=tr),
        grid=(n // tr,),
        in_specs=[pl.BlockSpec((tr, zl), lambda i: (i, 0))] + [full(a) for a in consts]
        + [ndir(w3d), ndir(b3d), ndir(decd)],
        out_specs=pl.BlockSpec((tr, HY_ORDER * C), lambda i: (i, 0)),
        out_shape=jax.ShapeDtypeStruct((n, HY_ORDER * C), F32),
        compiler_params=_cparams("parallel"),
        name="hyena_filter_mlp",
    )(z, *consts, w3d, b3d, decd)


def _pitched(rows, nb):
    return pltpu.VMEM((rows, nb + DFT_PITCH_PAD, LANES), F32)


def _block_rows(ref):
    return math.prod(ref.shape[:-2]), ref.shape[-2]


def _copy_in(ref, scr):
    rows, nb = _block_rows(ref)
    scr[:, :nb, :] = ref[...].reshape(rows, nb, LANES)


def _copy_out(scr, ref):
    rows, nb = _block_rows(ref)
    ref[...] = scr[:, :nb, :].reshape(ref.shape)


def _at_n2(scr, n):
    rows, pitch, _ = scr.shape
    return scr.reshape(rows * pitch, LANES).at[pl.ds(n, rows, stride=pitch), :]


def _outer_dft(mat_ref, x_ref, o_ref, xs, os):
    _copy_in(x_ref, xs)
    for n in range(x_ref.shape[-2]):
        _at_n2(os, n)[...] = _dot(mat_ref[...], _at_n2(xs, n)[...].astype(BF16))
    _copy_out(os, o_ref)


def _filter_stage_a_kernel(x_ref, fa_ref, o_ref, xs, os):
    _outer_dft(fa_ref, x_ref, o_ref, xs, os)


def _filter_stage_b_kernel(x_ref, fb_ref, o_ref):
    x = jnp.concatenate([x_ref[0, 0], x_ref[1, 0]], axis=0).astype(BF16)
    o_ref[0] = _dot(fb_ref[0], x)


def _hyena_filter_spectrum(filt, fa_filt, fb, nb=DFT_N2_BLOCK):
    n, oc = filt.shape
    a = pl.pallas_call(
        _filter_stage_a_kernel,
        grid=(FFT_N2 // nb, oc // LANES),
        in_specs=[pl.BlockSpec((FFT_N1, nb, LANES), lambda j, c: (0, j, c)),
                  pl.BlockSpec(fa_filt.shape, lambda j, c: (0, 0))],
        out_specs=pl.BlockSpec((2, FFT_N1, nb, LANES), lambda j, c: (0, 0, j, c)),
        out_shape=jax.ShapeDtypeStruct((2, FFT_N1, FFT_N2, oc), F32),
        scratch_shapes=[_pitched(FFT_N1, nb), _pitched(2 * FFT_N1, nb)],
        compiler_params=_cparams("parallel", "parallel"),
        name="hyena_filter_dft_a",
    )(filt.reshape(FFT_N1, FFT_N2, oc), fa_filt)
    return pl.pallas_call(
        _filter_stage_b_kernel,
        grid=(FFT_N1,),
        in_specs=[pl.BlockSpec((2, 1, FFT_N2, oc), lambda k: (0, k, 0, 0)),
                  pl.BlockSpec((1, 2 * FFT_N2, 2 * FFT_N2), lambda k: (k, 0, 0))],
        out_specs=pl.BlockSpec((1, 2 * FFT_N2, oc), lambda k: (k, 0, 0)),
        out_shape=jax.ShapeDtypeStruct((FFT_N1, 2 * FFT_N2, oc), F32),
        compiler_params=_cparams("parallel"),
        name="hyena_filter_dft_b",
    )(a, fb)


def _stage_a_kernel(x_ref, fa_ref, o_ref, xs, os):
    _outer_dft(fa_ref, x_ref, o_ref, xs, os)


def _stage_a(x4, fa, nb):
    B, r, n2, C = x4.shape
    return pl.pallas_call(
        _stage_a_kernel,
        grid=(B // 2, n2 // nb, C // LANES),
        in_specs=[pl.BlockSpec((2, r, nb, LANES), lambda p, j, c: (p, 0, j, c)),
                  pl.BlockSpec(fa.shape, lambda p, j, c: (0, 0))],
        out_specs=pl.BlockSpec((1, 2, FFT_N1, nb, LANES), lambda p, j, c: (p, 0, 0, j, c)),
        out_shape=jax.ShapeDtypeStruct((B // 2, 2, FFT_N1, n2, C), F32),
        scratch_shapes=[_pitched(2 * r, nb), _pitched(2 * FFT_N1, nb)],
        compiler_params=_cparams("parallel", "parallel", "parallel"),
        name="hyena_dft_a",
    )(x4, fa)


def _stage_b_kernel(x_ref, fb_ref, kf_ref, fbi_ref, o_ref):
    npair = x_ref.shape[0]
    n2 = x_ref.shape[3]
    x = jnp.concatenate(
        [jnp.concatenate([x_ref[p, 0, 0], x_ref[p, 1, 0]], axis=0) for p in range(npair)],
        axis=1).astype(BF16)
    g = _dot(fb_ref[0], x)
    gr, gi = g[:n2], g[n2:]
    kf = kf_ref[0]
    kr = jnp.concatenate([kf[:n2]] * npair, axis=1)
    ki = jnp.concatenate([kf[n2:]] * npair, axis=1)
    hcat = jnp.concatenate([gr * kr - gi * ki, gr * ki + gi * kr], axis=0).astype(BF16)
    y = _dot(fbi_ref[0], hcat)
    c = x_ref.shape[4]
    for p in range(npair):
        o_ref[p, 0, 0] = y[:n2, p * c:(p + 1) * c].astype(o_ref.dtype)
        o_ref[p, 1, 0] = y[n2:, p * c:(p + 1) * c].astype(o_ref.dtype)


def _stage_b(spec, fb, kf, fbi, order):
    npair, _, _, _, C = spec.shape
    blk = pl.BlockSpec((npair, 2, 1, FFT_N2, C), lambda k: (0, 0, k, 0, 0))
    mat = pl.BlockSpec((1, 2 * FFT_N2, 2 * FFT_N2), lambda k: (k, 0, 0))
    return pl.pallas_call(
        _stage_b_kernel,
        grid=(FFT_N1,),
        in_specs=[blk, mat, pl.BlockSpec((1, 2 * FFT_N2, C), lambda k: (k, 0, order)), mat],
        out_specs=blk,
        out_shape=jax.ShapeDtypeStruct(spec.shape, F32),
        compiler_params=_cparams("parallel"),
        name="hyena_dft_b",
    )(spec, fb, kf, fbi)


def _stage_c_kernel(y_ref, fc_ref, gate_ref, z_ref, skip_ref, *rest, stage_a):
    if stage_a:
        fa_ref, z_out, a_out, ys, gs, zs, os, as_ = rest
    else:
        z_out, ys, gs, zs, os = rest
    nb = gate_ref.shape[-2]
    _copy_in(y_ref, ys)
    _copy_in(gate_ref, gs)
    _copy_in(z_ref, zs)
    skip = skip_ref[...]
    for n in range(nb):
        conv = _dot(fc_ref[...], _at_n2(ys, n)[...].astype(BF16))
        _at_n2(os, n)[...] = _at_n2(gs, n)[...] * (conv + skip * _at_n2(zs, n)[...])
    _copy_out(os, z_out)
    if stage_a:
        for n in range(nb):
            _at_n2(as_, n)[...] = _dot(fa_ref[...], _at_n2(os, n)[...].astype(BF16))
        _copy_out(as_, a_out)


def _stage_c(yspec, fc, gate, zin, skip, fa=None, nb=DFT_N2_BLOCK):
    B, r, n2, C = gate.shape
    dat = pl.BlockSpec((2, r, nb, LANES), lambda p, j, c: (p, 0, j, c))
    spc = pl.BlockSpec((1, 2, FFT_N1, nb, LANES), lambda p, j, c: (p, 0, 0, j, c))
    in_specs = [spc, pl.BlockSpec(fc.shape, lambda p, j, c: (0, 0)), dat, dat,
                pl.BlockSpec((1, LANES), lambda p, j, c: (0, c))]
    out_specs = [dat]
    out_shape = [jax.ShapeDtypeStruct(gate.shape, F32)]
    args = [yspec, fc, gate, zin, skip.reshape(1, C)]
    scratch = [_pitched(2 * FFT_N1, nb)] + [_pitched(2 * r, nb)] * 3
    if fa is not None:
        in_specs.append(pl.BlockSpec(fa.shape, lambda p, j, c: (0, 0)))
        out_specs.append(spc)
        out_shape.append(jax.ShapeDtypeStruct(yspec.shape, F32))
        args.append(fa)
        scratch.append(_pitched(2 * FFT_N1, nb))
    return pl.pallas_call(
        functools.partial(_stage_c_kernel, stage_a=fa is not None),
        grid=(B // 2, n2 // nb, C // LANES),
        in_specs=in_specs, out_specs=out_specs, out_shape=out_shape,
        scratch_shapes=scratch,
        compiler_params=_cparams("parallel", "parallel", "parallel"),
        name="hyena_dft_c",
    )(*args)


def _hyena(x1, x2, v, skip, kf, consts, nb=DFT_N2_BLOCK):
    fa_data, _, fc, fb, fbi = consts
    B, S, C = v.shape
    split = lambda a: a.reshape(B, S // FFT_N2, FFT_N2, C)
    a0 = _stage_a(split(v), fa_data, nb)
    y0 = _stage_b(a0, fb, kf, fbi, 0)
    z1, a1 = _stage_c(y0, fc, split(x1), split(v), skip[0], fa=fa_data, nb=nb)
    y1 = _stage_b(a1, fb, kf, fbi, 1)
    (out,) = _stage_c(y1, fc, split(x2), z1, skip[1], nb=nb)
    return out.reshape(B, S, C)


def _memkv_kernel(m_ref, g_ref, w_ref, k_out, v_out):
    hm = _rms(m_ref[0], g_ref[...]).astype(BF16)
    kv = _dot(hm, w_ref[...])
    d = k_out.shape[2]
    k_out[0] = kv[:, :d].astype(BF16)
    v_out[0] = kv[:, d:].astype(BF16)


def _memkv(mem, g, w_mkv):
    B, M, D = mem.shape
    dk = w_mkv.shape[1] // 2
    w = w_mkv.astype(BF16)
    return pl.pallas_call(
        _memkv_kernel,
        grid=(B,),
        in_specs=[pl.BlockSpec((1, M, D), lambda b: (b, 0, 0)),
                  pl.BlockSpec((1, D), lambda b: (0, 0)),
                  pl.BlockSpec(w.shape, lambda b: (0, 0))],
        out_specs=[pl.BlockSpec((1, M, dk), lambda b: (b, 0, 0))] * 2,
        out_shape=[jax.ShapeDtypeStruct((B, M, dk), BF16)] * 2,
        compiler_params=_cparams("parallel"),
        name="mem_kv",
    )(mem, g.reshape(1, D), w)


def _route(logits):
    lane = lax.broadcasted_iota(jnp.int32, logits.shape, 1)
    ninf = -jnp.inf
    big = ROUTE_LANES
    first = lambda mask: jnp.min(jnp.where(mask, lane, big), axis=-1, keepdims=True)
    is_g = (lane >= N_EXPERTS) & (lane < N_EXPERTS + N_GROUPS)
    gl = jnp.where(is_g, logits, ninf)
    gmax = jnp.max(gl, axis=-1, keepdims=True)
    g_idx = first(gl == gmax) - N_EXPERTS
    p_group = 1.0 / jnp.sum(jnp.exp(gl - gmax), axis=-1, keepdims=True)
    in_g = (lane < N_EXPERTS) & ((lane // EXPERTS_PER_GROUP) == g_idx)
    el = jnp.where(in_g, logits, ninf)
    v1 = jnp.max(el, axis=-1, keepdims=True)
    i1 = first(el == v1)
    el2 = jnp.where(lane == i1, ninf, el)
    v2 = jnp.max(el2, axis=-1, keepdims=True)
    i2 = first(el2 == v2)
    e2 = jnp.exp(v2 - v1)
    p1 = 1.0 / (1.0 + e2)
    p2 = e2 / (1.0 + e2)
    sel = lambda n, val: jnp.where(lane == n, val, 0.0)
    return (sel(ROUTE_ID0, i1.astype(F32)) + sel(ROUTE_ID0 + 1, i2.astype(F32))
            + sel(ROUTE_W0, p_group * p1) + sel(ROUTE_W0 + 1, p_group * p2))


def _postmix_kernel(x_ref, a_ref, hy_ref, ag_ref, hg_ref, woa_ref, woh_ref, cg_ref, wmq_ref,
                    mk_ref, mv_ref, wmo_ref, fg_ref, wr_ref, br_ref, x_out, hn_out, route_out):
    ra = _rms(a_ref[...], ag_ref[...]).astype(BF16)
    rh = _rms(hy_ref[...], hg_ref[...]).astype(BF16)
    x = x_ref[...] + _dot(ra, woa_ref[...]) + _dot(rh, woh_ref[...])
    q = _dot(_rms(x, cg_ref[...]).astype(BF16), wmq_ref[...])
    dh = q.shape[1] // MEM_HEADS
    outs = []
    for h in range(MEM_HEADS):
        sl = slice(h * dh, (h + 1) * dh)
        s = lax.dot_general(q[:, sl].astype(BF16), mk_ref[0, :, sl], (((1,), (1,)), ((), ())),
                            preferred_element_type=F32) * dh ** -0.5
        p = jnp.exp(s - jnp.max(s, axis=-1, keepdims=True))
        l = jnp.sum(p, axis=-1, keepdims=True)
        outs.append(_dot(p.astype(BF16), mv_ref[0, :, sl]) / l)
    o = jnp.concatenate(outs, axis=1).astype(BF16)
    x = x + _dot(o, wmo_ref[...])
    x_out[...] = x
    hn = _rms(x, fg_ref[...])
    _store_row_tiles(hn_out, hn)
    route_out[...] = _route(_dot(hn.astype(BF16), wr_ref[...]) + br_ref[...])


def _postmix(x2d, a2d, hy2d, seq, ag, hg, w_out, cg, w_mq, mk, mv, w_mo, fg, w_rg, b_rg, w_re, b_re, tm=512):
    T, D = x2d.shape
    ca = a2d.shape[1]
    woa = w_out[:ca].astype(BF16)
    woh = w_out[ca:].astype(BF16)
    pad = ROUTE_LANES - N_EXPERTS - N_GROUPS
    wr = jnp.concatenate([w_re, w_rg, jnp.zeros((D, pad), F32)], 1).astype(BF16)
    br = jnp.concatenate([b_re, b_rg, jnp.zeros((pad,), F32)]).reshape(1, ROUTE_LANES)
    nseq = seq // tm
    full = lambda a: pl.BlockSpec(a.shape, lambda i: (0,) * a.ndim)
    row = lambda n: pl.BlockSpec((tm, n), lambda i: (i, 0))
    memb = pl.BlockSpec((1,) + mk.shape[1:], lambda i: (i // nseq, 0, 0))
    args = [x2d, a2d, hy2d, ag.reshape(1, -1), hg.reshape(1, -1), woa, woh, cg.reshape(1, D),
            w_mq.astype(BF16), mk, mv, w_mo.astype(BF16), fg.reshape(1, D), wr, br]
    in_specs = [row(D), row(ca), row(hy2d.shape[1])] + [full(a) for a in args[3:9]] + [memb, memb] \
        + [full(a) for a in args[11:]]
    return pl.pallas_call(
        _postmix_kernel,
        grid=(T // tm,),
        in_specs=in_specs,
        out_specs=[row(D), pl.BlockSpec((tm, SUBLANES, LANES), lambda i: (i, 0, 0)), row(ROUTE_LANES)],
        out_shape=[jax.ShapeDtypeStruct((T, D), F32), jax.ShapeDtypeStruct((T, SUBLANES, LANES), F32),
                   jax.ShapeDtypeStruct((T, ROUTE_LANES), F32)],
        compiler_params=_cparams("parallel"),
        name="postmix",
    )(*args)


def _rank_kernel(route_ref, rank_out, cnt_out, carry_ref):
    @pl.when(pl.program_id(0) == 0)
    def _():
        carry_ref[...] = jnp.zeros_like(carry_ref)

    route = route_ref[...]
    tr = route.shape[0]
    lane = lax.broadcasted_iota(jnp.int32, route.shape, 1).astype(F32)
    oh1 = lane == route[:, ROUTE_ID0:ROUTE_ID0 + 1]
    oh2 = lane == route[:, ROUTE_ID0 + 1:ROUTE_ID0 + 2]
    cnt = jnp.where(oh1 | oh2, 1.0, 0.0)
    r = lax.broadcasted_iota(jnp.int32, (tr, tr), 0)
    c = lax.broadcasted_iota(jnp.int32, (tr, tr), 1)
    below = jnp.where(c < r, 1.0, 0.0).astype(BF16)
    cum = _dot(below, cnt.astype(BF16)) + carry_ref[...]
    r1 = jnp.sum(jnp.where(oh1, cum, 0.0), axis=-1, keepdims=True)
    r2 = jnp.sum(jnp.where(oh2, cum, 0.0), axis=-1, keepdims=True)
    rank_out[...] = jnp.where(lane == 0, r1, 0.0) + jnp.where(lane == 1, r2, 0.0)
    carry_ref[...] += jnp.sum(cnt, axis=0, keepdims=True)
    cnt_out[...] = carry_ref[...]


def _pos_kernel(route_ref, rank_ref, base_ref, pos_out):
    route = route_ref[...]
    rank = rank_ref[...]
    lane = lax.broadcasted_iota(jnp.int32, route.shape, 1).astype(F32)
    base = base_ref[...]
    pick = lambda k: jnp.sum(jnp.where(lane == route[:, ROUTE_ID0 + k:ROUTE_ID0 + k + 1], base, 0.0),
                             axis=-1, keepdims=True) + rank[:, k:k + 1]
    pos_out[...] = (jnp.where(lane == 0, pick(0), 0.0) + jnp.where(lane == 1, pick(1), 0.0)).astype(jnp.int32)


def _dispatch_kernel(base_ref, cp_ref, pos_ref, hn_ref, xs_hbm, zero_ref, sem, *, tt, tmm):
    i = pl.program_id(0)
    row_copy = lambda src, dst: pltpu.make_async_copy(
        hn_ref.at[pl.ds(src, 1)], xs_hbm.at[pl.ds(dst, 1)], sem)

    @pl.when(i == 0)
    def _():
        zero_ref[...] = jnp.zeros_like(zero_ref)
        pad_copy = lambda e: pltpu.make_async_copy(
            zero_ref, xs_hbm.at[pl.ds(base_ref[e] + cp_ref[e] - tmm, tmm)], sem)
        for e in range(N_EXPERTS):
            @pl.when(cp_ref[e] > 0)
            def _():
                pad_copy(e).start()
        for e in range(N_EXPERTS):
            @pl.when(cp_ref[e] > 0)
            def _():
                pad_copy(e).wait()
        last = N_EXPERTS - 1
        tail_copy = lambda r: pltpu.make_async_copy(zero_ref, xs_hbm.at[pl.ds(r * tmm, tmm)], sem)
        first_free = (base_ref[last] + cp_ref[last]) // tmm
        n_tiles = xs_hbm.shape[0] // tmm
        lax.fori_loop(first_free, n_tiles, lambda r, c: (tail_copy(r).start(), c)[1], 0)
        lax.fori_loop(first_free, n_tiles, lambda r, c: (tail_copy(r).wait(), c)[1], 0)

    def start(t, carry):
        row_copy(t, pos_ref[2 * t]).start(priority=0)
        row_copy(t, pos_ref[2 * t + 1]).start(priority=1)
        return carry

    lax.fori_loop(0, tt, start, 0, unroll=DMA_UNROLL)

    def wait(t, carry):
        row_copy(0, 0).wait()
        row_copy(0, 0).wait()
        return carry

    lax.fori_loop(0, tt, wait, 0, unroll=DMA_UNROLL)


def _ffn_kernel(te_ref, nu_ref, xs_ref, wg_ref, wu_ref, wd_ref, ys_ref):
    used = pl.program_id(0) < nu_ref[0]

    @pl.when(used)
    def _():
        x = _load_row_tiles(xs_ref).astype(BF16)
        a = _dot(x, wg_ref[0].astype(BF16))
        b = _dot(x, wu_ref[0].astype(BF16))
        m = (a * jax.nn.sigmoid(a)) * b
        _store_row_tiles(ys_ref, _dot(m.astype(BF16), wd_ref[0].astype(BF16)))

    @pl.when(jnp.logical_not(used))
    def _():
        ys_ref[...] = jnp.zeros_like(ys_ref)


def _combine_kernel(pos_ref, pos_next_ref, ys_hbm, x_ref, route_ref, fg_ref, o_ref, buf_ref, sem, *, tc):
    i = pl.program_id(0)
    slot = i % 2
    row_copy = lambda s, k, t, p: pltpu.make_async_copy(
        ys_hbm.at[pl.ds(p, 1)], buf_ref.at[s, k, pl.ds(t, 1)], sem.at[s])

    def fetch(p_ref, s):
        def start(t, carry):
            row_copy(s, 0, t, p_ref[2 * t]).start(priority=0)
            row_copy(s, 1, t, p_ref[2 * t + 1]).start(priority=1)
            return carry

        lax.fori_loop(0, tc, start, 0, unroll=DMA_UNROLL)

    @pl.when(i == 0)
    def _():
        fetch(pos_ref, 0)

    @pl.when(i + 1 < pl.num_programs(0))
    def _():
        fetch(pos_next_ref, 1 - slot)

    def wait(t, carry):
        row_copy(slot, 0, 0, 0).wait()
        row_copy(slot, 1, 0, 0).wait()
        return carry

    lax.fori_loop(0, tc, wait, 0, unroll=DMA_UNROLL)
    route = route_ref[...]
    y = (x_ref[...] + route[:, ROUTE_W0:ROUTE_W0 + 1] * _load_row_tiles(buf_ref, (slot, 0))
         + route[:, ROUTE_W0 + 1:ROUTE_W0 + 2] * _load_row_tiles(buf_ref, (slot, 1)))
    o_ref[...] = _rms(y, fg_ref[...])


def _moe(hn, route, x2d, w_gate, w_up, w_down, fg, tr=512, tt=512, tc=256):
    T, D = x2d.shape
    E = N_EXPERTS
    F = w_gate.shape[-1]
    tmm = MOE_ROW_TILE
    row = lambda tm, n: pl.BlockSpec((tm, n), lambda i: (i, 0))
    rank, counts = pl.pallas_call(
        _rank_kernel,
        grid=(T // tr,),
        in_specs=[row(tr, ROUTE_LANES)],
        out_specs=[row(tr, ROUTE_LANES), pl.BlockSpec((1, ROUTE_LANES), lambda i: (0, 0))],
        out_shape=[jax.ShapeDtypeStruct((T, ROUTE_LANES), F32), jax.ShapeDtypeStruct((1, ROUTE_LANES), F32)],
        scratch_shapes=[pltpu.VMEM((1, ROUTE_LANES), F32)],
        compiler_params=_cparams("arbitrary"),
        name="moe_rank",
    )(route)

    cnt = counts[0, :E].astype(jnp.int32)
    cp = ((cnt + tmm - 1) // tmm) * tmm
    ends = jnp.cumsum(cp)
    base = ends - cp
    n_used = ends[-1] // tmm
    n_tiles = (2 * T) // tmm + E
    tile_start = jnp.minimum(jnp.arange(n_tiles, dtype=jnp.int32), n_used - 1) * tmm
    tile_expert = jnp.minimum(jnp.sum((tile_start[:, None] >= ends[None, :]).astype(jnp.int32), axis=1), E - 1)
    base_lanes = jnp.zeros((1, ROUTE_LANES), F32).at[0, :E].set(base.astype(F32))

    pos = pl.pallas_call(
        _pos_kernel,
        grid=(T // tr,),
        in_specs=[row(tr, ROUTE_LANES), row(tr, ROUTE_LANES), pl.BlockSpec((1, ROUTE_LANES), lambda i: (0, 0))],
        out_specs=row(tr, ROUTE_LANES),
        out_shape=jax.ShapeDtypeStruct((T, ROUTE_LANES), jnp.int32),
        compiler_params=_cparams("parallel"),
        name="moe_pos",
    )(route, rank, base_lanes)
    pos = pos[:, :2].reshape(2 * T)

    xs = pl.pallas_call(
        functools.partial(_dispatch_kernel, tt=tt, tmm=tmm),
        grid_spec=pltpu.PrefetchScalarGridSpec(
            num_scalar_prefetch=2,
            grid=(T // tt,),
            in_specs=[pl.BlockSpec((2 * tt,), lambda i, b, c: (i,), memory_space=pltpu.SMEM),
                      pl.BlockSpec((tt, SUBLANES, LANES), lambda i, b, c: (i, 0, 0))],
            out_specs=pl.BlockSpec(memory_space=pl.ANY),
            scratch_shapes=[pltpu.VMEM((tmm, SUBLANES, LANES), F32), pltpu.SemaphoreType.DMA(())]),
        out_shape=jax.ShapeDtypeStruct((n_tiles * tmm, SUBLANES, LANES), F32),
        compiler_params=_cparams("arbitrary"),
        name="moe_dispatch",
    )(base, cp, pos, hn)

    tile = lambda r, te, nu: (jnp.minimum(r, nu[0] - 1), 0, 0)
    ys = pl.pallas_call(
        _ffn_kernel,
        grid_spec=pltpu.PrefetchScalarGridSpec(
            num_scalar_prefetch=2,
            grid=(n_tiles,),
            in_specs=[pl.BlockSpec((tmm, SUBLANES, LANES), tile),
                      pl.BlockSpec((1, D, F), lambda r, te, nu: (te[r], 0, 0)),
                      pl.BlockSpec((1, D, F), lambda r, te, nu: (te[r], 0, 0)),
                      pl.BlockSpec((1, F, D), lambda r, te, nu: (te[r], 0, 0))],
            out_specs=pl.BlockSpec((tmm, SUBLANES, LANES), lambda r, te, nu: (r, 0, 0))),
        out_shape=jax.ShapeDtypeStruct((n_tiles * tmm, SUBLANES, LANES), F32),
        compiler_params=_cparams("arbitrary"),
        name="moe_ffn",
    )(tile_expert, n_used.reshape(1), xs, w_gate.reshape(E, D, F), w_up.reshape(E, D, F),
      w_down.reshape(E, F, D))

    return pl.pallas_call(
        functools.partial(_combine_kernel, tc=tc),
        grid=(T // tc,),
        in_specs=[pl.BlockSpec((2 * tc,), lambda i: (i,), memory_space=pltpu.SMEM),
                  pl.BlockSpec((2 * tc,), lambda i: (jnp.minimum(i + 1, T // tc - 1),),
                               memory_space=pltpu.SMEM),
                  pl.BlockSpec(memory_space=pl.ANY),
                  row(tc, D), row(tc, ROUTE_LANES), pl.BlockSpec((1, D), lambda i: (0, 0))],
        out_specs=row(tc, D),
        out_shape=jax.ShapeDtypeStruct((T, D), F32),
        scratch_shapes=[pltpu.VMEM((2, 2, tc, SUBLANES, LANES), F32), pltpu.SemaphoreType.DMA((2,))],
        compiler_params=_cparams("arbitrary"),
        name="moe_combine",
    )(pos, pos, ys, x2d, route, fg.reshape(1, D))


def kernel(x, mem, mix_norm_g, w_in, q_norm_g, kv_norm_g, w_uq, w_ukv, hy_conv_w, hy_conv_b, hy_w1, hy_b1, hy_freq, hy_w2, hy_b2, hy_w3, hy_b3, hy_decay, hy_skip, attn_out_g, hy_out_g, w_out, cross_norm_g, mem_norm_g, w_mq, w_mkv, w_mo, ffn_norm_g, w_route_group, b_route_group, w_route_expert, b_route_expert, w_gate, w_up, w_down, final_norm_g):
    B, S, D = x.shape
    depth = w_in.shape[0]
    consts = _dft_constants(S)
    xf = x.reshape(B * S, D)
    for l in range(depth):
        q, k, v, hx1, hx2, hv = _inproj(xf, S, mix_norm_g[l], w_in[l], q_norm_g[l], kv_norm_g[l],
                                        w_uq[l], w_ukv[l], hy_conv_w[l], hy_conv_b[l])
        HP = q.shape[1]
        a_out = _attention(q.reshape(B, S, HP), k.reshape(B, S, HP), v.reshape(B, S, HP))
        filt = _hyena_filter_time(S, hy_w1[l], hy_b1[l], hy_freq[l], hy_w2[l], hy_b2[l], hy_w3[l],
                                  hy_b3[l], hy_decay[l])
        kf = _hyena_filter_spectrum(filt, consts[1], consts[3])
        C = hv.shape[1]
        h_out = _hyena(hx1.reshape(B, S, C), hx2.reshape(B, S, C), hv.reshape(B, S, C),
                       hy_skip[l], kf, consts)
        mk, mv = _memkv(mem, mem_norm_g[l], w_mkv[l])
        x2, hn, route = _postmix(xf, a_out.reshape(B * S, -1), h_out.reshape(B * S, C), S,
                                attn_out_g[l], hy_out_g[l], w_out[l], cross_norm_g[l], w_mq[l], mk, mv,
                                w_mo[l], ffn_norm_g[l], w_route_group[l], b_route_group[l],
                                w_route_expert[l], b_route_expert[l])
        assert depth == 1
        xf = _moe(hn, route, x2, w_gate[l], w_up[l], w_down[l], final_norm_g)
    return xf.reshape(B, S, D)
```

```python
import functools
import math

import numpy as np
import jax
import jax.numpy as jnp
from jax import lax
from jax.experimental import pallas as pl
from jax.experimental.pallas import tpu as pltpu

F32 = jnp.float32
BF16 = jnp.bfloat16

EPS = 1e-6
MLA_HEADS = 8
MLA_NOPE = 64
MLA_ROPE = 32
MLA_V = 64
ROPE_BASE = 10000.0
HEAD_PAD = 128
HY_ORDER = 2
HY_DIRS = 2
HY_BANDS = 16
MEM_HEADS = 4
N_GROUPS = 4
EXPERTS_PER_GROUP = 8
N_EXPERTS = N_GROUPS * EXPERTS_PER_GROUP
ROUTE_LANES = 128
ROUTE_ID0 = 0
ROUTE_W0 = 2
MOE_ROW_TILE = 512
DMA_UNROLL = 8
ROW_SUBTILES = 1

FFT_N1 = 64
FFT_N2 = 128
DFT_K1_BLOCK = 2
DFT_N2_BLOCK = 32
DFT_PITCH_PAD = 8

VMEM_LIMIT = 56 * 1024 * 1024


def _cparams(*sem):
    return pltpu.CompilerParams(dimension_semantics=sem, vmem_limit_bytes=VMEM_LIMIT)


def _rms(x, g):
    return x * lax.rsqrt(jnp.mean(x * x, axis=-1, keepdims=True) + EPS) * g


def _dot(a, b):
    return jnp.dot(a, b, preferred_element_type=F32)


SUBLANES = 8
LANES = 128


def _load_row_tiles(ref, lead=()):
    rows = ref.shape[-3]
    flat = ref.reshape(*ref.shape[:-3], rows * SUBLANES, LANES)
    return jnp.concatenate(
        [flat[(*lead, pl.ds(j, rows, stride=SUBLANES), slice(None))] for j in range(SUBLANES)], axis=1)


def _store_row_tiles(ref, val):
    rows = ref.shape[0]
    flat = ref.reshape(rows * SUBLANES, LANES)
    for j in range(SUBLANES):
        flat[pl.ds(j, rows, stride=SUBLANES), :] = val[:, j * LANES:(j + 1) * LANES]


def _inproj_kernel(x_ref, xp_ref, xn_ref, g_ref, wq_ref, wkv_ref, wkra_ref, wkrb_ref, why_ref, qg_ref,
                   kvg_ref, wqa_ref, wqb_ref, wka_ref, wv_ref, tab_ref, cw_ref, cb_ref,
                   q_out, k_out, v_out, x1_out, x2_out, hv_out, *, nseq):
    tm = x_ref.shape[0]
    halo = xp_ref.shape[0]
    ts = tm // ROW_SUBTILES
    x_ext = jnp.concatenate([xp_ref[...], x_ref[...], xn_ref[...]], axis=0)
    i = pl.program_id(0) % nseq
    c = x1_out.shape[1]
    tile = lambda t: jnp.concatenate([t] * MLA_HEADS, axis=1)
    lane = lax.broadcasted_iota(jnp.int32, (1, v_out.shape[1]), 1) % HEAD_PAD
    for s in range(ROW_SUBTILES):
        rows = slice(s * ts, (s + 1) * ts)
        hf = _rms(x_ext[s * ts:(s + 1) * ts + 2 * halo], g_ref[...])
        h = hf[halo:halo + ts].astype(BF16)
        qn = _rms(_dot(h, wq_ref[...]), qg_ref[...]).astype(BF16)
        kvn = _rms(_dot(h, wkv_ref[...]), kvg_ref[...]).astype(BF16)
        tab = tab_ref[rows, :]
        cq, sq, ck, sk = (tab[:, j * HEAD_PAD:(j + 1) * HEAD_PAD] for j in range(4))
        q = _dot(qn, wqa_ref[...]) * tile(cq) + _dot(qn, wqb_ref[...]) * tile(sq)
        q_out[rows, :] = q.astype(BF16)
        kr = _dot(h, wkra_ref[...]) * ck + _dot(h, wkrb_ref[...]) * sk
        k_out[rows, :] = (_dot(kvn, wka_ref[...]) + tile(kr)).astype(BF16)
        v_out[rows, :] = (_dot(kvn, wv_ref[...]) + jnp.where(lane == MLA_V, 1.0, 0.0)).astype(BF16)
        hy = _dot(hf.astype(BF16), why_ref[...])
        row = lax.broadcasted_iota(jnp.int32, hy.shape, 0)
        outside = (row == halo - 1) & (i == 0) if s == 0 else None
        if s == ROW_SUBTILES - 1:
            last = (row == halo + ts) & (i == nseq - 1)
            outside = last if outside is None else outside | last
        if outside is not None:
            hy = jnp.where(outside, 0.0, hy)
        cw = cw_ref[...]
        u = (hy[halo - 1:halo - 1 + ts] * cw[0:1] + hy[halo:halo + ts] * cw[1:2]
             + hy[halo + 1:halo + 1 + ts] * cw[2:3] + cb_ref[...])
        x1_out[rows, :] = u[:, :c]
        x2_out[rows, :] = u[:, c:2 * c]
        hv_out[rows, :] = u[:, 2 * c:]


def _inproj(x2d, seq, mix_g, w_in, q_g, kv_g, w_uq, w_ukv, conv_w, conv_b, tm=512):
    T, D = x2d.shape
    per = tm // SUBLANES
    cb = conv_b.reshape(1, -1)
    q_rank, kv_rank = q_g.shape[0], kv_g.shape[0]
    off_kv = q_rank
    off_kr = off_kv + kv_rank
    off_hy = off_kr + MLA_ROPE
    C = (w_in.shape[1] - off_hy) // 3
    H = MLA_HEADS
    half = MLA_ROPE // 2
    wq = w_in[:, :off_kv].astype(BF16)
    wkv = w_in[:, off_kv:off_kr].astype(BF16)
    wkr = w_in[:, off_kr:off_hy]
    wkr_sw = jnp.concatenate([wkr[:, half:], wkr[:, :half]], axis=1)
    zpad = lambda n: jnp.zeros((D, n), F32)
    wkra = jnp.concatenate([zpad(MLA_NOPE), wkr, zpad(HEAD_PAD - MLA_NOPE - MLA_ROPE)], 1).astype(BF16)
    wkrb = jnp.concatenate([zpad(MLA_NOPE), wkr_sw, zpad(HEAD_PAD - MLA_NOPE - MLA_ROPE)], 1).astype(BF16)
    why = w_in[:, off_hy:].astype(BF16)

    uq = w_uq.reshape(q_rank, H, MLA_NOPE + MLA_ROPE)
    uq_n, uq_r = uq[..., :MLA_NOPE], uq[..., MLA_NOPE:]
    uq_rs = jnp.concatenate([uq_r[..., half:], uq_r[..., :half]], axis=-1)
    zq = lambda n: jnp.zeros((q_rank, H, n), F32)
    wqa = jnp.concatenate([uq_n, uq_r, zq(HEAD_PAD - MLA_NOPE - MLA_ROPE)], -1).reshape(q_rank, H * HEAD_PAD).astype(BF16)
    wqb = jnp.concatenate([zq(MLA_NOPE), uq_rs, zq(HEAD_PAD - MLA_NOPE - MLA_ROPE)], -1).reshape(q_rank, H * HEAD_PAD).astype(BF16)
    ukv = w_ukv.reshape(kv_rank, H, MLA_NOPE + MLA_V)
    zk = lambda n: jnp.zeros((kv_rank, H, n), F32)
    wka = jnp.concatenate([ukv[..., :MLA_NOPE], zk(HEAD_PAD - MLA_NOPE)], -1).reshape(kv_rank, H * HEAD_PAD).astype(BF16)
    wv = jnp.concatenate([ukv[..., MLA_NOPE:], zk(HEAD_PAD - MLA_V)], -1).reshape(kv_rank, H * HEAD_PAD).astype(BF16)

    ang = np.arange(seq)[:, None] * ROPE_BASE ** (-np.arange(half) / half)[None, :]
    cos2 = np.concatenate([np.cos(ang), np.cos(ang)], 1)
    sin2 = np.concatenate([-np.sin(ang), np.sin(ang)], 1)
    zs = lambda n: np.zeros((seq, n))
    scale = (MLA_NOPE + MLA_ROPE) ** -0.5 * math.log2(math.e)
    rest = HEAD_PAD - MLA_NOPE - MLA_ROPE
    cq = scale * np.concatenate([np.ones((seq, MLA_NOPE)), cos2, zs(rest)], 1)
    sq = scale * np.concatenate([zs(MLA_NOPE), sin2, zs(rest)], 1)
    ck = np.concatenate([zs(MLA_NOPE), cos2, zs(rest)], 1)
    sk = np.concatenate([zs(MLA_NOPE), sin2, zs(rest)], 1)
    tab = jnp.asarray(np.concatenate([cq, sq, ck, sk], 1), dtype=F32)

    nseq = seq // tm
    full = lambda a: pl.BlockSpec(a.shape, lambda i: (0,) * a.ndim)
    row = lambda n: pl.BlockSpec((tm, n), lambda i: (i, 0))
    consts = [mix_g.reshape(1, D), wq, wkv, wkra, wkrb, why, q_g.reshape(1, -1), kv_g.reshape(1, -1),
              wqa, wqb, wka, wv]
    HP = H * HEAD_PAD
    return pl.pallas_call(
        functools.partial(_inproj_kernel, nseq=nseq),
        grid=(T // tm,),
        in_specs=[row(D),
                  pl.BlockSpec((SUBLANES, D), lambda i: (jnp.maximum(i * per - 1, 0), 0)),
                  pl.BlockSpec((SUBLANES, D), lambda i: (jnp.minimum((i + 1) * per, T // SUBLANES - 1), 0))]
        + [full(a) for a in consts]
        + [pl.BlockSpec((tm, 4 * HEAD_PAD), lambda i: (i % nseq, 0)), full(conv_w), full(cb)],
        out_specs=[row(HP), row(HP), row(HP), row(C), row(C), row(C)],
        out_shape=[jax.ShapeDtypeStruct((T, HP), BF16)] * 3 + [jax.ShapeDtypeStruct((T, C), F32)] * 3,
        compiler_params=_cparams("parallel"),
        name="inproj",
    )(x2d, x2d, x2d, *consts, tab, conv_w, cb)


def _attn_kernel(q_ref, k_ref, v_ref, o_ref):
    outs = []
    for h in range(MLA_HEADS):
        sl = slice(h * HEAD_PAD, (h + 1) * HEAD_PAD)
        s = lax.dot_general(q_ref[0, :, sl], k_ref[0, :, sl], (((1,), (1,)), ((), ())),
                            preferred_element_type=F32).astype(BF16)
        p = jnp.exp2(s - jnp.max(s, axis=-1, keepdims=True))
        o = _dot(p, v_ref[0, :, sl])
        outs.append(o[:, :MLA_V] / o[:, MLA_V:MLA_V + 1])
    o_ref[0] = jnp.concatenate(outs, axis=1)


def _attention(q, k, v, tq=512):
    B, S, HP = q.shape
    return pl.pallas_call(
        _attn_kernel,
        grid=(B, S // tq),
        in_specs=[pl.BlockSpec((1, tq, HP), lambda b, i: (b, i, 0)),
                  pl.BlockSpec((1, S, HP), lambda b, i: (b, 0, 0)),
                  pl.BlockSpec((1, S, HP), lambda b, i: (b, 0, 0))],
        out_specs=pl.BlockSpec((1, tq, MLA_HEADS * MLA_V), lambda b, i: (b, i, 0)),
        out_shape=jax.ShapeDtypeStruct((B, S, MLA_HEADS * MLA_V), F32),
        compiler_params=_cparams("parallel", "arbitrary"),
        name="mla_attention",
    )(q, k, v)


def _dft_constants(seq):
    n = 2 * seq
    n1, n2 = FFT_N1, FFT_N2
    assert n1 * n2 == n
    r1 = np.arange(n1)
    r2 = np.arange(n2)
    blk = lambda z: np.block([[z.real, -z.imag], [z.imag, z.real]])
    w1 = np.exp(-2j * np.pi * np.outer(r1, r1) / n1)
    fa_data = blk(w1[:, :n1 // 2])
    fa_filt = np.concatenate([w1.real, w1.imag], axis=0)
    fc = blk(np.conj(w1).T[:n1 // 2, :])
    w2 = np.exp(-2j * np.pi * np.outer(r2, r2) / n2)
    tw = np.exp(-2j * np.pi * np.outer(r1, r2) / n)
    fb = np.stack([blk(w2 * tw[k][None, :]) for k in range(n1)])
    fbi = np.stack([blk(np.conj(w2).T * np.conj(tw[k])[:, None] / n) for k in range(n1)])
    as_bf = lambda a: jnp.asarray(a, dtype=F32).astype(BF16)
    return as_bf(fa_data), as_bf(fa_filt), as_bf(fc), as_bf(fb), as_bf(fbi)


def _filter_kernel(z_ref, w1_ref, b1_ref, fr_ref, w2_ref, b2_ref, w3_ref, b3_ref, dec_ref, o_ref, *, seq, tr):
    hp = lax.Precision.HIGHEST
    z = z_ref[...]
    fr = fr_ref[...]
    h = jnp.sin(fr[0:1] * (jnp.dot(z, w1_ref[...], precision=hp, preferred_element_type=F32) + b1_ref[...]))
    h = jnp.sin(fr[1:2] * (jnp.dot(h, w2_ref[...], precision=hp, preferred_element_type=F32) + b2_ref[...]))
    split = lambda a: (a.astype(BF16), (a - a.astype(BF16).astype(F32)).astype(BF16))
    (hh, hl), (wh, wl) = split(h), split(w3_ref[0])
    h = _dot(hh, wh) + _dot(hh, wl) + _dot(hl, wh) + b3_ref[0]
    h = h * jnp.exp(-z[:, 0:1] * jnp.abs(dec_ref[0]))
    n = pl.program_id(0) * tr + lax.broadcasted_iota(jnp.int32, h.shape, 0)
    o_ref[...] = jnp.where(n == seq, 0.0, h)


def _hyena_filter_time(seq, w1, b1, freq, w2, b2, w3, b3, decay, tr=512):
    n = 2 * seq
    emb, ffn = w1.shape
    C = w3.shape[1] // (HY_ORDER * HY_DIRS)
    off = np.arange(n)
    t = np.where(off < seq, off, n - off).astype(np.float64)
    bands = np.linspace(1e-4, HY_BANDS - 1, HY_BANDS)
    ang = 2.0 * math.pi * t[:, None] * bands[None, :] / seq
    z = np.concatenate([(t / seq)[:, None], np.cos(ang), -np.sin(ang)], axis=-1)
    zl = LANES
    z = jnp.asarray(np.pad(z, ((0, 0), (0, zl - emb))), dtype=F32)
    w1p = jnp.pad(w1, ((0, zl - emb), (0, 0)))
    by_dir = lambda a: jnp.moveaxis(a.reshape(a.shape[0], HY_ORDER, HY_DIRS, C), 2, 0).reshape(
        HY_DIRS, a.shape[0], HY_ORDER * C)
    w3d, b3d, decd = by_dir(w3), by_dir(b3.reshape(1, -1)), by_dir(decay.reshape(1, -1))
    full = lambda a: pl.BlockSpec(a.shape, lambda i: (0,) * a.ndim)
    ndir = lambda a: pl.BlockSpec((1,) + a.shape[1:], lambda i: ((i * tr) // seq, 0, 0))
    consts = [w1p, b1.reshape(1, -1), freq, w2, b2.reshape(1, -1)]
    return pl.pallas_call(
        functools.partial(_filter_kernel, seq=seq, tr=tr),
        grid=(n // tr,),
        in_specs=[pl.BlockSpec((tr, zl), lambda i: (i, 0))] + [full(a) for a in consts]
        + [ndir(w3d), ndir(b3d), ndir(decd)],
        out_specs=pl.BlockSpec((tr, HY_ORDER * C), lambda i: (i, 0)),
        out_shape=jax.ShapeDtypeStruct((n, HY_ORDER * C), F32),
        compiler_params=_cparams("parallel"),
        name="hyena_filter_mlp",
    )(z, *consts, w3d, b3d, decd)


def _pitched(rows, nb):
    return pltpu.VMEM((rows, nb + DFT_PITCH_PAD, LANES), F32)


def _block_rows(ref):
    return math.prod(ref.shape[:-2]), ref.shape[-2]


def _copy_in(ref, scr):
    rows, nb = _block_rows(ref)
    scr[:, :nb, :] = ref[...].reshape(rows, nb, LANES)


def _copy_out(scr, ref):
    rows, nb = _block_rows(ref)
    ref[...] = scr[:, :nb, :].reshape(ref.shape)


def _at_n2(scr, n):
    rows, pitch, _ = scr.shape
    return scr.reshape(rows * pitch, LANES).at[pl.ds(n, rows, stride=pitch), :]


def _outer_dft(mat_ref, x_ref, o_ref, xs, os):
    _copy_in(x_ref, xs)
    for n in range(x_ref.shape[-2]):
        _at_n2(os, n)[...] = _dot(mat_ref[...], _at_n2(xs, n)[...].astype(BF16))
    _copy_out(os, o_ref)


def _filter_stage_a_kernel(x_ref, fa_ref, o_ref, xs, os):
    _outer_dft(fa_ref, x_ref, o_ref, xs, os)


def _filter_stage_b_kernel(x_ref, fb_ref, o_ref):
    for kk in range(x_ref.shape[1]):
        x = jnp.concatenate([x_ref[0, kk], x_ref[1, kk]], axis=0).astype(BF16)
        o_ref[kk] = _dot(fb_ref[kk], x)


def _hyena_filter_spectrum(filt, fa_filt, fb, nb=DFT_N2_BLOCK):
    n, oc = filt.shape
    a = pl.pallas_call(
        _filter_stage_a_kernel,
        grid=(FFT_N2 // nb, oc // LANES),
        in_specs=[pl.BlockSpec((FFT_N1, nb, LANES), lambda j, c: (0, j, c)),
                  pl.BlockSpec(fa_filt.shape, lambda j, c: (0, 0))],
        out_specs=pl.BlockSpec((2, FFT_N1, nb, LANES), lambda j, c: (0, 0, j, c)),
        out_shape=jax.ShapeDtypeStruct((2, FFT_N1, FFT_N2, oc), F32),
        scratch_shapes=[_pitched(FFT_N1, nb), _pitched(2 * FFT_N1, nb)],
        compiler_params=_cparams("parallel", "parallel"),
        name="hyena_filter_dft_a",
    )(filt.reshape(FFT_N1, FFT_N2, oc), fa_filt)
    return pl.pallas_call(
        _filter_stage_b_kernel,
        grid=(FFT_N1 // DFT_K1_BLOCK,),
        in_specs=[pl.BlockSpec((2, DFT_K1_BLOCK, FFT_N2, oc), lambda k: (0, k, 0, 0)),
                  pl.BlockSpec((DFT_K1_BLOCK, 2 * FFT_N2, 2 * FFT_N2), lambda k: (k, 0, 0))],
        out_specs=pl.BlockSpec((DFT_K1_BLOCK, 2 * FFT_N2, oc), lambda k: (k, 0, 0)),
        out_shape=jax.ShapeDtypeStruct((FFT_N1, 2 * FFT_N2, oc), F32),
        compiler_params=_cparams("parallel"),
        name="hyena_filter_dft_b",
    )(a, fb)


def _stage_a_kernel(x_ref, fa_ref, o_ref, xs, os):
    _outer_dft(fa_ref, x_ref, o_ref, xs, os)


def _stage_a(x4, fa, nb):
    B, r, n2, C = x4.shape
    return pl.pallas_call(
        _stage_a_kernel,
        grid=(B // 2, n2 // nb, C // LANES),
        in_specs=[pl.BlockSpec((2, r, nb, LANES), lambda p, j, c: (p, 0, j, c)),
                  pl.BlockSpec(fa.shape, lambda p, j, c: (0, 0))],
        out_specs=pl.BlockSpec((1, 2, FFT_N1, nb, LANES), lambda p, j, c: (p, 0, 0, j, c)),
        out_shape=jax.ShapeDtypeStruct((B // 2, 2, FFT_N1, n2, C), F32),
        scratch_shapes=[_pitched(2 * r, nb), _pitched(2 * FFT_N1, nb)],
        compiler_params=_cparams("parallel", "parallel", "parallel"),
        name="hyena_dft_a",
    )(x4, fa)


def _stage_b_kernel(x_ref, fb_ref, kf_ref, fbi_ref, o_ref):
    npair = x_ref.shape[0]
    n2 = x_ref.shape[3]
    c = x_ref.shape[4]
    for kk in range(x_ref.shape[2]):
        x = jnp.concatenate(
            [jnp.concatenate([x_ref[p, 0, kk], x_ref[p, 1, kk]], axis=0) for p in range(npair)],
            axis=1).astype(BF16)
        g = _dot(fb_ref[kk], x)
        gr, gi = g[:n2], g[n2:]
        kf = kf_ref[kk]
        kr = jnp.concatenate([kf[:n2]] * npair, axis=1)
        ki = jnp.concatenate([kf[n2:]] * npair, axis=1)
        hcat = jnp.concatenate([gr * kr - gi * ki, gr * ki + gi * kr], axis=0).astype(BF16)
        y = _dot(fbi_ref[kk], hcat)
        for p in range(npair):
            o_ref[p, 0, kk] = y[:n2, p * c:(p + 1) * c].astype(o_ref.dtype)
            o_ref[p, 1, kk] = y[n2:, p * c:(p + 1) * c].astype(o_ref.dtype)


def _stage_b(spec, fb, kf, fbi, order, nk=DFT_K1_BLOCK):
    npair, _, _, _, C = spec.shape
    blk = pl.BlockSpec((npair, 2, nk, FFT_N2, C), lambda k: (0, 0, k, 0, 0))
    mat = pl.BlockSpec((nk, 2 * FFT_N2, 2 * FFT_N2), lambda k: (k, 0, 0))
    return pl.pallas_call(
        _stage_b_kernel,
        grid=(FFT_N1 // nk,),
        in_specs=[blk, mat, pl.BlockSpec((nk, 2 * FFT_N2, C), lambda k: (k, 0, order)), mat],
        out_specs=blk,
        out_shape=jax.ShapeDtypeStruct(spec.shape, F32),
        compiler_params=_cparams("parallel"),
        name="hyena_dft_b",
    )(spec, fb, kf, fbi)


def _stage_c_kernel(y_ref, fc_ref, gate_ref, z_ref, skip_ref, *rest, stage_a):
    if stage_a:
        fa_ref, z_out, a_out, ys, gs, zs, os, as_ = rest
    else:
        z_out, ys, gs, zs, os = rest
    nb = gate_ref.shape[-2]
    _copy_in(y_ref, ys)
    _copy_in(gate_ref, gs)
    _copy_in(z_ref, zs)
    skip = skip_ref[...]
    for n in range(nb):
        conv = _dot(fc_ref[...], _at_n2(ys, n)[...].astype(BF16))
        _at_n2(os, n)[...] = _at_n2(gs, n)[...] * (conv + skip * _at_n2(zs, n)[...])
    _copy_out(os, z_out)
    if stage_a:
        for n in range(nb):
            _at_n2(as_, n)[...] = _dot(fa_ref[...], _at_n2(os, n)[...].astype(BF16))
        _copy_out(as_, a_out)


def _stage_c(yspec, fc, gate, zin, skip, fa=None, nb=DFT_N2_BLOCK):
    B, r, n2, C = gate.shape
    dat = pl.BlockSpec((2, r, nb, LANES), lambda p, j, c: (p, 0, j, c))
    spc = pl.BlockSpec((1, 2, FFT_N1, nb, LANES), lambda p, j, c: (p, 0, 0, j, c))
    in_specs = [spc, pl.BlockSpec(fc.shape, lambda p, j, c: (0, 0)), dat, dat,
                pl.BlockSpec((1, LANES), lambda p, j, c: (0, c))]
    out_specs = [dat]
    out_shape = [jax.ShapeDtypeStruct(gate.shape, F32)]
    args = [yspec, fc, gate, zin, skip.reshape(1, C)]
    scratch = [_pitched(2 * FFT_N1, nb)] + [_pitched(2 * r, nb)] * 3
    if fa is not None:
        in_specs.append(pl.BlockSpec(fa.shape, lambda p, j, c: (0, 0)))
        out_specs.append(spc)
        out_shape.append(jax.ShapeDtypeStruct(yspec.shape, F32))
        args.append(fa)
        scratch.append(_pitched(2 * FFT_N1, nb))
    return pl.pallas_call(
        functools.partial(_stage_c_kernel, stage_a=fa is not None),
        grid=(B // 2, n2 // nb, C // LANES),
        in_specs=in_specs, out_specs=out_specs, out_shape=out_shape,
        scratch_shapes=scratch,
        compiler_params=_cparams("parallel", "parallel", "parallel"),
        name="hyena_dft_c",
    )(*args)


def _hyena(x1, x2, v, skip, kf, consts, nb=DFT_N2_BLOCK):
    fa_data, _, fc, fb, fbi = consts
    B, S, C = v.shape
    split = lambda a: a.reshape(B, S // FFT_N2, FFT_N2, C)
    a0 = _stage_a(split(v), fa_data, nb)
    y0 = _stage_b(a0, fb, kf, fbi, 0)
    z1, a1 = _stage_c(y0, fc, split(x1), split(v), skip[0], fa=fa_data, nb=nb)
    y1 = _stage_b(a1, fb, kf, fbi, 1)
    (out,) = _stage_c(y1, fc, split(x2), z1, skip[1], nb=nb)
    return out.reshape(B, S, C)


def _memkv_kernel(m_ref, g_ref, w_ref, k_out, v_out):
    hm = _rms(m_ref[0], g_ref[...]).astype(BF16)
    kv = _dot(hm, w_ref[...])
    d = k_out.shape[2]
    k_out[0] = kv[:, :d].astype(BF16)
    v_out[0] = kv[:, d:].astype(BF16)


def _memkv(mem, g, w_mkv):
    B, M, D = mem.shape
    dk = w_mkv.shape[1] // 2
    w = w_mkv.astype(BF16)
    return pl.pallas_call(
        _memkv_kernel,
        grid=(B,),
        in_specs=[pl.BlockSpec((1, M, D), lambda b: (b, 0, 0)),
                  pl.BlockSpec((1, D), lambda b: (0, 0)),
                  pl.BlockSpec(w.shape, lambda b: (0, 0))],
        out_specs=[pl.BlockSpec((1, M, dk), lambda b: (b, 0, 0))] * 2,
        out_shape=[jax.ShapeDtypeStruct((B, M, dk), BF16)] * 2,
        compiler_params=_cparams("parallel"),
        name="mem_kv",
    )(mem, g.reshape(1, D), w)


def _route(logits):
    lane = lax.broadcasted_iota(jnp.int32, logits.shape, 1)
    ninf = -jnp.inf
    big = ROUTE_LANES
    first = lambda mask: jnp.min(jnp.where(mask, lane, big), axis=-1, keepdims=True)
    is_g = (lane >= N_EXPERTS) & (lane < N_EXPERTS + N_GROUPS)
    gl = jnp.where(is_g, logits, ninf)
    gmax = jnp.max(gl, axis=-1, keepdims=True)
    g_idx = first(gl == gmax) - N_EXPERTS
    p_group = 1.0 / jnp.sum(jnp.exp(gl - gmax), axis=-1, keepdims=True)
    in_g = (lane < N_EXPERTS) & ((lane // EXPERTS_PER_GROUP) == g_idx)
    el = jnp.where(in_g, logits, ninf)
    v1 = jnp.max(el, axis=-1, keepdims=True)
    i1 = first(el == v1)
    el2 = jnp.where(lane == i1, ninf, el)
    v2 = jnp.max(el2, axis=-1, keepdims=True)
    i2 = first(el2 == v2)
    e2 = jnp.exp(v2 - v1)
    p1 = 1.0 / (1.0 + e2)
    p2 = e2 / (1.0 + e2)
    sel = lambda n, val: jnp.where(lane == n, val, 0.0)
    return (sel(ROUTE_ID0, i1.astype(F32)) + sel(ROUTE_ID0 + 1, i2.astype(F32))
            + sel(ROUTE_W0, p_group * p1) + sel(ROUTE_W0 + 1, p_group * p2))


def _postmix_kernel(x_ref, a_ref, hy_ref, ag_ref, hg_ref, woa_ref, woh_ref, cg_ref, wmq_ref,
                    mk_ref, mv_ref, wmo_ref, fg_ref, wr_ref, br_ref, x_out, hn_out, route_out):
    ts = x_ref.shape[0] // ROW_SUBTILES
    hn_flat = hn_out.reshape(hn_out.shape[0] * SUBLANES, LANES)
    for t in range(ROW_SUBTILES):
        rows = slice(t * ts, (t + 1) * ts)
        ra = _rms(a_ref[rows, :], ag_ref[...]).astype(BF16)
        rh = _rms(hy_ref[rows, :], hg_ref[...]).astype(BF16)
        x = x_ref[rows, :] + _dot(ra, woa_ref[...]) + _dot(rh, woh_ref[...])
        q = _dot(_rms(x, cg_ref[...]).astype(BF16), wmq_ref[...])
        dh = q.shape[1] // MEM_HEADS
        outs = []
        for h in range(MEM_HEADS):
            sl = slice(h * dh, (h + 1) * dh)
            s = lax.dot_general(q[:, sl].astype(BF16), mk_ref[0, :, sl], (((1,), (1,)), ((), ())),
                                preferred_element_type=F32) * dh ** -0.5
            p = jnp.exp(s - jnp.max(s, axis=-1, keepdims=True))
            l = jnp.sum(p, axis=-1, keepdims=True)
            outs.append(_dot(p.astype(BF16), mv_ref[0, :, sl]) / l)
        o = jnp.concatenate(outs, axis=1).astype(BF16)
        x = x + _dot(o, wmo_ref[...])
        x_out[rows, :] = x
        hn = _rms(x, fg_ref[...])
        for j in range(SUBLANES):
            hn_flat[pl.ds(t * ts * SUBLANES + j, ts, stride=SUBLANES), :] = hn[:, j * LANES:(j + 1) * LANES]
        route_out[rows, :] = _route(_dot(hn.astype(BF16), wr_ref[...]) + br_ref[...])


def _postmix(x2d, a2d, hy2d, seq, ag, hg, w_out, cg, w_mq, mk, mv, w_mo, fg, w_rg, b_rg, w_re, b_re, tm=512):
    T, D = x2d.shape
    ca = a2d.shape[1]
    woa = w_out[:ca].astype(BF16)
    woh = w_out[ca:].astype(BF16)
    pad = ROUTE_LANES - N_EXPERTS - N_GROUPS
    wr = jnp.concatenate([w_re, w_rg, jnp.zeros((D, pad), F32)], 1).astype(BF16)
    br = jnp.concatenate([b_re, b_rg, jnp.zeros((pad,), F32)]).reshape(1, ROUTE_LANES)
    nseq = seq // tm
    full = lambda a: pl.BlockSpec(a.shape, lambda i: (0,) * a.ndim)
    row = lambda n: pl.BlockSpec((tm, n), lambda i: (i, 0))
    memb = pl.BlockSpec((1,) + mk.shape[1:], lambda i: (i // nseq, 0, 0))
    args = [x2d, a2d, hy2d, ag.reshape(1, -1), hg.reshape(1, -1), woa, woh, cg.reshape(1, D),
            w_mq.astype(BF16), mk, mv, w_mo.astype(BF16), fg.reshape(1, D), wr, br]
    in_specs = [row(D), row(ca), row(hy2d.shape[1])] + [full(a) for a in args[3:9]] + [memb, memb] \
        + [full(a) for a in args[11:]]
    return pl.pallas_call(
        _postmix_kernel,
        grid=(T // tm,),
        in_specs=in_specs,
        out_specs=[row(D), pl.BlockSpec((tm, SUBLANES, LANES), lambda i: (i, 0, 0)), row(ROUTE_LANES)],
        out_shape=[jax.ShapeDtypeStruct((T, D), F32), jax.ShapeDtypeStruct((T, SUBLANES, LANES), F32),
                   jax.ShapeDtypeStruct((T, ROUTE_LANES), F32)],
        compiler_params=_cparams("parallel"),
        name="postmix",
    )(*args)


def _rank_kernel(route_ref, rank_out, cnt_out, carry_ref):
    @pl.when(pl.program_id(0) == 0)
    def _():
        carry_ref[...] = jnp.zeros_like(carry_ref)

    route = route_ref[...]
    tr = route.shape[0]
    lane = lax.broadcasted_iota(jnp.int32, route.shape, 1).astype(F32)
    oh1 = lane == route[:, ROUTE_ID0:ROUTE_ID0 + 1]
    oh2 = lane == route[:, ROUTE_ID0 + 1:ROUTE_ID0 + 2]
    cnt = jnp.where(oh1 | oh2, 1.0, 0.0)
    r = lax.broadcasted_iota(jnp.int32, (tr, tr), 0)
    c = lax.broadcasted_iota(jnp.int32, (tr, tr), 1)
    below = jnp.where(c < r, 1.0, 0.0).astype(BF16)
    cum = _dot(below, cnt.astype(BF16)) + carry_ref[...]
    r1 = jnp.sum(jnp.where(oh1, cum, 0.0), axis=-1, keepdims=True)
    r2 = jnp.sum(jnp.where(oh2, cum, 0.0), axis=-1, keepdims=True)
    rank_out[...] = jnp.where(lane == 0, r1, 0.0) + jnp.where(lane == 1, r2, 0.0)
    carry_ref[...] += jnp.sum(cnt, axis=0, keepdims=True)
    cnt_out[...] = carry_ref[...]


def _pos_kernel(route_ref, rank_ref, base_ref, pos_out):
    route = route_ref[...]
    rank = rank_ref[...]
    lane = lax.broadcasted_iota(jnp.int32, route.shape, 1).astype(F32)
    base = base_ref[...]
    pick = lambda k: jnp.sum(jnp.where(lane == route[:, ROUTE_ID0 + k:ROUTE_ID0 + k + 1], base, 0.0),
                             axis=-1, keepdims=True) + rank[:, k:k + 1]
    pos_out[...] = (jnp.where(lane == 0, pick(0), 0.0) + jnp.where(lane == 1, pick(1), 0.0)).astype(jnp.int32)


def _dispatch_kernel(base_ref, cp_ref, pos_ref, hn_ref, xs_hbm, zero_ref, sem, *, tt, tmm):
    i = pl.program_id(0)
    row_copy = lambda src, dst: pltpu.make_async_copy(
        hn_ref.at[pl.ds(src, 1)], xs_hbm.at[pl.ds(dst, 1)], sem)

    @pl.when(i == 0)
    def _():
        zero_ref[...] = jnp.zeros_like(zero_ref)
        pad_copy = lambda e: pltpu.make_async_copy(
            zero_ref, xs_hbm.at[pl.ds(base_ref[e] + cp_ref[e] - tmm, tmm)], sem)
        for e in range(N_EXPERTS):
            @pl.when(cp_ref[e] > 0)
            def _():
                pad_copy(e).start()
        for e in range(N_EXPERTS):
            @pl.when(cp_ref[e] > 0)
            def _():
                pad_copy(e).wait()
        last = N_EXPERTS - 1
        tail_copy = lambda r: pltpu.make_async_copy(zero_ref, xs_hbm.at[pl.ds(r * tmm, tmm)], sem)
        first_free = (base_ref[last] + cp_ref[last]) // tmm
        n_tiles = xs_hbm.shape[0] // tmm
        lax.fori_loop(first_free, n_tiles, lambda r, c: (tail_copy(r).start(), c)[1], 0)
        lax.fori_loop(first_free, n_tiles, lambda r, c: (tail_copy(r).wait(), c)[1], 0)

    def start(t, carry):
        row_copy(t, pos_ref[2 * t]).start(priority=0)
        row_copy(t, pos_ref[2 * t + 1]).start(priority=1)
        return carry

    lax.fori_loop(0, tt, start, 0, unroll=DMA_UNROLL)

    def wait(t, carry):
        row_copy(0, 0).wait()
        row_copy(0, 0).wait()
        return carry

    lax.fori_loop(0, tt, wait, 0, unroll=DMA_UNROLL)


def _ffn_kernel(te_ref, nu_ref, xs_ref, wg_ref, wu_ref, wd_ref, ys_ref):
    used = pl.program_id(0) < nu_ref[0]

    @pl.when(used)
    def _():
        x = _load_row_tiles(xs_ref).astype(BF16)
        a = _dot(x, wg_ref[0].astype(BF16))
        b = _dot(x, wu_ref[0].astype(BF16))
        m = (a * jax.nn.sigmoid(a)) * b
        _store_row_tiles(ys_ref, _dot(m.astype(BF16), wd_ref[0].astype(BF16)))

    @pl.when(jnp.logical_not(used))
    def _():
        ys_ref[...] = jnp.zeros_like(ys_ref)


def _combine_kernel(pos_ref, pos_next_ref, ys_hbm, x_ref, route_ref, fg_ref, o_ref, buf_ref, sem, *, tc):
    i = pl.program_id(0)
    slot = i % 2
    row_copy = lambda s, k, t, p: pltpu.make_async_copy(
        ys_hbm.at[pl.ds(p, 1)], buf_ref.at[s, k, pl.ds(t, 1)], sem.at[s])

    def fetch(p_ref, s):
        def start(t, carry):
            row_copy(s, 0, t, p_ref[2 * t]).start(priority=0)
            row_copy(s, 1, t, p_ref[2 * t + 1]).start(priority=1)
            return carry

        lax.fori_loop(0, tc, start, 0, unroll=DMA_UNROLL)

    @pl.when(i == 0)
    def _():
        fetch(pos_ref, 0)

    @pl.when(i + 1 < pl.num_programs(0))
    def _():
        fetch(pos_next_ref, 1 - slot)

    def wait(t, carry):
        row_copy(slot, 0, 0, 0).wait()
        row_copy(slot, 1, 0, 0).wait()
        return carry

    lax.fori_loop(0, tc, wait, 0, unroll=DMA_UNROLL)
    route = route_ref[...]
    y = (x_ref[...] + route[:, ROUTE_W0:ROUTE_W0 + 1] * _load_row_tiles(buf_ref, (slot, 0))
         + route[:, ROUTE_W0 + 1:ROUTE_W0 + 2] * _load_row_tiles(buf_ref, (slot, 1)))
    o_ref[...] = _rms(y, fg_ref[...])


def _moe(hn, route, x2d, w_gate, w_up, w_down, fg, tr=512, tt=512, tc=256):
    T, D = x2d.shape
    E = N_EXPERTS
    F = w_gate.shape[-1]
    tmm = MOE_ROW_TILE
    row = lambda tm, n: pl.BlockSpec((tm, n), lambda i: (i, 0))
    rank, counts = pl.pallas_call(
        _rank_kernel,
        grid=(T // tr,),
        in_specs=[row(tr, ROUTE_LANES)],
        out_specs=[row(tr, ROUTE_LANES), pl.BlockSpec((1, ROUTE_LANES), lambda i: (0, 0))],
        out_shape=[jax.ShapeDtypeStruct((T, ROUTE_LANES), F32), jax.ShapeDtypeStruct((1, ROUTE_LANES), F32)],
        scratch_shapes=[pltpu.VMEM((1, ROUTE_LANES), F32)],
        compiler_params=_cparams("arbitrary"),
        name="moe_rank",
    )(route)

    cnt = counts[0, :E].astype(jnp.int32)
    cp = ((cnt + tmm - 1) // tmm) * tmm
    ends = jnp.cumsum(cp)
    base = ends - cp
    n_used = ends[-1] // tmm
    n_tiles = (2 * T) // tmm + E
    tile_start = jnp.minimum(jnp.arange(n_tiles, dtype=jnp.int32), n_used - 1) * tmm
    tile_expert = jnp.minimum(jnp.sum((tile_start[:, None] >= ends[None, :]).astype(jnp.int32), axis=1), E - 1)
    base_lanes = jnp.zeros((1, ROUTE_LANES), F32).at[0, :E].set(base.astype(F32))

    pos = pl.pallas_call(
        _pos_kernel,
        grid=(T // tr,),
        in_specs=[row(tr, ROUTE_LANES), row(tr, ROUTE_LANES), pl.BlockSpec((1, ROUTE_LANES), lambda i: (0, 0))],
        out_specs=row(tr, ROUTE_LANES),
        out_shape=jax.ShapeDtypeStruct((T, ROUTE_LANES), jnp.int32),
        compiler_params=_cparams("parallel"),
        name="moe_pos",
    )(route, rank, base_lanes)
    pos = pos[:, :2].reshape(2 * T)

    xs = pl.pallas_call(
        functools.partial(_dispatch_kernel, tt=tt, tmm=tmm),
        grid_spec=pltpu.PrefetchScalarGridSpec(
            num_scalar_prefetch=2,
            grid=(T // tt,),
            in_specs=[pl.BlockSpec((2 * tt,), lambda i, b, c: (i,), memory_space=pltpu.SMEM),
                      pl.BlockSpec((tt, SUBLANES, LANES), lambda i, b, c: (i, 0, 0))],
            out_specs=pl.BlockSpec(memory_space=pl.ANY),
            scratch_shapes=[pltpu.VMEM((tmm, SUBLANES, LANES), F32), pltpu.SemaphoreType.DMA(())]),
        out_shape=jax.ShapeDtypeStruct((n_tiles * tmm, SUBLANES, LANES), F32),
        compiler_params=_cparams("arbitrary"),
        name="moe_dispatch",
    )(base, cp, pos, hn)

    tile = lambda r, te, nu: (jnp.minimum(r, nu[0] - 1), 0, 0)
    ys = pl.pallas_call(
        _ffn_kernel,
        grid_spec=pltpu.PrefetchScalarGridSpec(
            num_scalar_prefetch=2,
            grid=(n_tiles,),
            in_specs=[pl.BlockSpec((tmm, SUBLANES, LANES), tile),
                      pl.BlockSpec((1, D, F), lambda r, te, nu: (te[r], 0, 0)),
                      pl.BlockSpec((1, D, F), lambda r, te, nu: (te[r], 0, 0)),
                      pl.BlockSpec((1, F, D), lambda r, te, nu: (te[r], 0, 0))],
            out_specs=pl.BlockSpec((tmm, SUBLANES, LANES), lambda r, te, nu: (r, 0, 0))),
        out_shape=jax.ShapeDtypeStruct((n_tiles * tmm, SUBLANES, LANES), F32),
        compiler_params=_cparams("arbitrary"),
        name="moe_ffn",
    )(tile_expert, n_used.reshape(1), xs, w_gate.reshape(E, D, F), w_up.reshape(E, D, F),
      w_down.reshape(E, F, D))

    return pl.pallas_call(
        functools.partial(_combine_kernel, tc=tc),
        grid=(T // tc,),
        in_specs=[pl.BlockSpec((2 * tc,), lambda i: (i,), memory_space=pltpu.SMEM),
                  pl.BlockSpec((2 * tc,), lambda i: (jnp.minimum(i + 1, T // tc - 1),),
                               memory_space=pltpu.SMEM),
                  pl.BlockSpec(memory_space=pl.ANY),
                  row(tc, D), row(tc, ROUTE_LANES), pl.BlockSpec((1, D), lambda i: (0, 0))],
        out_specs=row(tc, D),
        out_shape=jax.ShapeDtypeStruct((T, D), F32),
        scratch_shapes=[pltpu.VMEM((2, 2, tc, SUBLANES, LANES), F32), pltpu.SemaphoreType.DMA((2,))],
        compiler_params=_cparams("arbitrary"),
        name="moe_combine",
    )(pos, pos, ys, x2d, route, fg.reshape(1, D))


def kernel(x, mem, mix_norm_g, w_in, q_norm_g, kv_norm_g, w_uq, w_ukv, hy_conv_w, hy_conv_b, hy_w1, hy_b1, hy_freq, hy_w2, hy_b2, hy_w3, hy_b3, hy_decay, hy_skip, attn_out_g, hy_out_g, w_out, cross_norm_g, mem_norm_g, w_mq, w_mkv, w_mo, ffn_norm_g, w_route_group, b_route_group, w_route_expert, b_route_expert, w_gate, w_up, w_down, final_norm_g):
    B, S, D = x.shape
    depth = w_in.shape[0]
    consts = _dft_constants(S)
    xf = x.reshape(B * S, D)
    for l in range(depth):
        q, k, v, hx1, hx2, hv = _inproj(xf, S, mix_norm_g[l], w_in[l], q_norm_g[l], kv_norm_g[l],
                                        w_uq[l], w_ukv[l], hy_conv_w[l], hy_conv_b[l])
        HP = q.shape[1]
        a_out = _attention(q.reshape(B, S, HP), k.reshape(B, S, HP), v.reshape(B, S, HP))
        filt = _hyena_filter_time(S, hy_w1[l], hy_b1[l], hy_freq[l], hy_w2[l], hy_b2[l], hy_w3[l],
                                  hy_b3[l], hy_decay[l])
        kf = _hyena_filter_spectrum(filt, consts[1], consts[3])
        C = hv.shape[1]
        h_out = _hyena(hx1.reshape(B, S, C), hx2.reshape(B, S, C), hv.reshape(B, S, C),
                       hy_skip[l], kf, consts)
        mk, mv = _memkv(mem, mem_norm_g[l], w_mkv[l])
        x2, hn, route = _postmix(xf, a_out.reshape(B * S, -1), h_out.reshape(B * S, C), S,
                                attn_out_g[l], hy_out_g[l], w_out[l], cross_norm_g[l], w_mq[l], mk, mv,
                                w_mo[l], ffn_norm_g[l], w_route_group[l], b_route_group[l],
                                w_route_expert[l], b_route_expert[l])
        assert depth == 1
        xf = _moe(hn, route, x2, w_gate[l], w_up[l], w_down[l], final_norm_g)
    return xf.reshape(B, S, D)
```

```python
import functools
import math

import numpy as np
import jax
import jax.numpy as jnp
from jax import lax
from jax.experimental import pallas as pl
from jax.experimental.pallas import tpu as pltpu

F32 = jnp.float32
BF16 = jnp.bfloat16

EPS = 1e-6
MLA_HEADS = 8
MLA_NOPE = 64
MLA_ROPE = 32
MLA_V = 64
ROPE_BASE = 10000.0
HEAD_PAD = 128
HY_ORDER = 2
HY_DIRS = 2
HY_BANDS = 16
MEM_HEADS = 4
N_GROUPS = 4
EXPERTS_PER_GROUP = 8
N_EXPERTS = N_GROUPS * EXPERTS_PER_GROUP
ROUTE_LANES = 128
ROUTE_ID0 = 0
ROUTE_W0 = 2
MOE_ROW_TILE = 512
DMA_UNROLL = 8
ROW_SUBTILES = 1

FFT_N1 = 64
FFT_N2 = 128
DFT_K1_BLOCK = 4
DFT_N2_BLOCK = 32
DFT_PITCH_PAD = 8

VMEM_LIMIT = 56 * 1024 * 1024


def _cparams(*sem):
    return pltpu.CompilerParams(dimension_semantics=sem, vmem_limit_bytes=VMEM_LIMIT)


def _rms(x, g):
    return x * lax.rsqrt(jnp.mean(x * x, axis=-1, keepdims=True) + EPS) * g


def _dot(a, b):
    return jnp.dot(a, b, preferred_element_type=F32)


SUBLANES = 8
LANES = 128


def _load_row_tiles(ref, lead=()):
    rows = ref.shape[-3]
    flat = ref.reshape(*ref.shape[:-3], rows * SUBLANES, LANES)
    return jnp.concatenate(
        [flat[(*lead, pl.ds(j, rows, stride=SUBLANES), slice(None))] for j in range(SUBLANES)], axis=1)


def _store_row_tiles(ref, val):
    rows = ref.shape[0]
    flat = ref.reshape(rows * SUBLANES, LANES)
    for j in range(SUBLANES):
        flat[pl.ds(j, rows, stride=SUBLANES), :] = val[:, j * LANES:(j + 1) * LANES]


def _inproj_kernel(x_ref, xp_ref, xn_ref, g_ref, wq_ref, wkv_ref, wkra_ref, wkrb_ref, why_ref, qg_ref,
                   kvg_ref, wqa_ref, wqb_ref, wka_ref, wv_ref, tab_ref, cw_ref, cb_ref,
                   q_out, k_out, v_out, x1_out, x2_out, hv_out, *, nseq):
    tm = x_ref.shape[0]
    halo = xp_ref.shape[0]
    ts = tm // ROW_SUBTILES
    x_ext = jnp.concatenate([xp_ref[...], x_ref[...], xn_ref[...]], axis=0)
    i = pl.program_id(0) % nseq
    c = x1_out.shape[1]
    tile = lambda t: jnp.concatenate([t] * MLA_HEADS, axis=1)
    lane = lax.broadcasted_iota(jnp.int32, (1, v_out.shape[1]), 1) % HEAD_PAD
    for s in range(ROW_SUBTILES):
        rows = slice(s * ts, (s + 1) * ts)
        hf = _rms(x_ext[s * ts:(s + 1) * ts + 2 * halo], g_ref[...])
        h = hf[halo:halo + ts].astype(BF16)
        qn = _rms(_dot(h, wq_ref[...]), qg_ref[...]).astype(BF16)
        kvn = _rms(_dot(h, wkv_ref[...]), kvg_ref[...]).astype(BF16)
        tab = tab_ref[rows, :]
        cq, sq, ck, sk = (tab[:, j * HEAD_PAD:(j + 1) * HEAD_PAD] for j in range(4))
        q = _dot(qn, wqa_ref[...]) * tile(cq) + _dot(qn, wqb_ref[...]) * tile(sq)
        q_out[rows, :] = q.astype(BF16)
        kr = _dot(h, wkra_ref[...]) * ck + _dot(h, wkrb_ref[...]) * sk
        k_out[rows, :] = (_dot(kvn, wka_ref[...]) + tile(kr)).astype(BF16)
        v_out[rows, :] = (_dot(kvn, wv_ref[...]) + jnp.where(lane == MLA_V, 1.0, 0.0)).astype(BF16)
        hy = _dot(hf.astype(BF16), why_ref[...])
        row = lax.broadcasted_iota(jnp.int32, hy.shape, 0)
        outside = (row == halo - 1) & (i == 0) if s == 0 else None
        if s == ROW_SUBTILES - 1:
            last = (row == halo + ts) & (i == nseq - 1)
            outside = last if outside is None else outside | last
        if outside is not None:
            hy = jnp.where(outside, 0.0, hy)
        cw = cw_ref[...]
        u = (hy[halo - 1:halo - 1 + ts] * cw[0:1] + hy[halo:halo + ts] * cw[1:2]
             + hy[halo + 1:halo + 1 + ts] * cw[2:3] + cb_ref[...])
        x1_out[rows, :] = u[:, :c]
        x2_out[rows, :] = u[:, c:2 * c]
        hv_out[rows, :] = u[:, 2 * c:]


def _inproj(x2d, seq, mix_g, w_in, q_g, kv_g, w_uq, w_ukv, conv_w, conv_b, tm=512):
    T, D = x2d.shape
    per = tm // SUBLANES
    cb = conv_b.reshape(1, -1)
    q_rank, kv_rank = q_g.shape[0], kv_g.shape[0]
    off_kv = q_rank
    off_kr = off_kv + kv_rank
    off_hy = off_kr + MLA_ROPE
    C = (w_in.shape[1] - off_hy) // 3
    H = MLA_HEADS
    half = MLA_ROPE // 2
    wq = w_in[:, :off_kv].astype(BF16)
    wkv = w_in[:, off_kv:off_kr].astype(BF16)
    wkr = w_in[:, off_kr:off_hy]
    wkr_sw = jnp.concatenate([wkr[:, half:], wkr[:, :half]], axis=1)
    zpad = lambda n: jnp.zeros((D, n), F32)
    wkra = jnp.concatenate([zpad(MLA_NOPE), wkr, zpad(HEAD_PAD - MLA_NOPE - MLA_ROPE)], 1).astype(BF16)
    wkrb = jnp.concatenate([zpad(MLA_NOPE), wkr_sw, zpad(HEAD_PAD - MLA_NOPE - MLA_ROPE)], 1).astype(BF16)
    why = w_in[:, off_hy:].astype(BF16)

    uq = w_uq.reshape(q_rank, H, MLA_NOPE + MLA_ROPE)
    uq_n, uq_r = uq[..., :MLA_NOPE], uq[..., MLA_NOPE:]
    uq_rs = jnp.concatenate([uq_r[..., half:], uq_r[..., :half]], axis=-1)
    zq = lambda n: jnp.zeros((q_rank, H, n), F32)
    wqa = jnp.concatenate([uq_n, uq_r, zq(HEAD_PAD - MLA_NOPE - MLA_ROPE)], -1).reshape(q_rank, H * HEAD_PAD).astype(BF16)
    wqb = jnp.concatenate([zq(MLA_NOPE), uq_rs, zq(HEAD_PAD - MLA_NOPE - MLA_ROPE)], -1).reshape(q_rank, H * HEAD_PAD).astype(BF16)
    ukv = w_ukv.reshape(kv_rank, H, MLA_NOPE + MLA_V)
    zk = lambda n: jnp.zeros((kv_rank, H, n), F32)
    wka = jnp.concatenate([ukv[..., :MLA_NOPE], zk(HEAD_PAD - MLA_NOPE)], -1).reshape(kv_rank, H * HEAD_PAD).astype(BF16)
    wv = jnp.concatenate([ukv[..., MLA_NOPE:], zk(HEAD_PAD - MLA_V)], -1).reshape(kv_rank, H * HEAD_PAD).astype(BF16)

    ang = np.arange(seq)[:, None] * ROPE_BASE ** (-np.arange(half) / half)[None, :]
    cos2 = np.concatenate([np.cos(ang), np.cos(ang)], 1)
    sin2 = np.concatenate([-np.sin(ang), np.sin(ang)], 1)
    zs = lambda n: np.zeros((seq, n))
    scale = (MLA_NOPE + MLA_ROPE) ** -0.5 * math.log2(math.e)
    rest = HEAD_PAD - MLA_NOPE - MLA_ROPE
    cq = scale * np.concatenate([np.ones((seq, MLA_NOPE)), cos2, zs(rest)], 1)
    sq = scale * np.concatenate([zs(MLA_NOPE), sin2, zs(rest)], 1)
    ck = np.concatenate([zs(MLA_NOPE), cos2, zs(rest)], 1)
    sk = np.concatenate([zs(MLA_NOPE), sin2, zs(rest)], 1)
    tab = jnp.asarray(np.concatenate([cq, sq, ck, sk], 1), dtype=F32)

    nseq = seq // tm
    full = lambda a: pl.BlockSpec(a.shape, lambda i: (0,) * a.ndim)
    row = lambda n: pl.BlockSpec((tm, n), lambda i: (i, 0))
    consts = [mix_g.reshape(1, D), wq, wkv, wkra, wkrb, why, q_g.reshape(1, -1), kv_g.reshape(1, -1),
              wqa, wqb, wka, wv]
    HP = H * HEAD_PAD
    return pl.pallas_call(
        functools.partial(_inproj_kernel, nseq=nseq),
        grid=(T // tm,),
        in_specs=[row(D),
                  pl.BlockSpec((SUBLANES, D), lambda i: (jnp.maximum(i * per - 1, 0), 0)),
                  pl.BlockSpec((SUBLANES, D), lambda i: (jnp.minimum((i + 1) * per, T // SUBLANES - 1), 0))]
        + [full(a) for a in consts]
        + [pl.BlockSpec((tm, 4 * HEAD_PAD), lambda i: (i % nseq, 0)), full(conv_w), full(cb)],
        out_specs=[row(HP), row(HP), row(HP), row(C), row(C), row(C)],
        out_shape=[jax.ShapeDtypeStruct((T, HP), BF16)] * 3 + [jax.ShapeDtypeStruct((T, C), F32)] * 3,
        compiler_params=_cparams("parallel"),
        name="inproj",
    )(x2d, x2d, x2d, *consts, tab, conv_w, cb)


def _attn_kernel(q_ref, k_ref, v_ref, o_ref):
    outs = []
    for h in range(MLA_HEADS):
        sl = slice(h * HEAD_PAD, (h + 1) * HEAD_PAD)
        s = lax.dot_general(q_ref[0, :, sl], k_ref[0, :, sl], (((1,), (1,)), ((), ())),
                            preferred_element_type=F32).astype(BF16)
        p = jnp.exp2(s - jnp.max(s, axis=-1, keepdims=True))
        o = _dot(p, v_ref[0, :, sl])
        outs.append(o[:, :MLA_V] / o[:, MLA_V:MLA_V + 1])
    o_ref[0] = jnp.concatenate(outs, axis=1)


def _attention(q, k, v, tq=512):
    B, S, HP = q.shape
    return pl.pallas_call(
        _attn_kernel,
        grid=(B, S // tq),
        in_specs=[pl.BlockSpec((1, tq, HP), lambda b, i: (b, i, 0)),
                  pl.BlockSpec((1, S, HP), lambda b, i: (b, 0, 0)),
                  pl.BlockSpec((1, S, HP), lambda b, i: (b, 0, 0))],
        out_specs=pl.BlockSpec((1, tq, MLA_HEADS * MLA_V), lambda b, i: (b, i, 0)),
        out_shape=jax.ShapeDtypeStruct((B, S, MLA_HEADS * MLA_V), F32),
        compiler_params=_cparams("parallel", "arbitrary"),
        name="mla_attention",
    )(q, k, v)


def _dft_constants(seq):
    n = 2 * seq
    n1, n2 = FFT_N1, FFT_N2
    assert n1 * n2 == n
    r1 = np.arange(n1)
    r2 = np.arange(n2)
    blk = lambda z: np.block([[z.real, -z.imag], [z.imag, z.real]])
    w1 = np.exp(-2j * np.pi * np.outer(r1, r1) / n1)
    fa_data = blk(w1[:, :n1 // 2])
    fa_filt = np.concatenate([w1.real, w1.imag], axis=0)
    fc = blk(np.conj(w1).T[:n1 // 2, :])
    w2 = np.exp(-2j * np.pi * np.outer(r2, r2) / n2)
    tw = np.exp(-2j * np.pi * np.outer(r1, r2) / n)
    fb = np.stack([blk(w2 * tw[k][None, :]) for k in range(n1)])
    fbi = np.stack([blk(np.conj(w2).T * np.conj(tw[k])[:, None] / n) for k in range(n1)])
    as_bf = lambda a: jnp.asarray(a, dtype=F32).astype(BF16)
    return as_bf(fa_data), as_bf(fa_filt), as_bf(fc), as_bf(fb), as_bf(fbi)


def _filter_kernel(z_ref, w1_ref, b1_ref, fr_ref, w2_ref, b2_ref, w3_ref, b3_ref, dec_ref, o_ref, *, seq, tr):
    hp = lax.Precision.HIGHEST
    z = z_ref[...]
    fr = fr_ref[...]
    h = jnp.sin(fr[0:1] * (jnp.dot(z, w1_ref[...], precision=hp, preferred_element_type=F32) + b1_ref[...]))
    h = jnp.sin(fr[1:2] * (jnp.dot(h, w2_ref[...], precision=hp, preferred_element_type=F32) + b2_ref[...]))
    split = lambda a: (a.astype(BF16), (a - a.astype(BF16).astype(F32)).astype(BF16))
    (hh, hl), (wh, wl) = split(h), split(w3_ref[0])
    h = _dot(hh, wh) + _dot(hh, wl) + _dot(hl, wh) + b3_ref[0]
    h = h * jnp.exp(-z[:, 0:1] * jnp.abs(dec_ref[0]))
    n = pl.program_id(0) * tr + lax.broadcasted_iota(jnp.int32, h.shape, 0)
    o_ref[...] = jnp.where(n == seq, 0.0, h)


def _hyena_filter_time(seq, w1, b1, freq, w2, b2, w3, b3, decay, tr=512):
    n = 2 * seq
    emb, ffn = w1.shape
    C = w3.shape[1] // (HY_ORDER * HY_DIRS)
    off = np.arange(n)
    t = np.where(off < seq, off, n - off).astype(np.float64)
    bands = np.linspace(1e-4, HY_BANDS - 1, HY_BANDS)
    ang = 2.0 * math.pi * t[:, None] * bands[None, :] / seq
    z = np.concatenate([(t / seq)[:, None], np.cos(ang), -np.sin(ang)], axis=-1)
    zl = LANES
    z = jnp.asarray(np.pad(z, ((0, 0), (0, zl - emb))), dtype=F32)
    w1p = jnp.pad(w1, ((0, zl - emb), (0, 0)))
    by_dir = lambda a: jnp.moveaxis(a.reshape(a.shape[0], HY_ORDER, HY_DIRS, C), 2, 0).reshape(
        HY_DIRS, a.shape[0], HY_ORDER * C)
    w3d, b3d, decd = by_dir(w3), by_dir(b3.reshape(1, -1)), by_dir(decay.reshape(1, -1))
    full = lambda a: pl.BlockSpec(a.shape, lambda i: (0,) * a.ndim)
    ndir = lambda a: pl.BlockSpec((1,) + a.shape[1:], lambda i: ((i * tr) // seq, 0, 0))
    consts = [w1p, b1.reshape(1, -1), freq, w2, b2.reshape(1, -1)]
    return pl.pallas_call(
        functools.partial(_filter_kernel, seq=seq, tr=tr),
        grid=(n // tr,),
        in_specs=[pl.BlockSpec((tr, zl), lambda i: (i, 0))] + [full(a) for a in consts]
        + [ndir(w3d), ndir(b3d), ndir(decd)],
        out_specs=pl.BlockSpec((tr, HY_ORDER * C), lambda i: (i, 0)),
        out_shape=jax.ShapeDtypeStruct((n, HY_ORDER * C), F32),
        compiler_params=_cparams("parallel"),
        name="hyena_filter_mlp",
    )(z, *consts, w3d, b3d, decd)


def _pitched(rows, nb):
    return pltpu.VMEM((rows, nb + DFT_PITCH_PAD, LANES), F32)


def _block_rows(ref):
    return math.prod(ref.shape[:-2]), ref.shape[-2]


def _copy_in(ref, scr):
    rows, nb = _block_rows(ref)
    scr[:, :nb, :] = ref[...].reshape(rows, nb, LANES)


def _copy_out(scr, ref):
    rows, nb = _block_rows(ref)
    ref[...] = scr[:, :nb, :].reshape(ref.shape)


def _at_n2(scr, n):
    rows, pitch, _ = scr.shape
    return scr.reshape(rows * pitch, LANES).at[pl.ds(n, rows, stride=pitch), :]


def _outer_dft(mat_ref, x_ref, o_ref, xs, os):
    _copy_in(x_ref, xs)
    for n in range(x_ref.shape[-2]):
        _at_n2(os, n)[...] = _dot(mat_ref[...], _at_n2(xs, n)[...].astype(BF16))
    _copy_out(os, o_ref)


def _filter_stage_a_kernel(x_ref, fa_ref, o_ref, xs, os):
    _outer_dft(fa_ref, x_ref, o_ref, xs, os)


def _filter_stage_b_kernel(x_ref, fb_ref, o_ref):
    for kk in range(x_ref.shape[1]):
        x = jnp.concatenate([x_ref[0, kk], x_ref[1, kk]], axis=0).astype(BF16)
        o_ref[kk] = _dot(fb_ref[kk], x)


def _hyena_filter_spectrum(filt, fa_filt, fb, nb=DFT_N2_BLOCK):
    n, oc = filt.shape
    a = pl.pallas_call(
        _filter_stage_a_kernel,
        grid=(FFT_N2 // nb, oc // LANES),
        in_specs=[pl.BlockSpec((FFT_N1, nb, LANES), lambda j, c: (0, j, c)),
                  pl.BlockSpec(fa_filt.shape, lambda j, c: (0, 0))],
        out_specs=pl.BlockSpec((2, FFT_N1, nb, LANES), lambda j, c: (0, 0, j, c)),
        out_shape=jax.ShapeDtypeStruct((2, FFT_N1, FFT_N2, oc), F32),
        scratch_shapes=[_pitched(FFT_N1, nb), _pitched(2 * FFT_N1, nb)],
        compiler_params=_cparams("parallel", "parallel"),
        name="hyena_filter_dft_a",
    )(filt.reshape(FFT_N1, FFT_N2, oc), fa_filt)
    return pl.pallas_call(
        _filter_stage_b_kernel,
        grid=(FFT_N1 // DFT_K1_BLOCK,),
        in_specs=[pl.BlockSpec((2, DFT_K1_BLOCK, FFT_N2, oc), lambda k: (0, k, 0, 0)),
                  pl.BlockSpec((DFT_K1_BLOCK, 2 * FFT_N2, 2 * FFT_N2), lambda k: (k, 0, 0))],
        out_specs=pl.BlockSpec((DFT_K1_BLOCK, 2 * FFT_N2, oc), lambda k: (k, 0, 0)),
        out_shape=jax.ShapeDtypeStruct((FFT_N1, 2 * FFT_N2, oc), F32),
        compiler_params=_cparams("parallel"),
        name="hyena_filter_dft_b",
    )(a, fb)


def _stage_a_kernel(x_ref, fa_ref, o_ref, xs, os):
    _outer_dft(fa_ref, x_ref, o_ref, xs, os)


def _stage_a(x4, fa, nb):
    B, r, n2, C = x4.shape
    return pl.pallas_call(
        _stage_a_kernel,
        grid=(B // 2, n2 // nb, C // LANES),
        in_specs=[pl.BlockSpec((2, r, nb, LANES), lambda p, j, c: (p, 0, j, c)),
                  pl.BlockSpec(fa.shape, lambda p, j, c: (0, 0))],
        out_specs=pl.BlockSpec((1, 2, FFT_N1, nb, LANES), lambda p, j, c: (p, 0, 0, j, c)),
        out_shape=jax.ShapeDtypeStruct((B // 2, 2, FFT_N1, n2, C), F32),
        scratch_shapes=[_pitched(2 * r, nb), _pitched(2 * FFT_N1, nb)],
        compiler_params=_cparams("parallel", "parallel", "parallel"),
        name="hyena_dft_a",
    )(x4, fa)


def _stage_b_kernel(x_ref, fb_ref, kf_ref, fbi_ref, o_ref):
    npair = x_ref.shape[0]
    n2 = x_ref.shape[3]
    c = x_ref.shape[4]
    for kk in range(x_ref.shape[2]):
        x = jnp.concatenate(
            [jnp.concatenate([x_ref[p, 0, kk], x_ref[p, 1, kk]], axis=0) for p in range(npair)],
            axis=1).astype(BF16)
        g = _dot(fb_ref[kk], x)
        gr, gi = g[:n2], g[n2:]
        kf = kf_ref[kk]
        kr = jnp.concatenate([kf[:n2]] * npair, axis=1)
        ki = jnp.concatenate([kf[n2:]] * npair, axis=1)
        hcat = jnp.concatenate([gr * kr - gi * ki, gr * ki + gi * kr], axis=0).astype(BF16)
        y = _dot(fbi_ref[kk], hcat)
        for p in range(npair):
            o_ref[p, 0, kk] = y[:n2, p * c:(p + 1) * c].astype(o_ref.dtype)
            o_ref[p, 1, kk] = y[n2:, p * c:(p + 1) * c].astype(o_ref.dtype)


def _stage_b(spec, fb, kf, fbi, order, nk=DFT_K1_BLOCK):
    npair, _, _, _, C = spec.shape
    blk = pl.BlockSpec((npair, 2, nk, FFT_N2, C), lambda k: (0, 0, k, 0, 0))
    mat = pl.BlockSpec((nk, 2 * FFT_N2, 2 * FFT_N2), lambda k: (k, 0, 0))
    return pl.pallas_call(
        _stage_b_kernel,
        grid=(FFT_N1 // nk,),
        in_specs=[blk, mat, pl.BlockSpec((nk, 2 * FFT_N2, C), lambda k: (k, 0, order)), mat],
        out_specs=blk,
        out_shape=jax.ShapeDtypeStruct(spec.shape, F32),
        compiler_params=_cparams("parallel"),
        name="hyena_dft_b",
    )(spec, fb, kf, fbi)


def _stage_c_kernel(y_ref, fc_ref, gate_ref, z_ref, skip_ref, *rest, stage_a):
    if stage_a:
        fa_ref, z_out, a_out, ys, gs, zs, os, as_ = rest
    else:
        z_out, ys, gs, zs, os = rest
    nb = gate_ref.shape[-2]
    _copy_in(y_ref, ys)
    _copy_in(gate_ref, gs)
    _copy_in(z_ref, zs)
    skip = skip_ref[...]
    for n in range(nb):
        conv = _dot(fc_ref[...], _at_n2(ys, n)[...].astype(BF16))
        _at_n2(os, n)[...] = _at_n2(gs, n)[...] * (conv + skip * _at_n2(zs, n)[...])
    _copy_out(os, z_out)
    if stage_a:
        for n in range(nb):
            _at_n2(as_, n)[...] = _dot(fa_ref[...], _at_n2(os, n)[...].astype(BF16))
        _copy_out(as_, a_out)


def _stage_c(yspec, fc, gate, zin, skip, fa=None, nb=DFT_N2_BLOCK):
    B, r, n2, C = gate.shape
    dat = pl.BlockSpec((2, r, nb, LANES), lambda p, j, c: (p, 0, j, c))
    spc = pl.BlockSpec((1, 2, FFT_N1, nb, LANES), lambda p, j, c: (p, 0, 0, j, c))
    in_specs = [spc, pl.BlockSpec(fc.shape, lambda p, j, c: (0, 0)), dat, dat,
                pl.BlockSpec((1, LANES), lambda p, j, c: (0, c))]
    out_specs = [dat]
    out_shape = [jax.ShapeDtypeStruct(gate.shape, F32)]
    args = [yspec, fc, gate, zin, skip.reshape(1, C)]
    scratch = [_pitched(2 * FFT_N1, nb)] + [_pitched(2 * r, nb)] * 3
    if fa is not None:
        in_specs.append(pl.BlockSpec(fa.shape, lambda p, j, c: (0, 0)))
        out_specs.append(spc)
        out_shape.append(jax.ShapeDtypeStruct(yspec.shape, F32))
        args.append(fa)
        scratch.append(_pitched(2 * FFT_N1, nb))
    return pl.pallas_call(
        functools.partial(_stage_c_kernel, stage_a=fa is not None),
        grid=(B // 2, n2 // nb, C // LANES),
        in_specs=in_specs, out_specs=out_specs, out_shape=out_shape,
        scratch_shapes=scratch,
        compiler_params=_cparams("parallel", "parallel", "parallel"),
        name="hyena_dft_c",
    )(*args)


def _hyena(x1, x2, v, skip, kf, consts, nb=DFT_N2_BLOCK):
    fa_data, _, fc, fb, fbi = consts
    B, S, C = v.shape
    split = lambda a: a.reshape(B, S // FFT_N2, FFT_N2, C)
    a0 = _stage_a(split(v), fa_data, nb)
    y0 = _stage_b(a0, fb, kf, fbi, 0)
    z1, a1 = _stage_c(y0, fc, split(x1), split(v), skip[0], fa=fa_data, nb=nb)
    y1 = _stage_b(a1, fb, kf, fbi, 1)
    (out,) = _stage_c(y1, fc, split(x2), z1, skip[1], nb=nb)
    return out.reshape(B, S, C)


def _memkv_kernel(m_ref, g_ref, w_ref, k_out, v_out):
    hm = _rms(m_ref[0], g_ref[...]).astype(BF16)
    kv = _dot(hm, w_ref[...])
    d = k_out.shape[2]
    k_out[0] = kv[:, :d].astype(BF16)
    v_out[0] = kv[:, d:].astype(BF16)


def _memkv(mem, g, w_mkv):
    B, M, D = mem.shape
    dk = w_mkv.shape[1] // 2
    w = w_mkv.astype(BF16)
    return pl.pallas_call(
        _memkv_kernel,
        grid=(B,),
        in_specs=[pl.BlockSpec((1, M, D), lambda b: (b, 0, 0)),
                  pl.BlockSpec((1, D), lambda b: (0, 0)),
                  pl.BlockSpec(w.shape, lambda b: (0, 0))],
        out_specs=[pl.BlockSpec((1, M, dk), lambda b: (b, 0, 0))] * 2,
        out_shape=[jax.ShapeDtypeStruct((B, M, dk), BF16)] * 2,
        compiler_params=_cparams("parallel"),
        name="mem_kv",
    )(mem, g.reshape(1, D), w)


def _route(logits):
    lane = lax.broadcasted_iota(jnp.int32, logits.shape, 1)
    ninf = -jnp.inf
    big = ROUTE_LANES
    first = lambda mask: jnp.min(jnp.where(mask, lane, big), axis=-1, keepdims=True)
    is_g = (lane >= N_EXPERTS) & (lane < N_EXPERTS + N_GROUPS)
    gl = jnp.where(is_g, logits, ninf)
    gmax = jnp.max(gl, axis=-1, keepdims=True)
    g_idx = first(gl == gmax) - N_EXPERTS
    p_group = 1.0 / jnp.sum(jnp.exp(gl - gmax), axis=-1, keepdims=True)
    in_g = (lane < N_EXPERTS) & ((lane // EXPERTS_PER_GROUP) == g_idx)
    el = jnp.where(in_g, logits, ninf)
    v1 = jnp.max(el, axis=-1, keepdims=True)
    i1 = first(el == v1)
    el2 = jnp.where(lane == i1, ninf, el)
    v2 = jnp.max(el2, axis=-1, keepdims=True)
    i2 = first(el2 == v2)
    e2 = jnp.exp(v2 - v1)
    p1 = 1.0 / (1.0 + e2)
    p2 = e2 / (1.0 + e2)
    sel = lambda n, val: jnp.where(lane == n, val, 0.0)
    return (sel(ROUTE_ID0, i1.astype(F32)) + sel(ROUTE_ID0 + 1, i2.astype(F32))
            + sel(ROUTE_W0, p_group * p1) + sel(ROUTE_W0 + 1, p_group * p2))


def _postmix_kernel(x_ref, a_ref, hy_ref, ag_ref, hg_ref, woa_ref, woh_ref, cg_ref, wmq_ref,
                    mk_ref, mv_ref, wmo_ref, fg_ref, wr_ref, br_ref, x_out, hn_out, route_out,
                    route_t_out):
    ts = x_ref.shape[0] // ROW_SUBTILES
    hn_flat = hn_out.reshape(hn_out.shape[0] * SUBLANES, LANES)
    for t in range(ROW_SUBTILES):
        rows = slice(t * ts, (t + 1) * ts)
        ra = _rms(a_ref[rows, :], ag_ref[...]).astype(BF16)
        rh = _rms(hy_ref[rows, :], hg_ref[...]).astype(BF16)
        x = x_ref[rows, :] + _dot(ra, woa_ref[...]) + _dot(rh, woh_ref[...])
        q = _dot(_rms(x, cg_ref[...]).astype(BF16), wmq_ref[...])
        dh = q.shape[1] // MEM_HEADS
        outs = []
        for h in range(MEM_HEADS):
            sl = slice(h * dh, (h + 1) * dh)
            s = lax.dot_general(q[:, sl].astype(BF16), mk_ref[0, :, sl], (((1,), (1,)), ((), ())),
                                preferred_element_type=F32) * dh ** -0.5
            p = jnp.exp(s - jnp.max(s, axis=-1, keepdims=True))
            l = jnp.sum(p, axis=-1, keepdims=True)
            outs.append(_dot(p.astype(BF16), mv_ref[0, :, sl]) / l)
        o = jnp.concatenate(outs, axis=1).astype(BF16)
        x = x + _dot(o, wmo_ref[...])
        x_out[rows, :] = x
        hn = _rms(x, fg_ref[...])
        for j in range(SUBLANES):
            hn_flat[pl.ds(t * ts * SUBLANES + j, ts, stride=SUBLANES), :] = hn[:, j * LANES:(j + 1) * LANES]
        route = _route(_dot(hn.astype(BF16), wr_ref[...]) + br_ref[...])
        route_out[rows, :] = route
        route_t_out[0, :, rows] = route.T[:SUBLANES, :]


def _postmix(x2d, a2d, hy2d, seq, ag, hg, w_out, cg, w_mq, mk, mv, w_mo, fg, w_rg, b_rg, w_re, b_re, tm=512):
    T, D = x2d.shape
    ca = a2d.shape[1]
    woa = w_out[:ca].astype(BF16)
    woh = w_out[ca:].astype(BF16)
    pad = ROUTE_LANES - N_EXPERTS - N_GROUPS
    wr = jnp.concatenate([w_re, w_rg, jnp.zeros((D, pad), F32)], 1).astype(BF16)
    br = jnp.concatenate([b_re, b_rg, jnp.zeros((pad,), F32)]).reshape(1, ROUTE_LANES)
    nseq = seq // tm
    full = lambda a: pl.BlockSpec(a.shape, lambda i: (0,) * a.ndim)
    row = lambda n: pl.BlockSpec((tm, n), lambda i: (i, 0))
    memb = pl.BlockSpec((1,) + mk.shape[1:], lambda i: (i // nseq, 0, 0))
    args = [x2d, a2d, hy2d, ag.reshape(1, -1), hg.reshape(1, -1), woa, woh, cg.reshape(1, D),
            w_mq.astype(BF16), mk, mv, w_mo.astype(BF16), fg.reshape(1, D), wr, br]
    in_specs = [row(D), row(ca), row(hy2d.shape[1])] + [full(a) for a in args[3:9]] + [memb, memb] \
        + [full(a) for a in args[11:]]
    return pl.pallas_call(
        _postmix_kernel,
        grid=(T // tm,),
        in_specs=in_specs,
        out_specs=[row(D), pl.BlockSpec((tm, SUBLANES, LANES), lambda i: (i, 0, 0)), row(ROUTE_LANES),
                   pl.BlockSpec((1, SUBLANES, tm), lambda i: (i, 0, 0))],
        out_shape=[jax.ShapeDtypeStruct((T, D), F32), jax.ShapeDtypeStruct((T, SUBLANES, LANES), F32),
                   jax.ShapeDtypeStruct((T, ROUTE_LANES), F32),
                   jax.ShapeDtypeStruct((T // tm, SUBLANES, tm), F32)],
        compiler_params=_cparams("parallel"),
        name="postmix",
    )(*args)


def _slot_onehots(rt):
    e = lax.broadcasted_iota(jnp.int32, (N_EXPERTS, rt.shape[1]), 0).astype(F32)
    return [e == rt[ROUTE_ID0 + k:ROUTE_ID0 + k + 1, :] for k in range(2)]


def _slot_rows(rows, width):
    sub = lax.broadcasted_iota(jnp.int32, (SUBLANES, width), 0)
    return jnp.where(sub == 0, rows[0], jnp.where(sub == 1, rows[1], 0.0))


def _rank_kernel(rt_ref, rank_out, cnt_out, carry_ref):
    @pl.when(pl.program_id(0) == 0)
    def _():
        carry_ref[...] = jnp.zeros_like(carry_ref)

    tr = rt_ref.shape[2]
    oh = _slot_onehots(rt_ref[0])
    cnt = jnp.where(oh[0] | oh[1], 1.0, 0.0)
    s = lax.broadcasted_iota(jnp.int32, (tr, tr), 0)
    t = lax.broadcasted_iota(jnp.int32, (tr, tr), 1)
    before = jnp.where(s < t, 1.0, 0.0).astype(BF16)
    cum = _dot(cnt.astype(BF16), before) + carry_ref[...]
    ranks = [jnp.sum(jnp.where(m, cum, 0.0), axis=0, keepdims=True) for m in oh]
    rank_out[0] = _slot_rows(ranks, tr)
    carry_ref[...] += jnp.sum(cnt, axis=1, keepdims=True)
    cnt_out[...] = carry_ref[...]


def _pos_kernel(rt_ref, rank_ref, base_ref, pos_out):
    tr = rt_ref.shape[2]
    oh = _slot_onehots(rt_ref[0])
    rank = rank_ref[0]
    pos = [jnp.sum(jnp.where(m, base_ref[...], 0.0), axis=0, keepdims=True) + rank[k:k + 1, :]
           for k, m in enumerate(oh)]
    pos_out[0] = _slot_rows(pos, tr).astype(jnp.int32)


def _dispatch_kernel(base_ref, cp_ref, pos0_ref, pos1_ref, hn_ref, xs_hbm, zero_ref, sem, *, tt, tmm):
    i = pl.program_id(0)
    row_copy = lambda src, dst: pltpu.make_async_copy(
        hn_ref.at[pl.ds(src, 1)], xs_hbm.at[pl.ds(dst, 1)], sem)

    @pl.when(i == 0)
    def _():
        zero_ref[...] = jnp.zeros_like(zero_ref)
        pad_copy = lambda e: pltpu.make_async_copy(
            zero_ref, xs_hbm.at[pl.ds(base_ref[e] + cp_ref[e] - tmm, tmm)], sem)
        for e in range(N_EXPERTS):
            @pl.when(cp_ref[e] > 0)
            def _():
                pad_copy(e).start()
        for e in range(N_EXPERTS):
            @pl.when(cp_ref[e] > 0)
            def _():
                pad_copy(e).wait()
        last = N_EXPERTS - 1
        tail_copy = lambda r: pltpu.make_async_copy(zero_ref, xs_hbm.at[pl.ds(r * tmm, tmm)], sem)
        first_free = (base_ref[last] + cp_ref[last]) // tmm
        n_tiles = xs_hbm.shape[0] // tmm
        lax.fori_loop(first_free, n_tiles, lambda r, c: (tail_copy(r).start(), c)[1], 0)
        lax.fori_loop(first_free, n_tiles, lambda r, c: (tail_copy(r).wait(), c)[1], 0)

    def start(t, carry):
        row_copy(t, pos0_ref[t]).start(priority=0)
        row_copy(t, pos1_ref[t]).start(priority=1)
        return carry

    lax.fori_loop(0, tt, start, 0, unroll=DMA_UNROLL)

    def wait(t, carry):
        row_copy(0, 0).wait()
        row_copy(0, 0).wait()
        return carry

    lax.fori_loop(0, tt, wait, 0, unroll=DMA_UNROLL)


def _ffn_kernel(te_ref, nu_ref, xs_ref, wg_ref, wu_ref, wd_ref, ys_ref):
    used = pl.program_id(0) < nu_ref[0]

    @pl.when(used)
    def _():
        x = _load_row_tiles(xs_ref).astype(BF16)
        a = _dot(x, wg_ref[0].astype(BF16))
        b = _dot(x, wu_ref[0].astype(BF16))
        m = (a * jax.nn.sigmoid(a)) * b
        _store_row_tiles(ys_ref, _dot(m.astype(BF16), wd_ref[0].astype(BF16)))

    @pl.when(jnp.logical_not(used))
    def _():
        ys_ref[...] = jnp.zeros_like(ys_ref)


def _combine_kernel(pos0_ref, pos1_ref, pos0_next_ref, pos1_next_ref, ys_hbm, x_ref, route_ref, fg_ref,
                    o_ref, buf_ref, sem, *, tc):
    i = pl.program_id(0)
    slot = i % 2
    row_copy = lambda s, k, t, p: pltpu.make_async_copy(
        ys_hbm.at[pl.ds(p, 1)], buf_ref.at[s, k, pl.ds(t, 1)], sem.at[s])

    def fetch(p0_ref, p1_ref, s):
        def start(t, carry):
            row_copy(s, 0, t, p0_ref[t]).start(priority=0)
            row_copy(s, 1, t, p1_ref[t]).start(priority=1)
            return carry

        lax.fori_loop(0, tc, start, 0, unroll=DMA_UNROLL)

    @pl.when(i == 0)
    def _():
        fetch(pos0_ref, pos1_ref, 0)

    @pl.when(i + 1 < pl.num_programs(0))
    def _():
        fetch(pos0_next_ref, pos1_next_ref, 1 - slot)

    def wait(t, carry):
        row_copy(slot, 0, 0, 0).wait()
        row_copy(slot, 1, 0, 0).wait()
        return carry

    lax.fori_loop(0, tc, wait, 0, unroll=DMA_UNROLL)
    route = route_ref[...]
    y = (x_ref[...] + route[:, ROUTE_W0:ROUTE_W0 + 1] * _load_row_tiles(buf_ref, (slot, 0))
         + route[:, ROUTE_W0 + 1:ROUTE_W0 + 2] * _load_row_tiles(buf_ref, (slot, 1)))
    o_ref[...] = _rms(y, fg_ref[...])


def _moe(hn, route, route_t, x2d, w_gate, w_up, w_down, fg, tt=512, tc=256):
    T, D = x2d.shape
    E = N_EXPERTS
    F = w_gate.shape[-1]
    tmm = MOE_ROW_TILE
    ntr, _, tr = route_t.shape
    row = lambda tm, n: pl.BlockSpec((tm, n), lambda i: (i, 0))
    rec = pl.BlockSpec((1, SUBLANES, tr), lambda i: (i, 0, 0))
    col = pl.BlockSpec((E, 1), lambda i: (0, 0))
    rank, counts = pl.pallas_call(
        _rank_kernel,
        grid=(ntr,),
        in_specs=[rec],
        out_specs=[rec, col],
        out_shape=[jax.ShapeDtypeStruct(route_t.shape, F32), jax.ShapeDtypeStruct((E, 1), F32)],
        scratch_shapes=[pltpu.VMEM((E, 1), F32)],
        compiler_params=_cparams("arbitrary"),
        name="moe_rank",
    )(route_t)

    cnt = counts[:, 0].astype(jnp.int32)
    cp = ((cnt + tmm - 1) // tmm) * tmm
    ends = jnp.cumsum(cp)
    base = ends - cp
    n_used = ends[-1] // tmm
    n_tiles = (2 * T) // tmm + E
    tile_start = jnp.minimum(jnp.arange(n_tiles, dtype=jnp.int32), n_used - 1) * tmm
    tile_expert = jnp.minimum(jnp.sum((tile_start[:, None] >= ends[None, :]).astype(jnp.int32), axis=1), E - 1)

    pos = pl.pallas_call(
        _pos_kernel,
        grid=(ntr,),
        in_specs=[rec, rec, col],
        out_specs=rec,
        out_shape=jax.ShapeDtypeStruct(route_t.shape, jnp.int32),
        compiler_params=_cparams("parallel"),
        name="moe_pos",
    )(route_t, rank, base.astype(F32).reshape(E, 1))
    pos0, pos1 = pos[:, 0, :].reshape(T), pos[:, 1, :].reshape(T)

    xs = pl.pallas_call(
        functools.partial(_dispatch_kernel, tt=tt, tmm=tmm),
        grid_spec=pltpu.PrefetchScalarGridSpec(
            num_scalar_prefetch=2,
            grid=(T // tt,),
            in_specs=[pl.BlockSpec((tt,), lambda i, b, c: (i,), memory_space=pltpu.SMEM),
                      pl.BlockSpec((tt,), lambda i, b, c: (i,), memory_space=pltpu.SMEM),
                      pl.BlockSpec((tt, SUBLANES, LANES), lambda i, b, c: (i, 0, 0))],
            out_specs=pl.BlockSpec(memory_space=pl.ANY),
            scratch_shapes=[pltpu.VMEM((tmm, SUBLANES, LANES), F32), pltpu.SemaphoreType.DMA(())]),
        out_shape=jax.ShapeDtypeStruct((n_tiles * tmm, SUBLANES, LANES), F32),
        compiler_params=_cparams("arbitrary"),
        name="moe_dispatch",
    )(base, cp, pos0, pos1, hn)

    tile = lambda r, te, nu: (jnp.minimum(r, nu[0] - 1), 0, 0)
    ys = pl.pallas_call(
        _ffn_kernel,
        grid_spec=pltpu.PrefetchScalarGridSpec(
            num_scalar_prefetch=2,
            grid=(n_tiles,),
            in_specs=[pl.BlockSpec((tmm, SUBLANES, LANES), tile),
                      pl.BlockSpec((1, D, F), lambda r, te, nu: (te[r], 0, 0)),
                      pl.BlockSpec((1, D, F), lambda r, te, nu: (te[r], 0, 0)),
                      pl.BlockSpec((1, F, D), lambda r, te, nu: (te[r], 0, 0))],
            out_specs=pl.BlockSpec((tmm, SUBLANES, LANES), lambda r, te, nu: (r, 0, 0))),
        out_shape=jax.ShapeDtypeStruct((n_tiles * tmm, SUBLANES, LANES), F32),
        compiler_params=_cparams("arbitrary"),
        name="moe_ffn",
    )(tile_expert, n_used.reshape(1), xs, w_gate.reshape(E, D, F), w_up.reshape(E, D, F),
      w_down.reshape(E, F, D))

    cur = pl.BlockSpec((tc,), lambda i: (i,), memory_space=pltpu.SMEM)
    nxt = pl.BlockSpec((tc,), lambda i: (jnp.minimum(i + 1, T // tc - 1),), memory_space=pltpu.SMEM)
    return pl.pallas_call(
        functools.partial(_combine_kernel, tc=tc),
        grid=(T // tc,),
        in_specs=[cur, cur, nxt, nxt,
                  pl.BlockSpec(memory_space=pl.ANY),
                  row(tc, D), row(tc, ROUTE_LANES), pl.BlockSpec((1, D), lambda i: (0, 0))],
        out_specs=row(tc, D),
        out_shape=jax.ShapeDtypeStruct((T, D), F32),
        scratch_shapes=[pltpu.VMEM((2, 2, tc, SUBLANES, LANES), F32), pltpu.SemaphoreType.DMA((2,))],
        compiler_params=_cparams("arbitrary"),
        name="moe_combine",
    )(pos0, pos1, pos0, pos1, ys, x2d, route, fg.reshape(1, D))


def kernel(x, mem, mix_norm_g, w_in, q_norm_g, kv_norm_g, w_uq, w_ukv, hy_conv_w, hy_conv_b, hy_w1, hy_b1, hy_freq, hy_w2, hy_b2, hy_w3, hy_b3, hy_decay, hy_skip, attn_out_g, hy_out_g, w_out, cross_norm_g, mem_norm_g, w_mq, w_mkv, w_mo, ffn_norm_g, w_route_group, b_route_group, w_route_expert, b_route_expert, w_gate, w_up, w_down, final_norm_g):
    B, S, D = x.shape
    depth = w_in.shape[0]
    consts = _dft_constants(S)
    xf = x.reshape(B * S, D)
    for l in range(depth):
        q, k, v, hx1, hx2, hv = _inproj(xf, S, mix_norm_g[l], w_in[l], q_norm_g[l], kv_norm_g[l],
                                        w_uq[l], w_ukv[l], hy_conv_w[l], hy_conv_b[l])
        HP = q.shape[1]
        a_out = _attention(q.reshape(B, S, HP), k.reshape(B, S, HP), v.reshape(B, S, HP))
        filt = _hyena_filter_time(S, hy_w1[l], hy_b1[l], hy_freq[l], hy_w2[l], hy_b2[l], hy_w3[l],
                                  hy_b3[l], hy_decay[l])
        kf = _hyena_filter_spectrum(filt, consts[1], consts[3])
        C = hv.shape[1]
        h_out = _hyena(hx1.reshape(B, S, C), hx2.reshape(B, S, C), hv.reshape(B, S, C),
                       hy_skip[l], kf, consts)
        mk, mv = _memkv(mem, mem_norm_g[l], w_mkv[l])
        x2, hn, route, route_t = _postmix(xf, a_out.reshape(B * S, -1), h_out.reshape(B * S, C), S,
                                attn_out_g[l], hy_out_g[l], w_out[l], cross_norm_g[l], w_mq[l], mk, mv,
                                w_mo[l], ffn_norm_g[l], w_route_group[l], b_route_group[l],
                                w_route_expert[l], b_route_expert[l])
        assert depth == 1
        xf = _moe(hn, route, route_t, x2, w_gate[l], w_up[l], w_down[l], final_norm_g)
    return xf.reshape(B, S, D)
```

```python
import functools
import math

import numpy as np
import jax
import jax.numpy as jnp
from jax import lax
from jax.experimental import pallas as pl
from jax.experimental.pallas import tpu as pltpu

F32 = jnp.float32
BF16 = jnp.bfloat16

EPS = 1e-6
MLA_HEADS = 8
MLA_NOPE = 64
MLA_ROPE = 32
MLA_V = 64
ROPE_BASE = 10000.0
HEAD_PAD = 128
HY_ORDER = 2
HY_DIRS = 2
HY_BANDS = 16
MEM_HEADS = 4
N_GROUPS = 4
EXPERTS_PER_GROUP = 8
N_EXPERTS = N_GROUPS * EXPERTS_PER_GROUP
ROUTE_LANES = 128
ROUTE_ID0 = 0
ROUTE_W0 = 2
MOE_ROW_TILE = 512
DMA_UNROLL = 8
ROW_SUBTILES = 1

FFT_N1 = 64
FFT_N2 = 128
DFT_K1_BLOCK = 4
DFT_N2_BLOCK = 32
DFT_PITCH_PAD = 8

VMEM_LIMIT = 56 * 1024 * 1024


def _cparams(*sem):
    return pltpu.CompilerParams(dimension_semantics=sem, vmem_limit_bytes=VMEM_LIMIT)


def _rms(x, g):
    return x * lax.rsqrt(jnp.mean(x * x, axis=-1, keepdims=True) + EPS) * g


def _dot(a, b):
    return jnp.dot(a, b, preferred_element_type=F32)


SUBLANES = 8
LANES = 128


def _load_row_tiles(ref, lead=()):
    rows = ref.shape[-3]
    flat = ref.reshape(*ref.shape[:-3], rows * SUBLANES, LANES)
    return jnp.concatenate(
        [flat[(*lead, pl.ds(j, rows, stride=SUBLANES), slice(None))] for j in range(SUBLANES)], axis=1)


def _store_row_tiles(ref, val):
    rows = ref.shape[0]
    flat = ref.reshape(rows * SUBLANES, LANES)
    for j in range(SUBLANES):
        flat[pl.ds(j, rows, stride=SUBLANES), :] = val[:, j * LANES:(j + 1) * LANES]


def _inproj_kernel(x_ref, xp_ref, xn_ref, g_ref, wq_ref, wkv_ref, wkra_ref, wkrb_ref, why_ref, qg_ref,
                   kvg_ref, wqa_ref, wqb_ref, wka_ref, wv_ref, tab_ref, cw_ref, cb_ref,
                   q_out, k_out, v_out, x1_out, x2_out, hv_out, *, nseq):
    tm = x_ref.shape[0]
    halo = xp_ref.shape[0]
    ts = tm // ROW_SUBTILES
    x_ext = jnp.concatenate([xp_ref[...], x_ref[...], xn_ref[...]], axis=0)
    i = pl.program_id(0) % nseq
    c = x1_out.shape[1]
    tile = lambda t: jnp.concatenate([t] * MLA_HEADS, axis=1)
    lane = lax.broadcasted_iota(jnp.int32, (1, v_out.shape[1]), 1) % HEAD_PAD
    for s in range(ROW_SUBTILES):
        rows = slice(s * ts, (s + 1) * ts)
        hf = _rms(x_ext[s * ts:(s + 1) * ts + 2 * halo], g_ref[...])
        h = hf[halo:halo + ts].astype(BF16)
        qn = _rms(_dot(h, wq_ref[...]), qg_ref[...]).astype(BF16)
        kvn = _rms(_dot(h, wkv_ref[...]), kvg_ref[...]).astype(BF16)
        tab = tab_ref[rows, :]
        cq, sq, ck, sk = (tab[:, j * HEAD_PAD:(j + 1) * HEAD_PAD] for j in range(4))
        q = _dot(qn, wqa_ref[...]) * tile(cq) + _dot(qn, wqb_ref[...]) * tile(sq)
        q_out[rows, :] = q.astype(BF16)
        kr = _dot(h, wkra_ref[...]) * ck + _dot(h, wkrb_ref[...]) * sk
        k_out[rows, :] = (_dot(kvn, wka_ref[...]) + tile(kr)).astype(BF16)
        v_out[rows, :] = (_dot(kvn, wv_ref[...]) + jnp.where(lane == MLA_V, 1.0, 0.0)).astype(BF16)
        hy = _dot(hf.astype(BF16), why_ref[...])
        row = lax.broadcasted_iota(jnp.int32, hy.shape, 0)
        outside = (row == halo - 1) & (i == 0) if s == 0 else None
        if s == ROW_SUBTILES - 1:
            last = (row == halo + ts) & (i == nseq - 1)
            outside = last if outside is None else outside | last
        if outside is not None:
            hy = jnp.where(outside, 0.0, hy)
        cw = cw_ref[...]
        u = (hy[halo - 1:halo - 1 + ts] * cw[0:1] + hy[halo:halo + ts] * cw[1:2]
             + hy[halo + 1:halo + 1 + ts] * cw[2:3] + cb_ref[...])
        x1_out[rows, :] = u[:, :c]
        x2_out[rows, :] = u[:, c:2 * c]
        hv_out[rows, :] = u[:, 2 * c:]


def _inproj(x2d, seq, mix_g, w_in, q_g, kv_g, w_uq, w_ukv, conv_w, conv_b, tm=512):
    T, D = x2d.shape
    per = tm // SUBLANES
    cb = conv_b.reshape(1, -1)
    q_rank, kv_rank = q_g.shape[0], kv_g.shape[0]
    off_kv = q_rank
    off_kr = off_kv + kv_rank
    off_hy = off_kr + MLA_ROPE
    C = (w_in.shape[1] - off_hy) // 3
    H = MLA_HEADS
    half = MLA_ROPE // 2
    wq = w_in[:, :off_kv].astype(BF16)
    wkv = w_in[:, off_kv:off_kr].astype(BF16)
    wkr = w_in[:, off_kr:off_hy]
    wkr_sw = jnp.concatenate([wkr[:, half:], wkr[:, :half]], axis=1)
    zpad = lambda n: jnp.zeros((D, n), F32)
    wkra = jnp.concatenate([zpad(MLA_NOPE), wkr, zpad(HEAD_PAD - MLA_NOPE - MLA_ROPE)], 1).astype(BF16)
    wkrb = jnp.concatenate([zpad(MLA_NOPE), wkr_sw, zpad(HEAD_PAD - MLA_NOPE - MLA_ROPE)], 1).astype(BF16)
    why = w_in[:, off_hy:].astype(BF16)

    uq = w_uq.reshape(q_rank, H, MLA_NOPE + MLA_ROPE)
    uq_n, uq_r = uq[..., :MLA_NOPE], uq[..., MLA_NOPE:]
    uq_rs = jnp.concatenate([uq_r[..., half:], uq_r[..., :half]], axis=-1)
    zq = lambda n: jnp.zeros((q_rank, H, n), F32)
    wqa = jnp.concatenate([uq_n, uq_r, zq(HEAD_PAD - MLA_NOPE - MLA_ROPE)], -1).reshape(q_rank, H * HEAD_PAD).astype(BF16)
    wqb = jnp.concatenate([zq(MLA_NOPE), uq_rs, zq(HEAD_PAD - MLA_NOPE - MLA_ROPE)], -1).reshape(q_rank, H * HEAD_PAD).astype(BF16)
    ukv = w_ukv.reshape(kv_rank, H, MLA_NOPE + MLA_V)
    zk = lambda n: jnp.zeros((kv_rank, H, n), F32)
    wka = jnp.concatenate([ukv[..., :MLA_NOPE], zk(HEAD_PAD - MLA_NOPE)], -1).reshape(kv_rank, H * HEAD_PAD).astype(BF16)
    wv = jnp.concatenate([ukv[..., MLA_NOPE:], zk(HEAD_PAD - MLA_V)], -1).reshape(kv_rank, H * HEAD_PAD).astype(BF16)

    ang = np.arange(seq)[:, None] * ROPE_BASE ** (-np.arange(half) / half)[None, :]
    cos2 = np.concatenate([np.cos(ang), np.cos(ang)], 1)
    sin2 = np.concatenate([-np.sin(ang), np.sin(ang)], 1)
    zs = lambda n: np.zeros((seq, n))
    scale = (MLA_NOPE + MLA_ROPE) ** -0.5 * math.log2(math.e)
    rest = HEAD_PAD - MLA_NOPE - MLA_ROPE
    cq = scale * np.concatenate([np.ones((seq, MLA_NOPE)), cos2, zs(rest)], 1)
    sq = scale * np.concatenate([zs(MLA_NOPE), sin2, zs(rest)], 1)
    ck = np.concatenate([zs(MLA_NOPE), cos2, zs(rest)], 1)
    sk = np.concatenate([zs(MLA_NOPE), sin2, zs(rest)], 1)
    tab = jnp.asarray(np.concatenate([cq, sq, ck, sk], 1), dtype=F32)

    nseq = seq // tm
    full = lambda a: pl.BlockSpec(a.shape, lambda i: (0,) * a.ndim)
    row = lambda n: pl.BlockSpec((tm, n), lambda i: (i, 0))
    consts = [mix_g.reshape(1, D), wq, wkv, wkra, wkrb, why, q_g.reshape(1, -1), kv_g.reshape(1, -1),
              wqa, wqb, wka, wv]
    HP = H * HEAD_PAD
    return pl.pallas_call(
        functools.partial(_inproj_kernel, nseq=nseq),
        grid=(T // tm,),
        in_specs=[row(D),
                  pl.BlockSpec((SUBLANES, D), lambda i: (jnp.maximum(i * per - 1, 0), 0)),
                  pl.BlockSpec((SUBLANES, D), lambda i: (jnp.minimum((i + 1) * per, T // SUBLANES - 1), 0))]
        + [full(a) for a in consts]
        + [pl.BlockSpec((tm, 4 * HEAD_PAD), lambda i: (i % nseq, 0)), full(conv_w), full(cb)],
        out_specs=[row(HP), row(HP), row(HP), row(C), row(C), row(C)],
        out_shape=[jax.ShapeDtypeStruct((T, HP), BF16)] * 3 + [jax.ShapeDtypeStruct((T, C), F32)] * 3,
        compiler_params=_cparams("parallel"),
        name="inproj",
    )(x2d, x2d, x2d, *consts, tab, conv_w, cb)


def _attn_kernel(q_ref, k_ref, v_ref, o_ref):
    outs = []
    for h in range(MLA_HEADS):
        sl = slice(h * HEAD_PAD, (h + 1) * HEAD_PAD)
        s = lax.dot_general(q_ref[0, :, sl], k_ref[0, :, sl], (((1,), (1,)), ((), ())),
                            preferred_element_type=F32).astype(BF16)
        p = jnp.exp2(s - jnp.max(s, axis=-1, keepdims=True))
        o = _dot(p, v_ref[0, :, sl])
        outs.append(o[:, :MLA_V] / o[:, MLA_V:MLA_V + 1])
    o_ref[0] = jnp.concatenate(outs, axis=1)


def _attention(q, k, v, tq=512):
    B, S, HP = q.shape
    return pl.pallas_call(
        _attn_kernel,
        grid=(B, S // tq),
        in_specs=[pl.BlockSpec((1, tq, HP), lambda b, i: (b, i, 0)),
                  pl.BlockSpec((1, S, HP), lambda b, i: (b, 0, 0)),
                  pl.BlockSpec((1, S, HP), lambda b, i: (b, 0, 0))],
        out_specs=pl.BlockSpec((1, tq, MLA_HEADS * MLA_V), lambda b, i: (b, i, 0)),
        out_shape=jax.ShapeDtypeStruct((B, S, MLA_HEADS * MLA_V), F32),
        compiler_params=_cparams("parallel", "arbitrary"),
        name="mla_attention",
    )(q, k, v)


def _dft_constants(seq):
    n = 2 * seq
    n1, n2 = FFT_N1, FFT_N2
    assert n1 * n2 == n
    r1 = np.arange(n1)
    r2 = np.arange(n2)
    blk = lambda z: np.block([[z.real, -z.imag], [z.imag, z.real]])
    w1 = np.exp(-2j * np.pi * np.outer(r1, r1) / n1)
    fa_data = blk(w1[:, :n1 // 2])
    fa_filt = np.concatenate([w1.real, w1.imag], axis=0)
    fc = blk(np.conj(w1).T[:n1 // 2, :])
    w2 = np.exp(-2j * np.pi * np.outer(r2, r2) / n2)
    tw = np.exp(-2j * np.pi * np.outer(r1, r2) / n)
    fb = np.stack([blk(w2 * tw[k][None, :]) for k in range(n1)])
    fbi = np.stack([blk(np.conj(w2).T * np.conj(tw[k])[:, None] / n) for k in range(n1)])
    as_bf = lambda a: jnp.asarray(a, dtype=F32).astype(BF16)
    return as_bf(fa_data), as_bf(fa_filt), as_bf(fc), as_bf(fb), as_bf(fbi)


def _filter_kernel(z_ref, w1_ref, b1_ref, fr_ref, w2_ref, b2_ref, w3_ref, b3_ref, dec_ref, o_ref, *, seq, tr):
    hp = lax.Precision.HIGHEST
    z = z_ref[...]
    fr = fr_ref[...]
    h = jnp.sin(fr[0:1] * (jnp.dot(z, w1_ref[...], precision=hp, preferred_element_type=F32) + b1_ref[...]))
    h = jnp.sin(fr[1:2] * (jnp.dot(h, w2_ref[...], precision=hp, preferred_element_type=F32) + b2_ref[...]))
    split = lambda a: (a.astype(BF16), (a - a.astype(BF16).astype(F32)).astype(BF16))
    (hh, hl), (wh, wl) = split(h), split(w3_ref[0])
    h = _dot(hh, wh) + _dot(hh, wl) + _dot(hl, wh) + b3_ref[0]
    h = h * jnp.exp(-z[:, 0:1] * jnp.abs(dec_ref[0]))
    n = pl.program_id(0) * tr + lax.broadcasted_iota(jnp.int32, h.shape, 0)
    o_ref[...] = jnp.where(n == seq, 0.0, h)


def _hyena_filter_time(seq, w1, b1, freq, w2, b2, w3, b3, decay, tr=512):
    n = 2 * seq
    emb, ffn = w1.shape
    C = w3.shape[1] // (HY_ORDER * HY_DIRS)
    off = np.arange(n)
    t = np.where(off < seq, off, n - off).astype(np.float64)
    bands = np.linspace(1e-4, HY_BANDS - 1, HY_BANDS)
    ang = 2.0 * math.pi * t[:, None] * bands[None, :] / seq
    z = np.concatenate([(t / seq)[:, None], np.cos(ang), -np.sin(ang)], axis=-1)
    zl = LANES
    z = jnp.asarray(np.pad(z, ((0, 0), (0, zl - emb))), dtype=F32)
    w1p = jnp.pad(w1, ((0, zl - emb), (0, 0)))
    by_dir = lambda a: jnp.moveaxis(a.reshape(a.shape[0], HY_ORDER, HY_DIRS, C), 2, 0).reshape(
        HY_DIRS, a.shape[0], HY_ORDER * C)
    w3d, b3d, decd = by_dir(w3), by_dir(b3.reshape(1, -1)), by_dir(decay.reshape(1, -1))
    full = lambda a: pl.BlockSpec(a.shape, lambda i: (0,) * a.ndim)
    ndir = lambda a: pl.BlockSpec((1,) + a.shape[1:], lambda i: ((i * tr) // seq, 0, 0))
    consts = [w1p, b1.reshape(1, -1), freq, w2, b2.reshape(1, -1)]
    return pl.pallas_call(
        functools.partial(_filter_kernel, seq=seq, tr=tr),
        grid=(n // tr,),
        in_specs=[pl.BlockSpec((tr, zl), lambda i: (i, 0))] + [full(a) for a in consts]
        + [ndir(w3d), ndir(b3d), ndir(decd)],
        out_specs=pl.BlockSpec((tr, HY_ORDER * C), lambda i: (i, 0)),
        out_shape=jax.ShapeDtypeStruct((n, HY_ORDER * C), F32),
        compiler_params=_cparams("parallel"),
        name="hyena_filter_mlp",
    )(z, *consts, w3d, b3d, decd)


def _pitched(rows, nb):
    return pltpu.VMEM((rows, nb + DFT_PITCH_PAD, LANES), F32)


def _block_rows(ref):
    return math.prod(ref.shape[:-2]), ref.shape[-2]


def _copy_in(ref, scr):
    rows, nb = _block_rows(ref)
    scr[:, :nb, :] = ref[...].reshape(rows, nb, LANES)


def _copy_out(scr, ref):
    rows, nb = _block_rows(ref)
    ref[...] = scr[:, :nb, :].reshape(ref.shape)


def _at_n2(scr, n):
    rows, pitch, _ = scr.shape
    return scr.reshape(rows * pitch, LANES).at[pl.ds(n, rows, stride=pitch), :]


def _outer_dft(mat_ref, x_ref, o_ref, xs, os):
    _copy_in(x_ref, xs)
    for n in range(x_ref.shape[-2]):
        _at_n2(os, n)[...] = _dot(mat_ref[...], _at_n2(xs, n)[...].astype(BF16))
    _copy_out(os, o_ref)


def _filter_stage_a_kernel(x_ref, fa_ref, o_ref, xs, os):
    _outer_dft(fa_ref, x_ref, o_ref, xs, os)


def _filter_stage_b_kernel(x_ref, fb_ref, o_ref):
    for kk in range(x_ref.shape[1]):
        x = jnp.concatenate([x_ref[0, kk], x_ref[1, kk]], axis=0).astype(BF16)
        o_ref[kk] = _dot(fb_ref[kk], x)


def _hyena_filter_spectrum(filt, fa_filt, fb, nb=DFT_N2_BLOCK):
    n, oc = filt.shape
    a = pl.pallas_call(
        _filter_stage_a_kernel,
        grid=(FFT_N2 // nb, oc // LANES),
        in_specs=[pl.BlockSpec((FFT_N1, nb, LANES), lambda j, c: (0, j, c)),
                  pl.BlockSpec(fa_filt.shape, lambda j, c: (0, 0))],
        out_specs=pl.BlockSpec((2, FFT_N1, nb, LANES), lambda j, c: (0, 0, j, c)),
        out_shape=jax.ShapeDtypeStruct((2, FFT_N1, FFT_N2, oc), F32),
        scratch_shapes=[_pitched(FFT_N1, nb), _pitched(2 * FFT_N1, nb)],
        compiler_params=_cparams("parallel", "parallel"),
        name="hyena_filter_dft_a",
    )(filt.reshape(FFT_N1, FFT_N2, oc), fa_filt)
    return pl.pallas_call(
        _filter_stage_b_kernel,
        grid=(FFT_N1 // DFT_K1_BLOCK,),
        in_specs=[pl.BlockSpec((2, DFT_K1_BLOCK, FFT_N2, oc), lambda k: (0, k, 0, 0)),
                  pl.BlockSpec((DFT_K1_BLOCK, 2 * FFT_N2, 2 * FFT_N2), lambda k: (k, 0, 0))],
        out_specs=pl.BlockSpec((DFT_K1_BLOCK, 2 * FFT_N2, oc), lambda k: (k, 0, 0)),
        out_shape=jax.ShapeDtypeStruct((FFT_N1, 2 * FFT_N2, oc), F32),
        compiler_params=_cparams("parallel"),
        name="hyena_filter_dft_b",
    )(a, fb)


def _stage_a_kernel(x_ref, fa_ref, o_ref, xs, os):
    _outer_dft(fa_ref, x_ref, o_ref, xs, os)


def _stage_a(x4, fa, nb):
    B, r, n2, C = x4.shape
    return pl.pallas_call(
        _stage_a_kernel,
        grid=(B // 2, n2 // nb, C // LANES),
        in_specs=[pl.BlockSpec((2, r, nb, LANES), lambda p, j, c: (p, 0, j, c)),
                  pl.BlockSpec(fa.shape, lambda p, j, c: (0, 0))],
        out_specs=pl.BlockSpec((1, 2, FFT_N1, nb, LANES), lambda p, j, c: (p, 0, 0, j, c)),
        out_shape=jax.ShapeDtypeStruct((B // 2, 2, FFT_N1, n2, C), F32),
        scratch_shapes=[_pitched(2 * r, nb), _pitched(2 * FFT_N1, nb)],
        compiler_params=_cparams("parallel", "parallel", "parallel"),
        name="hyena_dft_a",
    )(x4, fa)


def _stage_b_kernel(x_ref, fb_ref, kf_ref, fbi_ref, o_ref):
    npair = x_ref.shape[0]
    n2 = x_ref.shape[3]
    c = x_ref.shape[4]
    for kk in range(x_ref.shape[2]):
        x = jnp.concatenate(
            [jnp.concatenate([x_ref[p, 0, kk], x_ref[p, 1, kk]], axis=0) for p in range(npair)],
            axis=1).astype(BF16)
        g = _dot(fb_ref[kk], x)
        gr, gi = g[:n2], g[n2:]
        kf = kf_ref[kk]
        kr = jnp.concatenate([kf[:n2]] * npair, axis=1)
        ki = jnp.concatenate([kf[n2:]] * npair, axis=1)
        hcat = jnp.concatenate([gr * kr - gi * ki, gr * ki + gi * kr], axis=0).astype(BF16)
        y = _dot(fbi_ref[kk], hcat)
        for p in range(npair):
            o_ref[p, 0, kk] = y[:n2, p * c:(p + 1) * c].astype(o_ref.dtype)
            o_ref[p, 1, kk] = y[n2:, p * c:(p + 1) * c].astype(o_ref.dtype)


def _stage_b(spec, fb, kf, fbi, order, nk=DFT_K1_BLOCK):
    npair, _, _, _, C = spec.shape
    blk = pl.BlockSpec((npair, 2, nk, FFT_N2, C), lambda k: (0, 0, k, 0, 0))
    mat = pl.BlockSpec((nk, 2 * FFT_N2, 2 * FFT_N2), lambda k: (k, 0, 0))
    return pl.pallas_call(
        _stage_b_kernel,
        grid=(FFT_N1 // nk,),
        in_specs=[blk, mat, pl.BlockSpec((nk, 2 * FFT_N2, C), lambda k: (k, 0, order)), mat],
        out_specs=blk,
        out_shape=jax.ShapeDtypeStruct(spec.shape, F32),
        compiler_params=_cparams("parallel"),
        name="hyena_dft_b",
    )(spec, fb, kf, fbi)


def _stage_c_kernel(y_ref, fc_ref, gate_ref, z_ref, skip_ref, *rest, stage_a):
    if stage_a:
        fa_ref, z_out, a_out, ys, gs, zs, os, as_ = rest
    else:
        z_out, ys, gs, zs, os = rest
    nb = gate_ref.shape[-2]
    _copy_in(y_ref, ys)
    _copy_in(gate_ref, gs)
    _copy_in(z_ref, zs)
    skip = skip_ref[...]
    for n in range(nb):
        conv = _dot(fc_ref[...], _at_n2(ys, n)[...].astype(BF16))
        _at_n2(os, n)[...] = _at_n2(gs, n)[...] * (conv + skip * _at_n2(zs, n)[...])
    _copy_out(os, z_out)
    if stage_a:
        for n in range(nb):
            _at_n2(as_, n)[...] = _dot(fa_ref[...], _at_n2(os, n)[...].astype(BF16))
        _copy_out(as_, a_out)


def _stage_c(yspec, fc, gate, zin, skip, fa=None, nb=DFT_N2_BLOCK):
    B, r, n2, C = gate.shape
    dat = pl.BlockSpec((2, r, nb, LANES), lambda p, j, c: (p, 0, j, c))
    spc = pl.BlockSpec((1, 2, FFT_N1, nb, LANES), lambda p, j, c: (p, 0, 0, j, c))
    in_specs = [spc, pl.BlockSpec(fc.shape, lambda p, j, c: (0, 0)), dat, dat,
                pl.BlockSpec((1, LANES), lambda p, j, c: (0, c))]
    out_specs = [dat]
    out_shape = [jax.ShapeDtypeStruct(gate.shape, F32)]
    args = [yspec, fc, gate, zin, skip.reshape(1, C)]
    scratch = [_pitched(2 * FFT_N1, nb)] + [_pitched(2 * r, nb)] * 3
    if fa is not None:
        in_specs.append(pl.BlockSpec(fa.shape, lambda p, j, c: (0, 0)))
        out_specs.append(spc)
        out_shape.append(jax.ShapeDtypeStruct(yspec.shape, F32))
        args.append(fa)
        scratch.append(_pitched(2 * FFT_N1, nb))
    return pl.pallas_call(
        functools.partial(_stage_c_kernel, stage_a=fa is not None),
        grid=(B // 2, n2 // nb, C // LANES),
        in_specs=in_specs, out_specs=out_specs, out_shape=out_shape,
        scratch_shapes=scratch,
        compiler_params=_cparams("parallel", "parallel", "parallel"),
        name="hyena_dft_c",
    )(*args)


def _hyena(x1, x2, v, skip, kf, consts, nb=DFT_N2_BLOCK):
    fa_data, _, fc, fb, fbi = consts
    B, S, C = v.shape
    split = lambda a: a.reshape(B, S // FFT_N2, FFT_N2, C)
    a0 = _stage_a(split(v), fa_data, nb)
    y0 = _stage_b(a0, fb, kf, fbi, 0)
    z1, a1 = _stage_c(y0, fc, split(x1), split(v), skip[0], fa=fa_data, nb=nb)
    y1 = _stage_b(a1, fb, kf, fbi, 1)
    (out,) = _stage_c(y1, fc, split(x2), z1, skip[1], nb=nb)
    return out.reshape(B, S, C)


def _memkv_kernel(m_ref, g_ref, w_ref, k_out, v_out):
    hm = _rms(m_ref[0], g_ref[...]).astype(BF16)
    kv = _dot(hm, w_ref[...])
    d = k_out.shape[2]
    k_out[0] = kv[:, :d].astype(BF16)
    v_out[0] = kv[:, d:].astype(BF16)


def _memkv(mem, g, w_mkv):
    B, M, D = mem.shape
    dk = w_mkv.shape[1] // 2
    w = w_mkv.astype(BF16)
    return pl.pallas_call(
        _memkv_kernel,
        grid=(B,),
        in_specs=[pl.BlockSpec((1, M, D), lambda b: (b, 0, 0)),
                  pl.BlockSpec((1, D), lambda b: (0, 0)),
                  pl.BlockSpec(w.shape, lambda b: (0, 0))],
        out_specs=[pl.BlockSpec((1, M, dk), lambda b: (b, 0, 0))] * 2,
        out_shape=[jax.ShapeDtypeStruct((B, M, dk), BF16)] * 2,
        compiler_params=_cparams("parallel"),
        name="mem_kv",
    )(mem, g.reshape(1, D), w)


def _route(logits):
    lane = lax.broadcasted_iota(jnp.int32, logits.shape, 1)
    ninf = -jnp.inf
    big = ROUTE_LANES
    first = lambda mask: jnp.min(jnp.where(mask, lane, big), axis=-1, keepdims=True)
    is_g = (lane >= N_EXPERTS) & (lane < N_EXPERTS + N_GROUPS)
    gl = jnp.where(is_g, logits, ninf)
    gmax = jnp.max(gl, axis=-1, keepdims=True)
    g_idx = first(gl == gmax) - N_EXPERTS
    p_group = 1.0 / jnp.sum(jnp.exp(gl - gmax), axis=-1, keepdims=True)
    in_g = (lane < N_EXPERTS) & ((lane // EXPERTS_PER_GROUP) == g_idx)
    el = jnp.where(in_g, logits, ninf)
    v1 = jnp.max(el, axis=-1, keepdims=True)
    i1 = first(el == v1)
    el2 = jnp.where(lane == i1, ninf, el)
    v2 = jnp.max(el2, axis=-1, keepdims=True)
    i2 = first(el2 == v2)
    e2 = jnp.exp(v2 - v1)
    p1 = 1.0 / (1.0 + e2)
    p2 = e2 / (1.0 + e2)
    sel = lambda n, val: jnp.where(lane == n, val, 0.0)
    return (sel(ROUTE_ID0, i1.astype(F32)) + sel(ROUTE_ID0 + 1, i2.astype(F32))
            + sel(ROUTE_W0, p_group * p1) + sel(ROUTE_W0 + 1, p_group * p2))


def _postmix_kernel(x_ref, a_ref, hy_ref, ag_ref, hg_ref, woa_ref, woh_ref, cg_ref, wmq_ref,
                    mk_ref, mv_ref, wmo_ref, fg_ref, wr_ref, br_ref, x_out, hn_out, route_out,
                    route_t_out):
    ts = x_ref.shape[0] // ROW_SUBTILES
    hn_flat = hn_out.reshape(hn_out.shape[0] * SUBLANES, LANES)
    for t in range(ROW_SUBTILES):
        rows = slice(t * ts, (t + 1) * ts)
        ra = _rms(a_ref[rows, :], ag_ref[...]).astype(BF16)
        rh = _rms(hy_ref[rows, :], hg_ref[...]).astype(BF16)
        x = x_ref[rows, :] + _dot(ra, woa_ref[...]) + _dot(rh, woh_ref[...])
        q = _dot(_rms(x, cg_ref[...]).astype(BF16), wmq_ref[...])
        dh = q.shape[1] // MEM_HEADS
        outs = []
        for h in range(MEM_HEADS):
            sl = slice(h * dh, (h + 1) * dh)
            s = lax.dot_general(q[:, sl].astype(BF16), mk_ref[0, :, sl], (((1,), (1,)), ((), ())),
                                preferred_element_type=F32) * dh ** -0.5
            p = jnp.exp(s - jnp.max(s, axis=-1, keepdims=True))
            l = jnp.sum(p, axis=-1, keepdims=True)
            outs.append(_dot(p.astype(BF16), mv_ref[0, :, sl]) / l)
        o = jnp.concatenate(outs, axis=1).astype(BF16)
        x = x + _dot(o, wmo_ref[...])
        x_out[rows, :] = x
        hn = _rms(x, fg_ref[...])
        for j in range(SUBLANES):
            hn_flat[pl.ds(t * ts * SUBLANES + j, ts, stride=SUBLANES), :] = hn[:, j * LANES:(j + 1) * LANES]
        route = _route(_dot(hn.astype(BF16), wr_ref[...]) + br_ref[...])
        route_out[rows, :] = route
        route_t_out[0, :, rows] = route.T[:SUBLANES, :]


def _postmix(x2d, a2d, hy2d, seq, ag, hg, w_out, cg, w_mq, mk, mv, w_mo, fg, w_rg, b_rg, w_re, b_re, tm=512):
    T, D = x2d.shape
    ca = a2d.shape[1]
    woa = w_out[:ca].astype(BF16)
    woh = w_out[ca:].astype(BF16)
    pad = ROUTE_LANES - N_EXPERTS - N_GROUPS
    wr = jnp.concatenate([w_re, w_rg, jnp.zeros((D, pad), F32)], 1).astype(BF16)
    br = jnp.concatenate([b_re, b_rg, jnp.zeros((pad,), F32)]).reshape(1, ROUTE_LANES)
    nseq = seq // tm
    full = lambda a: pl.BlockSpec(a.shape, lambda i: (0,) * a.ndim)
    row = lambda n: pl.BlockSpec((tm, n), lambda i: (i, 0))
    memb = pl.BlockSpec((1,) + mk.shape[1:], lambda i: (i // nseq, 0, 0))
    args = [x2d, a2d, hy2d, ag.reshape(1, -1), hg.reshape(1, -1), woa, woh, cg.reshape(1, D),
            w_mq.astype(BF16), mk, mv, w_mo.astype(BF16), fg.reshape(1, D), wr, br]
    in_specs = [row(D), row(ca), row(hy2d.shape[1])] + [full(a) for a in args[3:9]] + [memb, memb] \
        + [full(a) for a in args[11:]]
    return pl.pallas_call(
        _postmix_kernel,
        grid=(T // tm,),
        in_specs=in_specs,
        out_specs=[row(D), pl.BlockSpec((tm, SUBLANES, LANES), lambda i: (i, 0, 0)), row(ROUTE_LANES),
                   pl.BlockSpec((1, SUBLANES, tm), lambda i: (i, 0, 0))],
        out_shape=[jax.ShapeDtypeStruct((T, D), F32), jax.ShapeDtypeStruct((T, SUBLANES, LANES), F32),
                   jax.ShapeDtypeStruct((T, ROUTE_LANES), F32),
                   jax.ShapeDtypeStruct((T // tm, SUBLANES, tm), F32)],
        compiler_params=_cparams("parallel"),
        name="postmix",
    )(*args)


def _slot_onehots(rt):
    e = lax.broadcasted_iota(jnp.int32, (N_EXPERTS, rt.shape[1]), 0).astype(F32)
    return [e == rt[ROUTE_ID0 + k:ROUTE_ID0 + k + 1, :] for k in range(2)]


def _slot_rows(rows, width):
    sub = lax.broadcasted_iota(jnp.int32, (SUBLANES, width), 0)
    return jnp.where(sub == 0, rows[0], jnp.where(sub == 1, rows[1], 0.0))


def _rank_kernel(rt_ref, rank_out, cnt_out, carry_ref):
    @pl.when(pl.program_id(0) == 0)
    def _():
        carry_ref[...] = jnp.zeros_like(carry_ref)

    tr = rt_ref.shape[2]
    oh = _slot_onehots(rt_ref[0])
    cnt = jnp.where(oh[0] | oh[1], 1.0, 0.0)
    s = lax.broadcasted_iota(jnp.int32, (tr, tr), 0)
    t = lax.broadcasted_iota(jnp.int32, (tr, tr), 1)
    before = jnp.where(s < t, 1.0, 0.0).astype(BF16)
    cum = _dot(cnt.astype(BF16), before) + carry_ref[...]
    ranks = [jnp.sum(jnp.where(m, cum, 0.0), axis=0, keepdims=True) for m in oh]
    rank_out[0] = _slot_rows(ranks, tr)
    carry_ref[...] += jnp.sum(cnt, axis=1, keepdims=True)
    cnt_out[...] = carry_ref[...]


def _pos_kernel(rt_ref, rank_ref, base_ref, pos_out):
    tr = rt_ref.shape[2]
    oh = _slot_onehots(rt_ref[0])
    rank = rank_ref[0]
    pos = [jnp.sum(jnp.where(m, base_ref[...], 0.0), axis=0, keepdims=True) + rank[k:k + 1, :]
           for k, m in enumerate(oh)]
    pos_out[0] = _slot_rows(pos, tr).astype(jnp.int32)


def _source_kernel(base_ref, cnt_ref, cp_ref, pos0_ref, pos1_ref, src_ref, *, tt):
    i = pl.program_id(0)

    @pl.when(i == 0)
    def _():
        def clear(lo, hi):
            def body(r, carry):
                src_ref[r] = 0
                return carry

            lax.fori_loop(lo, hi, body, 0)

        for e in range(N_EXPERTS):
            clear(base_ref[e] + cnt_ref[e], base_ref[e] + cp_ref[e])
        clear(base_ref[N_EXPERTS - 1] + cp_ref[N_EXPERTS - 1], src_ref.shape[0])

    def body(t, carry):
        src_ref[pos0_ref[t]] = i * tt + t
        src_ref[pos1_ref[t]] = i * tt + t
        return carry

    lax.fori_loop(0, tt, body, 0, unroll=DMA_UNROLL)


def _ffn_kernel(te_ref, nu_ref, tv_ref, src_ref, src_next_ref, hn_hbm, wg_ref, wu_ref, wd_ref, ys_ref,
                x_ref, sem):
    r = pl.program_id(0)
    slot = r % 2
    row_copy = lambda s, dst, tok: pltpu.make_async_copy(
        hn_hbm.at[pl.ds(tok, 1)], x_ref.at[s, pl.ds(dst, 1)], sem.at[s])

    def fetch(s_ref, s, rows):
        def start(j, carry):
            row_copy(s, 2 * j, s_ref[2 * j]).start(priority=0)

            @pl.when(2 * j + 1 < rows)
            def _():
                row_copy(s, 2 * j + 1, s_ref[2 * j + 1]).start(priority=1)

            return carry

        lax.fori_loop(0, (rows + 1) // 2, start, 0)

    @pl.when(r == 0)
    def _():
        x_ref[...] = jnp.zeros_like(x_ref)
        fetch(src_ref, 0, tv_ref[0])

    @pl.when(r + 1 < nu_ref[0])
    def _():
        fetch(src_next_ref, 1 - slot, tv_ref[r + 1])

    used = r < nu_ref[0]

    @pl.when(used)
    def _():
        def wait(t, carry):
            row_copy(slot, 0, 0).wait()
            return carry

        lax.fori_loop(0, tv_ref[r], wait, 0)
        x = _load_row_tiles(x_ref, (slot,)).astype(BF16)
        a = _dot(x, wg_ref[0].astype(BF16))
        b = _dot(x, wu_ref[0].astype(BF16))
        m = (a * jax.nn.sigmoid(a)) * b
        _store_row_tiles(ys_ref, _dot(m.astype(BF16), wd_ref[0].astype(BF16)))

    @pl.when(jnp.logical_not(used))
    def _():
        ys_ref[...] = jnp.zeros_like(ys_ref)


def _combine_kernel(pos0_ref, pos1_ref, pos0_next_ref, pos1_next_ref, ys_hbm, x_ref, route_ref, fg_ref,
                    o_ref, buf_ref, sem, *, tc):
    i = pl.program_id(0)
    slot = i % 2
    row_copy = lambda s, k, t, p: pltpu.make_async_copy(
        ys_hbm.at[pl.ds(p, 1)], buf_ref.at[s, k, pl.ds(t, 1)], sem.at[s])

    def fetch(p0_ref, p1_ref, s):
        def start(t, carry):
            row_copy(s, 0, t, p0_ref[t]).start(priority=0)
            row_copy(s, 1, t, p1_ref[t]).start(priority=1)
            return carry

        lax.fori_loop(0, tc, start, 0, unroll=DMA_UNROLL)

    @pl.when(i == 0)
    def _():
        fetch(pos0_ref, pos1_ref, 0)

    @pl.when(i + 1 < pl.num_programs(0))
    def _():
        fetch(pos0_next_ref, pos1_next_ref, 1 - slot)

    def wait(t, carry):
        row_copy(slot, 0, 0, 0).wait()
        row_copy(slot, 1, 0, 0).wait()
        return carry

    lax.fori_loop(0, tc, wait, 0, unroll=DMA_UNROLL)
    route = route_ref[...]
    y = (x_ref[...] + route[:, ROUTE_W0:ROUTE_W0 + 1] * _load_row_tiles(buf_ref, (slot, 0))
         + route[:, ROUTE_W0 + 1:ROUTE_W0 + 2] * _load_row_tiles(buf_ref, (slot, 1)))
    o_ref[...] = _rms(y, fg_ref[...])


def _moe(hn, route, route_t, x2d, w_gate, w_up, w_down, fg, tt=512, tc=256):
    T, D = x2d.shape
    E = N_EXPERTS
    F = w_gate.shape[-1]
    tmm = MOE_ROW_TILE
    ntr, _, tr = route_t.shape
    row = lambda tm, n: pl.BlockSpec((tm, n), lambda i: (i, 0))
    rec = pl.BlockSpec((1, SUBLANES, tr), lambda i: (i, 0, 0))
    col = pl.BlockSpec((E, 1), lambda i: (0, 0))
    rank, counts = pl.pallas_call(
        _rank_kernel,
        grid=(ntr,),
        in_specs=[rec],
        out_specs=[rec, col],
        out_shape=[jax.ShapeDtypeStruct(route_t.shape, F32), jax.ShapeDtypeStruct((E, 1), F32)],
        scratch_shapes=[pltpu.VMEM((E, 1), F32)],
        compiler_params=_cparams("arbitrary"),
        name="moe_rank",
    )(route_t)

    cnt = counts[:, 0].astype(jnp.int32)
    cp = ((cnt + tmm - 1) // tmm) * tmm
    ends = jnp.cumsum(cp)
    base = ends - cp
    n_used = ends[-1] // tmm
    n_tiles = (2 * T) // tmm + E
    tile_start = jnp.minimum(jnp.arange(n_tiles, dtype=jnp.int32), n_used - 1) * tmm
    tile_expert = jnp.minimum(jnp.sum((tile_start[:, None] >= ends[None, :]).astype(jnp.int32), axis=1), E - 1)

    pos = pl.pallas_call(
        _pos_kernel,
        grid=(ntr,),
        in_specs=[rec, rec, col],
        out_specs=rec,
        out_shape=jax.ShapeDtypeStruct(route_t.shape, jnp.int32),
        compiler_params=_cparams("parallel"),
        name="moe_pos",
    )(route_t, rank, base.astype(F32).reshape(E, 1))
    pos0, pos1 = pos[:, 0, :].reshape(T), pos[:, 1, :].reshape(T)

    n_rows = n_tiles * tmm
    tok = pl.BlockSpec((tt,), lambda i, b, n, c: (i,), memory_space=pltpu.SMEM)
    src = pl.pallas_call(
        functools.partial(_source_kernel, tt=tt),
        grid_spec=pltpu.PrefetchScalarGridSpec(
            num_scalar_prefetch=3,
            grid=(T // tt,),
            in_specs=[tok, tok],
            out_specs=pl.BlockSpec((n_rows,), lambda i, b, n, c: (0,), memory_space=pltpu.SMEM)),
        out_shape=jax.ShapeDtypeStruct((n_rows,), jnp.int32),
        compiler_params=_cparams("arbitrary"),
        name="moe_source",
    )(base, cnt, cp, pos0, pos1)

    tile_valid = jnp.clip(cnt[tile_expert] - (tile_start - base[tile_expert]), 0, tmm)
    tile_valid = jnp.where(jnp.arange(n_tiles) < n_used, tile_valid, 0).astype(jnp.int32)
    weights = lambda shape: pl.BlockSpec(shape, lambda r, te, nu, tv: (te[r], 0, 0))
    ys = pl.pallas_call(
        _ffn_kernel,
        grid_spec=pltpu.PrefetchScalarGridSpec(
            num_scalar_prefetch=3,
            grid=(n_tiles,),
            in_specs=[pl.BlockSpec((tmm,), lambda r, te, nu, tv: (jnp.minimum(r, nu[0] - 1),),
                                   memory_space=pltpu.SMEM),
                      pl.BlockSpec((tmm,), lambda r, te, nu, tv: (jnp.minimum(r + 1, nu[0] - 1),),
                                   memory_space=pltpu.SMEM),
                      pl.BlockSpec(memory_space=pl.ANY),
                      weights((1, D, F)), weights((1, D, F)), weights((1, F, D))],
            out_specs=pl.BlockSpec((tmm, SUBLANES, LANES), lambda r, te, nu, tv: (r, 0, 0)),
            scratch_shapes=[pltpu.VMEM((2, tmm, SUBLANES, LANES), F32), pltpu.SemaphoreType.DMA((2,))]),
        out_shape=jax.ShapeDtypeStruct((n_rows, SUBLANES, LANES), F32),
        compiler_params=_cparams("arbitrary"),
        name="moe_ffn",
    )(tile_expert, n_used.reshape(1), tile_valid, src, src, hn, w_gate.reshape(E, D, F),
      w_up.reshape(E, D, F), w_down.reshape(E, F, D))

    cur = pl.BlockSpec((tc,), lambda i: (i,), memory_space=pltpu.SMEM)
    nxt = pl.BlockSpec((tc,), lambda i: (jnp.minimum(i + 1, T // tc - 1),), memory_space=pltpu.SMEM)
    return pl.pallas_call(
        functools.partial(_combine_kernel, tc=tc),
        grid=(T // tc,),
        in_specs=[cur, cur, nxt, nxt,
                  pl.BlockSpec(memory_space=pl.ANY),
                  row(tc, D), row(tc, ROUTE_LANES), pl.BlockSpec((1, D), lambda i: (0, 0))],
        out_specs=row(tc, D),
        out_shape=jax.ShapeDtypeStruct((T, D), F32),
        scratch_shapes=[pltpu.VMEM((2, 2, tc, SUBLANES, LANES), F32), pltpu.SemaphoreType.DMA((2,))],
        compiler_params=_cparams("arbitrary"),
        name="moe_combine",
    )(pos0, pos1, pos0, pos1, ys, x2d, route, fg.reshape(1, D))


def kernel(x, mem, mix_norm_g, w_in, q_norm_g, kv_norm_g, w_uq, w_ukv, hy_conv_w, hy_conv_b, hy_w1, hy_b1, hy_freq, hy_w2, hy_b2, hy_w3, hy_b3, hy_decay, hy_skip, attn_out_g, hy_out_g, w_out, cross_norm_g, mem_norm_g, w_mq, w_mkv, w_mo, ffn_norm_g, w_route_group, b_route_group, w_route_expert, b_route_expert, w_gate, w_up, w_down, final_norm_g):
    B, S, D = x.shape
    depth = w_in.shape[0]
    consts = _dft_constants(S)
    xf = x.reshape(B * S, D)
    for l in range(depth):
        q, k, v, hx1, hx2, hv = _inproj(xf, S, mix_norm_g[l], w_in[l], q_norm_g[l], kv_norm_g[l],
                                        w_uq[l], w_ukv[l], hy_conv_w[l], hy_conv_b[l])
        HP = q.shape[1]
        a_out = _attention(q.reshape(B, S, HP), k.reshape(B, S, HP), v.reshape(B, S, HP))
        filt = _hyena_filter_time(S, hy_w1[l], hy_b1[l], hy_freq[l], hy_w2[l], hy_b2[l], hy_w3[l],
                                  hy_b3[l], hy_decay[l])
        kf = _hyena_filter_spectrum(filt, consts[1], consts[3])
        C = hv.shape[1]
        h_out = _hyena(hx1.reshape(B, S, C), hx2.reshape(B, S, C), hv.reshape(B, S, C),
                       hy_skip[l], kf, consts)
        mk, mv = _memkv(mem, mem_norm_g[l], w_mkv[l])
        x2, hn, route, route_t = _postmix(xf, a_out.reshape(B * S, -1), h_out.reshape(B * S, C), S,
                                attn_out_g[l], hy_out_g[l], w_out[l], cross_norm_g[l], w_mq[l], mk, mv,
                                w_mo[l], ffn_norm_g[l], w_route_group[l], b_route_group[l],
                                w_route_expert[l], b_route_expert[l])
        assert depth == 1
        xf = _moe(hn, route, route_t, x2, w_gate[l], w_up[l], w_down[l], final_norm_g)
    return xf.reshape(B, S, D)
```

```python
import functools
import math

import numpy as np
import jax
import jax.numpy as jnp
from jax import lax
from jax.experimental import pallas as pl
from jax.experimental.pallas import tpu as pltpu

F32 = jnp.float32
BF16 = jnp.bfloat16

EPS = 1e-6
MLA_HEADS = 8
MLA_NOPE = 64
MLA_ROPE = 32
MLA_V = 64
ROPE_BASE = 10000.0
HEAD_PAD = 128
HY_ORDER = 2
HY_DIRS = 2
HY_BANDS = 16
MEM_HEADS = 4
N_GROUPS = 4
EXPERTS_PER_GROUP = 8
N_EXPERTS = N_GROUPS * EXPERTS_PER_GROUP
ROUTE_LANES = 128
ROUTE_ID0 = 0
ROUTE_W0 = 2
MOE_ROW_TILE = 512
DMA_UNROLL = 8
ROW_SUBTILES = 1

FFT_N1 = 64
FFT_N2 = 128
DFT_K1_BLOCK = 4
DFT_N2_BLOCK = 32
DFT_PITCH_PAD = 8

VMEM_LIMIT = 56 * 1024 * 1024


def _cparams(*sem):
    return pltpu.CompilerParams(dimension_semantics=sem, vmem_limit_bytes=VMEM_LIMIT)


def _rms(x, g):
    return x * lax.rsqrt(jnp.mean(x * x, axis=-1, keepdims=True) + EPS) * g


def _dot(a, b):
    return jnp.dot(a, b, preferred_element_type=F32)


SUBLANES = 8
LANES = 128


def _load_row_tiles(ref, lead=()):
    rows = ref.shape[-3]
    flat = ref.reshape(*ref.shape[:-3], rows * SUBLANES, LANES)
    return jnp.concatenate(
        [flat[(*lead, pl.ds(j, rows, stride=SUBLANES), slice(None))] for j in range(SUBLANES)], axis=1)


def _store_row_tiles(ref, val):
    rows = ref.shape[0]
    flat = ref.reshape(rows * SUBLANES, LANES)
    for j in range(SUBLANES):
        flat[pl.ds(j, rows, stride=SUBLANES), :] = val[:, j * LANES:(j + 1) * LANES]


def _inproj_kernel(x_ref, xp_ref, xn_ref, g_ref, wq_ref, wkv_ref, wkra_ref, wkrb_ref, why_ref, qg_ref,
                   kvg_ref, wqa_ref, wqb_ref, wka_ref, wv_ref, tab_ref, cw_ref, cb_ref,
                   q_out, k_out, v_out, x1_out, x2_out, hv_out, *, nseq):
    tm = x_ref.shape[0]
    halo = xp_ref.shape[0]
    ts = tm // ROW_SUBTILES
    x_ext = jnp.concatenate([xp_ref[...], x_ref[...], xn_ref[...]], axis=0)
    i = pl.program_id(0) % nseq
    c = x1_out.shape[1]
    tile = lambda t: jnp.concatenate([t] * MLA_HEADS, axis=1)
    lane = lax.broadcasted_iota(jnp.int32, (1, v_out.shape[1]), 1) % HEAD_PAD
    for s in range(ROW_SUBTILES):
        rows = slice(s * ts, (s + 1) * ts)
        hf = _rms(x_ext[s * ts:(s + 1) * ts + 2 * halo], g_ref[...])
        h = hf[halo:halo + ts].astype(BF16)
        qn = _rms(_dot(h, wq_ref[...]), qg_ref[...]).astype(BF16)
        kvn = _rms(_dot(h, wkv_ref[...]), kvg_ref[...]).astype(BF16)
        tab = tab_ref[rows, :]
        cq, sq, ck, sk = (tab[:, j * HEAD_PAD:(j + 1) * HEAD_PAD] for j in range(4))
        q = _dot(qn, wqa_ref[...]) * tile(cq) + _dot(qn, wqb_ref[...]) * tile(sq)
        q_out[rows, :] = q.astype(BF16)
        kr = _dot(h, wkra_ref[...]) * ck + _dot(h, wkrb_ref[...]) * sk
        k_out[rows, :] = (_dot(kvn, wka_ref[...]) + tile(kr)).astype(BF16)
        v_out[rows, :] = (_dot(kvn, wv_ref[...]) + jnp.where(lane == MLA_V, 1.0, 0.0)).astype(BF16)
        hy = _dot(hf.astype(BF16), why_ref[...])
        row = lax.broadcasted_iota(jnp.int32, hy.shape, 0)
        outside = (row == halo - 1) & (i == 0) if s == 0 else None
        if s == ROW_SUBTILES - 1:
            last = (row == halo + ts) & (i == nseq - 1)
            outside = last if outside is None else outside | last
        if outside is not None:
            hy = jnp.where(outside, 0.0, hy)
        cw = cw_ref[...]
        u = (hy[halo - 1:halo - 1 + ts] * cw[0:1] + hy[halo:halo + ts] * cw[1:2]
             + hy[halo + 1:halo + 1 + ts] * cw[2:3] + cb_ref[...])
        x1_out[rows, :] = u[:, :c].astype(x1_out.dtype)
        x2_out[rows, :] = u[:, c:2 * c].astype(x2_out.dtype)
        hv_out[rows, :] = u[:, 2 * c:].astype(hv_out.dtype)


def _inproj(x2d, seq, mix_g, w_in, q_g, kv_g, w_uq, w_ukv, conv_w, conv_b, tm=512):
    T, D = x2d.shape
    per = tm // SUBLANES
    cb = conv_b.reshape(1, -1)
    q_rank, kv_rank = q_g.shape[0], kv_g.shape[0]
    off_kv = q_rank
    off_kr = off_kv + kv_rank
    off_hy = off_kr + MLA_ROPE
    C = (w_in.shape[1] - off_hy) // 3
    H = MLA_HEADS
    half = MLA_ROPE // 2
    wq = w_in[:, :off_kv].astype(BF16)
    wkv = w_in[:, off_kv:off_kr].astype(BF16)
    wkr = w_in[:, off_kr:off_hy]
    wkr_sw = jnp.concatenate([wkr[:, half:], wkr[:, :half]], axis=1)
    zpad = lambda n: jnp.zeros((D, n), F32)
    wkra = jnp.concatenate([zpad(MLA_NOPE), wkr, zpad(HEAD_PAD - MLA_NOPE - MLA_ROPE)], 1).astype(BF16)
    wkrb = jnp.concatenate([zpad(MLA_NOPE), wkr_sw, zpad(HEAD_PAD - MLA_NOPE - MLA_ROPE)], 1).astype(BF16)
    why = w_in[:, off_hy:].astype(BF16)

    uq = w_uq.reshape(q_rank, H, MLA_NOPE + MLA_ROPE)
    uq_n, uq_r = uq[..., :MLA_NOPE], uq[..., MLA_NOPE:]
    uq_rs = jnp.concatenate([uq_r[..., half:], uq_r[..., :half]], axis=-1)
    zq = lambda n: jnp.zeros((q_rank, H, n), F32)
    wqa = jnp.concatenate([uq_n, uq_r, zq(HEAD_PAD - MLA_NOPE - MLA_ROPE)], -1).reshape(q_rank, H * HEAD_PAD).astype(BF16)
    wqb = jnp.concatenate([zq(MLA_NOPE), uq_rs, zq(HEAD_PAD - MLA_NOPE - MLA_ROPE)], -1).reshape(q_rank, H * HEAD_PAD).astype(BF16)
    ukv = w_ukv.reshape(kv_rank, H, MLA_NOPE + MLA_V)
    zk = lambda n: jnp.zeros((kv_rank, H, n), F32)
    wka = jnp.concatenate([ukv[..., :MLA_NOPE], zk(HEAD_PAD - MLA_NOPE)], -1).reshape(kv_rank, H * HEAD_PAD).astype(BF16)
    wv = jnp.concatenate([ukv[..., MLA_NOPE:], zk(HEAD_PAD - MLA_V)], -1).reshape(kv_rank, H * HEAD_PAD).astype(BF16)

    ang = np.arange(seq)[:, None] * ROPE_BASE ** (-np.arange(half) / half)[None, :]
    cos2 = np.concatenate([np.cos(ang), np.cos(ang)], 1)
    sin2 = np.concatenate([-np.sin(ang), np.sin(ang)], 1)
    zs = lambda n: np.zeros((seq, n))
    scale = (MLA_NOPE + MLA_ROPE) ** -0.5 * math.log2(math.e)
    rest = HEAD_PAD - MLA_NOPE - MLA_ROPE
    cq = scale * np.concatenate([np.ones((seq, MLA_NOPE)), cos2, zs(rest)], 1)
    sq = scale * np.concatenate([zs(MLA_NOPE), sin2, zs(rest)], 1)
    ck = np.concatenate([zs(MLA_NOPE), cos2, zs(rest)], 1)
    sk = np.concatenate([zs(MLA_NOPE), sin2, zs(rest)], 1)
    tab = jnp.asarray(np.concatenate([cq, sq, ck, sk], 1), dtype=F32)

    nseq = seq // tm
    full = lambda a: pl.BlockSpec(a.shape, lambda i: (0,) * a.ndim)
    row = lambda n: pl.BlockSpec((tm, n), lambda i: (i, 0))
    consts = [mix_g.reshape(1, D), wq, wkv, wkra, wkrb, why, q_g.reshape(1, -1), kv_g.reshape(1, -1),
              wqa, wqb, wka, wv]
    HP = H * HEAD_PAD
    return pl.pallas_call(
        functools.partial(_inproj_kernel, nseq=nseq),
        grid=(T // tm,),
        in_specs=[row(D),
                  pl.BlockSpec((SUBLANES, D), lambda i: (jnp.maximum(i * per - 1, 0), 0)),
                  pl.BlockSpec((SUBLANES, D), lambda i: (jnp.minimum((i + 1) * per, T // SUBLANES - 1), 0))]
        + [full(a) for a in consts]
        + [pl.BlockSpec((tm, 4 * HEAD_PAD), lambda i: (i % nseq, 0)), full(conv_w), full(cb)],
        out_specs=[row(HP), row(HP), row(HP), row(C), row(C), row(C)],
        out_shape=[jax.ShapeDtypeStruct((T, HP), BF16)] * 3 + [jax.ShapeDtypeStruct((T, C), BF16)] * 3,
        compiler_params=_cparams("parallel"),
        name="inproj",
    )(x2d, x2d, x2d, *consts, tab, conv_w, cb)


def _attn_kernel(q_ref, k_ref, v_ref, o_ref):
    outs = []
    for h in range(MLA_HEADS):
        sl = slice(h * HEAD_PAD, (h + 1) * HEAD_PAD)
        s = lax.dot_general(q_ref[0, :, sl], k_ref[0, :, sl], (((1,), (1,)), ((), ())),
                            preferred_element_type=F32).astype(BF16)
        p = jnp.exp2(s - jnp.max(s, axis=-1, keepdims=True))
        o = _dot(p, v_ref[0, :, sl])
        outs.append(o[:, :MLA_V] / o[:, MLA_V:MLA_V + 1])
    o_ref[0] = jnp.concatenate(outs, axis=1).astype(o_ref.dtype)


def _attention(q, k, v, tq=512):
    B, S, HP = q.shape
    return pl.pallas_call(
        _attn_kernel,
        grid=(B, S // tq),
        in_specs=[pl.BlockSpec((1, tq, HP), lambda b, i: (b, i, 0)),
                  pl.BlockSpec((1, S, HP), lambda b, i: (b, 0, 0)),
                  pl.BlockSpec((1, S, HP), lambda b, i: (b, 0, 0))],
        out_specs=pl.BlockSpec((1, tq, MLA_HEADS * MLA_V), lambda b, i: (b, i, 0)),
        out_shape=jax.ShapeDtypeStruct((B, S, MLA_HEADS * MLA_V), BF16),
        compiler_params=_cparams("parallel", "arbitrary"),
        name="mla_attention",
    )(q, k, v)


def _dft_constants(seq):
    n = 2 * seq
    n1, n2 = FFT_N1, FFT_N2
    assert n1 * n2 == n
    r1 = np.arange(n1)
    r2 = np.arange(n2)
    blk = lambda z: np.block([[z.real, -z.imag], [z.imag, z.real]])
    w1 = np.exp(-2j * np.pi * np.outer(r1, r1) / n1)
    fa_data = blk(w1[:, :n1 // 2])
    fa_filt = np.concatenate([w1.real, w1.imag], axis=0)
    fc = blk(np.conj(w1).T[:n1 // 2, :])
    w2 = np.exp(-2j * np.pi * np.outer(r2, r2) / n2)
    tw = np.exp(-2j * np.pi * np.outer(r1, r2) / n)
    fb = np.stack([blk(w2 * tw[k][None, :]) for k in range(n1)])
    fbi = np.stack([blk(np.conj(w2).T * np.conj(tw[k])[:, None] / n) for k in range(n1)])
    as_bf = lambda a: jnp.asarray(a, dtype=F32).astype(BF16)
    return as_bf(fa_data), as_bf(fa_filt), as_bf(fc), as_bf(fb), as_bf(fbi)


def _filter_kernel(z_ref, w1_ref, b1_ref, fr_ref, w2_ref, b2_ref, w3_ref, b3_ref, dec_ref, o_ref, *, seq, tr):
    hp = lax.Precision.HIGHEST
    z = z_ref[...]
    fr = fr_ref[...]
    h = jnp.sin(fr[0:1] * (jnp.dot(z, w1_ref[...], precision=hp, preferred_element_type=F32) + b1_ref[...]))
    h = jnp.sin(fr[1:2] * (jnp.dot(h, w2_ref[...], precision=hp, preferred_element_type=F32) + b2_ref[...]))
    split = lambda a: (a.astype(BF16), (a - a.astype(BF16).astype(F32)).astype(BF16))
    (hh, hl), (wh, wl) = split(h), split(w3_ref[0])
    h = _dot(hh, wh) + _dot(hh, wl) + _dot(hl, wh) + b3_ref[0]
    h = h * jnp.exp(-z[:, 0:1] * jnp.abs(dec_ref[0]))
    n = pl.program_id(0) * tr + lax.broadcasted_iota(jnp.int32, h.shape, 0)
    o_ref[...] = jnp.where(n == seq, 0.0, h).astype(o_ref.dtype)


def _hyena_filter_time(seq, w1, b1, freq, w2, b2, w3, b3, decay, tr=512):
    n = 2 * seq
    emb, ffn = w1.shape
    C = w3.shape[1] // (HY_ORDER * HY_DIRS)
    off = np.arange(n)
    t = np.where(off < seq, off, n - off).astype(np.float64)
    bands = np.linspace(1e-4, HY_BANDS - 1, HY_BANDS)
    ang = 2.0 * math.pi * t[:, None] * bands[None, :] / seq
    z = np.concatenate([(t / seq)[:, None], np.cos(ang), -np.sin(ang)], axis=-1)
    zl = LANES
    z = jnp.asarray(np.pad(z, ((0, 0), (0, zl - emb))), dtype=F32)
    w1p = jnp.pad(w1, ((0, zl - emb), (0, 0)))
    by_dir = lambda a: jnp.moveaxis(a.reshape(a.shape[0], HY_ORDER, HY_DIRS, C), 2, 0).reshape(
        HY_DIRS, a.shape[0], HY_ORDER * C)
    w3d, b3d, decd = by_dir(w3), by_dir(b3.reshape(1, -1)), by_dir(decay.reshape(1, -1))
    full = lambda a: pl.BlockSpec(a.shape, lambda i: (0,) * a.ndim)
    ndir = lambda a: pl.BlockSpec((1,) + a.shape[1:], lambda i: ((i * tr) // seq, 0, 0))
    consts = [w1p, b1.reshape(1, -1), freq, w2, b2.reshape(1, -1)]
    return pl.pallas_call(
        functools.partial(_filter_kernel, seq=seq, tr=tr),
        grid=(n // tr,),
        in_specs=[pl.BlockSpec((tr, zl), lambda i: (i, 0))] + [full(a) for a in consts]
        + [ndir(w3d), ndir(b3d), ndir(decd)],
        out_specs=pl.BlockSpec((tr, HY_ORDER * C), lambda i: (i, 0)),
        out_shape=jax.ShapeDtypeStruct((n, HY_ORDER * C), BF16),
        compiler_params=_cparams("parallel"),
        name="hyena_filter_mlp",
    )(z, *consts, w3d, b3d, decd)


def _pitched(rows, nb):
    return pltpu.VMEM((rows, nb + DFT_PITCH_PAD, LANES), F32)


def _block_rows(ref):
    return math.prod(ref.shape[:-2]), ref.shape[-2]


def _copy_in(ref, scr):
    rows, nb = _block_rows(ref)
    scr[:, :nb, :] = ref[...].reshape(rows, nb, LANES).astype(scr.dtype)


def _copy_out(scr, ref):
    rows, nb = _block_rows(ref)
    ref[...] = scr[:, :nb, :].reshape(ref.shape).astype(ref.dtype)


def _at_n2(scr, n):
    rows, pitch, _ = scr.shape
    return scr.reshape(rows * pitch, LANES).at[pl.ds(n, rows, stride=pitch), :]


def _outer_dft(mat_ref, x_ref, o_ref, xs, os):
    _copy_in(x_ref, xs)
    for n in range(x_ref.shape[-2]):
        _at_n2(os, n)[...] = _dot(mat_ref[...], _at_n2(xs, n)[...].astype(BF16))
    _copy_out(os, o_ref)


def _filter_stage_a_kernel(x_ref, fa_ref, o_ref, xs, os):
    _outer_dft(fa_ref, x_ref, o_ref, xs, os)


def _filter_stage_b_kernel(x_ref, fb_ref, o_ref):
    for kk in range(x_ref.shape[1]):
        x = jnp.concatenate([x_ref[0, kk], x_ref[1, kk]], axis=0).astype(BF16)
        o_ref[kk] = _dot(fb_ref[kk], x)


def _hyena_filter_spectrum(filt, fa_filt, fb, nb=DFT_N2_BLOCK):
    n, oc = filt.shape
    a = pl.pallas_call(
        _filter_stage_a_kernel,
        grid=(FFT_N2 // nb, oc // LANES),
        in_specs=[pl.BlockSpec((FFT_N1, nb, LANES), lambda j, c: (0, j, c)),
                  pl.BlockSpec(fa_filt.shape, lambda j, c: (0, 0))],
        out_specs=pl.BlockSpec((2, FFT_N1, nb, LANES), lambda j, c: (0, 0, j, c)),
        out_shape=jax.ShapeDtypeStruct((2, FFT_N1, FFT_N2, oc), BF16),
        scratch_shapes=[_pitched(FFT_N1, nb), _pitched(2 * FFT_N1, nb)],
        compiler_params=_cparams("parallel", "parallel"),
        name="hyena_filter_dft_a",
    )(filt.reshape(FFT_N1, FFT_N2, oc), fa_filt)
    return pl.pallas_call(
        _filter_stage_b_kernel,
        grid=(FFT_N1 // DFT_K1_BLOCK,),
        in_specs=[pl.BlockSpec((2, DFT_K1_BLOCK, FFT_N2, oc), lambda k: (0, k, 0, 0)),
                  pl.BlockSpec((DFT_K1_BLOCK, 2 * FFT_N2, 2 * FFT_N2), lambda k: (k, 0, 0))],
        out_specs=pl.BlockSpec((DFT_K1_BLOCK, 2 * FFT_N2, oc), lambda k: (k, 0, 0)),
        out_shape=jax.ShapeDtypeStruct((FFT_N1, 2 * FFT_N2, oc), F32),
        compiler_params=_cparams("parallel"),
        name="hyena_filter_dft_b",
    )(a, fb)


def _stage_a_kernel(x_ref, fa_ref, o_ref, xs, os):
    _outer_dft(fa_ref, x_ref, o_ref, xs, os)


def _stage_a(x4, fa, nb):
    B, r, n2, C = x4.shape
    return pl.pallas_call(
        _stage_a_kernel,
        grid=(B // 2, n2 // nb, C // LANES),
        in_specs=[pl.BlockSpec((2, r, nb, LANES), lambda p, j, c: (p, 0, j, c)),
                  pl.BlockSpec(fa.shape, lambda p, j, c: (0, 0))],
        out_specs=pl.BlockSpec((1, 2, FFT_N1, nb, LANES), lambda p, j, c: (p, 0, 0, j, c)),
        out_shape=jax.ShapeDtypeStruct((B // 2, 2, FFT_N1, n2, C), BF16),
        scratch_shapes=[_pitched(2 * r, nb), _pitched(2 * FFT_N1, nb)],
        compiler_params=_cparams("parallel", "parallel", "parallel"),
        name="hyena_dft_a",
    )(x4, fa)


def _stage_b_kernel(x_ref, fb_ref, kf_ref, fbi_ref, o_ref):
    npair = x_ref.shape[0]
    n2 = x_ref.shape[3]
    c = x_ref.shape[4]
    for kk in range(x_ref.shape[2]):
        x = jnp.concatenate(
            [jnp.concatenate([x_ref[p, 0, kk], x_ref[p, 1, kk]], axis=0) for p in range(npair)],
            axis=1).astype(BF16)
        g = _dot(fb_ref[kk], x)
        gr, gi = g[:n2], g[n2:]
        kf = kf_ref[kk]
        kr = jnp.concatenate([kf[:n2]] * npair, axis=1)
        ki = jnp.concatenate([kf[n2:]] * npair, axis=1)
        hcat = jnp.concatenate([gr * kr - gi * ki, gr * ki + gi * kr], axis=0).astype(BF16)
        y = _dot(fbi_ref[kk], hcat)
        for p in range(npair):
            o_ref[p, 0, kk] = y[:n2, p * c:(p + 1) * c].astype(o_ref.dtype)
            o_ref[p, 1, kk] = y[n2:, p * c:(p + 1) * c].astype(o_ref.dtype)


def _stage_b(spec, fb, kf, fbi, order, nk=DFT_K1_BLOCK):
    npair, _, _, _, C = spec.shape
    blk = pl.BlockSpec((npair, 2, nk, FFT_N2, C), lambda k: (0, 0, k, 0, 0))
    mat = pl.BlockSpec((nk, 2 * FFT_N2, 2 * FFT_N2), lambda k: (k, 0, 0))
    return pl.pallas_call(
        _stage_b_kernel,
        grid=(FFT_N1 // nk,),
        in_specs=[blk, mat, pl.BlockSpec((nk, 2 * FFT_N2, C), lambda k: (k, 0, order)), mat],
        out_specs=blk,
        out_shape=jax.ShapeDtypeStruct(spec.shape, BF16),
        compiler_params=_cparams("parallel"),
        name="hyena_dft_b",
    )(spec, fb, kf, fbi)


def _stage_c_kernel(y_ref, fc_ref, gate_ref, z_ref, skip_ref, *rest, stage_a):
    if stage_a:
        fa_ref, z_out, a_out, ys, gs, zs, os, as_ = rest
    else:
        z_out, ys, gs, zs, os = rest
    nb = gate_ref.shape[-2]
    _copy_in(y_ref, ys)
    _copy_in(gate_ref, gs)
    _copy_in(z_ref, zs)
    skip = skip_ref[...]
    for n in range(nb):
        conv = _dot(fc_ref[...], _at_n2(ys, n)[...].astype(BF16))
        _at_n2(os, n)[...] = _at_n2(gs, n)[...] * (conv + skip * _at_n2(zs, n)[...])
    _copy_out(os, z_out)
    if stage_a:
        for n in range(nb):
            _at_n2(as_, n)[...] = _dot(fa_ref[...], _at_n2(os, n)[...].astype(BF16))
        _copy_out(as_, a_out)


def _stage_c(yspec, fc, gate, zin, skip, out_dtype, fa=None, nb=DFT_N2_BLOCK):
    B, r, n2, C = gate.shape
    dat = pl.BlockSpec((2, r, nb, LANES), lambda p, j, c: (p, 0, j, c))
    spc = pl.BlockSpec((1, 2, FFT_N1, nb, LANES), lambda p, j, c: (p, 0, 0, j, c))
    in_specs = [spc, pl.BlockSpec(fc.shape, lambda p, j, c: (0, 0)), dat, dat,
                pl.BlockSpec((1, LANES), lambda p, j, c: (0, c))]
    out_specs = [dat]
    out_shape = [jax.ShapeDtypeStruct(gate.shape, out_dtype)]
    args = [yspec, fc, gate, zin, skip.reshape(1, C)]
    scratch = [_pitched(2 * FFT_N1, nb)] + [_pitched(2 * r, nb)] * 3
    if fa is not None:
        in_specs.append(pl.BlockSpec(fa.shape, lambda p, j, c: (0, 0)))
        out_specs.append(spc)
        out_shape.append(jax.ShapeDtypeStruct(yspec.shape, BF16))
        args.append(fa)
        scratch.append(_pitched(2 * FFT_N1, nb))
    return pl.pallas_call(
        functools.partial(_stage_c_kernel, stage_a=fa is not None),
        grid=(B // 2, n2 // nb, C // LANES),
        in_specs=in_specs, out_specs=out_specs, out_shape=out_shape,
        scratch_shapes=scratch,
        compiler_params=_cparams("parallel", "parallel", "parallel"),
        name="hyena_dft_c",
    )(*args)


def _hyena(x1, x2, v, skip, kf, consts, nb=DFT_N2_BLOCK):
    fa_data, _, fc, fb, fbi = consts
    B, S, C = v.shape
    split = lambda a: a.reshape(B, S // FFT_N2, FFT_N2, C)
    a0 = _stage_a(split(v), fa_data, nb)
    y0 = _stage_b(a0, fb, kf, fbi, 0)
    z1, a1 = _stage_c(y0, fc, split(x1), split(v), skip[0], BF16, fa=fa_data, nb=nb)
    y1 = _stage_b(a1, fb, kf, fbi, 1)
    (out,) = _stage_c(y1, fc, split(x2), z1, skip[1], F32, nb=nb)
    return out.reshape(B, S, C)


def _memkv_kernel(m_ref, g_ref, w_ref, k_out, v_out):
    hm = _rms(m_ref[0], g_ref[...]).astype(BF16)
    kv = _dot(hm, w_ref[...])
    d = k_out.shape[2]
    k_out[0] = kv[:, :d].astype(BF16)
    v_out[0] = kv[:, d:].astype(BF16)


def _memkv(mem, g, w_mkv):
    B, M, D = mem.shape
    dk = w_mkv.shape[1] // 2
    w = w_mkv.astype(BF16)
    return pl.pallas_call(
        _memkv_kernel,
        grid=(B,),
        in_specs=[pl.BlockSpec((1, M, D), lambda b: (b, 0, 0)),
                  pl.BlockSpec((1, D), lambda b: (0, 0)),
                  pl.BlockSpec(w.shape, lambda b: (0, 0))],
        out_specs=[pl.BlockSpec((1, M, dk), lambda b: (b, 0, 0))] * 2,
        out_shape=[jax.ShapeDtypeStruct((B, M, dk), BF16)] * 2,
        compiler_params=_cparams("parallel"),
        name="mem_kv",
    )(mem, g.reshape(1, D), w)


def _route(logits):
    lane = lax.broadcasted_iota(jnp.int32, logits.shape, 1)
    ninf = -jnp.inf
    big = ROUTE_LANES
    first = lambda mask: jnp.min(jnp.where(mask, lane, big), axis=-1, keepdims=True)
    is_g = (lane >= N_EXPERTS) & (lane < N_EXPERTS + N_GROUPS)
    gl = jnp.where(is_g, logits, ninf)
    gmax = jnp.max(gl, axis=-1, keepdims=True)
    g_idx = first(gl == gmax) - N_EXPERTS
    p_group = 1.0 / jnp.sum(jnp.exp(gl - gmax), axis=-1, keepdims=True)
    in_g = (lane < N_EXPERTS) & ((lane // EXPERTS_PER_GROUP) == g_idx)
    el = jnp.where(in_g, logits, ninf)
    v1 = jnp.max(el, axis=-1, keepdims=True)
    i1 = first(el == v1)
    el2 = jnp.where(lane == i1, ninf, el)
    v2 = jnp.max(el2, axis=-1, keepdims=True)
    i2 = first(el2 == v2)
    e2 = jnp.exp(v2 - v1)
    p1 = 1.0 / (1.0 + e2)
    p2 = e2 / (1.0 + e2)
    sel = lambda n, val: jnp.where(lane == n, val, 0.0)
    return (sel(ROUTE_ID0, i1.astype(F32)) + sel(ROUTE_ID0 + 1, i2.astype(F32))
            + sel(ROUTE_W0, p_group * p1) + sel(ROUTE_W0 + 1, p_group * p2))


def _postmix_kernel(x_ref, a_ref, hy_ref, ag_ref, hg_ref, woa_ref, woh_ref, cg_ref, wmq_ref,
                    mk_ref, mv_ref, wmo_ref, fg_ref, wr_ref, br_ref, x_out, hn_out, route_out,
                    route_t_out):
    ts = x_ref.shape[0] // ROW_SUBTILES
    hn_flat = hn_out.reshape(hn_out.shape[0] * SUBLANES, LANES)
    for t in range(ROW_SUBTILES):
        rows = slice(t * ts, (t + 1) * ts)
        ra = _rms(a_ref[rows, :].astype(F32), ag_ref[...]).astype(BF16)
        rh = _rms(hy_ref[rows, :], hg_ref[...]).astype(BF16)
        x = x_ref[rows, :] + _dot(ra, woa_ref[...]) + _dot(rh, woh_ref[...])
        q = _dot(_rms(x, cg_ref[...]).astype(BF16), wmq_ref[...])
        dh = q.shape[1] // MEM_HEADS
        outs = []
        for h in range(MEM_HEADS):
            sl = slice(h * dh, (h + 1) * dh)
            s = lax.dot_general(q[:, sl].astype(BF16), mk_ref[0, :, sl], (((1,), (1,)), ((), ())),
                                preferred_element_type=F32) * dh ** -0.5
            p = jnp.exp(s - jnp.max(s, axis=-1, keepdims=True))
            l = jnp.sum(p, axis=-1, keepdims=True)
            outs.append(_dot(p.astype(BF16), mv_ref[0, :, sl]) / l)
        o = jnp.concatenate(outs, axis=1).astype(BF16)
        x = x + _dot(o, wmo_ref[...])
        x_out[rows, :] = x
        hn = _rms(x, fg_ref[...])
        for j in range(SUBLANES):
            hn_flat[pl.ds(t * ts * SUBLANES + j, ts, stride=SUBLANES), :] = hn[:, j * LANES:(j + 1) * LANES]
        route = _route(_dot(hn.astype(BF16), wr_ref[...]) + br_ref[...])
        route_out[rows, :] = route
        route_t_out[0, :, rows] = route.T[:SUBLANES, :]


def _postmix(x2d, a2d, hy2d, seq, ag, hg, w_out, cg, w_mq, mk, mv, w_mo, fg, w_rg, b_rg, w_re, b_re, tm=512):
    T, D = x2d.shape
    ca = a2d.shape[1]
    woa = w_out[:ca].astype(BF16)
    woh = w_out[ca:].astype(BF16)
    pad = ROUTE_LANES - N_EXPERTS - N_GROUPS
    wr = jnp.concatenate([w_re, w_rg, jnp.zeros((D, pad), F32)], 1).astype(BF16)
    br = jnp.concatenate([b_re, b_rg, jnp.zeros((pad,), F32)]).reshape(1, ROUTE_LANES)
    nseq = seq // tm
    full = lambda a: pl.BlockSpec(a.shape, lambda i: (0,) * a.ndim)
    row = lambda n: pl.BlockSpec((tm, n), lambda i: (i, 0))
    memb = pl.BlockSpec((1,) + mk.shape[1:], lambda i: (i // nseq, 0, 0))
    args = [x2d, a2d, hy2d, ag.reshape(1, -1), hg.reshape(1, -1), woa, woh, cg.reshape(1, D),
            w_mq.astype(BF16), mk, mv, w_mo.astype(BF16), fg.reshape(1, D), wr, br]
    in_specs = [row(D), row(ca), row(hy2d.shape[1])] + [full(a) for a in args[3:9]] + [memb, memb] \
        + [full(a) for a in args[11:]]
    return pl.pallas_call(
        _postmix_kernel,
        grid=(T // tm,),
        in_specs=in_specs,
        out_specs=[row(D), pl.BlockSpec((tm, SUBLANES, LANES), lambda i: (i, 0, 0)), row(ROUTE_LANES),
                   pl.BlockSpec((1, SUBLANES, tm), lambda i: (i, 0, 0))],
        out_shape=[jax.ShapeDtypeStruct((T, D), F32), jax.ShapeDtypeStruct((T, SUBLANES, LANES), F32),
                   jax.ShapeDtypeStruct((T, ROUTE_LANES), F32),
                   jax.ShapeDtypeStruct((T // tm, SUBLANES, tm), F32)],
        compiler_params=_cparams("parallel"),
        name="postmix",
    )(*args)


def _slot_onehots(rt):
    e = lax.broadcasted_iota(jnp.int32, (N_EXPERTS, rt.shape[1]), 0).astype(F32)
    return [e == rt[ROUTE_ID0 + k:ROUTE_ID0 + k + 1, :] for k in range(2)]


def _slot_rows(rows, width):
    sub = lax.broadcasted_iota(jnp.int32, (SUBLANES, width), 0)
    return jnp.where(sub == 0, rows[0], jnp.where(sub == 1, rows[1], 0.0))


def _rank_kernel(rt_ref, rank_out, cnt_out, carry_ref):
    @pl.when(pl.program_id(0) == 0)
    def _():
        carry_ref[...] = jnp.zeros_like(carry_ref)

    tr = rt_ref.shape[2]
    oh = _slot_onehots(rt_ref[0])
    cnt = jnp.where(oh[0] | oh[1], 1.0, 0.0)
    s = lax.broadcasted_iota(jnp.int32, (tr, tr), 0)
    t = lax.broadcasted_iota(jnp.int32, (tr, tr), 1)
    before = jnp.where(s < t, 1.0, 0.0).astype(BF16)
    cum = _dot(cnt.astype(BF16), before) + carry_ref[...]
    ranks = [jnp.sum(jnp.where(m, cum, 0.0), axis=0, keepdims=True) for m in oh]
    rank_out[0] = _slot_rows(ranks, tr)
    carry_ref[...] += jnp.sum(cnt, axis=1, keepdims=True)
    cnt_out[...] = carry_ref[...]


def _pos_kernel(rt_ref, rank_ref, base_ref, pos_out):
    tr = rt_ref.shape[2]
    oh = _slot_onehots(rt_ref[0])
    rank = rank_ref[0]
    pos = [jnp.sum(jnp.where(m, base_ref[...], 0.0), axis=0, keepdims=True) + rank[k:k + 1, :]
           for k, m in enumerate(oh)]
    pos_out[0] = _slot_rows(pos, tr).astype(jnp.int32)


def _dispatch_kernel(base_ref, cp_ref, pos0_ref, pos1_ref, hn_ref, xs_hbm, zero_ref, sem, *, tt, tmm):
    i = pl.program_id(0)
    row_copy = lambda src, dst: pltpu.make_async_copy(
        hn_ref.at[pl.ds(src, 1)], xs_hbm.at[pl.ds(dst, 1)], sem)

    @pl.when(i == 0)
    def _():
        zero_ref[...] = jnp.zeros_like(zero_ref)
        pad_copy = lambda e: pltpu.make_async_copy(
            zero_ref, xs_hbm.at[pl.ds(base_ref[e] + cp_ref[e] - tmm, tmm)], sem)
        for e in range(N_EXPERTS):
            @pl.when(cp_ref[e] > 0)
            def _():
                pad_copy(e).start()
        for e in range(N_EXPERTS):
            @pl.when(cp_ref[e] > 0)
            def _():
                pad_copy(e).wait()
        last = N_EXPERTS - 1
        tail_copy = lambda r: pltpu.make_async_copy(zero_ref, xs_hbm.at[pl.ds(r * tmm, tmm)], sem)
        first_free = (base_ref[last] + cp_ref[last]) // tmm
        n_tiles = xs_hbm.shape[0] // tmm
        lax.fori_loop(first_free, n_tiles, lambda r, c: (tail_copy(r).start(), c)[1], 0)
        lax.fori_loop(first_free, n_tiles, lambda r, c: (tail_copy(r).wait(), c)[1], 0)

    def start(t, carry):
        row_copy(t, pos0_ref[t]).start(priority=0)
        row_copy(t, pos1_ref[t]).start(priority=1)
        return carry

    lax.fori_loop(0, tt, start, 0, unroll=DMA_UNROLL)

    def wait(t, carry):
        row_copy(0, 0).wait()
        row_copy(0, 0).wait()
        return carry

    lax.fori_loop(0, tt, wait, 0, unroll=DMA_UNROLL)


def _ffn_kernel(te_ref, nu_ref, xs_ref, wg_ref, wu_ref, wd_ref, ys_ref):
    used = pl.program_id(0) < nu_ref[0]

    @pl.when(used)
    def _():
        x = _load_row_tiles(xs_ref).astype(BF16)
        a = _dot(x, wg_ref[0].astype(BF16))
        b = _dot(x, wu_ref[0].astype(BF16))
        m = (a * jax.nn.sigmoid(a)) * b
        _store_row_tiles(ys_ref, _dot(m.astype(BF16), wd_ref[0].astype(BF16)))

    @pl.when(jnp.logical_not(used))
    def _():
        ys_ref[...] = jnp.zeros_like(ys_ref)


def _combine_kernel(pos0_ref, pos1_ref, pos0_next_ref, pos1_next_ref, ys_hbm, x_ref, route_ref, fg_ref,
                    o_ref, buf_ref, sem, *, tc):
    i = pl.program_id(0)
    slot = i % 2
    row_copy = lambda s, k, t, p: pltpu.make_async_copy(
        ys_hbm.at[pl.ds(p, 1)], buf_ref.at[s, k, pl.ds(t, 1)], sem.at[s])

    def fetch(p0_ref, p1_ref, s):
        def start(t, carry):
            row_copy(s, 0, t, p0_ref[t]).start(priority=0)
            row_copy(s, 1, t, p1_ref[t]).start(priority=1)
            return carry

        lax.fori_loop(0, tc, start, 0, unroll=DMA_UNROLL)

    @pl.when(i == 0)
    def _():
        fetch(pos0_ref, pos1_ref, 0)

    @pl.when(i + 1 < pl.num_programs(0))
    def _():
        fetch(pos0_next_ref, pos1_next_ref, 1 - slot)

    def wait(t, carry):
        row_copy(slot, 0, 0, 0).wait()
        row_copy(slot, 1, 0, 0).wait()
        return carry

    lax.fori_loop(0, tc, wait, 0, unroll=DMA_UNROLL)
    route = route_ref[...]
    y = (x_ref[...] + route[:, ROUTE_W0:ROUTE_W0 + 1] * _load_row_tiles(buf_ref, (slot, 0))
         + route[:, ROUTE_W0 + 1:ROUTE_W0 + 2] * _load_row_tiles(buf_ref, (slot, 1)))
    o_ref[...] = _rms(y, fg_ref[...])


def _moe(hn, route, route_t, x2d, w_gate, w_up, w_down, fg, tt=512, tc=256):
    T, D = x2d.shape
    E = N_EXPERTS
    F = w_gate.shape[-1]
    tmm = MOE_ROW_TILE
    ntr, _, tr = route_t.shape
    row = lambda tm, n: pl.BlockSpec((tm, n), lambda i: (i, 0))
    rec = pl.BlockSpec((1, SUBLANES, tr), lambda i: (i, 0, 0))
    col = pl.BlockSpec((E, 1), lambda i: (0, 0))
    rank, counts = pl.pallas_call(
        _rank_kernel,
        grid=(ntr,),
        in_specs=[rec],
        out_specs=[rec, col],
        out_shape=[jax.ShapeDtypeStruct(route_t.shape, F32), jax.ShapeDtypeStruct((E, 1), F32)],
        scratch_shapes=[pltpu.VMEM((E, 1), F32)],
        compiler_params=_cparams("arbitrary"),
        name="moe_rank",
    )(route_t)

    cnt = counts[:, 0].astype(jnp.int32)
    cp = ((cnt + tmm - 1) // tmm) * tmm
    ends = jnp.cumsum(cp)
    base = ends - cp
    n_used = ends[-1] // tmm
    n_tiles = (2 * T) // tmm + E
    tile_start = jnp.minimum(jnp.arange(n_tiles, dtype=jnp.int32), n_used - 1) * tmm
    tile_expert = jnp.minimum(jnp.sum((tile_start[:, None] >= ends[None, :]).astype(jnp.int32), axis=1), E - 1)

    pos = pl.pallas_call(
        _pos_kernel,
        grid=(ntr,),
        in_specs=[rec, rec, col],
        out_specs=rec,
        out_shape=jax.ShapeDtypeStruct(route_t.shape, jnp.int32),
        compiler_params=_cparams("parallel"),
        name="moe_pos",
    )(route_t, rank, base.astype(F32).reshape(E, 1))
    pos0, pos1 = pos[:, 0, :].reshape(T), pos[:, 1, :].reshape(T)

    xs = pl.pallas_call(
        functools.partial(_dispatch_kernel, tt=tt, tmm=tmm),
        grid_spec=pltpu.PrefetchScalarGridSpec(
            num_scalar_prefetch=2,
            grid=(T // tt,),
            in_specs=[pl.BlockSpec((tt,), lambda i, b, c: (i,), memory_space=pltpu.SMEM),
                      pl.BlockSpec((tt,), lambda i, b, c: (i,), memory_space=pltpu.SMEM),
                      pl.BlockSpec((tt, SUBLANES, LANES), lambda i, b, c: (i, 0, 0))],
            out_specs=pl.BlockSpec(memory_space=pl.ANY),
            scratch_shapes=[pltpu.VMEM((tmm, SUBLANES, LANES), F32), pltpu.SemaphoreType.DMA(())]),
        out_shape=jax.ShapeDtypeStruct((n_tiles * tmm, SUBLANES, LANES), F32),
        compiler_params=_cparams("arbitrary"),
        name="moe_dispatch",
    )(base, cp, pos0, pos1, hn)

    tile = lambda r, te, nu: (jnp.minimum(r, nu[0] - 1), 0, 0)
    ys = pl.pallas_call(
        _ffn_kernel,
        grid_spec=pltpu.PrefetchScalarGridSpec(
            num_scalar_prefetch=2,
            grid=(n_tiles,),
            in_specs=[pl.BlockSpec((tmm, SUBLANES, LANES), tile),
                      pl.BlockSpec((1, D, F), lambda r, te, nu: (te[r], 0, 0)),
                      pl.BlockSpec((1, D, F), lambda r, te, nu: (te[r], 0, 0)),
                      pl.BlockSpec((1, F, D), lambda r, te, nu: (te[r], 0, 0))],
            out_specs=pl.BlockSpec((tmm, SUBLANES, LANES), lambda r, te, nu: (r, 0, 0))),
        out_shape=jax.ShapeDtypeStruct((n_tiles * tmm, SUBLANES, LANES), F32),
        compiler_params=_cparams("arbitrary"),
        name="moe_ffn",
    )(tile_expert, n_used.reshape(1), xs, w_gate.reshape(E, D, F), w_up.reshape(E, D, F),
      w_down.reshape(E, F, D))

    cur = pl.BlockSpec((tc,), lambda i: (i,), memory_space=pltpu.SMEM)
    nxt = pl.BlockSpec((tc,), lambda i: (jnp.minimum(i + 1, T // tc - 1),), memory_space=pltpu.SMEM)
    return pl.pallas_call(
        functools.partial(_combine_kernel, tc=tc),
        grid=(T // tc,),
        in_specs=[cur, cur, nxt, nxt,
                  pl.BlockSpec(memory_space=pl.ANY),
                  row(tc, D), row(tc, ROUTE_LANES), pl.BlockSpec((1, D), lambda i: (0, 0))],
        out_specs=row(tc, D),
        out_shape=jax.ShapeDtypeStruct((T, D), F32),
        scratch_shapes=[pltpu.VMEM((2, 2, tc, SUBLANES, LANES), F32), pltpu.SemaphoreType.DMA((2,))],
        compiler_params=_cparams("arbitrary"),
        name="moe_combine",
    )(pos0, pos1, pos0, pos1, ys, x2d, route, fg.reshape(1, D))


def kernel(x, mem, mix_norm_g, w_in, q_norm_g, kv_norm_g, w_uq, w_ukv, hy_conv_w, hy_conv_b, hy_w1, hy_b1, hy_freq, hy_w2, hy_b2, hy_w3, hy_b3, hy_decay, hy_skip, attn_out_g, hy_out_g, w_out, cross_norm_g, mem_norm_g, w_mq, w_mkv, w_mo, ffn_norm_g, w_route_group, b_route_group, w_route_expert, b_route_expert, w_gate, w_up, w_down, final_norm_g):
    B, S, D = x.shape
    depth = w_in.shape[0]
    consts = _dft_constants(S)
    xf = x.reshape(B * S, D)
    for l in range(depth):
        q, k, v, hx1, hx2, hv = _inproj(xf, S, mix_norm_g[l], w_in[l], q_norm_g[l], kv_norm_g[l],
                                        w_uq[l], w_ukv[l], hy_conv_w[l], hy_conv_b[l])
        HP = q.shape[1]
        a_out = _attention(q.reshape(B, S, HP), k.reshape(B, S, HP), v.reshape(B, S, HP))
        filt = _hyena_filter_time(S, hy_w1[l], hy_b1[l], hy_freq[l], hy_w2[l], hy_b2[l], hy_w3[l],
                                  hy_b3[l], hy_decay[l])
        kf = _hyena_filter_spectrum(filt, consts[1], consts[3])
        C = hv.shape[1]
        h_out = _hyena(hx1.reshape(B, S, C), hx2.reshape(B, S, C), hv.reshape(B, S, C),
                       hy_skip[l], kf, consts)
        mk, mv = _memkv(mem, mem_norm_g[l], w_mkv[l])
        x2, hn, route, route_t = _postmix(xf, a_out.reshape(B * S, -1), h_out.reshape(B * S, C), S,
                                attn_out_g[l], hy_out_g[l], w_out[l], cross_norm_g[l], w_mq[l], mk, mv,
                                w_mo[l], ffn_norm_g[l], w_route_group[l], b_route_group[l],
                                w_route_expert[l], b_route_expert[l])
        assert depth == 1
        xf = _moe(hn, route, route_t, x2, w_gate[l], w_up[l], w_down[l], final_norm_g)
    return xf.reshape(B, S, D)
```

```python
import functools
import math

import numpy as np
import jax
import jax.numpy as jnp
from jax import lax
from jax.experimental import pallas as pl
from jax.experimental.pallas import tpu as pltpu

F32 = jnp.float32
BF16 = jnp.bfloat16

EPS = 1e-6
MLA_HEADS = 8
MLA_NOPE = 64
MLA_ROPE = 32
MLA_V = 64
ROPE_BASE = 10000.0
HEAD_PAD = 128
HY_ORDER = 2
HY_DIRS = 2
HY_BANDS = 16
MEM_HEADS = 4
N_GROUPS = 4
EXPERTS_PER_GROUP = 8
N_EXPERTS = N_GROUPS * EXPERTS_PER_GROUP
ROUTE_LANES = 128
ROUTE_ID0 = 0
ROUTE_W0 = 2
MOE_ROW_TILE = 512
DMA_UNROLL = 8
ROW_SUBTILES = 1

FFT_N1 = 64
FFT_N2 = 128
DFT_K1_BLOCK = 4
DFT_N2_BLOCK = 32
DFT_PITCH_PAD = 8

VMEM_LIMIT = 56 * 1024 * 1024


def _cparams(*sem):
    return pltpu.CompilerParams(dimension_semantics=sem, vmem_limit_bytes=VMEM_LIMIT)


def _rms(x, g):
    return x * lax.rsqrt(jnp.mean(x * x, axis=-1, keepdims=True) + EPS) * g


def _dot(a, b):
    return jnp.dot(a, b, preferred_element_type=F32)


SUBLANES = 8
LANES = 128


def _load_row_tiles(ref, lead=()):
    rows = ref.shape[-3]
    flat = ref.reshape(*ref.shape[:-3], rows * SUBLANES, LANES)
    return jnp.concatenate(
        [flat[(*lead, pl.ds(j, rows, stride=SUBLANES), slice(None))] for j in range(SUBLANES)], axis=1)


def _store_row_tiles(ref, val):
    rows = ref.shape[0]
    flat = ref.reshape(rows * SUBLANES, LANES)
    for j in range(SUBLANES):
        flat[pl.ds(j, rows, stride=SUBLANES), :] = val[:, j * LANES:(j + 1) * LANES]


def _inproj_kernel(x_ref, xp_ref, xn_ref, g_ref, wq_ref, wkv_ref, wkra_ref, wkrb_ref, why_ref, qg_ref,
                   kvg_ref, wqa_ref, wqb_ref, wka_ref, wv_ref, tab_ref, cw_ref, cb_ref,
                   q_out, k_out, v_out, x1_out, x2_out, hv_out, *, nseq):
    tm = x_ref.shape[0]
    halo = xp_ref.shape[0]
    ts = tm // ROW_SUBTILES
    x_ext = jnp.concatenate([xp_ref[...], x_ref[...], xn_ref[...]], axis=0)
    i = pl.program_id(0) % nseq
    c = x1_out.shape[1]
    tile = lambda t: jnp.concatenate([t] * MLA_HEADS, axis=1)
    lane = lax.broadcasted_iota(jnp.int32, (1, v_out.shape[1]), 1) % HEAD_PAD
    for s in range(ROW_SUBTILES):
        rows = slice(s * ts, (s + 1) * ts)
        hf = _rms(x_ext[s * ts:(s + 1) * ts + 2 * halo], g_ref[...])
        h = hf[halo:halo + ts].astype(BF16)
        qn = _rms(_dot(h, wq_ref[...]), qg_ref[...]).astype(BF16)
        kvn = _rms(_dot(h, wkv_ref[...]), kvg_ref[...]).astype(BF16)
        tab = tab_ref[rows, :]
        cq, sq, ck, sk = (tab[:, j * HEAD_PAD:(j + 1) * HEAD_PAD] for j in range(4))
        q = _dot(qn, wqa_ref[...]) * tile(cq) + _dot(qn, wqb_ref[...]) * tile(sq)
        q_out[rows, :] = q.astype(BF16)
        kr = _dot(h, wkra_ref[...]) * ck + _dot(h, wkrb_ref[...]) * sk
        k_out[rows, :] = (_dot(kvn, wka_ref[...]) + tile(kr)).astype(BF16)
        v_out[rows, :] = (_dot(kvn, wv_ref[...]) + jnp.where(lane == MLA_V, 1.0, 0.0)).astype(BF16)
        hy = _dot(hf.astype(BF16), why_ref[...])
        row = lax.broadcasted_iota(jnp.int32, hy.shape, 0)
        outside = (row == halo - 1) & (i == 0) if s == 0 else None
        if s == ROW_SUBTILES - 1:
            last = (row == halo + ts) & (i == nseq - 1)
            outside = last if outside is None else outside | last
        if outside is not None:
            hy = jnp.where(outside, 0.0, hy)
        cw = cw_ref[...]
        u = (hy[halo - 1:halo - 1 + ts] * cw[0:1] + hy[halo:halo + ts] * cw[1:2]
             + hy[halo + 1:halo + 1 + ts] * cw[2:3] + cb_ref[...])
        x1_out[rows, :] = u[:, :c].astype(x1_out.dtype)
        x2_out[rows, :] = u[:, c:2 * c].astype(x2_out.dtype)
        hv_out[rows, :] = u[:, 2 * c:].astype(hv_out.dtype)


def _inproj(x2d, seq, mix_g, w_in, q_g, kv_g, w_uq, w_ukv, conv_w, conv_b, tm=1024):
    T, D = x2d.shape
    per = tm // SUBLANES
    cb = conv_b.reshape(1, -1)
    q_rank, kv_rank = q_g.shape[0], kv_g.shape[0]
    off_kv = q_rank
    off_kr = off_kv + kv_rank
    off_hy = off_kr + MLA_ROPE
    C = (w_in.shape[1] - off_hy) // 3
    H = MLA_HEADS
    half = MLA_ROPE // 2
    wq = w_in[:, :off_kv].astype(BF16)
    wkv = w_in[:, off_kv:off_kr].astype(BF16)
    wkr = w_in[:, off_kr:off_hy]
    wkr_sw = jnp.concatenate([wkr[:, half:], wkr[:, :half]], axis=1)
    zpad = lambda n: jnp.zeros((D, n), F32)
    wkra = jnp.concatenate([zpad(MLA_NOPE), wkr, zpad(HEAD_PAD - MLA_NOPE - MLA_ROPE)], 1).astype(BF16)
    wkrb = jnp.concatenate([zpad(MLA_NOPE), wkr_sw, zpad(HEAD_PAD - MLA_NOPE - MLA_ROPE)], 1).astype(BF16)
    why = w_in[:, off_hy:].astype(BF16)

    uq = w_uq.reshape(q_rank, H, MLA_NOPE + MLA_ROPE)
    uq_n, uq_r = uq[..., :MLA_NOPE], uq[..., MLA_NOPE:]
    uq_rs = jnp.concatenate([uq_r[..., half:], uq_r[..., :half]], axis=-1)
    zq = lambda n: jnp.zeros((q_rank, H, n), F32)
    wqa = jnp.concatenate([uq_n, uq_r, zq(HEAD_PAD - MLA_NOPE - MLA_ROPE)], -1).reshape(q_rank, H * HEAD_PAD).astype(BF16)
    wqb = jnp.concatenate([zq(MLA_NOPE), uq_rs, zq(HEAD_PAD - MLA_NOPE - MLA_ROPE)], -1).reshape(q_rank, H * HEAD_PAD).astype(BF16)
    ukv = w_ukv.reshape(kv_rank, H, MLA_NOPE + MLA_V)
    zk = lambda n: jnp.zeros((kv_rank, H, n), F32)
    wka = jnp.concatenate([ukv[..., :MLA_NOPE], zk(HEAD_PAD - MLA_NOPE)], -1).reshape(kv_rank, H * HEAD_PAD).astype(BF16)
    wv = jnp.concatenate([ukv[..., MLA_NOPE:], zk(HEAD_PAD - MLA_V)], -1).reshape(kv_rank, H * HEAD_PAD).astype(BF16)

    ang = np.arange(seq)[:, None] * ROPE_BASE ** (-np.arange(half) / half)[None, :]
    cos2 = np.concatenate([np.cos(ang), np.cos(ang)], 1)
    sin2 = np.concatenate([-np.sin(ang), np.sin(ang)], 1)
    zs = lambda n: np.zeros((seq, n))
    scale = (MLA_NOPE + MLA_ROPE) ** -0.5 * math.log2(math.e)
    rest = HEAD_PAD - MLA_NOPE - MLA_ROPE
    cq = scale * np.concatenate([np.ones((seq, MLA_NOPE)), cos2, zs(rest)], 1)
    sq = scale * np.concatenate([zs(MLA_NOPE), sin2, zs(rest)], 1)
    ck = np.concatenate([zs(MLA_NOPE), cos2, zs(rest)], 1)
    sk = np.concatenate([zs(MLA_NOPE), sin2, zs(rest)], 1)
    tab = jnp.asarray(np.concatenate([cq, sq, ck, sk], 1), dtype=F32)

    nseq = seq // tm
    full = lambda a: pl.BlockSpec(a.shape, lambda i: (0,) * a.ndim)
    row = lambda n: pl.BlockSpec((tm, n), lambda i: (i, 0))
    consts = [mix_g.reshape(1, D), wq, wkv, wkra, wkrb, why, q_g.reshape(1, -1), kv_g.reshape(1, -1),
              wqa, wqb, wka, wv]
    HP = H * HEAD_PAD
    return pl.pallas_call(
        functools.partial(_inproj_kernel, nseq=nseq),
        grid=(T // tm,),
        in_specs=[row(D),
                  pl.BlockSpec((SUBLANES, D), lambda i: (jnp.maximum(i * per - 1, 0), 0)),
                  pl.BlockSpec((SUBLANES, D), lambda i: (jnp.minimum((i + 1) * per, T // SUBLANES - 1), 0))]
        + [full(a) for a in consts]
        + [pl.BlockSpec((tm, 4 * HEAD_PAD), lambda i: (i % nseq, 0)), full(conv_w), full(cb)],
        out_specs=[row(HP), row(HP), row(HP), row(C), row(C), row(C)],
        out_shape=[jax.ShapeDtypeStruct((T, HP), BF16)] * 3 + [jax.ShapeDtypeStruct((T, C), BF16)] * 3,
        compiler_params=_cparams("parallel"),
        name="inproj",
    )(x2d, x2d, x2d, *consts, tab, conv_w, cb)


def _attn_kernel(q_ref, k_ref, v_ref, o_ref):
    outs = []
    for h in range(MLA_HEADS):
        sl = slice(h * HEAD_PAD, (h + 1) * HEAD_PAD)
        s = lax.dot_general(q_ref[0, :, sl], k_ref[0, :, sl], (((1,), (1,)), ((), ())),
                            preferred_element_type=F32).astype(BF16)
        p = jnp.exp2(s - jnp.max(s, axis=-1, keepdims=True))
        o = _dot(p, v_ref[0, :, sl])
        outs.append(o[:, :MLA_V] / o[:, MLA_V:MLA_V + 1])
    o_ref[0] = jnp.concatenate(outs, axis=1).astype(o_ref.dtype)


def _attention(q, k, v, tq=512):
    B, S, HP = q.shape
    return pl.pallas_call(
        _attn_kernel,
        grid=(B, S // tq),
        in_specs=[pl.BlockSpec((1, tq, HP), lambda b, i: (b, i, 0)),
                  pl.BlockSpec((1, S, HP), lambda b, i: (b, 0, 0)),
                  pl.BlockSpec((1, S, HP), lambda b, i: (b, 0, 0))],
        out_specs=pl.BlockSpec((1, tq, MLA_HEADS * MLA_V), lambda b, i: (b, i, 0)),
        out_shape=jax.ShapeDtypeStruct((B, S, MLA_HEADS * MLA_V), BF16),
        compiler_params=_cparams("parallel", "arbitrary"),
        name="mla_attention",
    )(q, k, v)


def _dft_constants(seq):
    n = 2 * seq
    n1, n2 = FFT_N1, FFT_N2
    assert n1 * n2 == n
    r1 = np.arange(n1)
    r2 = np.arange(n2)
    blk = lambda z: np.block([[z.real, -z.imag], [z.imag, z.real]])
    w1 = np.exp(-2j * np.pi * np.outer(r1, r1) / n1)
    fa_data = blk(w1[:, :n1 // 2])
    fa_filt = np.concatenate([w1.real, w1.imag], axis=0)
    fc = blk(np.conj(w1).T[:n1 // 2, :])
    w2 = np.exp(-2j * np.pi * np.outer(r2, r2) / n2)
    tw = np.exp(-2j * np.pi * np.outer(r1, r2) / n)
    fb = np.stack([blk(w2 * tw[k][None, :]) for k in range(n1)])
    fbi = np.stack([blk(np.conj(w2).T * np.conj(tw[k])[:, None] / n) for k in range(n1)])
    as_bf = lambda a: jnp.asarray(a, dtype=F32).astype(BF16)
    return as_bf(fa_data), as_bf(fa_filt), as_bf(fc), as_bf(fb), as_bf(fbi)


def _filter_kernel(z_ref, w1_ref, b1_ref, fr_ref, w2_ref, b2_ref, w3_ref, b3_ref, dec_ref, o_ref, *, seq, tr):
    hp = lax.Precision.HIGHEST
    z = z_ref[...]
    fr = fr_ref[...]
    h = jnp.sin(fr[0:1] * (jnp.dot(z, w1_ref[...], precision=hp, preferred_element_type=F32) + b1_ref[...]))
    h = jnp.sin(fr[1:2] * (jnp.dot(h, w2_ref[...], precision=hp, preferred_element_type=F32) + b2_ref[...]))
    split = lambda a: (a.astype(BF16), (a - a.astype(BF16).astype(F32)).astype(BF16))
    (hh, hl), (wh, wl) = split(h), split(w3_ref[0])
    h = _dot(hh, wh) + _dot(hh, wl) + _dot(hl, wh) + b3_ref[0]
    h = h * jnp.exp(-z[:, 0:1] * jnp.abs(dec_ref[0]))
    n = pl.program_id(0) * tr + lax.broadcasted_iota(jnp.int32, h.shape, 0)
    o_ref[...] = jnp.where(n == seq, 0.0, h).astype(o_ref.dtype)


def _hyena_filter_time(seq, w1, b1, freq, w2, b2, w3, b3, decay, tr=512):
    n = 2 * seq
    emb, ffn = w1.shape
    C = w3.shape[1] // (HY_ORDER * HY_DIRS)
    off = np.arange(n)
    t = np.where(off < seq, off, n - off).astype(np.float64)
    bands = np.linspace(1e-4, HY_BANDS - 1, HY_BANDS)
    ang = 2.0 * math.pi * t[:, None] * bands[None, :] / seq
    z = np.concatenate([(t / seq)[:, None], np.cos(ang), -np.sin(ang)], axis=-1)
    zl = LANES
    z = jnp.asarray(np.pad(z, ((0, 0), (0, zl - emb))), dtype=F32)
    w1p = jnp.pad(w1, ((0, zl - emb), (0, 0)))
    by_dir = lambda a: jnp.moveaxis(a.reshape(a.shape[0], HY_ORDER, HY_DIRS, C), 2, 0).reshape(
        HY_DIRS, a.shape[0], HY_ORDER * C)
    w3d, b3d, decd = by_dir(w3), by_dir(b3.reshape(1, -1)), by_dir(decay.reshape(1, -1))
    full = lambda a: pl.BlockSpec(a.shape, lambda i: (0,) * a.ndim)
    ndir = lambda a: pl.BlockSpec((1,) + a.shape[1:], lambda i: ((i * tr) // seq, 0, 0))
    consts = [w1p, b1.reshape(1, -1), freq, w2, b2.reshape(1, -1)]
    return pl.pallas_call(
        functools.partial(_filter_kernel, seq=seq, tr=tr),
        grid=(n // tr,),
        in_specs=[pl.BlockSpec((tr, zl), lambda i: (i, 0))] + [full(a) for a in consts]
        + [ndir(w3d), ndir(b3d), ndir(decd)],
        out_specs=pl.BlockSpec((tr, HY_ORDER * C), lambda i: (i, 0)),
        out_shape=jax.ShapeDtypeStruct((n, HY_ORDER * C), BF16),
        compiler_params=_cparams("parallel"),
        name="hyena_filter_mlp",
    )(z, *consts, w3d, b3d, decd)


def _pitched(rows, nb):
    return pltpu.VMEM((rows, nb + DFT_PITCH_PAD, LANES), F32)


def _block_rows(ref):
    return math.prod(ref.shape[:-2]), ref.shape[-2]


def _copy_in(ref, scr):
    rows, nb = _block_rows(ref)
    scr[:, :nb, :] = ref[...].reshape(rows, nb, LANES).astype(scr.dtype)


def _copy_out(scr, ref):
    rows, nb = _block_rows(ref)
    ref[...] = scr[:, :nb, :].reshape(ref.shape).astype(ref.dtype)


def _at_n2(scr, n):
    rows, pitch, _ = scr.shape
    return scr.reshape(rows * pitch, LANES).at[pl.ds(n, rows, stride=pitch), :]


def _outer_dft(mat_ref, x_ref, o_ref, xs, os):
    _copy_in(x_ref, xs)
    for n in range(x_ref.shape[-2]):
        _at_n2(os, n)[...] = _dot(mat_ref[...], _at_n2(xs, n)[...].astype(BF16))
    _copy_out(os, o_ref)


def _filter_stage_a_kernel(x_ref, fa_ref, o_ref, xs, os):
    _outer_dft(fa_ref, x_ref, o_ref, xs, os)


def _filter_stage_b_kernel(x_ref, fb_ref, o_ref):
    for kk in range(x_ref.shape[1]):
        x = jnp.concatenate([x_ref[0, kk], x_ref[1, kk]], axis=0).astype(BF16)
        o_ref[kk] = _dot(fb_ref[kk], x).astype(o_ref.dtype)


def _hyena_filter_spectrum(filt, fa_filt, fb, nb=DFT_N2_BLOCK):
    n, oc = filt.shape
    a = pl.pallas_call(
        _filter_stage_a_kernel,
        grid=(FFT_N2 // nb, oc // LANES),
        in_specs=[pl.BlockSpec((FFT_N1, nb, LANES), lambda j, c: (0, j, c)),
                  pl.BlockSpec(fa_filt.shape, lambda j, c: (0, 0))],
        out_specs=pl.BlockSpec((2, FFT_N1, nb, LANES), lambda j, c: (0, 0, j, c)),
        out_shape=jax.ShapeDtypeStruct((2, FFT_N1, FFT_N2, oc), BF16),
        scratch_shapes=[_pitched(FFT_N1, nb), _pitched(2 * FFT_N1, nb)],
        compiler_params=_cparams("parallel", "parallel"),
        name="hyena_filter_dft_a",
    )(filt.reshape(FFT_N1, FFT_N2, oc), fa_filt)
    return pl.pallas_call(
        _filter_stage_b_kernel,
        grid=(FFT_N1 // DFT_K1_BLOCK,),
        in_specs=[pl.BlockSpec((2, DFT_K1_BLOCK, FFT_N2, oc), lambda k: (0, k, 0, 0)),
                  pl.BlockSpec((DFT_K1_BLOCK, 2 * FFT_N2, 2 * FFT_N2), lambda k: (k, 0, 0))],
        out_specs=pl.BlockSpec((DFT_K1_BLOCK, 2 * FFT_N2, oc), lambda k: (k, 0, 0)),
        out_shape=jax.ShapeDtypeStruct((FFT_N1, 2 * FFT_N2, oc), BF16),
        compiler_params=_cparams("parallel"),
        name="hyena_filter_dft_b",
    )(a, fb)


def _stage_a_kernel(x_ref, fa_ref, o_ref, xs, os):
    _outer_dft(fa_ref, x_ref, o_ref, xs, os)


def _stage_a(x4, fa, nb):
    B, r, n2, C = x4.shape
    return pl.pallas_call(
        _stage_a_kernel,
        grid=(B // 2, n2 // nb, C // LANES),
        in_specs=[pl.BlockSpec((2, r, nb, LANES), lambda p, j, c: (p, 0, j, c)),
                  pl.BlockSpec(fa.shape, lambda p, j, c: (0, 0))],
        out_specs=pl.BlockSpec((1, 2, FFT_N1, nb, LANES), lambda p, j, c: (p, 0, 0, j, c)),
        out_shape=jax.ShapeDtypeStruct((B // 2, 2, FFT_N1, n2, C), BF16),
        scratch_shapes=[_pitched(2 * r, nb), _pitched(2 * FFT_N1, nb)],
        compiler_params=_cparams("parallel", "parallel", "parallel"),
        name="hyena_dft_a",
    )(x4, fa)


def _stage_b_kernel(x_ref, fb_ref, kf_ref, fbi_ref, o_ref):
    npair = x_ref.shape[0]
    n2 = x_ref.shape[3]
    c = x_ref.shape[4]
    for kk in range(x_ref.shape[2]):
        x = jnp.concatenate(
            [jnp.concatenate([x_ref[p, 0, kk], x_ref[p, 1, kk]], axis=0) for p in range(npair)],
            axis=1).astype(BF16)
        g = _dot(fb_ref[kk], x)
        gr, gi = g[:n2], g[n2:]
        kf = kf_ref[kk].astype(F32)
        kr = jnp.concatenate([kf[:n2]] * npair, axis=1)
        ki = jnp.concatenate([kf[n2:]] * npair, axis=1)
        hcat = jnp.concatenate([gr * kr - gi * ki, gr * ki + gi * kr], axis=0).astype(BF16)
        y = _dot(fbi_ref[kk], hcat)
        for p in range(npair):
            o_ref[p, 0, kk] = y[:n2, p * c:(p + 1) * c].astype(o_ref.dtype)
            o_ref[p, 1, kk] = y[n2:, p * c:(p + 1) * c].astype(o_ref.dtype)


def _stage_b(spec, fb, kf, fbi, order, nk=DFT_K1_BLOCK):
    npair, _, _, _, C = spec.shape
    blk = pl.BlockSpec((npair, 2, nk, FFT_N2, C), lambda k: (0, 0, k, 0, 0))
    mat = pl.BlockSpec((nk, 2 * FFT_N2, 2 * FFT_N2), lambda k: (k, 0, 0))
    return pl.pallas_call(
        _stage_b_kernel,
        grid=(FFT_N1 // nk,),
        in_specs=[blk, mat, pl.BlockSpec((nk, 2 * FFT_N2, C), lambda k: (k, 0, order)), mat],
        out_specs=blk,
        out_shape=jax.ShapeDtypeStruct(spec.shape, BF16),
        compiler_params=_cparams("parallel"),
        name="hyena_dft_b",
    )(spec, fb, kf, fbi)


def _stage_c_kernel(y_ref, fc_ref, gate_ref, z_ref, skip_ref, *rest, stage_a):
    if stage_a:
        fa_ref, z_out, a_out, ys, gs, zs, os, as_ = rest
    else:
        z_out, ys, gs, zs, os = rest
    nb = gate_ref.shape[-2]
    _copy_in(y_ref, ys)
    _copy_in(gate_ref, gs)
    _copy_in(z_ref, zs)
    skip = skip_ref[...]
    for n in range(nb):
        conv = _dot(fc_ref[...], _at_n2(ys, n)[...].astype(BF16))
        _at_n2(os, n)[...] = _at_n2(gs, n)[...] * (conv + skip * _at_n2(zs, n)[...])
    _copy_out(os, z_out)
    if stage_a:
        for n in range(nb):
            _at_n2(as_, n)[...] = _dot(fa_ref[...], _at_n2(os, n)[...].astype(BF16))
        _copy_out(as_, a_out)


def _stage_c(yspec, fc, gate, zin, skip, out_dtype, fa=None, nb=DFT_N2_BLOCK):
    B, r, n2, C = gate.shape
    dat = pl.BlockSpec((2, r, nb, LANES), lambda p, j, c: (p, 0, j, c))
    spc = pl.BlockSpec((1, 2, FFT_N1, nb, LANES), lambda p, j, c: (p, 0, 0, j, c))
    in_specs = [spc, pl.BlockSpec(fc.shape, lambda p, j, c: (0, 0)), dat, dat,
                pl.BlockSpec((1, LANES), lambda p, j, c: (0, c))]
    out_specs = [dat]
    out_shape = [jax.ShapeDtypeStruct(gate.shape, out_dtype)]
    args = [yspec, fc, gate, zin, skip.reshape(1, C)]
    scratch = [_pitched(2 * FFT_N1, nb)] + [_pitched(2 * r, nb)] * 3
    if fa is not None:
        in_specs.append(pl.BlockSpec(fa.shape, lambda p, j, c: (0, 0)))
        out_specs.append(spc)
        out_shape.append(jax.ShapeDtypeStruct(yspec.shape, BF16))
        args.append(fa)
        scratch.append(_pitched(2 * FFT_N1, nb))
    return pl.pallas_call(
        functools.partial(_stage_c_kernel, stage_a=fa is not None),
        grid=(B // 2, n2 // nb, C // LANES),
        in_specs=in_specs, out_specs=out_specs, out_shape=out_shape,
        scratch_shapes=scratch,
        compiler_params=_cparams("parallel", "parallel", "parallel"),
        name="hyena_dft_c",
    )(*args)


def _hyena(x1, x2, v, skip, kf, consts, nb=DFT_N2_BLOCK):
    fa_data, _, fc, fb, fbi = consts
    B, S, C = v.shape
    split = lambda a: a.reshape(B, S // FFT_N2, FFT_N2, C)
    a0 = _stage_a(split(v), fa_data, nb)
    y0 = _stage_b(a0, fb, kf, fbi, 0)
    z1, a1 = _stage_c(y0, fc, split(x1), split(v), skip[0], BF16, fa=fa_data, nb=nb)
    y1 = _stage_b(a1, fb, kf, fbi, 1)
    (out,) = _stage_c(y1, fc, split(x2), z1, skip[1], BF16, nb=nb)
    return out.reshape(B, S, C)


def _memkv_kernel(m_ref, g_ref, w_ref, k_out, v_out):
    hm = _rms(m_ref[0], g_ref[...]).astype(BF16)
    kv = _dot(hm, w_ref[...])
    d = k_out.shape[2]
    k_out[0] = kv[:, :d].astype(BF16)
    v_out[0] = kv[:, d:].astype(BF16)


def _memkv(mem, g, w_mkv):
    B, M, D = mem.shape
    dk = w_mkv.shape[1] // 2
    w = w_mkv.astype(BF16)
    return pl.pallas_call(
        _memkv_kernel,
        grid=(B,),
        in_specs=[pl.BlockSpec((1, M, D), lambda b: (b, 0, 0)),
                  pl.BlockSpec((1, D), lambda b: (0, 0)),
                  pl.BlockSpec(w.shape, lambda b: (0, 0))],
        out_specs=[pl.BlockSpec((1, M, dk), lambda b: (b, 0, 0))] * 2,
        out_shape=[jax.ShapeDtypeStruct((B, M, dk), BF16)] * 2,
        compiler_params=_cparams("parallel"),
        name="mem_kv",
    )(mem, g.reshape(1, D), w)


def _route(logits):
    lane = lax.broadcasted_iota(jnp.int32, logits.shape, 1)
    ninf = -jnp.inf
    big = ROUTE_LANES
    first = lambda mask: jnp.min(jnp.where(mask, lane, big), axis=-1, keepdims=True)
    is_g = (lane >= N_EXPERTS) & (lane < N_EXPERTS + N_GROUPS)
    gl = jnp.where(is_g, logits, ninf)
    gmax = jnp.max(gl, axis=-1, keepdims=True)
    g_idx = first(gl == gmax) - N_EXPERTS
    p_group = 1.0 / jnp.sum(jnp.exp(gl - gmax), axis=-1, keepdims=True)
    in_g = (lane < N_EXPERTS) & ((lane // EXPERTS_PER_GROUP) == g_idx)
    el = jnp.where(in_g, logits, ninf)
    v1 = jnp.max(el, axis=-1, keepdims=True)
    i1 = first(el == v1)
    el2 = jnp.where(lane == i1, ninf, el)
    v2 = jnp.max(el2, axis=-1, keepdims=True)
    i2 = first(el2 == v2)
    e2 = jnp.exp(v2 - v1)
    p1 = 1.0 / (1.0 + e2)
    p2 = e2 / (1.0 + e2)
    sel = lambda n, val: jnp.where(lane == n, val, 0.0)
    return (sel(ROUTE_ID0, i1.astype(F32)) + sel(ROUTE_ID0 + 1, i2.astype(F32))
            + sel(ROUTE_W0, p_group * p1) + sel(ROUTE_W0 + 1, p_group * p2))


def _postmix_kernel(x_ref, a_ref, hy_ref, ag_ref, hg_ref, woa_ref, woh_ref, cg_ref, wmq_ref,
                    mk_ref, mv_ref, wmo_ref, fg_ref, wr_ref, br_ref, x_out, hn_out, route_out,
                    route_t_out):
    ts = x_ref.shape[0] // ROW_SUBTILES
    hn_flat = hn_out.reshape(hn_out.shape[0] * SUBLANES, LANES)
    for t in range(ROW_SUBTILES):
        rows = slice(t * ts, (t + 1) * ts)
        ra = _rms(a_ref[rows, :].astype(F32), ag_ref[...]).astype(BF16)
        rh = _rms(hy_ref[rows, :].astype(F32), hg_ref[...]).astype(BF16)
        x = x_ref[rows, :] + _dot(ra, woa_ref[...]) + _dot(rh, woh_ref[...])
        q = _dot(_rms(x, cg_ref[...]).astype(BF16), wmq_ref[...])
        dh = q.shape[1] // MEM_HEADS
        outs = []
        for h in range(MEM_HEADS):
            sl = slice(h * dh, (h + 1) * dh)
            s = lax.dot_general(q[:, sl].astype(BF16), mk_ref[0, :, sl], (((1,), (1,)), ((), ())),
                                preferred_element_type=F32) * dh ** -0.5
            p = jnp.exp(s - jnp.max(s, axis=-1, keepdims=True))
            l = jnp.sum(p, axis=-1, keepdims=True)
            outs.append(_dot(p.astype(BF16), mv_ref[0, :, sl]) / l)
        o = jnp.concatenate(outs, axis=1).astype(BF16)
        x = x + _dot(o, wmo_ref[...])
        x_out[rows, :] = x
        hn = _rms(x, fg_ref[...])
        for j in range(SUBLANES):
            hn_flat[pl.ds(t * ts * SUBLANES + j, ts, stride=SUBLANES), :] = hn[:, j * LANES:(j + 1) * LANES]
        route = _route(_dot(hn.astype(BF16), wr_ref[...]) + br_ref[...])
        route_out[rows, :] = route
        route_t_out[0, :, rows] = route.T[:SUBLANES, :]


def _postmix(x2d, a2d, hy2d, seq, ag, hg, w_out, cg, w_mq, mk, mv, w_mo, fg, w_rg, b_rg, w_re, b_re, tm=1024):
    T, D = x2d.shape
    ca = a2d.shape[1]
    woa = w_out[:ca].astype(BF16)
    woh = w_out[ca:].astype(BF16)
    pad = ROUTE_LANES - N_EXPERTS - N_GROUPS
    wr = jnp.concatenate([w_re, w_rg, jnp.zeros((D, pad), F32)], 1).astype(BF16)
    br = jnp.concatenate([b_re, b_rg, jnp.zeros((pad,), F32)]).reshape(1, ROUTE_LANES)
    nseq = seq // tm
    full = lambda a: pl.BlockSpec(a.shape, lambda i: (0,) * a.ndim)
    row = lambda n: pl.BlockSpec((tm, n), lambda i: (i, 0))
    memb = pl.BlockSpec((1,) + mk.shape[1:], lambda i: (i // nseq, 0, 0))
    args = [x2d, a2d, hy2d, ag.reshape(1, -1), hg.reshape(1, -1), woa, woh, cg.reshape(1, D),
            w_mq.astype(BF16), mk, mv, w_mo.astype(BF16), fg.reshape(1, D), wr, br]
    in_specs = [row(D), row(ca), row(hy2d.shape[1])] + [full(a) for a in args[3:9]] + [memb, memb] \
        + [full(a) for a in args[11:]]
    return pl.pallas_call(
        _postmix_kernel,
        grid=(T // tm,),
        in_specs=in_specs,
        out_specs=[row(D), pl.BlockSpec((tm, SUBLANES, LANES), lambda i: (i, 0, 0)), row(ROUTE_LANES),
                   pl.BlockSpec((1, SUBLANES, tm), lambda i: (i, 0, 0))],
        out_shape=[jax.ShapeDtypeStruct((T, D), F32), jax.ShapeDtypeStruct((T, SUBLANES, LANES), F32),
                   jax.ShapeDtypeStruct((T, ROUTE_LANES), F32),
                   jax.ShapeDtypeStruct((T // tm, SUBLANES, tm), F32)],
        compiler_params=_cparams("parallel"),
        name="postmix",
    )(*args)


def _slot_onehots(rt):
    e = lax.broadcasted_iota(jnp.int32, (N_EXPERTS, rt.shape[1]), 0).astype(F32)
    return [e == rt[ROUTE_ID0 + k:ROUTE_ID0 + k + 1, :] for k in range(2)]


def _slot_rows(rows, width):
    sub = lax.broadcasted_iota(jnp.int32, (SUBLANES, width), 0)
    return jnp.where(sub == 0, rows[0], jnp.where(sub == 1, rows[1], 0.0))


def _rank_kernel(rt_ref, rank_out, cnt_out, carry_ref):
    @pl.when(pl.program_id(0) == 0)
    def _():
        carry_ref[...] = jnp.zeros_like(carry_ref)

    tr = rt_ref.shape[2]
    oh = _slot_onehots(rt_ref[0])
    cnt = jnp.where(oh[0] | oh[1], 1.0, 0.0)
    s = lax.broadcasted_iota(jnp.int32, (tr, tr), 0)
    t = lax.broadcasted_iota(jnp.int32, (tr, tr), 1)
    before = jnp.where(s < t, 1.0, 0.0).astype(BF16)
    cum = _dot(cnt.astype(BF16), before) + carry_ref[...]
    ranks = [jnp.sum(jnp.where(m, cum, 0.0), axis=0, keepdims=True) for m in oh]
    rank_out[0] = _slot_rows(ranks, tr)
    carry_ref[...] += jnp.sum(cnt, axis=1, keepdims=True)
    cnt_out[...] = carry_ref[...]


def _pos_kernel(rt_ref, rank_ref, base_ref, pos_out):
    tr = rt_ref.shape[2]
    oh = _slot_onehots(rt_ref[0])
    rank = rank_ref[0]
    pos = [jnp.sum(jnp.where(m, base_ref[...], 0.0), axis=0, keepdims=True) + rank[k:k + 1, :]
           for k, m in enumerate(oh)]
    pos_out[0] = _slot_rows(pos, tr).astype(jnp.int32)


def _dispatch_kernel(base_ref, cp_ref, pos0_ref, pos1_ref, hn_ref, xs_hbm, zero_ref, sem, *, tt, tmm):
    i = pl.program_id(0)
    row_copy = lambda src, dst: pltpu.make_async_copy(
        hn_ref.at[pl.ds(src, 1)], xs_hbm.at[pl.ds(dst, 1)], sem)

    @pl.when(i == 0)
    def _():
        zero_ref[...] = jnp.zeros_like(zero_ref)
        pad_copy = lambda e: pltpu.make_async_copy(
            zero_ref, xs_hbm.at[pl.ds(base_ref[e] + cp_ref[e] - tmm, tmm)], sem)
        for e in range(N_EXPERTS):
            @pl.when(cp_ref[e] > 0)
            def _():
                pad_copy(e).start()
        for e in range(N_EXPERTS):
            @pl.when(cp_ref[e] > 0)
            def _():
                pad_copy(e).wait()
        last = N_EXPERTS - 1
        tail_copy = lambda r: pltpu.make_async_copy(zero_ref, xs_hbm.at[pl.ds(r * tmm, tmm)], sem)
        first_free = (base_ref[last] + cp_ref[last]) // tmm
        n_tiles = xs_hbm.shape[0] // tmm
        lax.fori_loop(first_free, n_tiles, lambda r, c: (tail_copy(r).start(), c)[1], 0)
        lax.fori_loop(first_free, n_tiles, lambda r, c: (tail_copy(r).wait(), c)[1], 0)

    def start(t, carry):
        row_copy(t, pos0_ref[t]).start(priority=0)
        row_copy(t, pos1_ref[t]).start(priority=1)
        return carry

    lax.fori_loop(0, tt, start, 0, unroll=DMA_UNROLL)

    def wait(t, carry):
        row_copy(0, 0).wait()
        row_copy(0, 0).wait()
        return carry

    lax.fori_loop(0, tt, wait, 0, unroll=DMA_UNROLL)


def _ffn_kernel(te_ref, nu_ref, xs_ref, wg_ref, wu_ref, wd_ref, ys_ref):
    used = pl.program_id(0) < nu_ref[0]

    @pl.when(used)
    def _():
        x = _load_row_tiles(xs_ref).astype(BF16)
        a = _dot(x, wg_ref[0].astype(BF16))
        b = _dot(x, wu_ref[0].astype(BF16))
        m = (a * jax.nn.sigmoid(a)) * b
        _store_row_tiles(ys_ref, _dot(m.astype(BF16), wd_ref[0].astype(BF16)))

    @pl.when(jnp.logical_not(used))
    def _():
        ys_ref[...] = jnp.zeros_like(ys_ref)


def _combine_kernel(pos0_ref, pos1_ref, pos0_next_ref, pos1_next_ref, ys_hbm, x_ref, route_ref, fg_ref,
                    o_ref, buf_ref, sem, *, tc):
    i = pl.program_id(0)
    slot = i % 2
    row_copy = lambda s, k, t, p: pltpu.make_async_copy(
        ys_hbm.at[pl.ds(p, 1)], buf_ref.at[s, k, pl.ds(t, 1)], sem.at[s])

    def fetch(p0_ref, p1_ref, s):
        def start(t, carry):
            row_copy(s, 0, t, p0_ref[t]).start(priority=0)
            row_copy(s, 1, t, p1_ref[t]).start(priority=1)
            return carry

        lax.fori_loop(0, tc, start, 0, unroll=DMA_UNROLL)

    @pl.when(i == 0)
    def _():
        fetch(pos0_ref, pos1_ref, 0)

    @pl.when(i + 1 < pl.num_programs(0))
    def _():
        fetch(pos0_next_ref, pos1_next_ref, 1 - slot)

    def wait(t, carry):
        row_copy(slot, 0, 0, 0).wait()
        row_copy(slot, 1, 0, 0).wait()
        return carry

    lax.fori_loop(0, tc, wait, 0, unroll=DMA_UNROLL)
    route = route_ref[...]
    y = (x_ref[...] + route[:, ROUTE_W0:ROUTE_W0 + 1] * _load_row_tiles(buf_ref, (slot, 0))
         + route[:, ROUTE_W0 + 1:ROUTE_W0 + 2] * _load_row_tiles(buf_ref, (slot, 1)))
    o_ref[...] = _rms(y, fg_ref[...])


def _moe(hn, route, route_t, x2d, w_gate, w_up, w_down, fg, tt=512, tc=256):
    T, D = x2d.shape
    E = N_EXPERTS
    F = w_gate.shape[-1]
    tmm = MOE_ROW_TILE
    ntr, _, tr = route_t.shape
    row = lambda tm, n: pl.BlockSpec((tm, n), lambda i: (i, 0))
    rec = pl.BlockSpec((1, SUBLANES, tr), lambda i: (i, 0, 0))
    col = pl.BlockSpec((E, 1), lambda i: (0, 0))
    rank, counts = pl.pallas_call(
        _rank_kernel,
        grid=(ntr,),
        in_specs=[rec],
        out_specs=[rec, col],
        out_shape=[jax.ShapeDtypeStruct(route_t.shape, F32), jax.ShapeDtypeStruct((E, 1), F32)],
        scratch_shapes=[pltpu.VMEM((E, 1), F32)],
        compiler_params=_cparams("arbitrary"),
        name="moe_rank",
    )(route_t)

    cnt = counts[:, 0].astype(jnp.int32)
    cp = ((cnt + tmm - 1) // tmm) * tmm
    ends = jnp.cumsum(cp)
    base = ends - cp
    n_used = ends[-1] // tmm
    n_tiles = (2 * T) // tmm + E
    tile_start = jnp.minimum(jnp.arange(n_tiles, dtype=jnp.int32), n_used - 1) * tmm
    tile_expert = jnp.minimum(jnp.sum((tile_start[:, None] >= ends[None, :]).astype(jnp.int32), axis=1), E - 1)

    pos = pl.pallas_call(
        _pos_kernel,
        grid=(ntr,),
        in_specs=[rec, rec, col],
        out_specs=rec,
        out_shape=jax.ShapeDtypeStruct(route_t.shape, jnp.int32),
        compiler_params=_cparams("parallel"),
        name="moe_pos",
    )(route_t, rank, base.astype(F32).reshape(E, 1))
    pos0, pos1 = pos[:, 0, :].reshape(T), pos[:, 1, :].reshape(T)

    xs = pl.pallas_call(
        functools.partial(_dispatch_kernel, tt=tt, tmm=tmm),
        grid_spec=pltpu.PrefetchScalarGridSpec(
            num_scalar_prefetch=2,
            grid=(T // tt,),
            in_specs=[pl.BlockSpec((tt,), lambda i, b, c: (i,), memory_space=pltpu.SMEM),
                      pl.BlockSpec((tt,), lambda i, b, c: (i,), memory_space=pltpu.SMEM),
                      pl.BlockSpec((tt, SUBLANES, LANES), lambda i, b, c: (i, 0, 0))],
            out_specs=pl.BlockSpec(memory_space=pl.ANY),
            scratch_shapes=[pltpu.VMEM((tmm, SUBLANES, LANES), F32), pltpu.SemaphoreType.DMA(())]),
        out_shape=jax.ShapeDtypeStruct((n_tiles * tmm, SUBLANES, LANES), F32),
        compiler_params=_cparams("arbitrary"),
        name="moe_dispatch",
    )(base, cp, pos0, pos1, hn)

    tile = lambda r, te, nu: (jnp.minimum(r, nu[0] - 1), 0, 0)
    ys = pl.pallas_call(
        _ffn_kernel,
        grid_spec=pltpu.PrefetchScalarGridSpec(
            num_scalar_prefetch=2,
            grid=(n_tiles,),
            in_specs=[pl.BlockSpec((tmm, SUBLANES, LANES), tile),
                      pl.BlockSpec((1, D, F), lambda r, te, nu: (te[r], 0, 0)),
                      pl.BlockSpec((1, D, F), lambda r, te, nu: (te[r], 0, 0)),
                      pl.BlockSpec((1, F, D), lambda r, te, nu: (te[r], 0, 0))],
            out_specs=pl.BlockSpec((tmm, SUBLANES, LANES), lambda r, te, nu: (r, 0, 0))),
        out_shape=jax.ShapeDtypeStruct((n_tiles * tmm, SUBLANES, LANES), F32),
        compiler_params=_cparams("arbitrary"),
        name="moe_ffn",
    )(tile_expert, n_used.reshape(1), xs, w_gate.reshape(E, D, F), w_up.reshape(E, D, F),
      w_down.reshape(E, F, D))

    cur = pl.BlockSpec((tc,), lambda i: (i,), memory_space=pltpu.SMEM)
    nxt = pl.BlockSpec((tc,), lambda i: (jnp.minimum(i + 1, T // tc - 1),), memory_space=pltpu.SMEM)
    return pl.pallas_call(
        functools.partial(_combine_kernel, tc=tc),
        grid=(T // tc,),
        in_specs=[cur, cur, nxt, nxt,
                  pl.BlockSpec(memory_space=pl.ANY),
                  row(tc, D), row(tc, ROUTE_LANES), pl.BlockSpec((1, D), lambda i: (0, 0))],
        out_specs=row(tc, D),
        out_shape=jax.ShapeDtypeStruct((T, D), F32),
        scratch_shapes=[pltpu.VMEM((2, 2, tc, SUBLANES, LANES), F32), pltpu.SemaphoreType.DMA((2,))],
        compiler_params=_cparams("arbitrary"),
        name="moe_combine",
    )(pos0, pos1, pos0, pos1, ys, x2d, route, fg.reshape(1, D))


def kernel(x, mem, mix_norm_g, w_in, q_norm_g, kv_norm_g, w_uq, w_ukv, hy_conv_w, hy_conv_b, hy_w1, hy_b1, hy_freq, hy_w2, hy_b2, hy_w3, hy_b3, hy_decay, hy_skip, attn_out_g, hy_out_g, w_out, cross_norm_g, mem_norm_g, w_mq, w_mkv, w_mo, ffn_norm_g, w_route_group, b_route_group, w_route_expert, b_route_expert, w_gate, w_up, w_down, final_norm_g):
    B, S, D = x.shape
    depth = w_in.shape[0]
    consts = _dft_constants(S)
    xf = x.reshape(B * S, D)
    for l in range(depth):
        q, k, v, hx1, hx2, hv = _inproj(xf, S, mix_norm_g[l], w_in[l], q_norm_g[l], kv_norm_g[l],
                                        w_uq[l], w_ukv[l], hy_conv_w[l], hy_conv_b[l])
        HP = q.shape[1]
        a_out = _attention(q.reshape(B, S, HP), k.reshape(B, S, HP), v.reshape(B, S, HP))
        filt = _hyena_filter_time(S, hy_w1[l], hy_b1[l], hy_freq[l], hy_w2[l], hy_b2[l], hy_w3[l],
                                  hy_b3[l], hy_decay[l])
        kf = _hyena_filter_spectrum(filt, consts[1], consts[3])
        C = hv.shape[1]
        h_out = _hyena(hx1.reshape(B, S, C), hx2.reshape(B, S, C), hv.reshape(B, S, C),
                       hy_skip[l], kf, consts)
        mk, mv = _memkv(mem, mem_norm_g[l], w_mkv[l])
        x2, hn, route, route_t = _postmix(xf, a_out.reshape(B * S, -1), h_out.reshape(B * S, C), S,
                                attn_out_g[l], hy_out_g[l], w_out[l], cross_norm_g[l], w_mq[l], mk, mv,
                                w_mo[l], ffn_norm_g[l], w_route_group[l], b_route_group[l],
                                w_route_expert[l], b_route_expert[l])
        assert depth == 1
        xf = _moe(hn, route, route_t, x2, w_gate[l], w_up[l], w_down[l], final_norm_g)
    return xf.reshape(B, S, D)
```

```python
import functools
import math

import numpy as np
import jax
import jax.numpy as jnp
from jax import lax
from jax.experimental import pallas as pl
from jax.experimental.pallas import tpu as pltpu

F32 = jnp.float32
BF16 = jnp.bfloat16

EPS = 1e-6
MLA_HEADS = 8
MLA_NOPE = 64
MLA_ROPE = 32
MLA_V = 64
ROPE_BASE = 10000.0
HEAD_PAD = 128
HY_ORDER = 2
HY_DIRS = 2
HY_BANDS = 16
MEM_HEADS = 4
N_GROUPS = 4
EXPERTS_PER_GROUP = 8
N_EXPERTS = N_GROUPS * EXPERTS_PER_GROUP
ROUTE_LANES = 128
ROUTE_ID0 = 0
ROUTE_W0 = 2
MOE_ROW_TILE = 512
MOE_CLEAR_ROWS = 64
DMA_UNROLL = 8

FFT_N1 = 64
FFT_N2 = 128
DFT_K1_BLOCK = 4
DFT_N2_BLOCK = 32
DFT_PITCH_PAD = 8

VMEM_LIMIT = 56 * 1024 * 1024


def _cparams(*sem):
    return pltpu.CompilerParams(dimension_semantics=sem, vmem_limit_bytes=VMEM_LIMIT)


def _rms(x, g):
    return x * lax.rsqrt(jnp.mean(x * x, axis=-1, keepdims=True) + EPS) * g


def _dot(a, b):
    return jnp.dot(a, b, preferred_element_type=F32)


SUBLANES = 8
LANES = 128


def _load_row_tiles(ref, lead=()):
    rows = ref.shape[-3]
    flat = ref.reshape(*ref.shape[:-3], rows * SUBLANES, LANES)
    return jnp.concatenate(
        [flat[(*lead, pl.ds(j, rows, stride=SUBLANES), slice(None))] for j in range(SUBLANES)], axis=1)


def _store_row_tiles(ref, val):
    rows = ref.shape[0]
    flat = ref.reshape(rows * SUBLANES, LANES)
    for j in range(SUBLANES):
        flat[pl.ds(j, rows, stride=SUBLANES), :] = val[:, j * LANES:(j + 1) * LANES]


def _inproj_kernel(x_ref, xp_ref, xn_ref, g_ref, wq_ref, wkv_ref, wkra_ref, wkrb_ref, why_ref, qg_ref,
                   kvg_ref, wqa_ref, wqb_ref, wka_ref, wv_ref, tab_ref, cw_ref, cb_ref,
                   q_out, k_out, v_out, x1_out, x2_out, hv_out, *, nseq):
    tm = x_ref.shape[0]
    halo = xp_ref.shape[0]
    hf = _rms(jnp.concatenate([xp_ref[...], x_ref[...], xn_ref[...]], axis=0), g_ref[...])
    h = hf[halo:halo + tm].astype(BF16)
    qn = _rms(_dot(h, wq_ref[...]), qg_ref[...]).astype(BF16)
    kvn = _rms(_dot(h, wkv_ref[...]), kvg_ref[...]).astype(BF16)
    tab = tab_ref[...]
    cq, sq, ck, sk = (tab[:, j * HEAD_PAD:(j + 1) * HEAD_PAD] for j in range(4))
    tile = lambda t: jnp.concatenate([t] * MLA_HEADS, axis=1)
    q = _dot(qn, wqa_ref[...]) * tile(cq) + _dot(qn, wqb_ref[...]) * tile(sq)
    q_out[...] = q.astype(BF16)
    kr = _dot(h, wkra_ref[...]) * ck + _dot(h, wkrb_ref[...]) * sk
    k_out[...] = (_dot(kvn, wka_ref[...]) + tile(kr)).astype(BF16)
    lane = lax.broadcasted_iota(jnp.int32, (1, v_out.shape[1]), 1) % HEAD_PAD
    v_out[...] = (_dot(kvn, wv_ref[...]) + jnp.where(lane == MLA_V, 1.0, 0.0)).astype(BF16)
    hy = _dot(hf.astype(BF16), why_ref[...])
    i = pl.program_id(0) % nseq
    row = lax.broadcasted_iota(jnp.int32, hy.shape, 0)
    outside = ((row == halo - 1) & (i == 0)) | ((row == halo + tm) & (i == nseq - 1))
    hy = jnp.where(outside, 0.0, hy)
    cw = cw_ref[...]
    u = (hy[halo - 1:halo - 1 + tm] * cw[0:1] + hy[halo:halo + tm] * cw[1:2]
         + hy[halo + 1:halo + 1 + tm] * cw[2:3] + cb_ref[...])
    c = x1_out.shape[1]
    x1_out[...] = u[:, :c].astype(x1_out.dtype)
    x2_out[...] = u[:, c:2 * c].astype(x2_out.dtype)
    hv_out[...] = u[:, 2 * c:].astype(hv_out.dtype)


def _inproj(x2d, seq, mix_g, w_in, q_g, kv_g, w_uq, w_ukv, conv_w, conv_b, tm=1024):
    T, D = x2d.shape
    per = tm // SUBLANES
    cb = conv_b.reshape(1, -1)
    q_rank, kv_rank = q_g.shape[0], kv_g.shape[0]
    off_kv = q_rank
    off_kr = off_kv + kv_rank
    off_hy = off_kr + MLA_ROPE
    C = (w_in.shape[1] - off_hy) // 3
    H = MLA_HEADS
    half = MLA_ROPE // 2
    wq = w_in[:, :off_kv].astype(BF16)
    wkv = w_in[:, off_kv:off_kr].astype(BF16)
    wkr = w_in[:, off_kr:off_hy]
    wkr_sw = jnp.concatenate([wkr[:, half:], wkr[:, :half]], axis=1)
    zpad = lambda n: jnp.zeros((D, n), F32)
    wkra = jnp.concatenate([zpad(MLA_NOPE), wkr, zpad(HEAD_PAD - MLA_NOPE - MLA_ROPE)], 1).astype(BF16)
    wkrb = jnp.concatenate([zpad(MLA_NOPE), wkr_sw, zpad(HEAD_PAD - MLA_NOPE - MLA_ROPE)], 1).astype(BF16)
    why = w_in[:, off_hy:].astype(BF16)

    uq = w_uq.reshape(q_rank, H, MLA_NOPE + MLA_ROPE)
    uq_n, uq_r = uq[..., :MLA_NOPE], uq[..., MLA_NOPE:]
    uq_rs = jnp.concatenate([uq_r[..., half:], uq_r[..., :half]], axis=-1)
    zq = lambda n: jnp.zeros((q_rank, H, n), F32)
    wqa = jnp.concatenate([uq_n, uq_r, zq(HEAD_PAD - MLA_NOPE - MLA_ROPE)], -1).reshape(q_rank, H * HEAD_PAD).astype(BF16)
    wqb = jnp.concatenate([zq(MLA_NOPE), uq_rs, zq(HEAD_PAD - MLA_NOPE - MLA_ROPE)], -1).reshape(q_rank, H * HEAD_PAD).astype(BF16)
    ukv = w_ukv.reshape(kv_rank, H, MLA_NOPE + MLA_V)
    zk = lambda n: jnp.zeros((kv_rank, H, n), F32)
    wka = jnp.concatenate([ukv[..., :MLA_NOPE], zk(HEAD_PAD - MLA_NOPE)], -1).reshape(kv_rank, H * HEAD_PAD).astype(BF16)
    wv = jnp.concatenate([ukv[..., MLA_NOPE:], zk(HEAD_PAD - MLA_V)], -1).reshape(kv_rank, H * HEAD_PAD).astype(BF16)

    ang = np.arange(seq)[:, None] * ROPE_BASE ** (-np.arange(half) / half)[None, :]
    cos2 = np.concatenate([np.cos(ang), np.cos(ang)], 1)
    sin2 = np.concatenate([-np.sin(ang), np.sin(ang)], 1)
    zs = lambda n: np.zeros((seq, n))
    scale = (MLA_NOPE + MLA_ROPE) ** -0.5 * math.log2(math.e)
    rest = HEAD_PAD - MLA_NOPE - MLA_ROPE
    cq = scale * np.concatenate([np.ones((seq, MLA_NOPE)), cos2, zs(rest)], 1)
    sq = scale * np.concatenate([zs(MLA_NOPE), sin2, zs(rest)], 1)
    ck = np.concatenate([zs(MLA_NOPE), cos2, zs(rest)], 1)
    sk = np.concatenate([zs(MLA_NOPE), sin2, zs(rest)], 1)
    tab = jnp.asarray(np.concatenate([cq, sq, ck, sk], 1), dtype=F32)

    nseq = seq // tm
    full = lambda a: pl.BlockSpec(a.shape, lambda i: (0,) * a.ndim)
    row = lambda n: pl.BlockSpec((tm, n), lambda i: (i, 0))
    consts = [mix_g.reshape(1, D), wq, wkv, wkra, wkrb, why, q_g.reshape(1, -1), kv_g.reshape(1, -1),
              wqa, wqb, wka, wv]
    HP = H * HEAD_PAD
    return pl.pallas_call(
        functools.partial(_inproj_kernel, nseq=nseq),
        grid=(T // tm,),
        in_specs=[row(D),
                  pl.BlockSpec((SUBLANES, D), lambda i: (jnp.maximum(i * per - 1, 0), 0)),
                  pl.BlockSpec((SUBLANES, D), lambda i: (jnp.minimum((i + 1) * per, T // SUBLANES - 1), 0))]
        + [full(a) for a in consts]
        + [pl.BlockSpec((tm, 4 * HEAD_PAD), lambda i: (i % nseq, 0)), full(conv_w), full(cb)],
        out_specs=[row(HP), row(HP), row(HP), row(C), row(C), row(C)],
        out_shape=[jax.ShapeDtypeStruct((T, HP), BF16)] * 3 + [jax.ShapeDtypeStruct((T, C), BF16)] * 3,
        compiler_params=_cparams("parallel"),
        name="inproj",
    )(x2d, x2d, x2d, *consts, tab, conv_w, cb)


def _attn_kernel(q_ref, k_ref, v_ref, o_ref):
    outs = []
    for h in range(MLA_HEADS):
        sl = slice(h * HEAD_PAD, (h + 1) * HEAD_PAD)
        s = lax.dot_general(q_ref[0, :, sl], k_ref[0, :, sl], (((1,), (1,)), ((), ())),
                            preferred_element_type=F32).astype(BF16)
        p = jnp.exp2(s - jnp.max(s, axis=-1, keepdims=True))
        o = _dot(p, v_ref[0, :, sl])
        outs.append(o[:, :MLA_V] / o[:, MLA_V:MLA_V + 1])
    o_ref[0] = jnp.concatenate(outs, axis=1).astype(o_ref.dtype)


def _attention(q, k, v, tq=512):
    B, S, HP = q.shape
    return pl.pallas_call(
        _attn_kernel,
        grid=(B, S // tq),
        in_specs=[pl.BlockSpec((1, tq, HP), lambda b, i: (b, i, 0)),
                  pl.BlockSpec((1, S, HP), lambda b, i: (b, 0, 0)),
                  pl.BlockSpec((1, S, HP), lambda b, i: (b, 0, 0))],
        out_specs=pl.BlockSpec((1, tq, MLA_HEADS * MLA_V), lambda b, i: (b, i, 0)),
        out_shape=jax.ShapeDtypeStruct((B, S, MLA_HEADS * MLA_V), BF16),
        compiler_params=_cparams("parallel", "arbitrary"),
        name="mla_attention",
    )(q, k, v)


def _dft_constants(seq):
    n = 2 * seq
    n1, n2 = FFT_N1, FFT_N2
    assert n1 * n2 == n
    r1 = np.arange(n1)
    r2 = np.arange(n2)
    blk = lambda z: np.block([[z.real, -z.imag], [z.imag, z.real]])
    w1 = np.exp(-2j * np.pi * np.outer(r1, r1) / n1)
    fa_data = blk(w1[:, :n1 // 2])
    fa_filt = np.concatenate([w1.real, w1.imag], axis=0)
    fc = blk(np.conj(w1).T[:n1 // 2, :])
    w2 = np.exp(-2j * np.pi * np.outer(r2, r2) / n2)
    tw = np.exp(-2j * np.pi * np.outer(r1, r2) / n)
    fb = np.stack([blk(w2 * tw[k][None, :]) for k in range(n1)])
    fbi = np.stack([blk(np.conj(w2).T * np.conj(tw[k])[:, None] / n) for k in range(n1)])
    as_bf = lambda a: jnp.asarray(a, dtype=F32).astype(BF16)
    return as_bf(fa_data), as_bf(fa_filt), as_bf(fc), as_bf(fb), as_bf(fbi)


def _filter_kernel(z_ref, w1_ref, b1_ref, fr_ref, w2_ref, b2_ref, w3_ref, b3_ref, dec_ref, o_ref, *, seq, tr):
    hp = lax.Precision.HIGHEST
    z = z_ref[...]
    fr = fr_ref[...]
    h = jnp.sin(fr[0:1] * (jnp.dot(z, w1_ref[...], precision=hp, preferred_element_type=F32) + b1_ref[...]))
    h = jnp.sin(fr[1:2] * (jnp.dot(h, w2_ref[...], precision=hp, preferred_element_type=F32) + b2_ref[...]))
    split = lambda a: (a.astype(BF16), (a - a.astype(BF16).astype(F32)).astype(BF16))
    (hh, hl), (wh, wl) = split(h), split(w3_ref[0])
    h = _dot(hh, wh) + _dot(hh, wl) + _dot(hl, wh) + b3_ref[0]
    h = h * jnp.exp(-z[:, 0:1] * jnp.abs(dec_ref[0]))
    n = pl.program_id(0) * tr + lax.broadcasted_iota(jnp.int32, h.shape, 0)
    o_ref[...] = jnp.where(n == seq, 0.0, h).astype(o_ref.dtype)


def _hyena_filter_time(seq, w1, b1, freq, w2, b2, w3, b3, decay, tr=512):
    n = 2 * seq
    emb, ffn = w1.shape
    C = w3.shape[1] // (HY_ORDER * HY_DIRS)
    off = np.arange(n)
    t = np.where(off < seq, off, n - off).astype(np.float64)
    bands = np.linspace(1e-4, HY_BANDS - 1, HY_BANDS)
    ang = 2.0 * math.pi * t[:, None] * bands[None, :] / seq
    z = np.concatenate([(t / seq)[:, None], np.cos(ang), -np.sin(ang)], axis=-1)
    zl = LANES
    z = jnp.asarray(np.pad(z, ((0, 0), (0, zl - emb))), dtype=F32)
    w1p = jnp.pad(w1, ((0, zl - emb), (0, 0)))
    by_dir = lambda a: jnp.moveaxis(a.reshape(a.shape[0], HY_ORDER, HY_DIRS, C), 2, 0).reshape(
        HY_DIRS, a.shape[0], HY_ORDER * C)
    w3d, b3d, decd = by_dir(w3), by_dir(b3.reshape(1, -1)), by_dir(decay.reshape(1, -1))
    full = lambda a: pl.BlockSpec(a.shape, lambda i: (0,) * a.ndim)
    ndir = lambda a: pl.BlockSpec((1,) + a.shape[1:], lambda i: ((i * tr) // seq, 0, 0))
    consts = [w1p, b1.reshape(1, -1), freq, w2, b2.reshape(1, -1)]
    return pl.pallas_call(
        functools.partial(_filter_kernel, seq=seq, tr=tr),
        grid=(n // tr,),
        in_specs=[pl.BlockSpec((tr, zl), lambda i: (i, 0))] + [full(a) for a in consts]
        + [ndir(w3d), ndir(b3d), ndir(decd)],
        out_specs=pl.BlockSpec((tr, HY_ORDER * C), lambda i: (i, 0)),
        out_shape=jax.ShapeDtypeStruct((n, HY_ORDER * C), BF16),
        compiler_params=_cparams("parallel"),
        name="hyena_filter_mlp",
    )(z, *consts, w3d, b3d, decd)


def _pitched(rows, nb):
    return pltpu.VMEM((rows, nb + DFT_PITCH_PAD, LANES), F32)


def _block_rows(ref):
    return math.prod(ref.shape[:-2]), ref.shape[-2]


def _copy_in(ref, scr):
    rows, nb = _block_rows(ref)
    scr[:, :nb, :] = ref[...].reshape(rows, nb, LANES).astype(scr.dtype)


def _copy_out(scr, ref):
    rows, nb = _block_rows(ref)
    ref[...] = scr[:, :nb, :].reshape(ref.shape).astype(ref.dtype)


def _at_n2(scr, n):
    rows, pitch, _ = scr.shape
    return scr.reshape(rows * pitch, LANES).at[pl.ds(n, rows, stride=pitch), :]


def _outer_dft(mat_ref, x_ref, o_ref, xs, os):
    _copy_in(x_ref, xs)
    for n in range(x_ref.shape[-2]):
        _at_n2(os, n)[...] = _dot(mat_ref[...], _at_n2(xs, n)[...].astype(BF16))
    _copy_out(os, o_ref)


def _filter_stage_a_kernel(x_ref, fa_ref, o_ref, xs, os):
    _outer_dft(fa_ref, x_ref, o_ref, xs, os)


def _filter_stage_b_kernel(x_ref, fb_ref, o_ref):
    for kk in range(x_ref.shape[1]):
        x = jnp.concatenate([x_ref[0, kk], x_ref[1, kk]], axis=0).astype(BF16)
        o_ref[kk] = _dot(fb_ref[kk], x).astype(o_ref.dtype)


def _hyena_filter_spectrum(filt, fa_filt, fb, nb=DFT_N2_BLOCK):
    n, oc = filt.shape
    a = pl.pallas_call(
        _filter_stage_a_kernel,
        grid=(FFT_N2 // nb, oc // LANES),
        in_specs=[pl.BlockSpec((FFT_N1, nb, LANES), lambda j, c: (0, j, c)),
                  pl.BlockSpec(fa_filt.shape, lambda j, c: (0, 0))],
        out_specs=pl.BlockSpec((2, FFT_N1, nb, LANES), lambda j, c: (0, 0, j, c)),
        out_shape=jax.ShapeDtypeStruct((2, FFT_N1, FFT_N2, oc), BF16),
        scratch_shapes=[_pitched(FFT_N1, nb), _pitched(2 * FFT_N1, nb)],
        compiler_params=_cparams("parallel", "parallel"),
        name="hyena_filter_dft_a",
    )(filt.reshape(FFT_N1, FFT_N2, oc), fa_filt)
    return pl.pallas_call(
        _filter_stage_b_kernel,
        grid=(FFT_N1 // DFT_K1_BLOCK,),
        in_specs=[pl.BlockSpec((2, DFT_K1_BLOCK, FFT_N2, oc), lambda k: (0, k, 0, 0)),
                  pl.BlockSpec((DFT_K1_BLOCK, 2 * FFT_N2, 2 * FFT_N2), lambda k: (k, 0, 0))],
        out_specs=pl.BlockSpec((DFT_K1_BLOCK, 2 * FFT_N2, oc), lambda k: (k, 0, 0)),
        out_shape=jax.ShapeDtypeStruct((FFT_N1, 2 * FFT_N2, oc), BF16),
        compiler_params=_cparams("parallel"),
        name="hyena_filter_dft_b",
    )(a, fb)


def _stage_a_kernel(x_ref, fa_ref, o_ref, xs, os):
    _outer_dft(fa_ref, x_ref, o_ref, xs, os)


def _stage_a(x4, fa, nb):
    B, r, n2, C = x4.shape
    return pl.pallas_call(
        _stage_a_kernel,
        grid=(B // 2, n2 // nb, C // LANES),
        in_specs=[pl.BlockSpec((2, r, nb, LANES), lambda p, j, c: (p, 0, j, c)),
                  pl.BlockSpec(fa.shape, lambda p, j, c: (0, 0))],
        out_specs=pl.BlockSpec((1, 2, FFT_N1, nb, LANES), lambda p, j, c: (p, 0, 0, j, c)),
        out_shape=jax.ShapeDtypeStruct((B // 2, 2, FFT_N1, n2, C), BF16),
        scratch_shapes=[_pitched(2 * r, nb), _pitched(2 * FFT_N1, nb)],
        compiler_params=_cparams("parallel", "parallel", "parallel"),
        name="hyena_dft_a",
    )(x4, fa)


def _stage_b_kernel(x_ref, fb_ref, kf_ref, fbi_ref, o_ref):
    npair = x_ref.shape[0]
    n2 = x_ref.shape[3]
    c = x_ref.shape[4]
    for kk in range(x_ref.shape[2]):
        x = jnp.concatenate(
            [jnp.concatenate([x_ref[p, 0, kk], x_ref[p, 1, kk]], axis=0) for p in range(npair)],
            axis=1).astype(BF16)
        g = _dot(fb_ref[kk], x)
        gr, gi = g[:n2], g[n2:]
        kf = kf_ref[kk].astype(F32)
        kr = jnp.concatenate([kf[:n2]] * npair, axis=1)
        ki = jnp.concatenate([kf[n2:]] * npair, axis=1)
        hcat = jnp.concatenate([gr * kr - gi * ki, gr * ki + gi * kr], axis=0).astype(BF16)
        y = _dot(fbi_ref[kk], hcat)
        for p in range(npair):
            o_ref[p, 0, kk] = y[:n2, p * c:(p + 1) * c].astype(o_ref.dtype)
            o_ref[p, 1, kk] = y[n2:, p * c:(p + 1) * c].astype(o_ref.dtype)


def _stage_b(spec, fb, kf, fbi, order, nk=DFT_K1_BLOCK):
    npair, _, _, _, C = spec.shape
    blk = pl.BlockSpec((npair, 2, nk, FFT_N2, C), lambda k: (0, 0, k, 0, 0))
    mat = pl.BlockSpec((nk, 2 * FFT_N2, 2 * FFT_N2), lambda k: (k, 0, 0))
    return pl.pallas_call(
        _stage_b_kernel,
        grid=(FFT_N1 // nk,),
        in_specs=[blk, mat, pl.BlockSpec((nk, 2 * FFT_N2, C), lambda k: (k, 0, order)), mat],
        out_specs=blk,
        out_shape=jax.ShapeDtypeStruct(spec.shape, BF16),
        compiler_params=_cparams("parallel"),
        name="hyena_dft_b",
    )(spec, fb, kf, fbi)


def _stage_c_kernel(y_ref, fc_ref, gate_ref, z_ref, skip_ref, *rest, stage_a):
    if stage_a:
        fa_ref, z_out, a_out, ys, gs, zs, os, as_ = rest
    else:
        z_out, ys, gs, zs, os = rest
    nb = gate_ref.shape[-2]
    _copy_in(y_ref, ys)
    _copy_in(gate_ref, gs)
    _copy_in(z_ref, zs)
    skip = skip_ref[...]
    for n in range(nb):
        conv = _dot(fc_ref[...], _at_n2(ys, n)[...].astype(BF16))
        _at_n2(os, n)[...] = _at_n2(gs, n)[...] * (conv + skip * _at_n2(zs, n)[...])
    _copy_out(os, z_out)
    if stage_a:
        for n in range(nb):
            _at_n2(as_, n)[...] = _dot(fa_ref[...], _at_n2(os, n)[...].astype(BF16))
        _copy_out(as_, a_out)


def _stage_c(yspec, fc, gate, zin, skip, fa=None, nb=DFT_N2_BLOCK):
    B, r, n2, C = gate.shape
    dat = pl.BlockSpec((2, r, nb, LANES), lambda p, j, c: (p, 0, j, c))
    spc = pl.BlockSpec((1, 2, FFT_N1, nb, LANES), lambda p, j, c: (p, 0, 0, j, c))
    in_specs = [spc, pl.BlockSpec(fc.shape, lambda p, j, c: (0, 0)), dat, dat,
                pl.BlockSpec((1, LANES), lambda p, j, c: (0, c))]
    out_specs = [dat]
    out_shape = [jax.ShapeDtypeStruct(gate.shape, BF16)]
    args = [yspec, fc, gate, zin, skip.reshape(1, C)]
    scratch = [_pitched(2 * FFT_N1, nb)] + [_pitched(2 * r, nb)] * 3
    if fa is not None:
        in_specs.append(pl.BlockSpec(fa.shape, lambda p, j, c: (0, 0)))
        out_specs.append(spc)
        out_shape.append(jax.ShapeDtypeStruct(yspec.shape, BF16))
        args.append(fa)
        scratch.append(_pitched(2 * FFT_N1, nb))
    return pl.pallas_call(
        functools.partial(_stage_c_kernel, stage_a=fa is not None),
        grid=(B // 2, n2 // nb, C // LANES),
        in_specs=in_specs, out_specs=out_specs, out_shape=out_shape,
        scratch_shapes=scratch,
        compiler_params=_cparams("parallel", "parallel", "parallel"),
        name="hyena_dft_c",
    )(*args)


def _hyena(x1, x2, v, skip, kf, consts, nb=DFT_N2_BLOCK):
    fa_data, _, fc, fb, fbi = consts
    B, S, C = v.shape
    split = lambda a: a.reshape(B, S // FFT_N2, FFT_N2, C)
    a0 = _stage_a(split(v), fa_data, nb)
    y0 = _stage_b(a0, fb, kf, fbi, 0)
    z1, a1 = _stage_c(y0, fc, split(x1), split(v), skip[0], fa=fa_data, nb=nb)
    y1 = _stage_b(a1, fb, kf, fbi, 1)
    (out,) = _stage_c(y1, fc, split(x2), z1, skip[1], nb=nb)
    return out.reshape(B, S, C)


def _memkv_kernel(m_ref, g_ref, w_ref, k_out, v_out):
    hm = _rms(m_ref[0], g_ref[...]).astype(BF16)
    kv = _dot(hm, w_ref[...])
    d = k_out.shape[2]
    k_out[0] = kv[:, :d].astype(BF16)
    v_out[0] = kv[:, d:].astype(BF16)


def _memkv(mem, g, w_mkv):
    B, M, D = mem.shape
    dk = w_mkv.shape[1] // 2
    w = w_mkv.astype(BF16)
    return pl.pallas_call(
        _memkv_kernel,
        grid=(B,),
        in_specs=[pl.BlockSpec((1, M, D), lambda b: (b, 0, 0)),
                  pl.BlockSpec((1, D), lambda b: (0, 0)),
                  pl.BlockSpec(w.shape, lambda b: (0, 0))],
        out_specs=[pl.BlockSpec((1, M, dk), lambda b: (b, 0, 0))] * 2,
        out_shape=[jax.ShapeDtypeStruct((B, M, dk), BF16)] * 2,
        compiler_params=_cparams("parallel"),
        name="mem_kv",
    )(mem, g.reshape(1, D), w)


def _route(logits):
    lane = lax.broadcasted_iota(jnp.int32, logits.shape, 1)
    ninf = -jnp.inf
    big = ROUTE_LANES
    first = lambda mask: jnp.min(jnp.where(mask, lane, big), axis=-1, keepdims=True)
    is_g = (lane >= N_EXPERTS) & (lane < N_EXPERTS + N_GROUPS)
    gl = jnp.where(is_g, logits, ninf)
    gmax = jnp.max(gl, axis=-1, keepdims=True)
    g_idx = first(gl == gmax) - N_EXPERTS
    p_group = 1.0 / jnp.sum(jnp.exp(gl - gmax), axis=-1, keepdims=True)
    in_g = (lane < N_EXPERTS) & ((lane // EXPERTS_PER_GROUP) == g_idx)
    el = jnp.where(in_g, logits, ninf)
    v1 = jnp.max(el, axis=-1, keepdims=True)
    i1 = first(el == v1)
    el2 = jnp.where(lane == i1, ninf, el)
    v2 = jnp.max(el2, axis=-1, keepdims=True)
    i2 = first(el2 == v2)
    e2 = jnp.exp(v2 - v1)
    p1 = 1.0 / (1.0 + e2)
    p2 = e2 / (1.0 + e2)
    sel = lambda n, val: jnp.where(lane == n, val, 0.0)
    return (sel(ROUTE_ID0, i1.astype(F32)) + sel(ROUTE_ID0 + 1, i2.astype(F32))
            + sel(ROUTE_W0, p_group * p1) + sel(ROUTE_W0 + 1, p_group * p2))


def _postmix_kernel(x_ref, a_ref, hy_ref, ag_ref, hg_ref, woa_ref, woh_ref, cg_ref, wmq_ref,
                    mk_ref, mv_ref, wmo_ref, fg_ref, wr_ref, br_ref, x_out, hn_out, route_out,
                    route_t_out):
    ra = _rms(a_ref[...].astype(F32), ag_ref[...]).astype(BF16)
    rh = _rms(hy_ref[...].astype(F32), hg_ref[...]).astype(BF16)
    x = x_ref[...] + _dot(ra, woa_ref[...]) + _dot(rh, woh_ref[...])
    q = _dot(_rms(x, cg_ref[...]).astype(BF16), wmq_ref[...])
    dh = q.shape[1] // MEM_HEADS
    outs = []
    for h in range(MEM_HEADS):
        sl = slice(h * dh, (h + 1) * dh)
        s = lax.dot_general(q[:, sl].astype(BF16), mk_ref[0, :, sl], (((1,), (1,)), ((), ())),
                            preferred_element_type=F32) * dh ** -0.5
        p = jnp.exp(s - jnp.max(s, axis=-1, keepdims=True))
        l = jnp.sum(p, axis=-1, keepdims=True)
        outs.append(_dot(p.astype(BF16), mv_ref[0, :, sl]) / l)
    o = jnp.concatenate(outs, axis=1).astype(BF16)
    x = x + _dot(o, wmo_ref[...])
    x_out[...] = x
    hn = _rms(x, fg_ref[...])
    _store_row_tiles(hn_out, hn)
    route = _route(_dot(hn.astype(BF16), wr_ref[...]) + br_ref[...])
    route_out[...] = route
    route_t_out[0] = route.T[:SUBLANES, :]


def _postmix(x2d, a2d, hy2d, seq, ag, hg, w_out, cg, w_mq, mk, mv, w_mo, fg, w_rg, b_rg, w_re, b_re, tm=1024):
    T, D = x2d.shape
    ca = a2d.shape[1]
    woa = w_out[:ca].astype(BF16)
    woh = w_out[ca:].astype(BF16)
    pad = ROUTE_LANES - N_EXPERTS - N_GROUPS
    wr = jnp.concatenate([w_re, w_rg, jnp.zeros((D, pad), F32)], 1).astype(BF16)
    br = jnp.concatenate([b_re, b_rg, jnp.zeros((pad,), F32)]).reshape(1, ROUTE_LANES)
    nseq = seq // tm
    full = lambda a: pl.BlockSpec(a.shape, lambda i: (0,) * a.ndim)
    row = lambda n: pl.BlockSpec((tm, n), lambda i: (i, 0))
    memb = pl.BlockSpec((1,) + mk.shape[1:], lambda i: (i // nseq, 0, 0))
    args = [x2d, a2d, hy2d, ag.reshape(1, -1), hg.reshape(1, -1), woa, woh, cg.reshape(1, D),
            w_mq.astype(BF16), mk, mv, w_mo.astype(BF16), fg.reshape(1, D), wr, br]
    in_specs = [row(D), row(ca), row(hy2d.shape[1])] + [full(a) for a in args[3:9]] + [memb, memb] \
        + [full(a) for a in args[11:]]
    return pl.pallas_call(
        _postmix_kernel,
        grid=(T // tm,),
        in_specs=in_specs,
        out_specs=[row(D), pl.BlockSpec((tm, SUBLANES, LANES), lambda i: (i, 0, 0)), row(ROUTE_LANES),
                   pl.BlockSpec((1, SUBLANES, tm), lambda i: (i, 0, 0))],
        out_shape=[jax.ShapeDtypeStruct((T, D), F32), jax.ShapeDtypeStruct((T, SUBLANES, LANES), F32),
                   jax.ShapeDtypeStruct((T, ROUTE_LANES), F32),
                   jax.ShapeDtypeStruct((T // tm, SUBLANES, tm), F32)],
        compiler_params=_cparams("parallel"),
        name="postmix",
    )(*args)


def _slot_onehots(rt):
    e = lax.broadcasted_iota(jnp.int32, (N_EXPERTS, rt.shape[1]), 0).astype(F32)
    return [e == rt[ROUTE_ID0 + k:ROUTE_ID0 + k + 1, :] for k in range(2)]


def _slot_rows(rows, width):
    sub = lax.broadcasted_iota(jnp.int32, (SUBLANES, width), 0)
    return jnp.where(sub == 0, rows[0], jnp.where(sub == 1, rows[1], 0.0))


def _rank_kernel(rt_ref, rank_out, cnt_out, carry_ref):
    @pl.when(pl.program_id(0) == 0)
    def _():
        carry_ref[...] = jnp.zeros_like(carry_ref)

    tr = rt_ref.shape[2]
    oh = _slot_onehots(rt_ref[0])
    cnt = jnp.where(oh[0] | oh[1], 1.0, 0.0)
    s = lax.broadcasted_iota(jnp.int32, (tr, tr), 0)
    t = lax.broadcasted_iota(jnp.int32, (tr, tr), 1)
    before = jnp.where(s < t, 1.0, 0.0).astype(BF16)
    cum = _dot(cnt.astype(BF16), before) + carry_ref[...]
    ranks = [jnp.sum(jnp.where(m, cum, 0.0), axis=0, keepdims=True) for m in oh]
    rank_out[0] = _slot_rows(ranks, tr)
    carry_ref[...] += jnp.sum(cnt, axis=1, keepdims=True)
    cnt_out[...] = carry_ref[...]


def _pos_kernel(rt_ref, rank_ref, base_ref, pos_out):
    tr = rt_ref.shape[2]
    oh = _slot_onehots(rt_ref[0])
    rank = rank_ref[0]
    pos = [jnp.sum(jnp.where(m, base_ref[...], 0.0), axis=0, keepdims=True) + rank[k:k + 1, :]
           for k, m in enumerate(oh)]
    pos_out[0] = _slot_rows(pos, tr).astype(jnp.int32)


def _dispatch_kernel(base_ref, cnt_ref, cp_ref, pos0_ref, pos1_ref, hn_ref, xs_hbm, zero_ref, sem, *, tt):
    i = pl.program_id(0)
    row_copy = lambda src, dst: pltpu.make_async_copy(
        hn_ref.at[pl.ds(src, 1)], xs_hbm.at[pl.ds(dst, 1)], sem)

    @pl.when(i == 0)
    def _():
        zc = zero_ref.shape[0]
        zero_ref[...] = jnp.zeros_like(zero_ref)
        chunk = lambda c: pltpu.make_async_copy(zero_ref, xs_hbm.at[pl.ds(c * zc, zc)], sem)

        def clear(lo, hi):
            lax.fori_loop(lo, hi, lambda c, carry: (chunk(c).start(), carry)[1], 0)

        def drain(lo, hi):
            lax.fori_loop(lo, hi, lambda c, carry: (chunk(c).wait(), carry)[1], 0)

        last = N_EXPERTS - 1
        spans = [((base_ref[e] + cnt_ref[e]) // zc, (base_ref[e] + cp_ref[e]) // zc)
                 for e in range(N_EXPERTS)]
        spans.append(((base_ref[last] + cp_ref[last]) // zc, xs_hbm.shape[0] // zc))
        for lo, hi in spans:
            clear(lo, hi)
        for lo, hi in spans:
            drain(lo, hi)

    def start(t, carry):
        row_copy(t, pos0_ref[t]).start(priority=0)
        row_copy(t, pos1_ref[t]).start(priority=1)
        return carry

    lax.fori_loop(0, tt, start, 0, unroll=DMA_UNROLL)

    def wait(t, carry):
        row_copy(0, 0).wait()
        row_copy(0, 0).wait()
        return carry

    lax.fori_loop(0, tt, wait, 0, unroll=DMA_UNROLL)


def _ffn_kernel(te_ref, nu_ref, xs_ref, wg_ref, wu_ref, wd_ref, ys_ref):
    used = pl.program_id(0) < nu_ref[0]

    @pl.when(used)
    def _():
        x = _load_row_tiles(xs_ref).astype(BF16)
        a = _dot(x, wg_ref[0].astype(BF16))
        b = _dot(x, wu_ref[0].astype(BF16))
        m = (a * jax.nn.sigmoid(a)) * b
        _store_row_tiles(ys_ref, _dot(m.astype(BF16), wd_ref[0].astype(BF16)))

    @pl.when(jnp.logical_not(used))
    def _():
        ys_ref[...] = jnp.zeros_like(ys_ref)


def _combine_kernel(pos0_ref, pos1_ref, pos0_next_ref, pos1_next_ref, ys_hbm, x_ref, route_ref, fg_ref,
                    o_ref, buf_ref, sem, *, tc):
    i = pl.program_id(0)
    slot = i % 2
    row_copy = lambda s, k, t, p: pltpu.make_async_copy(
        ys_hbm.at[pl.ds(p, 1)], buf_ref.at[s, k, pl.ds(t, 1)], sem.at[s])

    def fetch(p0_ref, p1_ref, s):
        def start(t, carry):
            row_copy(s, 0, t, p0_ref[t]).start(priority=0)
            row_copy(s, 1, t, p1_ref[t]).start(priority=1)
            return carry

        lax.fori_loop(0, tc, start, 0, unroll=DMA_UNROLL)

    @pl.when(i == 0)
    def _():
        fetch(pos0_ref, pos1_ref, 0)

    @pl.when(i + 1 < pl.num_programs(0))
    def _():
        fetch(pos0_next_ref, pos1_next_ref, 1 - slot)

    def wait(t, carry):
        row_copy(slot, 0, 0, 0).wait()
        row_copy(slot, 1, 0, 0).wait()
        return carry

    lax.fori_loop(0, tc, wait, 0, unroll=DMA_UNROLL)
    route = route_ref[...]
    y = (x_ref[...] + route[:, ROUTE_W0:ROUTE_W0 + 1] * _load_row_tiles(buf_ref, (slot, 0))
         + route[:, ROUTE_W0 + 1:ROUTE_W0 + 2] * _load_row_tiles(buf_ref, (slot, 1)))
    o_ref[...] = _rms(y, fg_ref[...])


def _moe(hn, route, route_t, x2d, w_gate, w_up, w_down, fg, tt=512, tc=256):
    T, D = x2d.shape
    E = N_EXPERTS
    F = w_gate.shape[-1]
    tmm = MOE_ROW_TILE
    ntr, _, tr = route_t.shape
    row = lambda tm, n: pl.BlockSpec((tm, n), lambda i: (i, 0))
    rec = pl.BlockSpec((1, SUBLANES, tr), lambda i: (i, 0, 0))
    col = pl.BlockSpec((E, 1), lambda i: (0, 0))
    rank, counts = pl.pallas_call(
        _rank_kernel,
        grid=(ntr,),
        in_specs=[rec],
        out_specs=[rec, col],
        out_shape=[jax.ShapeDtypeStruct(route_t.shape, F32), jax.ShapeDtypeStruct((E, 1), F32)],
        scratch_shapes=[pltpu.VMEM((E, 1), F32)],
        compiler_params=_cparams("arbitrary"),
        name="moe_rank",
    )(route_t)

    cnt = counts[:, 0].astype(jnp.int32)
    cp = ((cnt + tmm - 1) // tmm) * tmm
    ends = jnp.cumsum(cp)
    base = ends - cp
    n_used = ends[-1] // tmm
    n_tiles = (2 * T) // tmm + E
    tile_start = jnp.minimum(jnp.arange(n_tiles, dtype=jnp.int32), n_used - 1) * tmm
    tile_expert = jnp.minimum(jnp.sum((tile_start[:, None] >= ends[None, :]).astype(jnp.int32), axis=1), E - 1)

    pos = pl.pallas_call(
        _pos_kernel,
        grid=(ntr,),
        in_specs=[rec, rec, col],
        out_specs=rec,
        out_shape=jax.ShapeDtypeStruct(route_t.shape, jnp.int32),
        compiler_params=_cparams("parallel"),
        name="moe_pos",
    )(route_t, rank, base.astype(F32).reshape(E, 1))
    pos0, pos1 = pos[:, 0, :].reshape(T), pos[:, 1, :].reshape(T)

    xs = pl.pallas_call(
        functools.partial(_dispatch_kernel, tt=tt),
        grid_spec=pltpu.PrefetchScalarGridSpec(
            num_scalar_prefetch=3,
            grid=(T // tt,),
            in_specs=[pl.BlockSpec((tt,), lambda i, b, n, c: (i,), memory_space=pltpu.SMEM),
                      pl.BlockSpec((tt,), lambda i, b, n, c: (i,), memory_space=pltpu.SMEM),
                      pl.BlockSpec((tt, SUBLANES, LANES), lambda i, b, n, c: (i, 0, 0))],
            out_specs=pl.BlockSpec(memory_space=pl.ANY),
            scratch_shapes=[pltpu.VMEM((MOE_CLEAR_ROWS, SUBLANES, LANES), F32),
                            pltpu.SemaphoreType.DMA(())]),
        out_shape=jax.ShapeDtypeStruct((n_tiles * tmm, SUBLANES, LANES), F32),
        compiler_params=_cparams("arbitrary"),
        name="moe_dispatch",
    )(base, cnt, cp, pos0, pos1, hn)

    tile = lambda r, te, nu: (jnp.minimum(r, nu[0] - 1), 0, 0)
    ys = pl.pallas_call(
        _ffn_kernel,
        grid_spec=pltpu.PrefetchScalarGridSpec(
            num_scalar_prefetch=2,
            grid=(n_tiles,),
            in_specs=[pl.BlockSpec((tmm, SUBLANES, LANES), tile),
                      pl.BlockSpec((1, D, F), lambda r, te, nu: (te[r], 0, 0)),
                      pl.BlockSpec((1, D, F), lambda r, te, nu: (te[r], 0, 0)),
                      pl.BlockSpec((1, F, D), lambda r, te, nu: (te[r], 0, 0))],
            out_specs=pl.BlockSpec((tmm, SUBLANES, LANES), lambda r, te, nu: (r, 0, 0))),
        out_shape=jax.ShapeDtypeStruct((n_tiles * tmm, SUBLANES, LANES), F32),
        compiler_params=_cparams("arbitrary"),
        name="moe_ffn",
    )(tile_expert, n_used.reshape(1), xs, w_gate.reshape(E, D, F), w_up.reshape(E, D, F),
      w_down.reshape(E, F, D))

    cur = pl.BlockSpec((tc,), lambda i: (i,), memory_space=pltpu.SMEM)
    nxt = pl.BlockSpec((tc,), lambda i: (jnp.minimum(i + 1, T // tc - 1),), memory_space=pltpu.SMEM)
    return pl.pallas_call(
        functools.partial(_combine_kernel, tc=tc),
        grid=(T // tc,),
        in_specs=[cur, cur, nxt, nxt,
                  pl.BlockSpec(memory_space=pl.ANY),
                  row(tc, D), row(tc, ROUTE_LANES), pl.BlockSpec((1, D), lambda i: (0, 0))],
        out_specs=row(tc, D),
        out_shape=jax.ShapeDtypeStruct((T, D), F32),
        scratch_shapes=[pltpu.VMEM((2, 2, tc, SUBLANES, LANES), F32), pltpu.SemaphoreType.DMA((2,))],
        compiler_params=_cparams("arbitrary"),
        name="moe_combine",
    )(pos0, pos1, pos0, pos1, ys, x2d, route, fg.reshape(1, D))


def kernel(x, mem, mix_norm_g, w_in, q_norm_g, kv_norm_g, w_uq, w_ukv, hy_conv_w, hy_conv_b, hy_w1, hy_b1, hy_freq, hy_w2, hy_b2, hy_w3, hy_b3, hy_decay, hy_skip, attn_out_g, hy_out_g, w_out, cross_norm_g, mem_norm_g, w_mq, w_mkv, w_mo, ffn_norm_g, w_route_group, b_route_group, w_route_expert, b_route_expert, w_gate, w_up, w_down, final_norm_g):
    B, S, D = x.shape
    depth = w_in.shape[0]
    consts = _dft_constants(S)
    xf = x.reshape(B * S, D)
    for l in range(depth):
        q, k, v, hx1, hx2, hv = _inproj(xf, S, mix_norm_g[l], w_in[l], q_norm_g[l], kv_norm_g[l],
                                        w_uq[l], w_ukv[l], hy_conv_w[l], hy_conv_b[l])
        HP = q.shape[1]
        a_out = _attention(q.reshape(B, S, HP), k.reshape(B, S, HP), v.reshape(B, S, HP))
        filt = _hyena_filter_time(S, hy_w1[l], hy_b1[l], hy_freq[l], hy_w2[l], hy_b2[l], hy_w3[l],
                                  hy_b3[l], hy_decay[l])
        kf = _hyena_filter_spectrum(filt, consts[1], consts[3])
        C = hv.shape[1]
        h_out = _hyena(hx1.reshape(B, S, C), hx2.reshape(B, S, C), hv.reshape(B, S, C),
                       hy_skip[l], kf, consts)
        mk, mv = _memkv(mem, mem_norm_g[l], w_mkv[l])
        x2, hn, route, route_t = _postmix(xf, a_out.reshape(B * S, -1), h_out.reshape(B * S, C), S,
                                attn_out_g[l], hy_out_g[l], w_out[l], cross_norm_g[l], w_mq[l], mk, mv,
                                w_mo[l], ffn_norm_g[l], w_route_group[l], b_route_group[l],
                                w_route_expert[l], b_route_expert[l])
        assert depth == 1
        xf = _moe(hn, route, route_t, x2, w_gate[l], w_up[l], w_down[l], final_norm_g)
    return xf.reshape(B, S, D)
```

```python
import functools
import math

import numpy as np
import jax
import jax.numpy as jnp
from jax import lax
from jax.experimental import pallas as pl
from jax.experimental.pallas import tpu as pltpu

F32 = jnp.float32
BF16 = jnp.bfloat16

EPS = 1e-6
MLA_HEADS = 8
MLA_NOPE = 64
MLA_ROPE = 32
MLA_V = 64
ROPE_BASE = 10000.0
HEAD_PAD = 128
HY_ORDER = 2
HY_DIRS = 2
HY_BANDS = 16
MEM_HEADS = 4
N_GROUPS = 4
EXPERTS_PER_GROUP = 8
N_EXPERTS = N_GROUPS * EXPERTS_PER_GROUP
ROUTE_LANES = 128
ROUTE_ID0 = 0
ROUTE_W0 = 2
MOE_ROW_TILE = 512
MOE_CLEAR_ROWS = 64
DMA_UNROLL = 8

FFT_N1 = 64
FFT_N2 = 128
DFT_K1_BLOCK = 8
DFT_N2_BLOCK = 64
DFT_PITCH_PAD = 8

VMEM_LIMIT = 56 * 1024 * 1024


def _cparams(*sem):
    return pltpu.CompilerParams(dimension_semantics=sem, vmem_limit_bytes=VMEM_LIMIT)


def _rms(x, g):
    return x * lax.rsqrt(jnp.mean(x * x, axis=-1, keepdims=True) + EPS) * g


def _dot(a, b):
    return jnp.dot(a, b, preferred_element_type=F32)


SUBLANES = 8
LANES = 128


def _load_row_tiles(ref, lead=()):
    rows = ref.shape[-3]
    flat = ref.reshape(*ref.shape[:-3], rows * SUBLANES, LANES)
    return jnp.concatenate(
        [flat[(*lead, pl.ds(j, rows, stride=SUBLANES), slice(None))] for j in range(SUBLANES)], axis=1)


def _store_row_tiles(ref, val):
    rows = ref.shape[0]
    flat = ref.reshape(rows * SUBLANES, LANES)
    for j in range(SUBLANES):
        flat[pl.ds(j, rows, stride=SUBLANES), :] = val[:, j * LANES:(j + 1) * LANES]


def _inproj_kernel(x_ref, xp_ref, xn_ref, g_ref, wq_ref, wkv_ref, wkra_ref, wkrb_ref, why_ref, qg_ref,
                   kvg_ref, wqa_ref, wqb_ref, wka_ref, wv_ref, tab_ref, cw_ref, cb_ref,
                   q_out, k_out, v_out, x1_out, x2_out, hv_out, *, nseq):
    tm = x_ref.shape[0]
    halo = xp_ref.shape[0]
    hf = _rms(jnp.concatenate([xp_ref[...], x_ref[...], xn_ref[...]], axis=0), g_ref[...])
    h = hf[halo:halo + tm].astype(BF16)
    qn = _rms(_dot(h, wq_ref[...]), qg_ref[...]).astype(BF16)
    kvn = _rms(_dot(h, wkv_ref[...]), kvg_ref[...]).astype(BF16)
    tab = tab_ref[...]
    cq, sq, ck, sk = (tab[:, j * HEAD_PAD:(j + 1) * HEAD_PAD] for j in range(4))
    tile = lambda t: jnp.concatenate([t] * MLA_HEADS, axis=1)
    q = _dot(qn, wqa_ref[...]) * tile(cq) + _dot(qn, wqb_ref[...]) * tile(sq)
    q_out[...] = q.astype(BF16)
    kr = _dot(h, wkra_ref[...]) * ck + _dot(h, wkrb_ref[...]) * sk
    k_out[...] = (_dot(kvn, wka_ref[...]) + tile(kr)).astype(BF16)
    lane = lax.broadcasted_iota(jnp.int32, (1, v_out.shape[1]), 1) % HEAD_PAD
    v_out[...] = (_dot(kvn, wv_ref[...]) + jnp.where(lane == MLA_V, 1.0, 0.0)).astype(BF16)
    hy = _dot(hf.astype(BF16), why_ref[...])
    i = pl.program_id(0) % nseq
    row = lax.broadcasted_iota(jnp.int32, hy.shape, 0)
    outside = ((row == halo - 1) & (i == 0)) | ((row == halo + tm) & (i == nseq - 1))
    hy = jnp.where(outside, 0.0, hy)
    cw = cw_ref[...]
    u = (hy[halo - 1:halo - 1 + tm] * cw[0:1] + hy[halo:halo + tm] * cw[1:2]
         + hy[halo + 1:halo + 1 + tm] * cw[2:3] + cb_ref[...])
    c = x1_out.shape[1]
    x1_out[...] = u[:, :c].astype(x1_out.dtype)
    x2_out[...] = u[:, c:2 * c].astype(x2_out.dtype)
    hv_out[...] = u[:, 2 * c:].astype(hv_out.dtype)


def _inproj(x2d, seq, mix_g, w_in, q_g, kv_g, w_uq, w_ukv, conv_w, conv_b, tm=1024):
    T, D = x2d.shape
    per = tm // SUBLANES
    cb = conv_b.reshape(1, -1)
    q_rank, kv_rank = q_g.shape[0], kv_g.shape[0]
    off_kv = q_rank
    off_kr = off_kv + kv_rank
    off_hy = off_kr + MLA_ROPE
    C = (w_in.shape[1] - off_hy) // 3
    H = MLA_HEADS
    half = MLA_ROPE // 2
    wq = w_in[:, :off_kv].astype(BF16)
    wkv = w_in[:, off_kv:off_kr].astype(BF16)
    wkr = w_in[:, off_kr:off_hy]
    wkr_sw = jnp.concatenate([wkr[:, half:], wkr[:, :half]], axis=1)
    zpad = lambda n: jnp.zeros((D, n), F32)
    wkra = jnp.concatenate([zpad(MLA_NOPE), wkr, zpad(HEAD_PAD - MLA_NOPE - MLA_ROPE)], 1).astype(BF16)
    wkrb = jnp.concatenate([zpad(MLA_NOPE), wkr_sw, zpad(HEAD_PAD - MLA_NOPE - MLA_ROPE)], 1).astype(BF16)
    why = w_in[:, off_hy:].astype(BF16)

    uq = w_uq.reshape(q_rank, H, MLA_NOPE + MLA_ROPE)
    uq_n, uq_r = uq[..., :MLA_NOPE], uq[..., MLA_NOPE:]
    uq_rs = jnp.concatenate([uq_r[..., half:], uq_r[..., :half]], axis=-1)
    zq = lambda n: jnp.zeros((q_rank, H, n), F32)
    wqa = jnp.concatenate([uq_n, uq_r, zq(HEAD_PAD - MLA_NOPE - MLA_ROPE)], -1).reshape(q_rank, H * HEAD_PAD).astype(BF16)
    wqb = jnp.concatenate([zq(MLA_NOPE), uq_rs, zq(HEAD_PAD - MLA_NOPE - MLA_ROPE)], -1).reshape(q_rank, H * HEAD_PAD).astype(BF16)
    ukv = w_ukv.reshape(kv_rank, H, MLA_NOPE + MLA_V)
    zk = lambda n: jnp.zeros((kv_rank, H, n), F32)
    wka = jnp.concatenate([ukv[..., :MLA_NOPE], zk(HEAD_PAD - MLA_NOPE)], -1).reshape(kv_rank, H * HEAD_PAD).astype(BF16)
    wv = jnp.concatenate([ukv[..., MLA_NOPE:], zk(HEAD_PAD - MLA_V)], -1).reshape(kv_rank, H * HEAD_PAD).astype(BF16)

    ang = np.arange(seq)[:, None] * ROPE_BASE ** (-np.arange(half) / half)[None, :]
    cos2 = np.concatenate([np.cos(ang), np.cos(ang)], 1)
    sin2 = np.concatenate([-np.sin(ang), np.sin(ang)], 1)
    zs = lambda n: np.zeros((seq, n))
    scale = (MLA_NOPE + MLA_ROPE) ** -0.5 * math.log2(math.e)
    rest = HEAD_PAD - MLA_NOPE - MLA_ROPE
    cq = scale * np.concatenate([np.ones((seq, MLA_NOPE)), cos2, zs(rest)], 1)
    sq = scale * np.concatenate([zs(MLA_NOPE), sin2, zs(rest)], 1)
    ck = np.concatenate([zs(MLA_NOPE), cos2, zs(rest)], 1)
    sk = np.concatenate([zs(MLA_NOPE), sin2, zs(rest)], 1)
    tab = jnp.asarray(np.concatenate([cq, sq, ck, sk], 1), dtype=F32)

    nseq = seq // tm
    full = lambda a: pl.BlockSpec(a.shape, lambda i: (0,) * a.ndim)
    row = lambda n: pl.BlockSpec((tm, n), lambda i: (i, 0))
    consts = [mix_g.reshape(1, D), wq, wkv, wkra, wkrb, why, q_g.reshape(1, -1), kv_g.reshape(1, -1),
              wqa, wqb, wka, wv]
    HP = H * HEAD_PAD
    return pl.pallas_call(
        functools.partial(_inproj_kernel, nseq=nseq),
        grid=(T // tm,),
        in_specs=[row(D),
                  pl.BlockSpec((SUBLANES, D), lambda i: (jnp.maximum(i * per - 1, 0), 0)),
                  pl.BlockSpec((SUBLANES, D), lambda i: (jnp.minimum((i + 1) * per, T // SUBLANES - 1), 0))]
        + [full(a) for a in consts]
        + [pl.BlockSpec((tm, 4 * HEAD_PAD), lambda i: (i % nseq, 0)), full(conv_w), full(cb)],
        out_specs=[row(HP), row(HP), row(HP), row(C), row(C), row(C)],
        out_shape=[jax.ShapeDtypeStruct((T, HP), BF16)] * 3 + [jax.ShapeDtypeStruct((T, C), BF16)] * 3,
        compiler_params=_cparams("parallel"),
        name="inproj",
    )(x2d, x2d, x2d, *consts, tab, conv_w, cb)


def _attn_kernel(q_ref, k_ref, v_ref, o_ref):
    outs = []
    for h in range(MLA_HEADS):
        sl = slice(h * HEAD_PAD, (h + 1) * HEAD_PAD)
        s = lax.dot_general(q_ref[0, :, sl], k_ref[0, :, sl], (((1,), (1,)), ((), ())),
                            preferred_element_type=F32).astype(BF16)
        p = jnp.exp2(s - jnp.max(s, axis=-1, keepdims=True))
        o = _dot(p, v_ref[0, :, sl])
        outs.append(o[:, :MLA_V] / o[:, MLA_V:MLA_V + 1])
    o_ref[0] = jnp.concatenate(outs, axis=1).astype(o_ref.dtype)


def _attention(q, k, v, tq=512):
    B, S, HP = q.shape
    return pl.pallas_call(
        _attn_kernel,
        grid=(B, S // tq),
        in_specs=[pl.BlockSpec((1, tq, HP), lambda b, i: (b, i, 0)),
                  pl.BlockSpec((1, S, HP), lambda b, i: (b, 0, 0)),
                  pl.BlockSpec((1, S, HP), lambda b, i: (b, 0, 0))],
        out_specs=pl.BlockSpec((1, tq, MLA_HEADS * MLA_V), lambda b, i: (b, i, 0)),
        out_shape=jax.ShapeDtypeStruct((B, S, MLA_HEADS * MLA_V), BF16),
        compiler_params=_cparams("parallel", "arbitrary"),
        name="mla_attention",
    )(q, k, v)


def _dft_constants(seq):
    n = 2 * seq
    n1, n2 = FFT_N1, FFT_N2
    assert n1 * n2 == n
    r1 = np.arange(n1)
    r2 = np.arange(n2)
    blk = lambda z: np.block([[z.real, -z.imag], [z.imag, z.real]])
    w1 = np.exp(-2j * np.pi * np.outer(r1, r1) / n1)
    fa_data = blk(w1[:, :n1 // 2])
    fa_filt = np.concatenate([w1.real, w1.imag], axis=0)
    fc = blk(np.conj(w1).T[:n1 // 2, :])
    w2 = np.exp(-2j * np.pi * np.outer(r2, r2) / n2)
    tw = np.exp(-2j * np.pi * np.outer(r1, r2) / n)
    fb = np.stack([blk(w2 * tw[k][None, :]) for k in range(n1)])
    fbi = np.stack([blk(np.conj(w2).T * np.conj(tw[k])[:, None] / n) for k in range(n1)])
    as_bf = lambda a: jnp.asarray(a, dtype=F32).astype(BF16)
    return as_bf(fa_data), as_bf(fa_filt), as_bf(fc), as_bf(fb), as_bf(fbi)


def _filter_kernel(z_ref, w1_ref, b1_ref, fr_ref, w2_ref, b2_ref, w3_ref, b3_ref, dec_ref, o_ref, *, seq, tr):
    hp = lax.Precision.HIGHEST
    z = z_ref[...]
    fr = fr_ref[...]
    h = jnp.sin(fr[0:1] * (jnp.dot(z, w1_ref[...], precision=hp, preferred_element_type=F32) + b1_ref[...]))
    h = jnp.sin(fr[1:2] * (jnp.dot(h, w2_ref[...], precision=hp, preferred_element_type=F32) + b2_ref[...]))
    split = lambda a: (a.astype(BF16), (a - a.astype(BF16).astype(F32)).astype(BF16))
    (hh, hl), (wh, wl) = split(h), split(w3_ref[0])
    h = _dot(hh, wh) + _dot(hh, wl) + _dot(hl, wh) + b3_ref[0]
    h = h * jnp.exp(-z[:, 0:1] * jnp.abs(dec_ref[0]))
    n = pl.program_id(0) * tr + lax.broadcasted_iota(jnp.int32, h.shape, 0)
    o_ref[...] = jnp.where(n == seq, 0.0, h).astype(o_ref.dtype)


def _hyena_filter_time(seq, w1, b1, freq, w2, b2, w3, b3, decay, tr=512):
    n = 2 * seq
    emb, ffn = w1.shape
    C = w3.shape[1] // (HY_ORDER * HY_DIRS)
    off = np.arange(n)
    t = np.where(off < seq, off, n - off).astype(np.float64)
    bands = np.linspace(1e-4, HY_BANDS - 1, HY_BANDS)
    ang = 2.0 * math.pi * t[:, None] * bands[None, :] / seq
    z = np.concatenate([(t / seq)[:, None], np.cos(ang), -np.sin(ang)], axis=-1)
    zl = LANES
    z = jnp.asarray(np.pad(z, ((0, 0), (0, zl - emb))), dtype=F32)
    w1p = jnp.pad(w1, ((0, zl - emb), (0, 0)))
    by_dir = lambda a: jnp.moveaxis(a.reshape(a.shape[0], HY_ORDER, HY_DIRS, C), 2, 0).reshape(
        HY_DIRS, a.shape[0], HY_ORDER * C)
    w3d, b3d, decd = by_dir(w3), by_dir(b3.reshape(1, -1)), by_dir(decay.reshape(1, -1))
    full = lambda a: pl.BlockSpec(a.shape, lambda i: (0,) * a.ndim)
    ndir = lambda a: pl.BlockSpec((1,) + a.shape[1:], lambda i: ((i * tr) // seq, 0, 0))
    consts = [w1p, b1.reshape(1, -1), freq, w2, b2.reshape(1, -1)]
    return pl.pallas_call(
        functools.partial(_filter_kernel, seq=seq, tr=tr),
        grid=(n // tr,),
        in_specs=[pl.BlockSpec((tr, zl), lambda i: (i, 0))] + [full(a) for a in consts]
        + [ndir(w3d), ndir(b3d), ndir(decd)],
        out_specs=pl.BlockSpec((tr, HY_ORDER * C), lambda i: (i, 0)),
        out_shape=jax.ShapeDtypeStruct((n, HY_ORDER * C), BF16),
        compiler_params=_cparams("parallel"),
        name="hyena_filter_mlp",
    )(z, *consts, w3d, b3d, decd)


def _pitched(rows, nb):
    return pltpu.VMEM((rows, nb + DFT_PITCH_PAD, LANES), F32)


def _block_rows(ref):
    return math.prod(ref.shape[:-2]), ref.shape[-2]


def _copy_in(ref, scr):
    rows, nb = _block_rows(ref)
    scr[:, :nb, :] = ref[...].reshape(rows, nb, LANES).astype(scr.dtype)


def _copy_out(scr, ref):
    rows, nb = _block_rows(ref)
    ref[...] = scr[:, :nb, :].reshape(ref.shape).astype(ref.dtype)


def _at_n2(scr, n):
    rows, pitch, _ = scr.shape
    return scr.reshape(rows * pitch, LANES).at[pl.ds(n, rows, stride=pitch), :]


def _outer_dft(mat_ref, x_ref, o_ref, xs, os):
    _copy_in(x_ref, xs)
    for n in range(x_ref.shape[-2]):
        _at_n2(os, n)[...] = _dot(mat_ref[...], _at_n2(xs, n)[...].astype(BF16))
    _copy_out(os, o_ref)


def _filter_stage_a_kernel(x_ref, fa_ref, o_ref, xs, os):
    _outer_dft(fa_ref, x_ref, o_ref, xs, os)


def _filter_stage_b_kernel(x_ref, fb_ref, o_ref):
    for kk in range(x_ref.shape[1]):
        x = jnp.concatenate([x_ref[0, kk], x_ref[1, kk]], axis=0).astype(BF16)
        o_ref[kk] = _dot(fb_ref[kk], x).astype(o_ref.dtype)


def _hyena_filter_spectrum(filt, fa_filt, fb, nb=DFT_N2_BLOCK):
    n, oc = filt.shape
    a = pl.pallas_call(
        _filter_stage_a_kernel,
        grid=(FFT_N2 // nb, oc // LANES),
        in_specs=[pl.BlockSpec((FFT_N1, nb, LANES), lambda j, c: (0, j, c)),
                  pl.BlockSpec(fa_filt.shape, lambda j, c: (0, 0))],
        out_specs=pl.BlockSpec((2, FFT_N1, nb, LANES), lambda j, c: (0, 0, j, c)),
        out_shape=jax.ShapeDtypeStruct((2, FFT_N1, FFT_N2, oc), BF16),
        scratch_shapes=[_pitched(FFT_N1, nb), _pitched(2 * FFT_N1, nb)],
        compiler_params=_cparams("parallel", "parallel"),
        name="hyena_filter_dft_a",
    )(filt.reshape(FFT_N1, FFT_N2, oc), fa_filt)
    return pl.pallas_call(
        _filter_stage_b_kernel,
        grid=(FFT_N1 // DFT_K1_BLOCK,),
        in_specs=[pl.BlockSpec((2, DFT_K1_BLOCK, FFT_N2, oc), lambda k: (0, k, 0, 0)),
                  pl.BlockSpec((DFT_K1_BLOCK, 2 * FFT_N2, 2 * FFT_N2), lambda k: (k, 0, 0))],
        out_specs=pl.BlockSpec((DFT_K1_BLOCK, 2 * FFT_N2, oc), lambda k: (k, 0, 0)),
        out_shape=jax.ShapeDtypeStruct((FFT_N1, 2 * FFT_N2, oc), BF16),
        compiler_params=_cparams("parallel"),
        name="hyena_filter_dft_b",
    )(a, fb)


def _stage_a_kernel(x_ref, fa_ref, o_ref, xs, os):
    _outer_dft(fa_ref, x_ref, o_ref, xs, os)


def _stage_a(x4, fa, nb):
    B, r, n2, C = x4.shape
    return pl.pallas_call(
        _stage_a_kernel,
        grid=(B // 2, n2 // nb, C // LANES),
        in_specs=[pl.BlockSpec((2, r, nb, LANES), lambda p, j, c: (p, 0, j, c)),
                  pl.BlockSpec(fa.shape, lambda p, j, c: (0, 0))],
        out_specs=pl.BlockSpec((1, 2, FFT_N1, nb, LANES), lambda p, j, c: (p, 0, 0, j, c)),
        out_shape=jax.ShapeDtypeStruct((B // 2, 2, FFT_N1, n2, C), BF16),
        scratch_shapes=[_pitched(2 * r, nb), _pitched(2 * FFT_N1, nb)],
        compiler_params=_cparams("parallel", "parallel", "parallel"),
        name="hyena_dft_a",
    )(x4, fa)


def _stage_b_kernel(x_ref, fb_ref, kf_ref, fbi_ref, o_ref):
    npair = x_ref.shape[0]
    n2 = x_ref.shape[3]
    c = x_ref.shape[4]
    for kk in range(x_ref.shape[2]):
        x = jnp.concatenate(
            [jnp.concatenate([x_ref[p, 0, kk], x_ref[p, 1, kk]], axis=0) for p in range(npair)],
            axis=1).astype(BF16)
        g = _dot(fb_ref[kk], x)
        gr, gi = g[:n2], g[n2:]
        kf = kf_ref[kk].astype(F32)
        kr = jnp.concatenate([kf[:n2]] * npair, axis=1)
        ki = jnp.concatenate([kf[n2:]] * npair, axis=1)
        hcat = jnp.concatenate([gr * kr - gi * ki, gr * ki + gi * kr], axis=0).astype(BF16)
        y = _dot(fbi_ref[kk], hcat)
        for p in range(npair):
            o_ref[p, 0, kk] = y[:n2, p * c:(p + 1) * c].astype(o_ref.dtype)
            o_ref[p, 1, kk] = y[n2:, p * c:(p + 1) * c].astype(o_ref.dtype)


def _stage_b(spec, fb, kf, fbi, order, nk=DFT_K1_BLOCK):
    npair, _, _, _, C = spec.shape
    blk = pl.BlockSpec((npair, 2, nk, FFT_N2, C), lambda k: (0, 0, k, 0, 0))
    mat = pl.BlockSpec((nk, 2 * FFT_N2, 2 * FFT_N2), lambda k: (k, 0, 0))
    return pl.pallas_call(
        _stage_b_kernel,
        grid=(FFT_N1 // nk,),
        in_specs=[blk, mat, pl.BlockSpec((nk, 2 * FFT_N2, C), lambda k: (k, 0, order)), mat],
        out_specs=blk,
        out_shape=jax.ShapeDtypeStruct(spec.shape, BF16),
        compiler_params=_cparams("parallel"),
        name="hyena_dft_b",
    )(spec, fb, kf, fbi)


def _stage_c_kernel(y_ref, fc_ref, gate_ref, z_ref, skip_ref, *rest, stage_a):
    if stage_a:
        fa_ref, z_out, a_out, ys, gs, zs, os, as_ = rest
    else:
        z_out, ys, gs, zs, os = rest
    nb = gate_ref.shape[-2]
    _copy_in(y_ref, ys)
    _copy_in(gate_ref, gs)
    _copy_in(z_ref, zs)
    skip = skip_ref[...]
    for n in range(nb):
        conv = _dot(fc_ref[...], _at_n2(ys, n)[...].astype(BF16))
        _at_n2(os, n)[...] = _at_n2(gs, n)[...] * (conv + skip * _at_n2(zs, n)[...])
    _copy_out(os, z_out)
    if stage_a:
        for n in range(nb):
            _at_n2(as_, n)[...] = _dot(fa_ref[...], _at_n2(os, n)[...].astype(BF16))
        _copy_out(as_, a_out)


def _stage_c(yspec, fc, gate, zin, skip, fa=None, nb=DFT_N2_BLOCK):
    B, r, n2, C = gate.shape
    dat = pl.BlockSpec((2, r, nb, LANES), lambda p, j, c: (p, 0, j, c))
    spc = pl.BlockSpec((1, 2, FFT_N1, nb, LANES), lambda p, j, c: (p, 0, 0, j, c))
    in_specs = [spc, pl.BlockSpec(fc.shape, lambda p, j, c: (0, 0)), dat, dat,
                pl.BlockSpec((1, LANES), lambda p, j, c: (0, c))]
    out_specs = [dat]
    out_shape = [jax.ShapeDtypeStruct(gate.shape, BF16)]
    args = [yspec, fc, gate, zin, skip.reshape(1, C)]
    scratch = [_pitched(2 * FFT_N1, nb)] + [_pitched(2 * r, nb)] * 3
    if fa is not None:
        in_specs.append(pl.BlockSpec(fa.shape, lambda p, j, c: (0, 0)))
        out_specs.append(spc)
        out_shape.append(jax.ShapeDtypeStruct(yspec.shape, BF16))
        args.append(fa)
        scratch.append(_pitched(2 * FFT_N1, nb))
    return pl.pallas_call(
        functools.partial(_stage_c_kernel, stage_a=fa is not None),
        grid=(B // 2, n2 // nb, C // LANES),
        in_specs=in_specs, out_specs=out_specs, out_shape=out_shape,
        scratch_shapes=scratch,
        compiler_params=_cparams("parallel", "parallel", "parallel"),
        name="hyena_dft_c",
    )(*args)


def _hyena(x1, x2, v, skip, kf, consts, nb=DFT_N2_BLOCK):
    fa_data, _, fc, fb, fbi = consts
    B, S, C = v.shape
    split = lambda a: a.reshape(B, S // FFT_N2, FFT_N2, C)
    a0 = _stage_a(split(v), fa_data, nb)
    y0 = _stage_b(a0, fb, kf, fbi, 0)
    z1, a1 = _stage_c(y0, fc, split(x1), split(v), skip[0], fa=fa_data, nb=nb)
    y1 = _stage_b(a1, fb, kf, fbi, 1)
    (out,) = _stage_c(y1, fc, split(x2), z1, skip[1], nb=nb)
    return out.reshape(B, S, C)


def _memkv_kernel(m_ref, g_ref, w_ref, k_out, v_out):
    hm = _rms(m_ref[0], g_ref[...]).astype(BF16)
    kv = _dot(hm, w_ref[...])
    d = k_out.shape[2]
    k_out[0] = kv[:, :d].astype(BF16)
    v_out[0] = kv[:, d:].astype(BF16)


def _memkv(mem, g, w_mkv):
    B, M, D = mem.shape
    dk = w_mkv.shape[1] // 2
    w = w_mkv.astype(BF16)
    return pl.pallas_call(
        _memkv_kernel,
        grid=(B,),
        in_specs=[pl.BlockSpec((1, M, D), lambda b: (b, 0, 0)),
                  pl.BlockSpec((1, D), lambda b: (0, 0)),
                  pl.BlockSpec(w.shape, lambda b: (0, 0))],
        out_specs=[pl.BlockSpec((1, M, dk), lambda b: (b, 0, 0))] * 2,
        out_shape=[jax.ShapeDtypeStruct((B, M, dk), BF16)] * 2,
        compiler_params=_cparams("parallel"),
        name="mem_kv",
    )(mem, g.reshape(1, D), w)


def _route(logits):
    lane = lax.broadcasted_iota(jnp.int32, logits.shape, 1)
    ninf = -jnp.inf
    big = ROUTE_LANES
    first = lambda mask: jnp.min(jnp.where(mask, lane, big), axis=-1, keepdims=True)
    is_g = (lane >= N_EXPERTS) & (lane < N_EXPERTS + N_GROUPS)
    gl = jnp.where(is_g, logits, ninf)
    gmax = jnp.max(gl, axis=-1, keepdims=True)
    g_idx = first(gl == gmax) - N_EXPERTS
    p_group = 1.0 / jnp.sum(jnp.exp(gl - gmax), axis=-1, keepdims=True)
    in_g = (lane < N_EXPERTS) & ((lane // EXPERTS_PER_GROUP) == g_idx)
    el = jnp.where(in_g, logits, ninf)
    v1 = jnp.max(el, axis=-1, keepdims=True)
    i1 = first(el == v1)
    el2 = jnp.where(lane == i1, ninf, el)
    v2 = jnp.max(el2, axis=-1, keepdims=True)
    i2 = first(el2 == v2)
    e2 = jnp.exp(v2 - v1)
    p1 = 1.0 / (1.0 + e2)
    p2 = e2 / (1.0 + e2)
    sel = lambda n, val: jnp.where(lane == n, val, 0.0)
    return (sel(ROUTE_ID0, i1.astype(F32)) + sel(ROUTE_ID0 + 1, i2.astype(F32))
            + sel(ROUTE_W0, p_group * p1) + sel(ROUTE_W0 + 1, p_group * p2))


def _postmix_kernel(x_ref, a_ref, hy_ref, ag_ref, hg_ref, woa_ref, woh_ref, cg_ref, wmq_ref,
                    mk_ref, mv_ref, wmo_ref, fg_ref, wr_ref, br_ref, x_out, hn_out, route_out,
                    route_t_out):
    ra = _rms(a_ref[...].astype(F32), ag_ref[...]).astype(BF16)
    rh = _rms(hy_ref[...].astype(F32), hg_ref[...]).astype(BF16)
    x = x_ref[...] + _dot(ra, woa_ref[...]) + _dot(rh, woh_ref[...])
    q = _dot(_rms(x, cg_ref[...]).astype(BF16), wmq_ref[...])
    dh = q.shape[1] // MEM_HEADS
    outs = []
    for h in range(MEM_HEADS):
        sl = slice(h * dh, (h + 1) * dh)
        s = lax.dot_general(q[:, sl].astype(BF16), mk_ref[0, :, sl], (((1,), (1,)), ((), ())),
                            preferred_element_type=F32) * dh ** -0.5
        p = jnp.exp(s - jnp.max(s, axis=-1, keepdims=True))
        l = jnp.sum(p, axis=-1, keepdims=True)
        outs.append(_dot(p.astype(BF16), mv_ref[0, :, sl]) / l)
    o = jnp.concatenate(outs, axis=1).astype(BF16)
    x = x + _dot(o, wmo_ref[...])
    x_out[...] = x
    hn = _rms(x, fg_ref[...])
    _store_row_tiles(hn_out, hn)
    route = _route(_dot(hn.astype(BF16), wr_ref[...]) + br_ref[...])
    route_out[...] = route
    route_t_out[0] = route.T[:SUBLANES, :]


def _postmix(x2d, a2d, hy2d, seq, ag, hg, w_out, cg, w_mq, mk, mv, w_mo, fg, w_rg, b_rg, w_re, b_re, tm=1024):
    T, D = x2d.shape
    ca = a2d.shape[1]
    woa = w_out[:ca].astype(BF16)
    woh = w_out[ca:].astype(BF16)
    pad = ROUTE_LANES - N_EXPERTS - N_GROUPS
    wr = jnp.concatenate([w_re, w_rg, jnp.zeros((D, pad), F32)], 1).astype(BF16)
    br = jnp.concatenate([b_re, b_rg, jnp.zeros((pad,), F32)]).reshape(1, ROUTE_LANES)
    nseq = seq // tm
    full = lambda a: pl.BlockSpec(a.shape, lambda i: (0,) * a.ndim)
    row = lambda n: pl.BlockSpec((tm, n), lambda i: (i, 0))
    memb = pl.BlockSpec((1,) + mk.shape[1:], lambda i: (i // nseq, 0, 0))
    args = [x2d, a2d, hy2d, ag.reshape(1, -1), hg.reshape(1, -1), woa, woh, cg.reshape(1, D),
            w_mq.astype(BF16), mk, mv, w_mo.astype(BF16), fg.reshape(1, D), wr, br]
    in_specs = [row(D), row(ca), row(hy2d.shape[1])] + [full(a) for a in args[3:9]] + [memb, memb] \
        + [full(a) for a in args[11:]]
    return pl.pallas_call(
        _postmix_kernel,
        grid=(T // tm,),
        in_specs=in_specs,
        out_specs=[row(D), pl.BlockSpec((tm, SUBLANES, LANES), lambda i: (i, 0, 0)), row(ROUTE_LANES),
                   pl.BlockSpec((1, SUBLANES, tm), lambda i: (i, 0, 0))],
        out_shape=[jax.ShapeDtypeStruct((T, D), F32), jax.ShapeDtypeStruct((T, SUBLANES, LANES), F32),
                   jax.ShapeDtypeStruct((T, ROUTE_LANES), F32),
                   jax.ShapeDtypeStruct((T // tm, SUBLANES, tm), F32)],
        compiler_params=_cparams("parallel"),
        name="postmix",
    )(*args)


def _slot_onehots(rt):
    e = lax.broadcasted_iota(jnp.int32, (N_EXPERTS, rt.shape[1]), 0).astype(F32)
    return [e == rt[ROUTE_ID0 + k:ROUTE_ID0 + k + 1, :] for k in range(2)]


def _slot_rows(rows, width):
    sub = lax.broadcasted_iota(jnp.int32, (SUBLANES, width), 0)
    return jnp.where(sub == 0, rows[0], jnp.where(sub == 1, rows[1], 0.0))


def _rank_kernel(rt_ref, rank_out, cnt_out, carry_ref):
    @pl.when(pl.program_id(0) == 0)
    def _():
        carry_ref[...] = jnp.zeros_like(carry_ref)

    tr = rt_ref.shape[2]
    oh = _slot_onehots(rt_ref[0])
    cnt = jnp.where(oh[0] | oh[1], 1.0, 0.0)
    s = lax.broadcasted_iota(jnp.int32, (tr, tr), 0)
    t = lax.broadcasted_iota(jnp.int32, (tr, tr), 1)
    before = jnp.where(s < t, 1.0, 0.0).astype(BF16)
    cum = _dot(cnt.astype(BF16), before) + carry_ref[...]
    ranks = [jnp.sum(jnp.where(m, cum, 0.0), axis=0, keepdims=True) for m in oh]
    rank_out[0] = _slot_rows(ranks, tr)
    carry_ref[...] += jnp.sum(cnt, axis=1, keepdims=True)
    cnt_out[...] = carry_ref[...]


def _pos_kernel(rt_ref, rank_ref, base_ref, pos_out):
    tr = rt_ref.shape[2]
    oh = _slot_onehots(rt_ref[0])
    rank = rank_ref[0]
    pos = [jnp.sum(jnp.where(m, base_ref[...], 0.0), axis=0, keepdims=True) + rank[k:k + 1, :]
           for k, m in enumerate(oh)]
    pos_out[0] = _slot_rows(pos, tr).astype(jnp.int32)


def _dispatch_kernel(base_ref, cnt_ref, cp_ref, pos0_ref, pos1_ref, hn_ref, xs_hbm, zero_ref, sem, *, tt):
    i = pl.program_id(0)
    row_copy = lambda src, dst: pltpu.make_async_copy(
        hn_ref.at[pl.ds(src, 1)], xs_hbm.at[pl.ds(dst, 1)], sem)

    @pl.when(i == 0)
    def _():
        zc = zero_ref.shape[0]
        zero_ref[...] = jnp.zeros_like(zero_ref)
        chunk = lambda c: pltpu.make_async_copy(zero_ref, xs_hbm.at[pl.ds(c * zc, zc)], sem)

        def clear(lo, hi):
            lax.fori_loop(lo, hi, lambda c, carry: (chunk(c).start(), carry)[1], 0)

        def drain(lo, hi):
            lax.fori_loop(lo, hi, lambda c, carry: (chunk(c).wait(), carry)[1], 0)

        last = N_EXPERTS - 1
        spans = [((base_ref[e] + cnt_ref[e]) // zc, (base_ref[e] + cp_ref[e]) // zc)
                 for e in range(N_EXPERTS)]
        spans.append(((base_ref[last] + cp_ref[last]) // zc, xs_hbm.shape[0] // zc))
        for lo, hi in spans:
            clear(lo, hi)
        for lo, hi in spans:
            drain(lo, hi)

    def start(t, carry):
        row_copy(t, pos0_ref[t]).start(priority=0)
        row_copy(t, pos1_ref[t]).start(priority=1)
        return carry

    lax.fori_loop(0, tt, start, 0, unroll=DMA_UNROLL)

    def wait(t, carry):
        row_copy(0, 0).wait()
        row_copy(0, 0).wait()
        return carry

    lax.fori_loop(0, tt, wait, 0, unroll=DMA_UNROLL)


def _ffn_kernel(te_ref, nu_ref, xs_ref, wg_ref, wu_ref, wd_ref, ys_ref):
    used = pl.program_id(0) < nu_ref[0]

    @pl.when(used)
    def _():
        x = _load_row_tiles(xs_ref).astype(BF16)
        a = _dot(x, wg_ref[0].astype(BF16))
        b = _dot(x, wu_ref[0].astype(BF16))
        m = (a * jax.nn.sigmoid(a)) * b
        _store_row_tiles(ys_ref, _dot(m.astype(BF16), wd_ref[0].astype(BF16)))

    @pl.when(jnp.logical_not(used))
    def _():
        ys_ref[...] = jnp.zeros_like(ys_ref)


def _combine_kernel(pos0_ref, pos1_ref, pos0_next_ref, pos1_next_ref, ys_hbm, x_ref, route_ref, fg_ref,
                    o_ref, buf_ref, sem, *, tc):
    i = pl.program_id(0)
    slot = i % 2
    row_copy = lambda s, k, t, p: pltpu.make_async_copy(
        ys_hbm.at[pl.ds(p, 1)], buf_ref.at[s, k, pl.ds(t, 1)], sem.at[s])

    def fetch(p0_ref, p1_ref, s):
        def start(t, carry):
            row_copy(s, 0, t, p0_ref[t]).start(priority=0)
            row_copy(s, 1, t, p1_ref[t]).start(priority=1)
            return carry

        lax.fori_loop(0, tc, start, 0, unroll=DMA_UNROLL)

    @pl.when(i == 0)
    def _():
        fetch(pos0_ref, pos1_ref, 0)

    @pl.when(i + 1 < pl.num_programs(0))
    def _():
        fetch(pos0_next_ref, pos1_next_ref, 1 - slot)

    def wait(t, carry):
        row_copy(slot, 0, 0, 0).wait()
        row_copy(slot, 1, 0, 0).wait()
        return carry

    lax.fori_loop(0, tc, wait, 0, unroll=DMA_UNROLL)
    route = route_ref[...]
    y = (x_ref[...] + route[:, ROUTE_W0:ROUTE_W0 + 1] * _load_row_tiles(buf_ref, (slot, 0))
         + route[:, ROUTE_W0 + 1:ROUTE_W0 + 2] * _load_row_tiles(buf_ref, (slot, 1)))
    o_ref[...] = _rms(y, fg_ref[...])


def _moe(hn, route, route_t, x2d, w_gate, w_up, w_down, fg, tt=512, tc=256):
    T, D = x2d.shape
    E = N_EXPERTS
    F = w_gate.shape[-1]
    tmm = MOE_ROW_TILE
    ntr, _, tr = route_t.shape
    row = lambda tm, n: pl.BlockSpec((tm, n), lambda i: (i, 0))
    rec = pl.BlockSpec((1, SUBLANES, tr), lambda i: (i, 0, 0))
    col = pl.BlockSpec((E, 1), lambda i: (0, 0))
    rank, counts = pl.pallas_call(
        _rank_kernel,
        grid=(ntr,),
        in_specs=[rec],
        out_specs=[rec, col],
        out_shape=[jax.ShapeDtypeStruct(route_t.shape, F32), jax.ShapeDtypeStruct((E, 1), F32)],
        scratch_shapes=[pltpu.VMEM((E, 1), F32)],
        compiler_params=_cparams("arbitrary"),
        name="moe_rank",
    )(route_t)

    cnt = counts[:, 0].astype(jnp.int32)
    cp = ((cnt + tmm - 1) // tmm) * tmm
    ends = jnp.cumsum(cp)
    base = ends - cp
    n_used = ends[-1] // tmm
    n_tiles = (2 * T) // tmm + E
    tile_start = jnp.minimum(jnp.arange(n_tiles, dtype=jnp.int32), n_used - 1) * tmm
    tile_expert = jnp.minimum(jnp.sum((tile_start[:, None] >= ends[None, :]).astype(jnp.int32), axis=1), E - 1)

    pos = pl.pallas_call(
        _pos_kernel,
        grid=(ntr,),
        in_specs=[rec, rec, col],
        out_specs=rec,
        out_shape=jax.ShapeDtypeStruct(route_t.shape, jnp.int32),
        compiler_params=_cparams("parallel"),
        name="moe_pos",
    )(route_t, rank, base.astype(F32).reshape(E, 1))
    pos0, pos1 = pos[:, 0, :].reshape(T), pos[:, 1, :].reshape(T)

    xs = pl.pallas_call(
        functools.partial(_dispatch_kernel, tt=tt),
        grid_spec=pltpu.PrefetchScalarGridSpec(
            num_scalar_prefetch=3,
            grid=(T // tt,),
            in_specs=[pl.BlockSpec((tt,), lambda i, b, n, c: (i,), memory_space=pltpu.SMEM),
                      pl.BlockSpec((tt,), lambda i, b, n, c: (i,), memory_space=pltpu.SMEM),
                      pl.BlockSpec((tt, SUBLANES, LANES), lambda i, b, n, c: (i, 0, 0))],
            out_specs=pl.BlockSpec(memory_space=pl.ANY),
            scratch_shapes=[pltpu.VMEM((MOE_CLEAR_ROWS, SUBLANES, LANES), F32),
                            pltpu.SemaphoreType.DMA(())]),
        out_shape=jax.ShapeDtypeStruct((n_tiles * tmm, SUBLANES, LANES), F32),
        compiler_params=_cparams("arbitrary"),
        name="moe_dispatch",
    )(base, cnt, cp, pos0, pos1, hn)

    tile = lambda r, te, nu: (jnp.minimum(r, nu[0] - 1), 0, 0)
    ys = pl.pallas_call(
        _ffn_kernel,
        grid_spec=pltpu.PrefetchScalarGridSpec(
            num_scalar_prefetch=2,
            grid=(n_tiles,),
            in_specs=[pl.BlockSpec((tmm, SUBLANES, LANES), tile),
                      pl.BlockSpec((1, D, F), lambda r, te, nu: (te[r], 0, 0)),
                      pl.BlockSpec((1, D, F), lambda r, te, nu: (te[r], 0, 0)),
                      pl.BlockSpec((1, F, D), lambda r, te, nu: (te[r], 0, 0))],
            out_specs=pl.BlockSpec((tmm, SUBLANES, LANES), lambda r, te, nu: (r, 0, 0))),
        out_shape=jax.ShapeDtypeStruct((n_tiles * tmm, SUBLANES, LANES), F32),
        compiler_params=_cparams("arbitrary"),
        name="moe_ffn",
    )(tile_expert, n_used.reshape(1), xs, w_gate.reshape(E, D, F), w_up.reshape(E, D, F),
      w_down.reshape(E, F, D))

    cur = pl.BlockSpec((tc,), lambda i: (i,), memory_space=pltpu.SMEM)
    nxt = pl.BlockSpec((tc,), lambda i: (jnp.minimum(i + 1, T // tc - 1),), memory_space=pltpu.SMEM)
    return pl.pallas_call(
        functools.partial(_combine_kernel, tc=tc),
        grid=(T // tc,),
        in_specs=[cur, cur, nxt, nxt,
                  pl.BlockSpec(memory_space=pl.ANY),
                  row(tc, D), row(tc, ROUTE_LANES), pl.BlockSpec((1, D), lambda i: (0, 0))],
        out_specs=row(tc, D),
        out_shape=jax.ShapeDtypeStruct((T, D), F32),
        scratch_shapes=[pltpu.VMEM((2, 2, tc, SUBLANES, LANES), F32), pltpu.SemaphoreType.DMA((2,))],
        compiler_params=_cparams("arbitrary"),
        name="moe_combine",
    )(pos0, pos1, pos0, pos1, ys, x2d, route, fg.reshape(1, D))


def kernel(x, mem, mix_norm_g, w_in, q_norm_g, kv_norm_g, w_uq, w_ukv, hy_conv_w, hy_conv_b, hy_w1, hy_b1, hy_freq, hy_w2, hy_b2, hy_w3, hy_b3, hy_decay, hy_skip, attn_out_g, hy_out_g, w_out, cross_norm_g, mem_norm_g, w_mq, w_mkv, w_mo, ffn_norm_g, w_route_group, b_route_group, w_route_expert, b_route_expert, w_gate, w_up, w_down, final_norm_g):
    B, S, D = x.shape
    depth = w_in.shape[0]
    consts = _dft_constants(S)
    xf = x.reshape(B * S, D)
    for l in range(depth):
        q, k, v, hx1, hx2, hv = _inproj(xf, S, mix_norm_g[l], w_in[l], q_norm_g[l], kv_norm_g[l],
                                        w_uq[l], w_ukv[l], hy_conv_w[l], hy_conv_b[l])
        HP = q.shape[1]
        a_out = _attention(q.reshape(B, S, HP), k.reshape(B, S, HP), v.reshape(B, S, HP))
        filt = _hyena_filter_time(S, hy_w1[l], hy_b1[l], hy_freq[l], hy_w2[l], hy_b2[l], hy_w3[l],
                                  hy_b3[l], hy_decay[l])
        kf = _hyena_filter_spectrum(filt, consts[1], consts[3])
        C = hv.shape[1]
        h_out = _hyena(hx1.reshape(B, S, C), hx2.reshape(B, S, C), hv.reshape(B, S, C),
                       hy_skip[l], kf, consts)
        mk, mv = _memkv(mem, mem_norm_g[l], w_mkv[l])
        x2, hn, route, route_t = _postmix(xf, a_out.reshape(B * S, -1), h_out.reshape(B * S, C), S,
                                attn_out_g[l], hy_out_g[l], w_out[l], cross_norm_g[l], w_mq[l], mk, mv,
                                w_mo[l], ffn_norm_g[l], w_route_group[l], b_route_group[l],
                                w_route_expert[l], b_route_expert[l])
        assert depth == 1
        xf = _moe(hn, route, route_t, x2, w_gate[l], w_up[l], w_down[l], final_norm_g)
    return xf.reshape(B, S, D)
```

```python
import functools
import math

import numpy as np
import jax
import jax.numpy as jnp
from jax import lax
from jax.experimental import pallas as pl
from jax.experimental.pallas import tpu as pltpu

F32 = jnp.float32
BF16 = jnp.bfloat16

EPS = 1e-6
MLA_HEADS = 8
MLA_NOPE = 64
MLA_ROPE = 32
MLA_V = 64
ROPE_BASE = 10000.0
HEAD_PAD = 128
HY_ORDER = 2
HY_DIRS = 2
HY_BANDS = 16
MEM_HEADS = 4
N_GROUPS = 4
EXPERTS_PER_GROUP = 8
N_EXPERTS = N_GROUPS * EXPERTS_PER_GROUP
ROUTE_LANES = 128
ROUTE_ID0 = 0
ROUTE_W0 = 2
MOE_ROW_TILE = 512
MOE_CLEAR_ROWS = 64
DMA_UNROLL = 8

FFT_N1 = 64
FFT_N2 = 128
DFT_K1_BLOCK = 8
DFT_N2_BLOCK = 64
DFT_PITCH_PAD = 8

VMEM_LIMIT = 56 * 1024 * 1024


def _cparams(*sem):
    return pltpu.CompilerParams(dimension_semantics=sem, vmem_limit_bytes=VMEM_LIMIT)


def _rms(x, g):
    return x * lax.rsqrt(jnp.mean(x * x, axis=-1, keepdims=True) + EPS) * g


def _dot(a, b):
    return jnp.dot(a, b, preferred_element_type=F32)


SUBLANES = 8
LANES = 128


def _load_row_tiles(ref, lead=()):
    rows = ref.shape[-3]
    flat = ref.reshape(*ref.shape[:-3], rows * SUBLANES, LANES)
    return jnp.concatenate(
        [flat[(*lead, pl.ds(j, rows, stride=SUBLANES), slice(None))] for j in range(SUBLANES)], axis=1)


def _store_row_tiles(ref, val):
    rows = ref.shape[0]
    flat = ref.reshape(rows * SUBLANES, LANES)
    for j in range(SUBLANES):
        flat[pl.ds(j, rows, stride=SUBLANES), :] = val[:, j * LANES:(j + 1) * LANES]


def _inproj_kernel(x_ref, xp_ref, xn_ref, g_ref, wq_ref, wkv_ref, wkra_ref, wkrb_ref, why_ref, qg_ref,
                   kvg_ref, wqa_ref, wqb_ref, wka_ref, wv_ref, tab_ref, cw_ref, cb_ref,
                   q_out, k_out, v_out, x1_out, x2_out, hv_out, *, nseq):
    tm = x_ref.shape[0]
    halo = xp_ref.shape[0]
    hf = _rms(jnp.concatenate([xp_ref[...], x_ref[...], xn_ref[...]], axis=0), g_ref[...])
    h = hf[halo:halo + tm].astype(BF16)
    qn = _rms(_dot(h, wq_ref[...]), qg_ref[...]).astype(BF16)
    kvn = _rms(_dot(h, wkv_ref[...]), kvg_ref[...]).astype(BF16)
    tab = tab_ref[...]
    cq, sq, ck, sk = (tab[:, j * HEAD_PAD:(j + 1) * HEAD_PAD] for j in range(4))
    tile = lambda t: jnp.concatenate([t] * MLA_HEADS, axis=1)
    q = _dot(qn, wqa_ref[...]) * tile(cq) + _dot(qn, wqb_ref[...]) * tile(sq)
    q_out[...] = q.astype(BF16)
    kr = _dot(h, wkra_ref[...]) * ck + _dot(h, wkrb_ref[...]) * sk
    k_out[...] = (_dot(kvn, wka_ref[...]) + tile(kr)).astype(BF16)
    lane = lax.broadcasted_iota(jnp.int32, (1, v_out.shape[1]), 1) % HEAD_PAD
    v_out[...] = (_dot(kvn, wv_ref[...]) + jnp.where(lane == MLA_V, 1.0, 0.0)).astype(BF16)
    hy = _dot(hf.astype(BF16), why_ref[...])
    i = pl.program_id(0) % nseq
    row = lax.broadcasted_iota(jnp.int32, hy.shape, 0)
    outside = ((row == halo - 1) & (i == 0)) | ((row == halo + tm) & (i == nseq - 1))
    hy = jnp.where(outside, 0.0, hy)
    cw = cw_ref[...]
    u = (hy[halo - 1:halo - 1 + tm] * cw[0:1] + hy[halo:halo + tm] * cw[1:2]
         + hy[halo + 1:halo + 1 + tm] * cw[2:3] + cb_ref[...])
    c = x1_out.shape[1]
    x1_out[...] = u[:, :c].astype(x1_out.dtype)
    x2_out[...] = u[:, c:2 * c].astype(x2_out.dtype)
    hv_out[...] = u[:, 2 * c:].astype(hv_out.dtype)


def _inproj(x2d, seq, mix_g, w_in, q_g, kv_g, w_uq, w_ukv, conv_w, conv_b, tm=1024):
    T, D = x2d.shape
    per = tm // SUBLANES
    cb = conv_b.reshape(1, -1)
    q_rank, kv_rank = q_g.shape[0], kv_g.shape[0]
    off_kv = q_rank
    off_kr = off_kv + kv_rank
    off_hy = off_kr + MLA_ROPE
    C = (w_in.shape[1] - off_hy) // 3
    H = MLA_HEADS
    half = MLA_ROPE // 2
    wq = w_in[:, :off_kv].astype(BF16)
    wkv = w_in[:, off_kv:off_kr].astype(BF16)
    wkr = w_in[:, off_kr:off_hy]
    wkr_sw = jnp.concatenate([wkr[:, half:], wkr[:, :half]], axis=1)
    zpad = lambda n: jnp.zeros((D, n), F32)
    wkra = jnp.concatenate([zpad(MLA_NOPE), wkr, zpad(HEAD_PAD - MLA_NOPE - MLA_ROPE)], 1).astype(BF16)
    wkrb = jnp.concatenate([zpad(MLA_NOPE), wkr_sw, zpad(HEAD_PAD - MLA_NOPE - MLA_ROPE)], 1).astype(BF16)
    why = w_in[:, off_hy:].astype(BF16)

    uq = w_uq.reshape(q_rank, H, MLA_NOPE + MLA_ROPE)
    uq_n, uq_r = uq[..., :MLA_NOPE], uq[..., MLA_NOPE:]
    uq_rs = jnp.concatenate([uq_r[..., half:], uq_r[..., :half]], axis=-1)
    zq = lambda n: jnp.zeros((q_rank, H, n), F32)
    wqa = jnp.concatenate([uq_n, uq_r, zq(HEAD_PAD - MLA_NOPE - MLA_ROPE)], -1).reshape(q_rank, H * HEAD_PAD).astype(BF16)
    wqb = jnp.concatenate([zq(MLA_NOPE), uq_rs, zq(HEAD_PAD - MLA_NOPE - MLA_ROPE)], -1).reshape(q_rank, H * HEAD_PAD).astype(BF16)
    ukv = w_ukv.reshape(kv_rank, H, MLA_NOPE + MLA_V)
    zk = lambda n: jnp.zeros((kv_rank, H, n), F32)
    wka = jnp.concatenate([ukv[..., :MLA_NOPE], zk(HEAD_PAD - MLA_NOPE)], -1).reshape(kv_rank, H * HEAD_PAD).astype(BF16)
    wv = jnp.concatenate([ukv[..., MLA_NOPE:], zk(HEAD_PAD - MLA_V)], -1).reshape(kv_rank, H * HEAD_PAD).astype(BF16)

    ang = np.arange(seq)[:, None] * ROPE_BASE ** (-np.arange(half) / half)[None, :]
    cos2 = np.concatenate([np.cos(ang), np.cos(ang)], 1)
    sin2 = np.concatenate([-np.sin(ang), np.sin(ang)], 1)
    zs = lambda n: np.zeros((seq, n))
    scale = (MLA_NOPE + MLA_ROPE) ** -0.5 * math.log2(math.e)
    rest = HEAD_PAD - MLA_NOPE - MLA_ROPE
    cq = scale * np.concatenate([np.ones((seq, MLA_NOPE)), cos2, zs(rest)], 1)
    sq = scale * np.concatenate([zs(MLA_NOPE), sin2, zs(rest)], 1)
    ck = np.concatenate([zs(MLA_NOPE), cos2, zs(rest)], 1)
    sk = np.concatenate([zs(MLA_NOPE), sin2, zs(rest)], 1)
    tab = jnp.asarray(np.concatenate([cq, sq, ck, sk], 1), dtype=F32)

    nseq = seq // tm
    full = lambda a: pl.BlockSpec(a.shape, lambda i: (0,) * a.ndim)
    row = lambda n: pl.BlockSpec((tm, n), lambda i: (i, 0))
    consts = [mix_g.reshape(1, D), wq, wkv, wkra, wkrb, why, q_g.reshape(1, -1), kv_g.reshape(1, -1),
              wqa, wqb, wka, wv]
    HP = H * HEAD_PAD
    return pl.pallas_call(
        functools.partial(_inproj_kernel, nseq=nseq),
        grid=(T // tm,),
        in_specs=[row(D),
                  pl.BlockSpec((SUBLANES, D), lambda i: (jnp.maximum(i * per - 1, 0), 0)),
                  pl.BlockSpec((SUBLANES, D), lambda i: (jnp.minimum((i + 1) * per, T // SUBLANES - 1), 0))]
        + [full(a) for a in consts]
        + [pl.BlockSpec((tm, 4 * HEAD_PAD), lambda i: (i % nseq, 0)), full(conv_w), full(cb)],
        out_specs=[row(HP), row(HP), row(HP), row(C), row(C), row(C)],
        out_shape=[jax.ShapeDtypeStruct((T, HP), BF16)] * 3 + [jax.ShapeDtypeStruct((T, C), BF16)] * 3,
        compiler_params=_cparams("parallel"),
        name="inproj",
    )(x2d, x2d, x2d, *consts, tab, conv_w, cb)


def _attn_kernel(q_ref, k_ref, v_ref, o_ref):
    outs = []
    for h in range(MLA_HEADS):
        sl = slice(h * HEAD_PAD, (h + 1) * HEAD_PAD)
        s = lax.dot_general(q_ref[0, :, sl], k_ref[0, :, sl], (((1,), (1,)), ((), ())),
                            preferred_element_type=F32).astype(BF16)
        p = jnp.exp2(s - jnp.max(s, axis=-1, keepdims=True))
        o = _dot(p, v_ref[0, :, sl])
        outs.append(o[:, :MLA_V] / o[:, MLA_V:MLA_V + 1])
    o_ref[0] = jnp.concatenate(outs, axis=1).astype(o_ref.dtype)


def _attention(q, k, v, tq=512):
    B, S, HP = q.shape
    return pl.pallas_call(
        _attn_kernel,
        grid=(B, S // tq),
        in_specs=[pl.BlockSpec((1, tq, HP), lambda b, i: (b, i, 0)),
                  pl.BlockSpec((1, S, HP), lambda b, i: (b, 0, 0)),
                  pl.BlockSpec((1, S, HP), lambda b, i: (b, 0, 0))],
        out_specs=pl.BlockSpec((1, tq, MLA_HEADS * MLA_V), lambda b, i: (b, i, 0)),
        out_shape=jax.ShapeDtypeStruct((B, S, MLA_HEADS * MLA_V), BF16),
        compiler_params=_cparams("parallel", "arbitrary"),
        name="mla_attention",
    )(q, k, v)


def _dft_constants(seq):
    n = 2 * seq
    n1, n2 = FFT_N1, FFT_N2
    assert n1 * n2 == n
    r1 = np.arange(n1)
    r2 = np.arange(n2)
    blk = lambda z: np.block([[z.real, -z.imag], [z.imag, z.real]])
    w1 = np.exp(-2j * np.pi * np.outer(r1, r1) / n1)
    fa_data = blk(w1[:, :n1 // 2])
    fa_filt = np.concatenate([w1.real, w1.imag], axis=0)
    fc = blk(np.conj(w1).T[:n1 // 2, :])
    w2 = np.exp(-2j * np.pi * np.outer(r2, r2) / n2)
    tw = np.exp(-2j * np.pi * np.outer(r1, r2) / n)
    fb = np.stack([blk(w2 * tw[k][None, :]) for k in range(n1)])
    fbi = np.stack([blk(np.conj(w2).T * np.conj(tw[k])[:, None] / n) for k in range(n1)])
    as_bf = lambda a: jnp.asarray(a, dtype=F32).astype(BF16)
    return as_bf(fa_data), as_bf(fa_filt), as_bf(fc), as_bf(fb), as_bf(fbi)


def _filter_kernel(z_ref, w1_ref, b1_ref, fr_ref, w2_ref, b2_ref, w3_ref, b3_ref, dec_ref, o_ref, *, seq, tr):
    hp = lax.Precision.HIGHEST
    z = z_ref[...]
    fr = fr_ref[...]
    def sin_rows(a):
        half, w = a.shape[0] // 2, a.shape[1]
        s = jnp.sin(jnp.concatenate([a[:half], a[half:]], axis=1))
        return jnp.concatenate([s[:, :w], s[:, w:]], axis=0)

    h = sin_rows(fr[0:1] * (jnp.dot(z, w1_ref[...], precision=hp, preferred_element_type=F32) + b1_ref[...]))
    h = sin_rows(fr[1:2] * (jnp.dot(h, w2_ref[...], precision=hp, preferred_element_type=F32) + b2_ref[...]))
    split = lambda a: (a.astype(BF16), (a - a.astype(BF16).astype(F32)).astype(BF16))
    (hh, hl), (wh, wl) = split(h), split(w3_ref[0])
    h = _dot(hh, wh) + _dot(hh, wl) + _dot(hl, wh) + b3_ref[0]
    h = h * jnp.exp(-z[:, 0:1] * jnp.abs(dec_ref[0]))
    n = pl.program_id(0) * tr + lax.broadcasted_iota(jnp.int32, h.shape, 0)
    o_ref[...] = jnp.where(n == seq, 0.0, h).astype(o_ref.dtype)


def _hyena_filter_time(seq, w1, b1, freq, w2, b2, w3, b3, decay, tr=512):
    n = 2 * seq
    emb, ffn = w1.shape
    C = w3.shape[1] // (HY_ORDER * HY_DIRS)
    off = np.arange(n)
    t = np.where(off < seq, off, n - off).astype(np.float64)
    bands = np.linspace(1e-4, HY_BANDS - 1, HY_BANDS)
    ang = 2.0 * math.pi * t[:, None] * bands[None, :] / seq
    z = np.concatenate([(t / seq)[:, None], np.cos(ang), -np.sin(ang)], axis=-1)
    zl = LANES
    z = jnp.asarray(np.pad(z, ((0, 0), (0, zl - emb))), dtype=F32)
    w1p = jnp.pad(w1, ((0, zl - emb), (0, 0)))
    by_dir = lambda a: jnp.moveaxis(a.reshape(a.shape[0], HY_ORDER, HY_DIRS, C), 2, 0).reshape(
        HY_DIRS, a.shape[0], HY_ORDER * C)
    w3d, b3d, decd = by_dir(w3), by_dir(b3.reshape(1, -1)), by_dir(decay.reshape(1, -1))
    full = lambda a: pl.BlockSpec(a.shape, lambda i: (0,) * a.ndim)
    ndir = lambda a: pl.BlockSpec((1,) + a.shape[1:], lambda i: ((i * tr) // seq, 0, 0))
    consts = [w1p, b1.reshape(1, -1), freq, w2, b2.reshape(1, -1)]
    return pl.pallas_call(
        functools.partial(_filter_kernel, seq=seq, tr=tr),
        grid=(n // tr,),
        in_specs=[pl.BlockSpec((tr, zl), lambda i: (i, 0))] + [full(a) for a in consts]
        + [ndir(w3d), ndir(b3d), ndir(decd)],
        out_specs=pl.BlockSpec((tr, HY_ORDER * C), lambda i: (i, 0)),
        out_shape=jax.ShapeDtypeStruct((n, HY_ORDER * C), BF16),
        compiler_params=_cparams("parallel"),
        name="hyena_filter_mlp",
    )(z, *consts, w3d, b3d, decd)


def _pitched(rows, nb):
    return pltpu.VMEM((rows, nb + DFT_PITCH_PAD, LANES), F32)


def _block_rows(ref):
    return math.prod(ref.shape[:-2]), ref.shape[-2]


def _copy_in(ref, scr):
    rows, nb = _block_rows(ref)
    scr[:, :nb, :] = ref[...].reshape(rows, nb, LANES).astype(scr.dtype)


def _copy_out(scr, ref):
    rows, nb = _block_rows(ref)
    ref[...] = scr[:, :nb, :].reshape(ref.shape).astype(ref.dtype)


def _at_n2(scr, n):
    rows, pitch, _ = scr.shape
    return scr.reshape(rows * pitch, LANES).at[pl.ds(n, rows, stride=pitch), :]


def _outer_dft(mat_ref, x_ref, o_ref, xs, os):
    _copy_in(x_ref, xs)
    for n in range(x_ref.shape[-2]):
        _at_n2(os, n)[...] = _dot(mat_ref[...], _at_n2(xs, n)[...].astype(BF16))
    _copy_out(os, o_ref)


def _filter_stage_a_kernel(x_ref, fa_ref, o_ref, xs, os):
    _outer_dft(fa_ref, x_ref, o_ref, xs, os)


def _filter_stage_b_kernel(x_ref, fb_ref, o_ref):
    for kk in range(x_ref.shape[1]):
        x = jnp.concatenate([x_ref[0, kk], x_ref[1, kk]], axis=0).astype(BF16)
        o_ref[kk] = _dot(fb_ref[kk], x).astype(o_ref.dtype)


def _hyena_filter_spectrum(filt, fa_filt, fb, nb=DFT_N2_BLOCK):
    n, oc = filt.shape
    a = pl.pallas_call(
        _filter_stage_a_kernel,
        grid=(FFT_N2 // nb, oc // LANES),
        in_specs=[pl.BlockSpec((FFT_N1, nb, LANES), lambda j, c: (0, j, c)),
                  pl.BlockSpec(fa_filt.shape, lambda j, c: (0, 0))],
        out_specs=pl.BlockSpec((2, FFT_N1, nb, LANES), lambda j, c: (0, 0, j, c)),
        out_shape=jax.ShapeDtypeStruct((2, FFT_N1, FFT_N2, oc), BF16),
        scratch_shapes=[_pitched(FFT_N1, nb), _pitched(2 * FFT_N1, nb)],
        compiler_params=_cparams("parallel", "parallel"),
        name="hyena_filter_dft_a",
    )(filt.reshape(FFT_N1, FFT_N2, oc), fa_filt)
    return pl.pallas_call(
        _filter_stage_b_kernel,
        grid=(FFT_N1 // DFT_K1_BLOCK,),
        in_specs=[pl.BlockSpec((2, DFT_K1_BLOCK, FFT_N2, oc), lambda k: (0, k, 0, 0)),
                  pl.BlockSpec((DFT_K1_BLOCK, 2 * FFT_N2, 2 * FFT_N2), lambda k: (k, 0, 0))],
        out_specs=pl.BlockSpec((DFT_K1_BLOCK, 2 * FFT_N2, oc), lambda k: (k, 0, 0)),
        out_shape=jax.ShapeDtypeStruct((FFT_N1, 2 * FFT_N2, oc), BF16),
        compiler_params=_cparams("parallel"),
        name="hyena_filter_dft_b",
    )(a, fb)


def _stage_a_kernel(x_ref, fa_ref, o_ref, xs, os):
    _outer_dft(fa_ref, x_ref, o_ref, xs, os)


def _stage_a(x4, fa, nb):
    B, r, n2, C = x4.shape
    return pl.pallas_call(
        _stage_a_kernel,
        grid=(B // 2, n2 // nb, C // LANES),
        in_specs=[pl.BlockSpec((2, r, nb, LANES), lambda p, j, c: (p, 0, j, c)),
                  pl.BlockSpec(fa.shape, lambda p, j, c: (0, 0))],
        out_specs=pl.BlockSpec((1, 2, FFT_N1, nb, LANES), lambda p, j, c: (p, 0, 0, j, c)),
        out_shape=jax.ShapeDtypeStruct((B // 2, 2, FFT_N1, n2, C), BF16),
        scratch_shapes=[_pitched(2 * r, nb), _pitched(2 * FFT_N1, nb)],
        compiler_params=_cparams("parallel", "parallel", "parallel"),
        name="hyena_dft_a",
    )(x4, fa)


def _stage_b_kernel(x_ref, fb_ref, kf_ref, fbi_ref, o_ref):
    npair = x_ref.shape[0]
    n2 = x_ref.shape[3]
    c = x_ref.shape[4]
    for kk in range(x_ref.shape[2]):
        x = jnp.concatenate(
            [jnp.concatenate([x_ref[p, 0, kk], x_ref[p, 1, kk]], axis=0) for p in range(npair)],
            axis=1).astype(BF16)
        g = _dot(fb_ref[kk], x)
        gr, gi = g[:n2], g[n2:]
        kf = kf_ref[kk].astype(F32)
        kr = jnp.concatenate([kf[:n2]] * npair, axis=1)
        ki = jnp.concatenate([kf[n2:]] * npair, axis=1)
        hcat = jnp.concatenate([gr * kr - gi * ki, gr * ki + gi * kr], axis=0).astype(BF16)
        y = _dot(fbi_ref[kk], hcat)
        for p in range(npair):
            o_ref[p, 0, kk] = y[:n2, p * c:(p + 1) * c].astype(o_ref.dtype)
            o_ref[p, 1, kk] = y[n2:, p * c:(p + 1) * c].astype(o_ref.dtype)


def _stage_b(spec, fb, kf, fbi, order, nk=DFT_K1_BLOCK):
    npair, _, _, _, C = spec.shape
    blk = pl.BlockSpec((npair, 2, nk, FFT_N2, C), lambda k: (0, 0, k, 0, 0))
    mat = pl.BlockSpec((nk, 2 * FFT_N2, 2 * FFT_N2), lambda k: (k, 0, 0))
    return pl.pallas_call(
        _stage_b_kernel,
        grid=(FFT_N1 // nk,),
        in_specs=[blk, mat, pl.BlockSpec((nk, 2 * FFT_N2, C), lambda k: (k, 0, order)), mat],
        out_specs=blk,
        out_shape=jax.ShapeDtypeStruct(spec.shape, BF16),
        compiler_params=_cparams("parallel"),
        name="hyena_dft_b",
    )(spec, fb, kf, fbi)


def _stage_c_kernel(y_ref, fc_ref, gate_ref, z_ref, skip_ref, *rest, stage_a):
    if stage_a:
        fa_ref, z_out, a_out, ys, gs, zs, os, as_ = rest
    else:
        z_out, ys, gs, zs, os = rest
    nb = gate_ref.shape[-2]
    _copy_in(y_ref, ys)
    _copy_in(gate_ref, gs)
    _copy_in(z_ref, zs)
    skip = skip_ref[...]
    for n in range(nb):
        conv = _dot(fc_ref[...], _at_n2(ys, n)[...].astype(BF16))
        _at_n2(os, n)[...] = _at_n2(gs, n)[...] * (conv + skip * _at_n2(zs, n)[...])
    _copy_out(os, z_out)
    if stage_a:
        for n in range(nb):
            _at_n2(as_, n)[...] = _dot(fa_ref[...], _at_n2(os, n)[...].astype(BF16))
        _copy_out(as_, a_out)


def _stage_c(yspec, fc, gate, zin, skip, fa=None, nb=DFT_N2_BLOCK):
    B, r, n2, C = gate.shape
    dat = pl.BlockSpec((2, r, nb, LANES), lambda p, j, c: (p, 0, j, c))
    spc = pl.BlockSpec((1, 2, FFT_N1, nb, LANES), lambda p, j, c: (p, 0, 0, j, c))
    in_specs = [spc, pl.BlockSpec(fc.shape, lambda p, j, c: (0, 0)), dat, dat,
                pl.BlockSpec((1, LANES), lambda p, j, c: (0, c))]
    out_specs = [dat]
    out_shape = [jax.ShapeDtypeStruct(gate.shape, BF16)]
    args = [yspec, fc, gate, zin, skip.reshape(1, C)]
    scratch = [_pitched(2 * FFT_N1, nb)] + [_pitched(2 * r, nb)] * 3
    if fa is not None:
        in_specs.append(pl.BlockSpec(fa.shape, lambda p, j, c: (0, 0)))
        out_specs.append(spc)
        out_shape.append(jax.ShapeDtypeStruct(yspec.shape, BF16))
        args.append(fa)
        scratch.append(_pitched(2 * FFT_N1, nb))
    return pl.pallas_call(
        functools.partial(_stage_c_kernel, stage_a=fa is not None),
        grid=(B // 2, n2 // nb, C // LANES),
        in_specs=in_specs, out_specs=out_specs, out_shape=out_shape,
        scratch_shapes=scratch,
        compiler_params=_cparams("parallel", "parallel", "parallel"),
        name="hyena_dft_c",
    )(*args)


def _hyena(x1, x2, v, skip, kf, consts, nb=DFT_N2_BLOCK):
    fa_data, _, fc, fb, fbi = consts
    B, S, C = v.shape
    split = lambda a: a.reshape(B, S // FFT_N2, FFT_N2, C)
    a0 = _stage_a(split(v), fa_data, nb)
    y0 = _stage_b(a0, fb, kf, fbi, 0)
    z1, a1 = _stage_c(y0, fc, split(x1), split(v), skip[0], fa=fa_data, nb=nb)
    y1 = _stage_b(a1, fb, kf, fbi, 1)
    (out,) = _stage_c(y1, fc, split(x2), z1, skip[1], nb=nb)
    return out.reshape(B, S, C)


def _memkv_kernel(m_ref, g_ref, w_ref, k_out, v_out):
    hm = _rms(m_ref[0], g_ref[...]).astype(BF16)
    kv = _dot(hm, w_ref[...])
    d = k_out.shape[2]
    k_out[0] = kv[:, :d].astype(BF16)
    v_out[0] = kv[:, d:].astype(BF16)


def _memkv(mem, g, w_mkv):
    B, M, D = mem.shape
    dk = w_mkv.shape[1] // 2
    w = w_mkv.astype(BF16)
    return pl.pallas_call(
        _memkv_kernel,
        grid=(B,),
        in_specs=[pl.BlockSpec((1, M, D), lambda b: (b, 0, 0)),
                  pl.BlockSpec((1, D), lambda b: (0, 0)),
                  pl.BlockSpec(w.shape, lambda b: (0, 0))],
        out_specs=[pl.BlockSpec((1, M, dk), lambda b: (b, 0, 0))] * 2,
        out_shape=[jax.ShapeDtypeStruct((B, M, dk), BF16)] * 2,
        compiler_params=_cparams("parallel"),
        name="mem_kv",
    )(mem, g.reshape(1, D), w)


def _route(logits):
    lane = lax.broadcasted_iota(jnp.int32, logits.shape, 1)
    ninf = -jnp.inf
    big = ROUTE_LANES
    first = lambda mask: jnp.min(jnp.where(mask, lane, big), axis=-1, keepdims=True)
    is_g = (lane >= N_EXPERTS) & (lane < N_EXPERTS + N_GROUPS)
    gl = jnp.where(is_g, logits, ninf)
    gmax = jnp.max(gl, axis=-1, keepdims=True)
    g_idx = first(gl == gmax) - N_EXPERTS
    p_group = 1.0 / jnp.sum(jnp.exp(gl - gmax), axis=-1, keepdims=True)
    in_g = (lane < N_EXPERTS) & ((lane // EXPERTS_PER_GROUP) == g_idx)
    el = jnp.where(in_g, logits, ninf)
    v1 = jnp.max(el, axis=-1, keepdims=True)
    i1 = first(el == v1)
    el2 = jnp.where(lane == i1, ninf, el)
    v2 = jnp.max(el2, axis=-1, keepdims=True)
    i2 = first(el2 == v2)
    e2 = jnp.exp(v2 - v1)
    p1 = 1.0 / (1.0 + e2)
    p2 = e2 / (1.0 + e2)
    sel = lambda n, val: jnp.where(lane == n, val, 0.0)
    return (sel(ROUTE_ID0, i1.astype(F32)) + sel(ROUTE_ID0 + 1, i2.astype(F32))
            + sel(ROUTE_W0, p_group * p1) + sel(ROUTE_W0 + 1, p_group * p2))


def _postmix_kernel(x_ref, a_ref, hy_ref, ag_ref, hg_ref, woa_ref, woh_ref, cg_ref, wmq_ref,
                    mk_ref, mv_ref, wmo_ref, fg_ref, wr_ref, br_ref, x_out, hn_out, route_out,
                    route_t_out):
    ra = _rms(a_ref[...].astype(F32), ag_ref[...]).astype(BF16)
    rh = _rms(hy_ref[...].astype(F32), hg_ref[...]).astype(BF16)
    x = x_ref[...] + _dot(ra, woa_ref[...]) + _dot(rh, woh_ref[...])
    q = _dot(_rms(x, cg_ref[...]).astype(BF16), wmq_ref[...])
    dh = q.shape[1] // MEM_HEADS
    outs = []
    for h in range(MEM_HEADS):
        sl = slice(h * dh, (h + 1) * dh)
        s = lax.dot_general(q[:, sl].astype(BF16), mk_ref[0, :, sl], (((1,), (1,)), ((), ())),
                            preferred_element_type=F32) * dh ** -0.5
        p = jnp.exp(s - jnp.max(s, axis=-1, keepdims=True))
        l = jnp.sum(p, axis=-1, keepdims=True)
        outs.append(_dot(p.astype(BF16), mv_ref[0, :, sl]) / l)
    o = jnp.concatenate(outs, axis=1).astype(BF16)
    x = x + _dot(o, wmo_ref[...])
    x_out[...] = x
    hn = _rms(x, fg_ref[...])
    _store_row_tiles(hn_out, hn)
    route = _route(_dot(hn.astype(BF16), wr_ref[...]) + br_ref[...])
    route_out[...] = route
    route_t_out[0] = route.T[:SUBLANES, :]


def _postmix(x2d, a2d, hy2d, seq, ag, hg, w_out, cg, w_mq, mk, mv, w_mo, fg, w_rg, b_rg, w_re, b_re, tm=1024):
    T, D = x2d.shape
    ca = a2d.shape[1]
    woa = w_out[:ca].astype(BF16)
    woh = w_out[ca:].astype(BF16)
    pad = ROUTE_LANES - N_EXPERTS - N_GROUPS
    wr = jnp.concatenate([w_re, w_rg, jnp.zeros((D, pad), F32)], 1).astype(BF16)
    br = jnp.concatenate([b_re, b_rg, jnp.zeros((pad,), F32)]).reshape(1, ROUTE_LANES)
    nseq = seq // tm
    full = lambda a: pl.BlockSpec(a.shape, lambda i: (0,) * a.ndim)
    row = lambda n: pl.BlockSpec((tm, n), lambda i: (i, 0))
    memb = pl.BlockSpec((1,) + mk.shape[1:], lambda i: (i // nseq, 0, 0))
    args = [x2d, a2d, hy2d, ag.reshape(1, -1), hg.reshape(1, -1), woa, woh, cg.reshape(1, D),
            w_mq.astype(BF16), mk, mv, w_mo.astype(BF16), fg.reshape(1, D), wr, br]
    in_specs = [row(D), row(ca), row(hy2d.shape[1])] + [full(a) for a in args[3:9]] + [memb, memb] \
        + [full(a) for a in args[11:]]
    return pl.pallas_call(
        _postmix_kernel,
        grid=(T // tm,),
        in_specs=in_specs,
        out_specs=[row(D), pl.BlockSpec((tm, SUBLANES, LANES), lambda i: (i, 0, 0)), row(ROUTE_LANES),
                   pl.BlockSpec((1, SUBLANES, tm), lambda i: (i, 0, 0))],
        out_shape=[jax.ShapeDtypeStruct((T, D), F32), jax.ShapeDtypeStruct((T, SUBLANES, LANES), F32),
                   jax.ShapeDtypeStruct((T, ROUTE_LANES), F32),
                   jax.ShapeDtypeStruct((T // tm, SUBLANES, tm), F32)],
        compiler_params=_cparams("parallel"),
        name="postmix",
    )(*args)


def _slot_onehots(rt):
    e = lax.broadcasted_iota(jnp.int32, (N_EXPERTS, rt.shape[1]), 0).astype(F32)
    return [e == rt[ROUTE_ID0 + k:ROUTE_ID0 + k + 1, :] for k in range(2)]


def _slot_rows(rows, width):
    sub = lax.broadcasted_iota(jnp.int32, (SUBLANES, width), 0)
    return jnp.where(sub == 0, rows[0], jnp.where(sub == 1, rows[1], 0.0))


def _rank_kernel(rt_ref, rank_out, cnt_out, carry_ref):
    @pl.when(pl.program_id(0) == 0)
    def _():
        carry_ref[...] = jnp.zeros_like(carry_ref)

    tr = rt_ref.shape[2]
    oh = _slot_onehots(rt_ref[0])
    cnt = jnp.where(oh[0] | oh[1], 1.0, 0.0)
    s = lax.broadcasted_iota(jnp.int32, (tr, tr), 0)
    t = lax.broadcasted_iota(jnp.int32, (tr, tr), 1)
    before = jnp.where(s < t, 1.0, 0.0).astype(BF16)
    cum = _dot(cnt.astype(BF16), before) + carry_ref[...]
    ranks = [jnp.sum(jnp.where(m, cum, 0.0), axis=0, keepdims=True) for m in oh]
    rank_out[0] = _slot_rows(ranks, tr)
    carry_ref[...] += jnp.sum(cnt, axis=1, keepdims=True)
    cnt_out[...] = carry_ref[...]


def _pos_kernel(rt_ref, rank_ref, base_ref, pos_out):
    tr = rt_ref.shape[2]
    oh = _slot_onehots(rt_ref[0])
    rank = rank_ref[0]
    pos = [jnp.sum(jnp.where(m, base_ref[...], 0.0), axis=0, keepdims=True) + rank[k:k + 1, :]
           for k, m in enumerate(oh)]
    pos_out[0] = _slot_rows(pos, tr).astype(jnp.int32)


def _dispatch_kernel(base_ref, cnt_ref, cp_ref, pos0_ref, pos1_ref, hn_ref, xs_hbm, zero_ref, sem, *, tt):
    i = pl.program_id(0)
    row_copy = lambda src, dst: pltpu.make_async_copy(
        hn_ref.at[pl.ds(src, 1)], xs_hbm.at[pl.ds(dst, 1)], sem)

    @pl.when(i == 0)
    def _():
        zc = zero_ref.shape[0]
        zero_ref[...] = jnp.zeros_like(zero_ref)
        chunk = lambda c: pltpu.make_async_copy(zero_ref, xs_hbm.at[pl.ds(c * zc, zc)], sem)

        def clear(lo, hi):
            lax.fori_loop(lo, hi, lambda c, carry: (chunk(c).start(), carry)[1], 0)

        def drain(lo, hi):
            lax.fori_loop(lo, hi, lambda c, carry: (chunk(c).wait(), carry)[1], 0)

        last = N_EXPERTS - 1
        spans = [((base_ref[e] + cnt_ref[e]) // zc, (base_ref[e] + cp_ref[e]) // zc)
                 for e in range(N_EXPERTS)]
        spans.append(((base_ref[last] + cp_ref[last]) // zc, xs_hbm.shape[0] // zc))
        for lo, hi in spans:
            clear(lo, hi)
        for lo, hi in spans:
            drain(lo, hi)

    def start(t, carry):
        row_copy(t, pos0_ref[t]).start(priority=0)
        row_copy(t, pos1_ref[t]).start(priority=1)
        return carry

    lax.fori_loop(0, tt, start, 0, unroll=DMA_UNROLL)

    def wait(t, carry):
        row_copy(0, 0).wait()
        row_copy(0, 0).wait()
        return carry

    lax.fori_loop(0, tt, wait, 0, unroll=DMA_UNROLL)


def _ffn_kernel(te_ref, nu_ref, xs_ref, wg_ref, wu_ref, wd_ref, ys_ref):
    used = pl.program_id(0) < nu_ref[0]

    @pl.when(used)
    def _():
        x = _load_row_tiles(xs_ref).astype(BF16)
        a = _dot(x, wg_ref[0].astype(BF16))
        b = _dot(x, wu_ref[0].astype(BF16))
        m = (a * jax.nn.sigmoid(a)) * b
        _store_row_tiles(ys_ref, _dot(m.astype(BF16), wd_ref[0].astype(BF16)))

    @pl.when(jnp.logical_not(used))
    def _():
        ys_ref[...] = jnp.zeros_like(ys_ref)


def _combine_kernel(pos0_ref, pos1_ref, pos0_next_ref, pos1_next_ref, ys_hbm, x_ref, route_ref, fg_ref,
                    o_ref, buf_ref, sem, *, tc):
    i = pl.program_id(0)
    slot = i % 2
    row_copy = lambda s, k, t, p: pltpu.make_async_copy(
        ys_hbm.at[pl.ds(p, 1)], buf_ref.at[s, k, pl.ds(t, 1)], sem.at[s])

    def fetch(p0_ref, p1_ref, s):
        def start(t, carry):
            row_copy(s, 0, t, p0_ref[t]).start(priority=0)
            row_copy(s, 1, t, p1_ref[t]).start(priority=1)
            return carry

        lax.fori_loop(0, tc, start, 0, unroll=DMA_UNROLL)

    @pl.when(i == 0)
    def _():
        fetch(pos0_ref, pos1_ref, 0)

    @pl.when(i + 1 < pl.num_programs(0))
    def _():
        fetch(pos0_next_ref, pos1_next_ref, 1 - slot)

    def wait(t, carry):
        row_copy(slot, 0, 0, 0).wait()
        row_copy(slot, 1, 0, 0).wait()
        return carry

    lax.fori_loop(0, tc, wait, 0, unroll=DMA_UNROLL)
    route = route_ref[...]
    y = (x_ref[...] + route[:, ROUTE_W0:ROUTE_W0 + 1] * _load_row_tiles(buf_ref, (slot, 0))
         + route[:, ROUTE_W0 + 1:ROUTE_W0 + 2] * _load_row_tiles(buf_ref, (slot, 1)))
    o_ref[...] = _rms(y, fg_ref[...])


def _moe(hn, route, route_t, x2d, w_gate, w_up, w_down, fg, tt=1024, tc=512):
    T, D = x2d.shape
    E = N_EXPERTS
    F = w_gate.shape[-1]
    tmm = MOE_ROW_TILE
    ntr, _, tr = route_t.shape
    row = lambda tm, n: pl.BlockSpec((tm, n), lambda i: (i, 0))
    rec = pl.BlockSpec((1, SUBLANES, tr), lambda i: (i, 0, 0))
    col = pl.BlockSpec((E, 1), lambda i: (0, 0))
    rank, counts = pl.pallas_call(
        _rank_kernel,
        grid=(ntr,),
        in_specs=[rec],
        out_specs=[rec, col],
        out_shape=[jax.ShapeDtypeStruct(route_t.shape, F32), jax.ShapeDtypeStruct((E, 1), F32)],
        scratch_shapes=[pltpu.VMEM((E, 1), F32)],
        compiler_params=_cparams("arbitrary"),
        name="moe_rank",
    )(route_t)

    cnt = counts[:, 0].astype(jnp.int32)
    cp = ((cnt + tmm - 1) // tmm) * tmm
    ends = jnp.cumsum(cp)
    base = ends - cp
    n_used = ends[-1] // tmm
    n_tiles = (2 * T) // tmm + E
    tile_start = jnp.minimum(jnp.arange(n_tiles, dtype=jnp.int32), n_used - 1) * tmm
    tile_expert = jnp.minimum(jnp.sum((tile_start[:, None] >= ends[None, :]).astype(jnp.int32), axis=1), E - 1)

    pos = pl.pallas_call(
        _pos_kernel,
        grid=(ntr,),
        in_specs=[rec, rec, col],
        out_specs=rec,
        out_shape=jax.ShapeDtypeStruct(route_t.shape, jnp.int32),
        compiler_params=_cparams("parallel"),
        name="moe_pos",
    )(route_t, rank, base.astype(F32).reshape(E, 1))
    pos0, pos1 = pos[:, 0, :].reshape(T), pos[:, 1, :].reshape(T)

    xs = pl.pallas_call(
        functools.partial(_dispatch_kernel, tt=tt),
        grid_spec=pltpu.PrefetchScalarGridSpec(
            num_scalar_prefetch=3,
            grid=(T // tt,),
            in_specs=[pl.BlockSpec((tt,), lambda i, b, n, c: (i,), memory_space=pltpu.SMEM),
                      pl.BlockSpec((tt,), lambda i, b, n, c: (i,), memory_space=pltpu.SMEM),
                      pl.BlockSpec((tt, SUBLANES, LANES), lambda i, b, n, c: (i, 0, 0))],
            out_specs=pl.BlockSpec(memory_space=pl.ANY),
            scratch_shapes=[pltpu.VMEM((MOE_CLEAR_ROWS, SUBLANES, LANES), F32),
                            pltpu.SemaphoreType.DMA(())]),
        out_shape=jax.ShapeDtypeStruct((n_tiles * tmm, SUBLANES, LANES), F32),
        compiler_params=_cparams("arbitrary"),
        name="moe_dispatch",
    )(base, cnt, cp, pos0, pos1, hn)

    tile = lambda r, te, nu: (jnp.minimum(r, nu[0] - 1), 0, 0)
    ys = pl.pallas_call(
        _ffn_kernel,
        grid_spec=pltpu.PrefetchScalarGridSpec(
            num_scalar_prefetch=2,
            grid=(n_tiles,),
            in_specs=[pl.BlockSpec((tmm, SUBLANES, LANES), tile),
                      pl.BlockSpec((1, D, F), lambda r, te, nu: (te[r], 0, 0)),
                      pl.BlockSpec((1, D, F), lambda r, te, nu: (te[r], 0, 0)),
                      pl.BlockSpec((1, F, D), lambda r, te, nu: (te[r], 0, 0))],
            out_specs=pl.BlockSpec((tmm, SUBLANES, LANES), lambda r, te, nu: (r, 0, 0))),
        out_shape=jax.ShapeDtypeStruct((n_tiles * tmm, SUBLANES, LANES), F32),
        compiler_params=_cparams("arbitrary"),
        name="moe_ffn",
    )(tile_expert, n_used.reshape(1), xs, w_gate.reshape(E, D, F), w_up.reshape(E, D, F),
      w_down.reshape(E, F, D))

    cur = pl.BlockSpec((tc,), lambda i: (i,), memory_space=pltpu.SMEM)
    nxt = pl.BlockSpec((tc,), lambda i: (jnp.minimum(i + 1, T // tc - 1),), memory_space=pltpu.SMEM)
    return pl.pallas_call(
        functools.partial(_combine_kernel, tc=tc),
        grid=(T // tc,),
        in_specs=[cur, cur, nxt, nxt,
                  pl.BlockSpec(memory_space=pl.ANY),
                  row(tc, D), row(tc, ROUTE_LANES), pl.BlockSpec((1, D), lambda i: (0, 0))],
        out_specs=row(tc, D),
        out_shape=jax.ShapeDtypeStruct((T, D), F32),
        scratch_shapes=[pltpu.VMEM((2, 2, tc, SUBLANES, LANES), F32), pltpu.SemaphoreType.DMA((2,))],
        compiler_params=_cparams("arbitrary"),
        name="moe_combine",
    )(pos0, pos1, pos0, pos1, ys, x2d, route, fg.reshape(1, D))


def kernel(x, mem, mix_norm_g, w_in, q_norm_g, kv_norm_g, w_uq, w_ukv, hy_conv_w, hy_conv_b, hy_w1, hy_b1, hy_freq, hy_w2, hy_b2, hy_w3, hy_b3, hy_decay, hy_skip, attn_out_g, hy_out_g, w_out, cross_norm_g, mem_norm_g, w_mq, w_mkv, w_mo, ffn_norm_g, w_route_group, b_route_group, w_route_expert, b_route_expert, w_gate, w_up, w_down, final_norm_g):
    B, S, D = x.shape
    depth = w_in.shape[0]
    consts = _dft_constants(S)
    xf = x.reshape(B * S, D)
    for l in range(depth):
        q, k, v, hx1, hx2, hv = _inproj(xf, S, mix_norm_g[l], w_in[l], q_norm_g[l], kv_norm_g[l],
                                        w_uq[l], w_ukv[l], hy_conv_w[l], hy_conv_b[l])
        HP = q.shape[1]
        a_out = _attention(q.reshape(B, S, HP), k.reshape(B, S, HP), v.reshape(B, S, HP))
        filt = _hyena_filter_time(S, hy_w1[l], hy_b1[l], hy_freq[l], hy_w2[l], hy_b2[l], hy_w3[l],
                                  hy_b3[l], hy_decay[l])
        kf = _hyena_filter_spectrum(filt, consts[1], consts[3])
        C = hv.shape[1]
        h_out = _hyena(hx1.reshape(B, S, C), hx2.reshape(B, S, C), hv.reshape(B, S, C),
                       hy_skip[l], kf, consts)
        mk, mv = _memkv(mem, mem_norm_g[l], w_mkv[l])
        x2, hn, route, route_t = _postmix(xf, a_out.reshape(B * S, -1), h_out.reshape(B * S, C), S,
                                attn_out_g[l], hy_out_g[l], w_out[l], cross_norm_g[l], w_mq[l], mk, mv,
                                w_mo[l], ffn_norm_g[l], w_route_group[l], b_route_group[l],
                                w_route_expert[l], b_route_expert[l])
        assert depth == 1
        xf = _moe(hn, route, route_t, x2, w_gate[l], w_up[l], w_down[l], final_norm_g)
    return xf.reshape(B, S, D)
```

```python
import functools
import math

import numpy as np
import jax
import jax.numpy as jnp
from jax import lax
from jax.experimental import pallas as pl
from jax.experimental.pallas import tpu as pltpu

F32 = jnp.float32
BF16 = jnp.bfloat16

EPS = 1e-6
MLA_HEADS = 8
MLA_NOPE = 64
MLA_ROPE = 32
MLA_V = 64
ROPE_BASE = 10000.0
HEAD_PAD = 128
HY_ORDER = 2
HY_DIRS = 2
HY_BANDS = 16
MEM_HEADS = 4
N_GROUPS = 4
EXPERTS_PER_GROUP = 8
N_EXPERTS = N_GROUPS * EXPERTS_PER_GROUP
ROUTE_LANES = 128
ROUTE_ID0 = 0
ROUTE_W0 = 2
MOE_ROW_TILE = 512
MOE_CLEAR_ROWS = 64
DMA_UNROLL = 8

FFT_N1 = 64
FFT_N2 = 128
DFT_K1_BLOCK = 8
DFT_N2_BLOCK = 64
DFT_PITCH_PAD = 8

VMEM_LIMIT = 56 * 1024 * 1024


def _cparams(*sem):
    return pltpu.CompilerParams(dimension_semantics=sem, vmem_limit_bytes=VMEM_LIMIT)


def _rms(x, g):
    return x * lax.rsqrt(jnp.mean(x * x, axis=-1, keepdims=True) + EPS) * g


def _dot(a, b):
    return jnp.dot(a, b, preferred_element_type=F32)


SUBLANES = 8
LANES = 128


def _load_row_tiles(ref, lead=()):
    rows = ref.shape[-3]
    flat = ref.reshape(*ref.shape[:-3], rows * SUBLANES, LANES)
    return jnp.concatenate(
        [flat[(*lead, pl.ds(j, rows, stride=SUBLANES), slice(None))] for j in range(SUBLANES)], axis=1)


def _store_row_tiles(ref, val):
    rows = ref.shape[0]
    flat = ref.reshape(rows * SUBLANES, LANES)
    for j in range(SUBLANES):
        flat[pl.ds(j, rows, stride=SUBLANES), :] = val[:, j * LANES:(j + 1) * LANES]


def _inproj_kernel(x_ref, xp_ref, xn_ref, g_ref, wq_ref, wkv_ref, wkra_ref, wkrb_ref, why_ref, qg_ref,
                   kvg_ref, wqa_ref, wqb_ref, wka_ref, wv_ref, tab_ref, cw_ref, cb_ref,
                   q_out, k_out, v_out, x1_out, x2_out, hv_out, *, nseq):
    tm = x_ref.shape[0]
    halo = xp_ref.shape[0]
    hf = _rms(jnp.concatenate([xp_ref[...], x_ref[...], xn_ref[...]], axis=0), g_ref[...])
    h = hf[halo:halo + tm].astype(BF16)
    qn = _rms(_dot(h, wq_ref[...]), qg_ref[...]).astype(BF16)
    kvn = _rms(_dot(h, wkv_ref[...]), kvg_ref[...]).astype(BF16)
    tab = tab_ref[...]
    cq, sq, ck, sk = (tab[:, j * HEAD_PAD:(j + 1) * HEAD_PAD] for j in range(4))
    tile = lambda t: jnp.concatenate([t] * MLA_HEADS, axis=1)
    q = _dot(qn, wqa_ref[...]) * tile(cq) + _dot(qn, wqb_ref[...]) * tile(sq)
    q_out[...] = q.astype(BF16)
    kr = _dot(h, wkra_ref[...]) * ck + _dot(h, wkrb_ref[...]) * sk
    k_out[...] = (_dot(kvn, wka_ref[...]) + tile(kr)).astype(BF16)
    lane = lax.broadcasted_iota(jnp.int32, (1, v_out.shape[1]), 1) % HEAD_PAD
    v_out[...] = (_dot(kvn, wv_ref[...]) + jnp.where(lane == MLA_V, 1.0, 0.0)).astype(BF16)
    hy = _dot(hf.astype(BF16), why_ref[...])
    i = pl.program_id(0) % nseq
    row = lax.broadcasted_iota(jnp.int32, hy.shape, 0)
    outside = ((row == halo - 1) & (i == 0)) | ((row == halo + tm) & (i == nseq - 1))
    hy = jnp.where(outside, 0.0, hy)
    cw = cw_ref[...]
    u = (hy[halo - 1:halo - 1 + tm] * cw[0:1] + hy[halo:halo + tm] * cw[1:2]
         + hy[halo + 1:halo + 1 + tm] * cw[2:3] + cb_ref[...])
    c = x1_out.shape[1]
    x1_out[...] = u[:, :c].astype(x1_out.dtype)
    x2_out[...] = u[:, c:2 * c].astype(x2_out.dtype)
    hv_out[...] = u[:, 2 * c:].astype(hv_out.dtype)


def _inproj(x2d, seq, mix_g, w_in, q_g, kv_g, w_uq, w_ukv, conv_w, conv_b, tm=1024):
    T, D = x2d.shape
    per = tm // SUBLANES
    cb = conv_b.reshape(1, -1)
    q_rank, kv_rank = q_g.shape[0], kv_g.shape[0]
    off_kv = q_rank
    off_kr = off_kv + kv_rank
    off_hy = off_kr + MLA_ROPE
    C = (w_in.shape[1] - off_hy) // 3
    H = MLA_HEADS
    half = MLA_ROPE // 2
    wq = w_in[:, :off_kv].astype(BF16)
    wkv = w_in[:, off_kv:off_kr].astype(BF16)
    wkr = w_in[:, off_kr:off_hy]
    wkr_sw = jnp.concatenate([wkr[:, half:], wkr[:, :half]], axis=1)
    zpad = lambda n: jnp.zeros((D, n), F32)
    wkra = jnp.concatenate([zpad(MLA_NOPE), wkr, zpad(HEAD_PAD - MLA_NOPE - MLA_ROPE)], 1).astype(BF16)
    wkrb = jnp.concatenate([zpad(MLA_NOPE), wkr_sw, zpad(HEAD_PAD - MLA_NOPE - MLA_ROPE)], 1).astype(BF16)
    why = w_in[:, off_hy:].astype(BF16)

    uq = w_uq.reshape(q_rank, H, MLA_NOPE + MLA_ROPE)
    uq_n, uq_r = uq[..., :MLA_NOPE], uq[..., MLA_NOPE:]
    uq_rs = jnp.concatenate([uq_r[..., half:], uq_r[..., :half]], axis=-1)
    zq = lambda n: jnp.zeros((q_rank, H, n), F32)
    wqa = jnp.concatenate([uq_n, uq_r, zq(HEAD_PAD - MLA_NOPE - MLA_ROPE)], -1).reshape(q_rank, H * HEAD_PAD).astype(BF16)
    wqb = jnp.concatenate([zq(MLA_NOPE), uq_rs, zq(HEAD_PAD - MLA_NOPE - MLA_ROPE)], -1).reshape(q_rank, H * HEAD_PAD).astype(BF16)
    ukv = w_ukv.reshape(kv_rank, H, MLA_NOPE + MLA_V)
    zk = lambda n: jnp.zeros((kv_rank, H, n), F32)
    wka = jnp.concatenate([ukv[..., :MLA_NOPE], zk(HEAD_PAD - MLA_NOPE)], -1).reshape(kv_rank, H * HEAD_PAD).astype(BF16)
    wv = jnp.concatenate([ukv[..., MLA_NOPE:], zk(HEAD_PAD - MLA_V)], -1).reshape(kv_rank, H * HEAD_PAD).astype(BF16)

    ang = np.arange(seq)[:, None] * ROPE_BASE ** (-np.arange(half) / half)[None, :]
    cos2 = np.concatenate([np.cos(ang), np.cos(ang)], 1)
    sin2 = np.concatenate([-np.sin(ang), np.sin(ang)], 1)
    zs = lambda n: np.zeros((seq, n))
    scale = (MLA_NOPE + MLA_ROPE) ** -0.5 * math.log2(math.e)
    rest = HEAD_PAD - MLA_NOPE - MLA_ROPE
    cq = scale * np.concatenate([np.ones((seq, MLA_NOPE)), cos2, zs(rest)], 1)
    sq = scale * np.concatenate([zs(MLA_NOPE), sin2, zs(rest)], 1)
    ck = np.concatenate([zs(MLA_NOPE), cos2, zs(rest)], 1)
    sk = np.concatenate([zs(MLA_NOPE), sin2, zs(rest)], 1)
    tab = jnp.asarray(np.concatenate([cq, sq, ck, sk], 1), dtype=F32)

    nseq = seq // tm
    full = lambda a: pl.BlockSpec(a.shape, lambda i: (0,) * a.ndim)
    row = lambda n: pl.BlockSpec((tm, n), lambda i: (i, 0))
    consts = [mix_g.reshape(1, D), wq, wkv, wkra, wkrb, why, q_g.reshape(1, -1), kv_g.reshape(1, -1),
              wqa, wqb, wka, wv]
    HP = H * HEAD_PAD
    return pl.pallas_call(
        functools.partial(_inproj_kernel, nseq=nseq),
        grid=(T // tm,),
        in_specs=[row(D),
                  pl.BlockSpec((SUBLANES, D), lambda i: (jnp.maximum(i * per - 1, 0), 0)),
                  pl.BlockSpec((SUBLANES, D), lambda i: (jnp.minimum((i + 1) * per, T // SUBLANES - 1), 0))]
        + [full(a) for a in consts]
        + [pl.BlockSpec((tm, 4 * HEAD_PAD), lambda i: (i % nseq, 0)), full(conv_w), full(cb)],
        out_specs=[row(HP), row(HP), row(HP), row(C), row(C), row(C)],
        out_shape=[jax.ShapeDtypeStruct((T, HP), BF16)] * 3 + [jax.ShapeDtypeStruct((T, C), BF16)] * 3,
        compiler_params=_cparams("parallel"),
        name="inproj",
    )(x2d, x2d, x2d, *consts, tab, conv_w, cb)


def _attn_kernel(q_ref, k_ref, v_ref, o_ref):
    outs = []
    for h in range(MLA_HEADS):
        sl = slice(h * HEAD_PAD, (h + 1) * HEAD_PAD)
        s = lax.dot_general(q_ref[0, :, sl], k_ref[0, :, sl], (((1,), (1,)), ((), ())),
                            preferred_element_type=F32).astype(BF16)
        p = jnp.exp2(s - jnp.max(s, axis=-1, keepdims=True))
        o = _dot(p, v_ref[0, :, sl])
        outs.append(o[:, :MLA_V] / o[:, MLA_V:MLA_V + 1])
    o_ref[0] = jnp.concatenate(outs, axis=1).astype(o_ref.dtype)


def _attention(q, k, v, tq=512):
    B, S, HP = q.shape
    return pl.pallas_call(
        _attn_kernel,
        grid=(B, S // tq),
        in_specs=[pl.BlockSpec((1, tq, HP), lambda b, i: (b, i, 0)),
                  pl.BlockSpec((1, S, HP), lambda b, i: (b, 0, 0)),
                  pl.BlockSpec((1, S, HP), lambda b, i: (b, 0, 0))],
        out_specs=pl.BlockSpec((1, tq, MLA_HEADS * MLA_V), lambda b, i: (b, i, 0)),
        out_shape=jax.ShapeDtypeStruct((B, S, MLA_HEADS * MLA_V), BF16),
        compiler_params=_cparams("parallel", "arbitrary"),
        name="mla_attention",
    )(q, k, v)


def _dft_constants(seq):
    n = 2 * seq
    n1, n2 = FFT_N1, FFT_N2
    assert n1 * n2 == n
    r1 = np.arange(n1)
    r2 = np.arange(n2)
    blk = lambda z: np.block([[z.real, -z.imag], [z.imag, z.real]])
    w1 = np.exp(-2j * np.pi * np.outer(r1, r1) / n1)
    fa_data = blk(w1[:, :n1 // 2])
    fa_filt = np.concatenate([w1.real, w1.imag], axis=0)
    fc = blk(np.conj(w1).T[:n1 // 2, :])
    w2 = np.exp(-2j * np.pi * np.outer(r2, r2) / n2)
    tw = np.exp(-2j * np.pi * np.outer(r1, r2) / n)
    fb = np.stack([blk(w2 * tw[k][None, :]) for k in range(n1)])
    fbi = np.stack([blk(np.conj(w2).T * np.conj(tw[k])[:, None] / n) for k in range(n1)])
    as_bf = lambda a: jnp.asarray(a, dtype=F32).astype(BF16)
    return as_bf(fa_data), as_bf(fa_filt), as_bf(fc), as_bf(fb), as_bf(fbi)


def _filter_kernel(z_ref, w1_ref, b1_ref, fr_ref, w2_ref, b2_ref, w3_ref, b3_ref, dec_ref, o_ref, *, seq, tr):
    hp = lax.Precision.HIGHEST
    z = z_ref[...]
    fr = fr_ref[...]
    def sin_rows(a):
        half, w = a.shape[0] // 2, a.shape[1]
        s = jnp.sin(jnp.concatenate([a[:half], a[half:]], axis=1))
        return jnp.concatenate([s[:, :w], s[:, w:]], axis=0)

    h = sin_rows(fr[0:1] * (jnp.dot(z, w1_ref[...], precision=hp, preferred_element_type=F32) + b1_ref[...]))
    h = sin_rows(fr[1:2] * (jnp.dot(h, w2_ref[...], precision=hp, preferred_element_type=F32) + b2_ref[...]))
    split = lambda a: (a.astype(BF16), (a - a.astype(BF16).astype(F32)).astype(BF16))
    (hh, hl), (wh, wl) = split(h), split(w3_ref[0])
    h = _dot(hh, wh) + _dot(hh, wl) + _dot(hl, wh) + b3_ref[0]
    h = h * jnp.exp(-z[:, 0:1] * jnp.abs(dec_ref[0]))
    n = pl.program_id(0) * tr + lax.broadcasted_iota(jnp.int32, h.shape, 0)
    o_ref[...] = jnp.where(n == seq, 0.0, h).astype(o_ref.dtype)


def _hyena_filter_time(seq, w1, b1, freq, w2, b2, w3, b3, decay, tr=512):
    n = 2 * seq
    emb, ffn = w1.shape
    C = w3.shape[1] // (HY_ORDER * HY_DIRS)
    off = np.arange(n)
    t = np.where(off < seq, off, n - off).astype(np.float64)
    bands = np.linspace(1e-4, HY_BANDS - 1, HY_BANDS)
    ang = 2.0 * math.pi * t[:, None] * bands[None, :] / seq
    z = np.concatenate([(t / seq)[:, None], np.cos(ang), -np.sin(ang)], axis=-1)
    zl = LANES
    z = jnp.asarray(np.pad(z, ((0, 0), (0, zl - emb))), dtype=F32)
    w1p = jnp.pad(w1, ((0, zl - emb), (0, 0)))
    by_dir = lambda a: jnp.moveaxis(a.reshape(a.shape[0], HY_ORDER, HY_DIRS, C), 2, 0).reshape(
        HY_DIRS, a.shape[0], HY_ORDER * C)
    w3d, b3d, decd = by_dir(w3), by_dir(b3.reshape(1, -1)), by_dir(decay.reshape(1, -1))
    full = lambda a: pl.BlockSpec(a.shape, lambda i: (0,) * a.ndim)
    ndir = lambda a: pl.BlockSpec((1,) + a.shape[1:], lambda i: ((i * tr) // seq, 0, 0))
    consts = [w1p, b1.reshape(1, -1), freq, w2, b2.reshape(1, -1)]
    return pl.pallas_call(
        functools.partial(_filter_kernel, seq=seq, tr=tr),
        grid=(n // tr,),
        in_specs=[pl.BlockSpec((tr, zl), lambda i: (i, 0))] + [full(a) for a in consts]
        + [ndir(w3d), ndir(b3d), ndir(decd)],
        out_specs=pl.BlockSpec((tr, HY_ORDER * C), lambda i: (i, 0)),
        out_shape=jax.ShapeDtypeStruct((n, HY_ORDER * C), BF16),
        compiler_params=_cparams("parallel"),
        name="hyena_filter_mlp",
    )(z, *consts, w3d, b3d, decd)


def _pitched(rows, nb):
    return pltpu.VMEM((rows, nb + DFT_PITCH_PAD, LANES), F32)


def _block_rows(ref):
    return math.prod(ref.shape[:-2]), ref.shape[-2]


def _copy_in(ref, scr):
    rows, nb = _block_rows(ref)
    scr[:, :nb, :] = ref[...].reshape(rows, nb, LANES).astype(scr.dtype)


def _copy_out(scr, ref):
    rows, nb = _block_rows(ref)
    ref[...] = scr[:, :nb, :].reshape(ref.shape).astype(ref.dtype)


def _at_n2(scr, n):
    rows, pitch, _ = scr.shape
    return scr.reshape(rows * pitch, LANES).at[pl.ds(n, rows, stride=pitch), :]


def _outer_dft(mat_ref, x_ref, o_ref, xs, os):
    _copy_in(x_ref, xs)
    for n in range(x_ref.shape[-2]):
        _at_n2(os, n)[...] = _dot(mat_ref[...], _at_n2(xs, n)[...].astype(BF16))
    _copy_out(os, o_ref)


def _filter_stage_a_kernel(x_ref, fa_ref, o_ref, xs, os):
    _outer_dft(fa_ref, x_ref, o_ref, xs, os)


def _filter_stage_b_kernel(x_ref, fb_ref, o_ref):
    for kk in range(x_ref.shape[1]):
        x = jnp.concatenate([x_ref[0, kk], x_ref[1, kk]], axis=0).astype(BF16)
        o_ref[kk] = _dot(fb_ref[kk], x).astype(o_ref.dtype)


def _hyena_filter_spectrum(filt, fa_filt, fb, nb=DFT_N2_BLOCK):
    n, oc = filt.shape
    a = pl.pallas_call(
        _filter_stage_a_kernel,
        grid=(FFT_N2 // nb, oc // LANES),
        in_specs=[pl.BlockSpec((FFT_N1, nb, LANES), lambda j, c: (0, j, c)),
                  pl.BlockSpec(fa_filt.shape, lambda j, c: (0, 0))],
        out_specs=pl.BlockSpec((2, FFT_N1, nb, LANES), lambda j, c: (0, 0, j, c)),
        out_shape=jax.ShapeDtypeStruct((2, FFT_N1, FFT_N2, oc), BF16),
        scratch_shapes=[_pitched(FFT_N1, nb), _pitched(2 * FFT_N1, nb)],
        compiler_params=_cparams("parallel", "parallel"),
        name="hyena_filter_dft_a",
    )(filt.reshape(FFT_N1, FFT_N2, oc), fa_filt)
    return pl.pallas_call(
        _filter_stage_b_kernel,
        grid=(FFT_N1 // DFT_K1_BLOCK,),
        in_specs=[pl.BlockSpec((2, DFT_K1_BLOCK, FFT_N2, oc), lambda k: (0, k, 0, 0)),
                  pl.BlockSpec((DFT_K1_BLOCK, 2 * FFT_N2, 2 * FFT_N2), lambda k: (k, 0, 0))],
        out_specs=pl.BlockSpec((DFT_K1_BLOCK, 2 * FFT_N2, oc), lambda k: (k, 0, 0)),
        out_shape=jax.ShapeDtypeStruct((FFT_N1, 2 * FFT_N2, oc), BF16),
        compiler_params=_cparams("parallel"),
        name="hyena_filter_dft_b",
    )(a, fb)


def _stage_a_kernel(x_ref, fa_ref, o_ref, xs, os):
    _outer_dft(fa_ref, x_ref, o_ref, xs, os)


def _stage_a(x4, fa, nb):
    B, r, n2, C = x4.shape
    return pl.pallas_call(
        _stage_a_kernel,
        grid=(B // 2, n2 // nb, C // LANES),
        in_specs=[pl.BlockSpec((2, r, nb, LANES), lambda p, j, c: (p, 0, j, c)),
                  pl.BlockSpec(fa.shape, lambda p, j, c: (0, 0))],
        out_specs=pl.BlockSpec((1, 2, FFT_N1, nb, LANES), lambda p, j, c: (p, 0, 0, j, c)),
        out_shape=jax.ShapeDtypeStruct((B // 2, 2, FFT_N1, n2, C), BF16),
        scratch_shapes=[_pitched(2 * r, nb), _pitched(2 * FFT_N1, nb)],
        compiler_params=_cparams("parallel", "parallel", "parallel"),
        name="hyena_dft_a",
    )(x4, fa)


def _stage_b_kernel(x_ref, fb_ref, kf_ref, fbi_ref, o_ref):
    npair = x_ref.shape[0]
    n2 = x_ref.shape[3]
    c = x_ref.shape[4]
    for kk in range(x_ref.shape[2]):
        x = jnp.concatenate(
            [jnp.concatenate([x_ref[p, 0, kk], x_ref[p, 1, kk]], axis=0) for p in range(npair)],
            axis=1).astype(BF16)
        g = _dot(fb_ref[kk], x)
        gr, gi = g[:n2], g[n2:]
        kf = kf_ref[kk].astype(F32)
        kr = jnp.concatenate([kf[:n2]] * npair, axis=1)
        ki = jnp.concatenate([kf[n2:]] * npair, axis=1)
        hcat = jnp.concatenate([gr * kr - gi * ki, gr * ki + gi * kr], axis=0).astype(BF16)
        y = _dot(fbi_ref[kk], hcat)
        for p in range(npair):
            o_ref[p, 0, kk] = y[:n2, p * c:(p + 1) * c].astype(o_ref.dtype)
            o_ref[p, 1, kk] = y[n2:, p * c:(p + 1) * c].astype(o_ref.dtype)


def _stage_b(spec, fb, kf, fbi, order, nk=DFT_K1_BLOCK):
    npair, _, _, _, C = spec.shape
    blk = pl.BlockSpec((npair, 2, nk, FFT_N2, C), lambda k: (0, 0, k, 0, 0))
    mat = pl.BlockSpec((nk, 2 * FFT_N2, 2 * FFT_N2), lambda k: (k, 0, 0))
    return pl.pallas_call(
        _stage_b_kernel,
        grid=(FFT_N1 // nk,),
        in_specs=[blk, mat, pl.BlockSpec((nk, 2 * FFT_N2, C), lambda k: (k, 0, order)), mat],
        out_specs=blk,
        out_shape=jax.ShapeDtypeStruct(spec.shape, BF16),
        compiler_params=_cparams("parallel"),
        name="hyena_dft_b",
    )(spec, fb, kf, fbi)


def _stage_c_kernel(y_ref, fc_ref, gate_ref, z_ref, skip_ref, *rest, stage_a):
    if stage_a:
        fa_ref, z_out, a_out, ys, gs, zs, os, as_ = rest
    else:
        z_out, ys, gs, zs, os = rest
    nb = gate_ref.shape[-2]
    _copy_in(y_ref, ys)
    _copy_in(gate_ref, gs)
    _copy_in(z_ref, zs)
    skip = skip_ref[...]
    for n in range(nb):
        conv = _dot(fc_ref[...], _at_n2(ys, n)[...].astype(BF16))
        _at_n2(os, n)[...] = _at_n2(gs, n)[...] * (conv + skip * _at_n2(zs, n)[...])
    _copy_out(os, z_out)
    if stage_a:
        for n in range(nb):
            _at_n2(as_, n)[...] = _dot(fa_ref[...], _at_n2(os, n)[...].astype(BF16))
        _copy_out(as_, a_out)


def _stage_c(yspec, fc, gate, zin, skip, fa=None, nb=DFT_N2_BLOCK):
    B, r, n2, C = gate.shape
    dat = pl.BlockSpec((2, r, nb, LANES), lambda p, j, c: (p, 0, j, c))
    spc = pl.BlockSpec((1, 2, FFT_N1, nb, LANES), lambda p, j, c: (p, 0, 0, j, c))
    in_specs = [spc, pl.BlockSpec(fc.shape, lambda p, j, c: (0, 0)), dat, dat,
                pl.BlockSpec((1, LANES), lambda p, j, c: (0, c))]
    out_specs = [dat]
    out_shape = [jax.ShapeDtypeStruct(gate.shape, BF16)]
    args = [yspec, fc, gate, zin, skip.reshape(1, C)]
    scratch = [_pitched(2 * FFT_N1, nb)] + [_pitched(2 * r, nb)] * 3
    if fa is not None:
        in_specs.append(pl.BlockSpec(fa.shape, lambda p, j, c: (0, 0)))
        out_specs.append(spc)
        out_shape.append(jax.ShapeDtypeStruct(yspec.shape, BF16))
        args.append(fa)
        scratch.append(_pitched(2 * FFT_N1, nb))
    return pl.pallas_call(
        functools.partial(_stage_c_kernel, stage_a=fa is not None),
        grid=(B // 2, n2 // nb, C // LANES),
        in_specs=in_specs, out_specs=out_specs, out_shape=out_shape,
        scratch_shapes=scratch,
        compiler_params=_cparams("parallel", "parallel", "parallel"),
        name="hyena_dft_c",
    )(*args)


def _hyena(x1, x2, v, skip, kf, consts, nb=DFT_N2_BLOCK):
    fa_data, _, fc, fb, fbi = consts
    B, S, C = v.shape
    split = lambda a: a.reshape(B, S // FFT_N2, FFT_N2, C)
    a0 = _stage_a(split(v), fa_data, nb)
    y0 = _stage_b(a0, fb, kf, fbi, 0)
    z1, a1 = _stage_c(y0, fc, split(x1), split(v), skip[0], fa=fa_data, nb=nb)
    y1 = _stage_b(a1, fb, kf, fbi, 1)
    (out,) = _stage_c(y1, fc, split(x2), z1, skip[1], nb=nb)
    return out.reshape(B, S, C)


def _memkv_kernel(m_ref, g_ref, w_ref, k_out, v_out):
    hm = _rms(m_ref[0], g_ref[...]).astype(BF16)
    kv = _dot(hm, w_ref[...])
    d = k_out.shape[2]
    k_out[0] = kv[:, :d].astype(BF16)
    v_out[0] = kv[:, d:].astype(BF16)


def _memkv(mem, g, w_mkv):
    B, M, D = mem.shape
    dk = w_mkv.shape[1] // 2
    w = w_mkv.astype(BF16)
    return pl.pallas_call(
        _memkv_kernel,
        grid=(B,),
        in_specs=[pl.BlockSpec((1, M, D), lambda b: (b, 0, 0)),
                  pl.BlockSpec((1, D), lambda b: (0, 0)),
                  pl.BlockSpec(w.shape, lambda b: (0, 0))],
        out_specs=[pl.BlockSpec((1, M, dk), lambda b: (b, 0, 0))] * 2,
        out_shape=[jax.ShapeDtypeStruct((B, M, dk), BF16)] * 2,
        compiler_params=_cparams("parallel"),
        name="mem_kv",
    )(mem, g.reshape(1, D), w)


def _route(lt):
    tokens = lt.shape[1]
    ninf = -jnp.inf
    col_max = lambda a: jnp.max(a, axis=0, keepdims=True)
    first = lambda mask, idx, n: jnp.min(jnp.where(mask, idx, float(n)), axis=0, keepdims=True)
    gl = lt[N_EXPERTS:N_EXPERTS + N_GROUPS]
    gi = lax.broadcasted_iota(jnp.int32, gl.shape, 0).astype(F32)
    gmax = col_max(gl)
    g_idx = first(gl == gmax, gi, N_GROUPS)
    p_group = 1.0 / jnp.sum(jnp.exp(gl - gmax), axis=0, keepdims=True)
    ei = lax.broadcasted_iota(jnp.int32, (N_EXPERTS, tokens), 0)
    in_g = (ei // EXPERTS_PER_GROUP).astype(F32) == g_idx
    ei = ei.astype(F32)
    el = jnp.where(in_g, lt[:N_EXPERTS], ninf)
    v1 = col_max(el)
    i1 = first(el == v1, ei, N_EXPERTS)
    el2 = jnp.where(ei == i1, ninf, el)
    v2 = col_max(el2)
    i2 = first(el2 == v2, ei, N_EXPERTS)
    e2 = jnp.exp(v2 - v1)
    p1 = 1.0 / (1.0 + e2)
    p2 = e2 / (1.0 + e2)
    sub = lax.broadcasted_iota(jnp.int32, (SUBLANES, tokens), 0)
    sel = lambda n, val: jnp.where(sub == n, val, 0.0)
    return (sel(ROUTE_ID0, i1) + sel(ROUTE_ID0 + 1, i2)
            + sel(ROUTE_W0, p_group * p1) + sel(ROUTE_W0 + 1, p_group * p2))


def _postmix_kernel(x_ref, a_ref, hy_ref, ag_ref, hg_ref, woa_ref, woh_ref, cg_ref, wmq_ref,
                    mk_ref, mv_ref, wmo_ref, fg_ref, wr_ref, br_ref, x_out, hn_out, route_t_out):
    ra = _rms(a_ref[...].astype(F32), ag_ref[...]).astype(BF16)
    rh = _rms(hy_ref[...].astype(F32), hg_ref[...]).astype(BF16)
    x = x_ref[...] + _dot(ra, woa_ref[...]) + _dot(rh, woh_ref[...])
    q = _dot(_rms(x, cg_ref[...]).astype(BF16), wmq_ref[...])
    dh = q.shape[1] // MEM_HEADS
    outs = []
    for h in range(MEM_HEADS):
        sl = slice(h * dh, (h + 1) * dh)
        s = lax.dot_general(q[:, sl].astype(BF16), mk_ref[0, :, sl], (((1,), (1,)), ((), ())),
                            preferred_element_type=F32) * dh ** -0.5
        p = jnp.exp(s - jnp.max(s, axis=-1, keepdims=True))
        l = jnp.sum(p, axis=-1, keepdims=True)
        outs.append(_dot(p.astype(BF16), mv_ref[0, :, sl]) / l)
    o = jnp.concatenate(outs, axis=1).astype(BF16)
    x = x + _dot(o, wmo_ref[...])
    x_out[...] = x
    hn = _rms(x, fg_ref[...])
    _store_row_tiles(hn_out, hn)
    logits = _dot(hn.astype(BF16), wr_ref[...]) + br_ref[...]
    route_t_out[0] = _route(logits.T)


def _postmix(x2d, a2d, hy2d, seq, ag, hg, w_out, cg, w_mq, mk, mv, w_mo, fg, w_rg, b_rg, w_re, b_re, tm=1024):
    T, D = x2d.shape
    ca = a2d.shape[1]
    woa = w_out[:ca].astype(BF16)
    woh = w_out[ca:].astype(BF16)
    pad = ROUTE_LANES - N_EXPERTS - N_GROUPS
    wr = jnp.concatenate([w_re, w_rg, jnp.zeros((D, pad), F32)], 1).astype(BF16)
    br = jnp.concatenate([b_re, b_rg, jnp.zeros((pad,), F32)]).reshape(1, ROUTE_LANES)
    nseq = seq // tm
    full = lambda a: pl.BlockSpec(a.shape, lambda i: (0,) * a.ndim)
    row = lambda n: pl.BlockSpec((tm, n), lambda i: (i, 0))
    memb = pl.BlockSpec((1,) + mk.shape[1:], lambda i: (i // nseq, 0, 0))
    args = [x2d, a2d, hy2d, ag.reshape(1, -1), hg.reshape(1, -1), woa, woh, cg.reshape(1, D),
            w_mq.astype(BF16), mk, mv, w_mo.astype(BF16), fg.reshape(1, D), wr, br]
    in_specs = [row(D), row(ca), row(hy2d.shape[1])] + [full(a) for a in args[3:9]] + [memb, memb] \
        + [full(a) for a in args[11:]]
    return pl.pallas_call(
        _postmix_kernel,
        grid=(T // tm,),
        in_specs=in_specs,
        out_specs=[row(D), pl.BlockSpec((tm, SUBLANES, LANES), lambda i: (i, 0, 0)),
                   pl.BlockSpec((1, SUBLANES, tm), lambda i: (i, 0, 0))],
        out_shape=[jax.ShapeDtypeStruct((T, D), F32), jax.ShapeDtypeStruct((T, SUBLANES, LANES), F32),
                   jax.ShapeDtypeStruct((T // tm, SUBLANES, tm), F32)],
        compiler_params=_cparams("parallel"),
        name="postmix",
    )(*args)


def _slot_onehots(rt):
    e = lax.broadcasted_iota(jnp.int32, (N_EXPERTS, rt.shape[1]), 0).astype(F32)
    return [e == rt[ROUTE_ID0 + k:ROUTE_ID0 + k + 1, :] for k in range(2)]


def _slot_rows(rows, width):
    sub = lax.broadcasted_iota(jnp.int32, (SUBLANES, width), 0)
    return jnp.where(sub == 0, rows[0], jnp.where(sub == 1, rows[1], 0.0))


def _rank_kernel(rt_ref, rank_out, cnt_out, carry_ref):
    @pl.when(pl.program_id(0) == 0)
    def _():
        carry_ref[...] = jnp.zeros_like(carry_ref)

    tr = rt_ref.shape[2]
    oh = _slot_onehots(rt_ref[0])
    cnt = jnp.where(oh[0] | oh[1], 1.0, 0.0)
    s = lax.broadcasted_iota(jnp.int32, (tr, tr), 0)
    t = lax.broadcasted_iota(jnp.int32, (tr, tr), 1)
    before = jnp.where(s < t, 1.0, 0.0).astype(BF16)
    cum = _dot(cnt.astype(BF16), before) + carry_ref[...]
    ranks = [jnp.sum(jnp.where(m, cum, 0.0), axis=0, keepdims=True) for m in oh]
    rank_out[0] = _slot_rows(ranks, tr)
    carry_ref[...] += jnp.sum(cnt, axis=1, keepdims=True)
    cnt_out[...] = carry_ref[...]


def _pos_kernel(rt_ref, rank_ref, base_ref, pos_out):
    tr = rt_ref.shape[2]
    oh = _slot_onehots(rt_ref[0])
    rank = rank_ref[0]
    pos = [jnp.sum(jnp.where(m, base_ref[...], 0.0), axis=0, keepdims=True) + rank[k:k + 1, :]
           for k, m in enumerate(oh)]
    pos_out[0] = _slot_rows(pos, tr).astype(jnp.int32)


def _dispatch_kernel(base_ref, cnt_ref, cp_ref, pos0_ref, pos1_ref, hn_ref, xs_hbm, zero_ref, sem, *, tt):
    i = pl.program_id(0)
    row_copy = lambda src, dst: pltpu.make_async_copy(
        hn_ref.at[pl.ds(src, 1)], xs_hbm.at[pl.ds(dst, 1)], sem)

    @pl.when(i == 0)
    def _():
        zc = zero_ref.shape[0]
        zero_ref[...] = jnp.zeros_like(zero_ref)
        chunk = lambda c: pltpu.make_async_copy(zero_ref, xs_hbm.at[pl.ds(c * zc, zc)], sem)

        def clear(lo, hi):
            lax.fori_loop(lo, hi, lambda c, carry: (chunk(c).start(), carry)[1], 0)

        def drain(lo, hi):
            lax.fori_loop(lo, hi, lambda c, carry: (chunk(c).wait(), carry)[1], 0)

        last = N_EXPERTS - 1
        spans = [((base_ref[e] + cnt_ref[e]) // zc, (base_ref[e] + cp_ref[e]) // zc)
                 for e in range(N_EXPERTS)]
        spans.append(((base_ref[last] + cp_ref[last]) // zc, xs_hbm.shape[0] // zc))
        for lo, hi in spans:
            clear(lo, hi)
        for lo, hi in spans:
            drain(lo, hi)

    def start(t, carry):
        row_copy(t, pos0_ref[t]).start(priority=0)
        row_copy(t, pos1_ref[t]).start(priority=1)
        return carry

    lax.fori_loop(0, tt, start, 0, unroll=DMA_UNROLL)

    def wait(t, carry):
        row_copy(0, 0).wait()
        row_copy(0, 0).wait()
        return carry

    lax.fori_loop(0, tt, wait, 0, unroll=DMA_UNROLL)


def _ffn_kernel(te_ref, nu_ref, xs_ref, wg_ref, wu_ref, wd_ref, ys_ref):
    used = pl.program_id(0) < nu_ref[0]

    @pl.when(used)
    def _():
        x = _load_row_tiles(xs_ref).astype(BF16)
        a = _dot(x, wg_ref[0].astype(BF16))
        b = _dot(x, wu_ref[0].astype(BF16))
        m = (a * jax.nn.sigmoid(a)) * b
        _store_row_tiles(ys_ref, _dot(m.astype(BF16), wd_ref[0].astype(BF16)))

    @pl.when(jnp.logical_not(used))
    def _():
        ys_ref[...] = jnp.zeros_like(ys_ref)


def _combine_kernel(pos0_ref, pos1_ref, pos0_next_ref, pos1_next_ref, ys_hbm, x_ref, rt_ref, fg_ref,
                    o_ref, buf_ref, sem, *, tc):
    i = pl.program_id(0)
    slot = i % 2
    row_copy = lambda s, k, t, p: pltpu.make_async_copy(
        ys_hbm.at[pl.ds(p, 1)], buf_ref.at[s, k, pl.ds(t, 1)], sem.at[s])

    def fetch(p0_ref, p1_ref, s):
        def start(t, carry):
            row_copy(s, 0, t, p0_ref[t]).start(priority=0)
            row_copy(s, 1, t, p1_ref[t]).start(priority=1)
            return carry

        lax.fori_loop(0, tc, start, 0, unroll=DMA_UNROLL)

    @pl.when(i == 0)
    def _():
        fetch(pos0_ref, pos1_ref, 0)

    @pl.when(i + 1 < pl.num_programs(0))
    def _():
        fetch(pos0_next_ref, pos1_next_ref, 1 - slot)

    def wait(t, carry):
        row_copy(slot, 0, 0, 0).wait()
        row_copy(slot, 1, 0, 0).wait()
        return carry

    lax.fori_loop(0, tc, wait, 0, unroll=DMA_UNROLL)
    rec = rt_ref[0]
    route = jnp.concatenate([rec, jnp.zeros((LANES - SUBLANES, tc), F32)], axis=0).T
    y = (x_ref[...] + route[:, ROUTE_W0:ROUTE_W0 + 1] * _load_row_tiles(buf_ref, (slot, 0))
         + route[:, ROUTE_W0 + 1:ROUTE_W0 + 2] * _load_row_tiles(buf_ref, (slot, 1)))
    o_ref[...] = _rms(y, fg_ref[...])


def _moe(hn, route_t, x2d, w_gate, w_up, w_down, fg, tt=2048, tc=256):
    T, D = x2d.shape
    E = N_EXPERTS
    F = w_gate.shape[-1]
    tmm = MOE_ROW_TILE
    ntr, _, tr = route_t.shape
    row = lambda tm, n: pl.BlockSpec((tm, n), lambda i: (i, 0))
    rec = pl.BlockSpec((1, SUBLANES, tr), lambda i: (i, 0, 0))
    col = pl.BlockSpec((E, 1), lambda i: (0, 0))
    rank, counts = pl.pallas_call(
        _rank_kernel,
        grid=(ntr,),
        in_specs=[rec],
        out_specs=[rec, col],
        out_shape=[jax.ShapeDtypeStruct(route_t.shape, F32), jax.ShapeDtypeStruct((E, 1), F32)],
        scratch_shapes=[pltpu.VMEM((E, 1), F32)],
        compiler_params=_cparams("arbitrary"),
        name="moe_rank",
    )(route_t)

    cnt = counts[:, 0].astype(jnp.int32)
    cp = ((cnt + tmm - 1) // tmm) * tmm
    ends = jnp.cumsum(cp)
    base = ends - cp
    n_used = ends[-1] // tmm
    n_tiles = (2 * T) // tmm + E
    tile_start = jnp.minimum(jnp.arange(n_tiles, dtype=jnp.int32), n_used - 1) * tmm
    tile_expert = jnp.minimum(jnp.sum((tile_start[:, None] >= ends[None, :]).astype(jnp.int32), axis=1), E - 1)

    pos = pl.pallas_call(
        _pos_kernel,
        grid=(ntr,),
        in_specs=[rec, rec, col],
        out_specs=rec,
        out_shape=jax.ShapeDtypeStruct(route_t.shape, jnp.int32),
        compiler_params=_cparams("parallel"),
        name="moe_pos",
    )(route_t, rank, base.astype(F32).reshape(E, 1))
    pos0, pos1 = pos[:, 0, :].reshape(T), pos[:, 1, :].reshape(T)

    xs = pl.pallas_call(
        functools.partial(_dispatch_kernel, tt=tt),
        grid_spec=pltpu.PrefetchScalarGridSpec(
            num_scalar_prefetch=3,
            grid=(T // tt,),
            in_specs=[pl.BlockSpec((tt,), lambda i, b, n, c: (i,), memory_space=pltpu.SMEM),
                      pl.BlockSpec((tt,), lambda i, b, n, c: (i,), memory_space=pltpu.SMEM),
                      pl.BlockSpec((tt, SUBLANES, LANES), lambda i, b, n, c: (i, 0, 0))],
            out_specs=pl.BlockSpec(memory_space=pl.ANY),
            scratch_shapes=[pltpu.VMEM((MOE_CLEAR_ROWS, SUBLANES, LANES), F32),
                            pltpu.SemaphoreType.DMA(())]),
        out_shape=jax.ShapeDtypeStruct((n_tiles * tmm, SUBLANES, LANES), F32),
        compiler_params=_cparams("arbitrary"),
        name="moe_dispatch",
    )(base, cnt, cp, pos0, pos1, hn)

    tile = lambda r, te, nu: (jnp.minimum(r, nu[0] - 1), 0, 0)
    ys = pl.pallas_call(
        _ffn_kernel,
        grid_spec=pltpu.PrefetchScalarGridSpec(
            num_scalar_prefetch=2,
            grid=(n_tiles,),
            in_specs=[pl.BlockSpec((tmm, SUBLANES, LANES), tile),
                      pl.BlockSpec((1, D, F), lambda r, te, nu: (te[r], 0, 0)),
                      pl.BlockSpec((1, D, F), lambda r, te, nu: (te[r], 0, 0)),
                      pl.BlockSpec((1, F, D), lambda r, te, nu: (te[r], 0, 0))],
            out_specs=pl.BlockSpec((tmm, SUBLANES, LANES), lambda r, te, nu: (r, 0, 0))),
        out_shape=jax.ShapeDtypeStruct((n_tiles * tmm, SUBLANES, LANES), F32),
        compiler_params=_cparams("arbitrary"),
        name="moe_ffn",
    )(tile_expert, n_used.reshape(1), xs, w_gate.reshape(E, D, F), w_up.reshape(E, D, F),
      w_down.reshape(E, F, D))

    cur = pl.BlockSpec((tc,), lambda i: (i,), memory_space=pltpu.SMEM)
    nxt = pl.BlockSpec((tc,), lambda i: (jnp.minimum(i + 1, T // tc - 1),), memory_space=pltpu.SMEM)
    return pl.pallas_call(
        functools.partial(_combine_kernel, tc=tc),
        grid=(T // tc,),
        in_specs=[cur, cur, nxt, nxt,
                  pl.BlockSpec(memory_space=pl.ANY),
                  row(tc, D), pl.BlockSpec((1, SUBLANES, tc), lambda i: (i // (tr // tc), 0, i % (tr // tc))),
                  pl.BlockSpec((1, D), lambda i: (0, 0))],
        out_specs=row(tc, D),
        out_shape=jax.ShapeDtypeStruct((T, D), F32),
        scratch_shapes=[pltpu.VMEM((2, 2, tc, SUBLANES, LANES), F32), pltpu.SemaphoreType.DMA((2,))],
        compiler_params=_cparams("arbitrary"),
        name="moe_combine",
    )(pos0, pos1, pos0, pos1, ys, x2d, route_t, fg.reshape(1, D))


def kernel(x, mem, mix_norm_g, w_in, q_norm_g, kv_norm_g, w_uq, w_ukv, hy_conv_w, hy_conv_b, hy_w1, hy_b1, hy_freq, hy_w2, hy_b2, hy_w3, hy_b3, hy_decay, hy_skip, attn_out_g, hy_out_g, w_out, cross_norm_g, mem_norm_g, w_mq, w_mkv, w_mo, ffn_norm_g, w_route_group, b_route_group, w_route_expert, b_route_expert, w_gate, w_up, w_down, final_norm_g):
    B, S, D = x.shape
    depth = w_in.shape[0]
    consts = _dft_constants(S)
    xf = x.reshape(B * S, D)
    for l in range(depth):
        q, k, v, hx1, hx2, hv = _inproj(xf, S, mix_norm_g[l], w_in[l], q_norm_g[l], kv_norm_g[l],
                                        w_uq[l], w_ukv[l], hy_conv_w[l], hy_conv_b[l])
        HP = q.shape[1]
        a_out = _attention(q.reshape(B, S, HP), k.reshape(B, S, HP), v.reshape(B, S, HP))
        filt = _hyena_filter_time(S, hy_w1[l], hy_b1[l], hy_freq[l], hy_w2[l], hy_b2[l], hy_w3[l],
                                  hy_b3[l], hy_decay[l])
        kf = _hyena_filter_spectrum(filt, consts[1], consts[3])
        C = hv.shape[1]
        h_out = _hyena(hx1.reshape(B, S, C), hx2.reshape(B, S, C), hv.reshape(B, S, C),
                       hy_skip[l], kf, consts)
        mk, mv = _memkv(mem, mem_norm_g[l], w_mkv[l])
        x2, hn, route_t = _postmix(xf, a_out.reshape(B * S, -1), h_out.reshape(B * S, C), S,
                                attn_out_g[l], hy_out_g[l], w_out[l], cross_norm_g[l], w_mq[l], mk, mv,
                                w_mo[l], ffn_norm_g[l], w_route_group[l], b_route_group[l],
                                w_route_expert[l], b_route_expert[l])
        assert depth == 1
        xf = _moe(hn, route_t, x2, w_gate[l], w_up[l], w_down[l], final_norm_g)
    return xf.reshape(B, S, D)
```

```python
import functools
import math

import numpy as np
import jax
import jax.numpy as jnp
from jax import lax
from jax.experimental import pallas as pl
from jax.experimental.pallas import tpu as pltpu

F32 = jnp.float32
BF16 = jnp.bfloat16

EPS = 1e-6
MLA_HEADS = 8
MLA_NOPE = 64
MLA_ROPE = 32
MLA_V = 64
ROPE_BASE = 10000.0
HEAD_PAD = 128
HY_ORDER = 2
HY_DIRS = 2
HY_BANDS = 16
MEM_HEADS = 4
N_GROUPS = 4
EXPERTS_PER_GROUP = 8
N_EXPERTS = N_GROUPS * EXPERTS_PER_GROUP
ROUTE_LANES = 128
ROUTE_ID0 = 0
ROUTE_W0 = 2
MOE_ROW_TILE = 512
MOE_CLEAR_ROWS = 64
DMA_UNROLL = 8

FFT_N1 = 64
FFT_N2 = 128
DFT_K1_BLOCK = 8
DFT_N2_BLOCK = 64
DFT_PITCH_PAD = 8

VMEM_LIMIT = 56 * 1024 * 1024


def _cparams(*sem):
    return pltpu.CompilerParams(dimension_semantics=sem, vmem_limit_bytes=VMEM_LIMIT)


def _rms(x, g):
    return x * lax.rsqrt(jnp.mean(x * x, axis=-1, keepdims=True) + EPS) * g


def _dot(a, b):
    return jnp.dot(a, b, preferred_element_type=F32)


SUBLANES = 8
LANES = 128


def _load_row_tiles(ref, lead=()):
    rows = ref.shape[-3]
    flat = ref.reshape(*ref.shape[:-3], rows * SUBLANES, LANES)
    return jnp.concatenate(
        [flat[(*lead, pl.ds(j, rows, stride=SUBLANES), slice(None))] for j in range(SUBLANES)], axis=1)


def _store_row_tiles(ref, val):
    rows = ref.shape[0]
    flat = ref.reshape(rows * SUBLANES, LANES)
    for j in range(SUBLANES):
        flat[pl.ds(j, rows, stride=SUBLANES), :] = val[:, j * LANES:(j + 1) * LANES]


def _inproj_kernel(x_ref, xp_ref, xn_ref, g_ref, wq_ref, wkv_ref, wkra_ref, wkrb_ref, why_ref, qg_ref,
                   kvg_ref, wqa_ref, wqb_ref, wka_ref, wv_ref, tab_ref, cw_ref, cb_ref,
                   q_out, k_out, v_out, x1_out, x2_out, hv_out, *, nseq):
    tm = x_ref.shape[0]
    halo = xp_ref.shape[0]
    hf = _rms(jnp.concatenate([xp_ref[...], x_ref[...], xn_ref[...]], axis=0), g_ref[...])
    h = hf[halo:halo + tm].astype(BF16)
    qn = _rms(_dot(h, wq_ref[...]), qg_ref[...]).astype(BF16)
    kvn = _rms(_dot(h, wkv_ref[...]), kvg_ref[...]).astype(BF16)
    tab = tab_ref[...]
    cq, sq, ck, sk = (tab[:, j * HEAD_PAD:(j + 1) * HEAD_PAD] for j in range(4))
    tile = lambda t: jnp.concatenate([t] * MLA_HEADS, axis=1)
    q = _dot(qn, wqa_ref[...]) * tile(cq) + _dot(qn, wqb_ref[...]) * tile(sq)
    q_out[...] = q.astype(BF16)
    kr = _dot(h, wkra_ref[...]) * ck + _dot(h, wkrb_ref[...]) * sk
    k_out[...] = (_dot(kvn, wka_ref[...]) + tile(kr)).astype(BF16)
    lane = lax.broadcasted_iota(jnp.int32, (1, v_out.shape[1]), 1) % HEAD_PAD
    v_out[...] = (_dot(kvn, wv_ref[...]) + jnp.where(lane == MLA_V, 1.0, 0.0)).astype(BF16)
    hy = _dot(hf.astype(BF16), why_ref[...])
    i = pl.program_id(0) % nseq
    row = lax.broadcasted_iota(jnp.int32, hy.shape, 0)
    outside = ((row == halo - 1) & (i == 0)) | ((row == halo + tm) & (i == nseq - 1))
    hy = jnp.where(outside, 0.0, hy)
    cw = cw_ref[...]
    u = (hy[halo - 1:halo - 1 + tm] * cw[0:1] + hy[halo:halo + tm] * cw[1:2]
         + hy[halo + 1:halo + 1 + tm] * cw[2:3] + cb_ref[...])
    c = x1_out.shape[1]
    x1_out[...] = u[:, :c].astype(x1_out.dtype)
    x2_out[...] = u[:, c:2 * c].astype(x2_out.dtype)
    hv_out[...] = u[:, 2 * c:].astype(hv_out.dtype)


def _inproj(x2d, seq, mix_g, w_in, q_g, kv_g, w_uq, w_ukv, conv_w, conv_b, tm=1024):
    T, D = x2d.shape
    per = tm // SUBLANES
    cb = conv_b.reshape(1, -1)
    q_rank, kv_rank = q_g.shape[0], kv_g.shape[0]
    off_kv = q_rank
    off_kr = off_kv + kv_rank
    off_hy = off_kr + MLA_ROPE
    C = (w_in.shape[1] - off_hy) // 3
    H = MLA_HEADS
    half = MLA_ROPE // 2
    wq = w_in[:, :off_kv].astype(BF16)
    wkv = w_in[:, off_kv:off_kr].astype(BF16)
    wkr = w_in[:, off_kr:off_hy]
    wkr_sw = jnp.concatenate([wkr[:, half:], wkr[:, :half]], axis=1)
    zpad = lambda n: jnp.zeros((D, n), F32)
    wkra = jnp.concatenate([zpad(MLA_NOPE), wkr, zpad(HEAD_PAD - MLA_NOPE - MLA_ROPE)], 1).astype(BF16)
    wkrb = jnp.concatenate([zpad(MLA_NOPE), wkr_sw, zpad(HEAD_PAD - MLA_NOPE - MLA_ROPE)], 1).astype(BF16)
    why = w_in[:, off_hy:].astype(BF16)

    uq = w_uq.reshape(q_rank, H, MLA_NOPE + MLA_ROPE)
    uq_n, uq_r = uq[..., :MLA_NOPE], uq[..., MLA_NOPE:]
    uq_rs = jnp.concatenate([uq_r[..., half:], uq_r[..., :half]], axis=-1)
    zq = lambda n: jnp.zeros((q_rank, H, n), F32)
    wqa = jnp.concatenate([uq_n, uq_r, zq(HEAD_PAD - MLA_NOPE - MLA_ROPE)], -1).reshape(q_rank, H * HEAD_PAD).astype(BF16)
    wqb = jnp.concatenate([zq(MLA_NOPE), uq_rs, zq(HEAD_PAD - MLA_NOPE - MLA_ROPE)], -1).reshape(q_rank, H * HEAD_PAD).astype(BF16)
    ukv = w_ukv.reshape(kv_rank, H, MLA_NOPE + MLA_V)
    zk = lambda n: jnp.zeros((kv_rank, H, n), F32)
    wka = jnp.concatenate([ukv[..., :MLA_NOPE], zk(HEAD_PAD - MLA_NOPE)], -1).reshape(kv_rank, H * HEAD_PAD).astype(BF16)
    wv = jnp.concatenate([ukv[..., MLA_NOPE:], zk(HEAD_PAD - MLA_V)], -1).reshape(kv_rank, H * HEAD_PAD).astype(BF16)

    ang = np.arange(seq)[:, None] * ROPE_BASE ** (-np.arange(half) / half)[None, :]
    cos2 = np.concatenate([np.cos(ang), np.cos(ang)], 1)
    sin2 = np.concatenate([-np.sin(ang), np.sin(ang)], 1)
    zs = lambda n: np.zeros((seq, n))
    scale = (MLA_NOPE + MLA_ROPE) ** -0.5 * math.log2(math.e)
    rest = HEAD_PAD - MLA_NOPE - MLA_ROPE
    cq = scale * np.concatenate([np.ones((seq, MLA_NOPE)), cos2, zs(rest)], 1)
    sq = scale * np.concatenate([zs(MLA_NOPE), sin2, zs(rest)], 1)
    ck = np.concatenate([zs(MLA_NOPE), cos2, zs(rest)], 1)
    sk = np.concatenate([zs(MLA_NOPE), sin2, zs(rest)], 1)
    tab = jnp.asarray(np.concatenate([cq, sq, ck, sk], 1), dtype=F32)

    nseq = seq // tm
    full = lambda a: pl.BlockSpec(a.shape, lambda i: (0,) * a.ndim)
    row = lambda n: pl.BlockSpec((tm, n), lambda i: (i, 0))
    consts = [mix_g.reshape(1, D), wq, wkv, wkra, wkrb, why, q_g.reshape(1, -1), kv_g.reshape(1, -1),
              wqa, wqb, wka, wv]
    HP = H * HEAD_PAD
    return pl.pallas_call(
        functools.partial(_inproj_kernel, nseq=nseq),
        grid=(T // tm,),
        in_specs=[row(D),
                  pl.BlockSpec((SUBLANES, D), lambda i: (jnp.maximum(i * per - 1, 0), 0)),
                  pl.BlockSpec((SUBLANES, D), lambda i: (jnp.minimum((i + 1) * per, T // SUBLANES - 1), 0))]
        + [full(a) for a in consts]
        + [pl.BlockSpec((tm, 4 * HEAD_PAD), lambda i: (i % nseq, 0)), full(conv_w), full(cb)],
        out_specs=[row(HP), row(HP), row(HP), row(C), row(C), row(C)],
        out_shape=[jax.ShapeDtypeStruct((T, HP), BF16)] * 3 + [jax.ShapeDtypeStruct((T, C), BF16)] * 3,
        compiler_params=_cparams("parallel"),
        name="inproj",
    )(x2d, x2d, x2d, *consts, tab, conv_w, cb)


def _attn_kernel(q_ref, k_ref, v_ref, o_ref):
    outs = []
    for h in range(MLA_HEADS):
        sl = slice(h * HEAD_PAD, (h + 1) * HEAD_PAD)
        s = lax.dot_general(q_ref[0, :, sl], k_ref[0, :, sl], (((1,), (1,)), ((), ())),
                            preferred_element_type=F32).astype(BF16)
        p = jnp.exp2(s - jnp.max(s, axis=-1, keepdims=True))
        o = _dot(p, v_ref[0, :, sl])
        outs.append(o[:, :MLA_V] / o[:, MLA_V:MLA_V + 1])
    o_ref[0] = jnp.concatenate(outs, axis=1).astype(o_ref.dtype)


def _attention(q, k, v, tq=512):
    B, S, HP = q.shape
    return pl.pallas_call(
        _attn_kernel,
        grid=(B, S // tq),
        in_specs=[pl.BlockSpec((1, tq, HP), lambda b, i: (b, i, 0)),
                  pl.BlockSpec((1, S, HP), lambda b, i: (b, 0, 0)),
                  pl.BlockSpec((1, S, HP), lambda b, i: (b, 0, 0))],
        out_specs=pl.BlockSpec((1, tq, MLA_HEADS * MLA_V), lambda b, i: (b, i, 0)),
        out_shape=jax.ShapeDtypeStruct((B, S, MLA_HEADS * MLA_V), BF16),
        compiler_params=_cparams("parallel", "arbitrary"),
        name="mla_attention",
    )(q, k, v)


def _dft_constants(seq):
    n = 2 * seq
    n1, n2 = FFT_N1, FFT_N2
    assert n1 * n2 == n
    r1 = np.arange(n1)
    r2 = np.arange(n2)
    blk = lambda z: np.block([[z.real, -z.imag], [z.imag, z.real]])
    w1 = np.exp(-2j * np.pi * np.outer(r1, r1) / n1)
    fa_data = blk(w1[:, :n1 // 2])
    fa_filt = np.concatenate([w1.real, w1.imag], axis=0)
    fc = blk(np.conj(w1).T[:n1 // 2, :])
    w2 = np.exp(-2j * np.pi * np.outer(r2, r2) / n2)
    tw = np.exp(-2j * np.pi * np.outer(r1, r2) / n)
    fb = np.stack([blk(w2 * tw[k][None, :]) for k in range(n1)])
    fbi = np.stack([blk(np.conj(w2).T * np.conj(tw[k])[:, None] / n) for k in range(n1)])
    as_bf = lambda a: jnp.asarray(a, dtype=F32).astype(BF16)
    return as_bf(fa_data), as_bf(fa_filt), as_bf(fc), as_bf(fb), as_bf(fbi)


def _filter_kernel(z_ref, w1_ref, b1_ref, fr_ref, w2_ref, b2_ref, w3_ref, b3_ref, dec_ref, o_ref, *, seq, tr):
    hp = lax.Precision.HIGHEST
    z = z_ref[...]
    fr = fr_ref[...]
    def sin_rows(a):
        half, w = a.shape[0] // 2, a.shape[1]
        s = jnp.sin(jnp.concatenate([a[:half], a[half:]], axis=1))
        return jnp.concatenate([s[:, :w], s[:, w:]], axis=0)

    h = sin_rows(fr[0:1] * (jnp.dot(z, w1_ref[...], precision=hp, preferred_element_type=F32) + b1_ref[...]))
    h = sin_rows(fr[1:2] * (jnp.dot(h, w2_ref[...], precision=hp, preferred_element_type=F32) + b2_ref[...]))
    split = lambda a: (a.astype(BF16), (a - a.astype(BF16).astype(F32)).astype(BF16))
    (hh, hl), (wh, wl) = split(h), split(w3_ref[0])
    h = _dot(hh, wh) + _dot(hh, wl) + _dot(hl, wh) + b3_ref[0]
    h = h * jnp.exp(-z[:, 0:1] * jnp.abs(dec_ref[0]))
    n = pl.program_id(0) * tr + lax.broadcasted_iota(jnp.int32, h.shape, 0)
    o_ref[...] = jnp.where(n == seq, 0.0, h).astype(o_ref.dtype)


def _hyena_filter_time(seq, w1, b1, freq, w2, b2, w3, b3, decay, tr=512):
    n = 2 * seq
    emb, ffn = w1.shape
    C = w3.shape[1] // (HY_ORDER * HY_DIRS)
    off = np.arange(n)
    t = np.where(off < seq, off, n - off).astype(np.float64)
    bands = np.linspace(1e-4, HY_BANDS - 1, HY_BANDS)
    ang = 2.0 * math.pi * t[:, None] * bands[None, :] / seq
    z = np.concatenate([(t / seq)[:, None], np.cos(ang), -np.sin(ang)], axis=-1)
    zl = LANES
    z = jnp.asarray(np.pad(z, ((0, 0), (0, zl - emb))), dtype=F32)
    w1p = jnp.pad(w1, ((0, zl - emb), (0, 0)))
    by_dir = lambda a: jnp.moveaxis(a.reshape(a.shape[0], HY_ORDER, HY_DIRS, C), 2, 0).reshape(
        HY_DIRS, a.shape[0], HY_ORDER * C)
    w3d, b3d, decd = by_dir(w3), by_dir(b3.reshape(1, -1)), by_dir(decay.reshape(1, -1))
    full = lambda a: pl.BlockSpec(a.shape, lambda i: (0,) * a.ndim)
    ndir = lambda a: pl.BlockSpec((1,) + a.shape[1:], lambda i: ((i * tr) // seq, 0, 0))
    consts = [w1p, b1.reshape(1, -1), freq, w2, b2.reshape(1, -1)]
    return pl.pallas_call(
        functools.partial(_filter_kernel, seq=seq, tr=tr),
        grid=(n // tr,),
        in_specs=[pl.BlockSpec((tr, zl), lambda i: (i, 0))] + [full(a) for a in consts]
        + [ndir(w3d), ndir(b3d), ndir(decd)],
        out_specs=pl.BlockSpec((tr, HY_ORDER * C), lambda i: (i, 0)),
        out_shape=jax.ShapeDtypeStruct((n, HY_ORDER * C), BF16),
        compiler_params=_cparams("parallel"),
        name="hyena_filter_mlp",
    )(z, *consts, w3d, b3d, decd)


def _pitched(rows, nb):
    return pltpu.VMEM((rows, nb + DFT_PITCH_PAD, LANES), F32)


def _block_rows(ref):
    return math.prod(ref.shape[:-2]), ref.shape[-2]


def _copy_in(ref, scr):
    rows, nb = _block_rows(ref)
    scr[:, :nb, :] = ref[...].reshape(rows, nb, LANES).astype(scr.dtype)


def _copy_out(scr, ref):
    rows, nb = _block_rows(ref)
    ref[...] = scr[:, :nb, :].reshape(ref.shape).astype(ref.dtype)


def _at_n2(scr, n):
    rows, pitch, _ = scr.shape
    return scr.reshape(rows * pitch, LANES).at[pl.ds(n, rows, stride=pitch), :]


def _outer_dft(mat_ref, x_ref, o_ref, xs, os):
    _copy_in(x_ref, xs)
    for n in range(x_ref.shape[-2]):
        _at_n2(os, n)[...] = _dot(mat_ref[...], _at_n2(xs, n)[...].astype(BF16))
    _copy_out(os, o_ref)


def _filter_stage_a_kernel(x_ref, fa_ref, o_ref, xs, os):
    _outer_dft(fa_ref, x_ref, o_ref, xs, os)


def _filter_stage_b_kernel(x_ref, fb_ref, o_ref):
    for kk in range(x_ref.shape[1]):
        x = jnp.concatenate([x_ref[0, kk], x_ref[1, kk]], axis=0).astype(BF16)
        o_ref[kk] = _dot(fb_ref[kk], x).astype(o_ref.dtype)


def _hyena_filter_spectrum(filt, fa_filt, fb, nb=DFT_N2_BLOCK):
    n, oc = filt.shape
    a = pl.pallas_call(
        _filter_stage_a_kernel,
        grid=(FFT_N2 // nb, oc // LANES),
        in_specs=[pl.BlockSpec((FFT_N1, nb, LANES), lambda j, c: (0, j, c)),
                  pl.BlockSpec(fa_filt.shape, lambda j, c: (0, 0))],
        out_specs=pl.BlockSpec((2, FFT_N1, nb, LANES), lambda j, c: (0, 0, j, c)),
        out_shape=jax.ShapeDtypeStruct((2, FFT_N1, FFT_N2, oc), BF16),
        scratch_shapes=[_pitched(FFT_N1, nb), _pitched(2 * FFT_N1, nb)],
        compiler_params=_cparams("parallel", "parallel"),
        name="hyena_filter_dft_a",
    )(filt.reshape(FFT_N1, FFT_N2, oc), fa_filt)
    return pl.pallas_call(
        _filter_stage_b_kernel,
        grid=(FFT_N1 // DFT_K1_BLOCK,),
        in_specs=[pl.BlockSpec((2, DFT_K1_BLOCK, FFT_N2, oc), lambda k: (0, k, 0, 0)),
                  pl.BlockSpec((DFT_K1_BLOCK, 2 * FFT_N2, 2 * FFT_N2), lambda k: (k, 0, 0))],
        out_specs=pl.BlockSpec((DFT_K1_BLOCK, 2 * FFT_N2, oc), lambda k: (k, 0, 0)),
        out_shape=jax.ShapeDtypeStruct((FFT_N1, 2 * FFT_N2, oc), BF16),
        compiler_params=_cparams("parallel"),
        name="hyena_filter_dft_b",
    )(a, fb)


def _stage_a_kernel(x_ref, fa_ref, o_ref, xs, os):
    _outer_dft(fa_ref, x_ref, o_ref, xs, os)


def _stage_a(x4, fa, nb):
    B, r, n2, C = x4.shape
    return pl.pallas_call(
        _stage_a_kernel,
        grid=(B // 2, n2 // nb, C // LANES),
        in_specs=[pl.BlockSpec((2, r, nb, LANES), lambda p, j, c: (p, 0, j, c)),
                  pl.BlockSpec(fa.shape, lambda p, j, c: (0, 0))],
        out_specs=pl.BlockSpec((1, 2, FFT_N1, nb, LANES), lambda p, j, c: (p, 0, 0, j, c)),
        out_shape=jax.ShapeDtypeStruct((B // 2, 2, FFT_N1, n2, C), BF16),
        scratch_shapes=[_pitched(2 * r, nb), _pitched(2 * FFT_N1, nb)],
        compiler_params=_cparams("parallel", "parallel", "parallel"),
        name="hyena_dft_a",
    )(x4, fa)


def _stage_b_kernel(x_ref, fb_ref, kf_ref, fbi_ref, o_ref):
    npair = x_ref.shape[0]
    n2 = x_ref.shape[3]
    c = x_ref.shape[4]
    for kk in range(x_ref.shape[2]):
        x = jnp.concatenate(
            [jnp.concatenate([x_ref[p, 0, kk], x_ref[p, 1, kk]], axis=0) for p in range(npair)],
            axis=1).astype(BF16)
        g = _dot(fb_ref[kk], x)
        gr, gi = g[:n2], g[n2:]
        kf = kf_ref[kk].astype(F32)
        kr = jnp.concatenate([kf[:n2]] * npair, axis=1)
        ki = jnp.concatenate([kf[n2:]] * npair, axis=1)
        hcat = jnp.concatenate([gr * kr - gi * ki, gr * ki + gi * kr], axis=0).astype(BF16)
        y = _dot(fbi_ref[kk], hcat)
        for p in range(npair):
            o_ref[p, 0, kk] = y[:n2, p * c:(p + 1) * c].astype(o_ref.dtype)
            o_ref[p, 1, kk] = y[n2:, p * c:(p + 1) * c].astype(o_ref.dtype)


def _stage_b(spec, fb, kf, fbi, order, nk=DFT_K1_BLOCK):
    npair, _, _, _, C = spec.shape
    blk = pl.BlockSpec((npair, 2, nk, FFT_N2, C), lambda k: (0, 0, k, 0, 0))
    mat = pl.BlockSpec((nk, 2 * FFT_N2, 2 * FFT_N2), lambda k: (k, 0, 0))
    return pl.pallas_call(
        _stage_b_kernel,
        grid=(FFT_N1 // nk,),
        in_specs=[blk, mat, pl.BlockSpec((nk, 2 * FFT_N2, C), lambda k: (k, 0, order)), mat],
        out_specs=blk,
        out_shape=jax.ShapeDtypeStruct(spec.shape, BF16),
        compiler_params=_cparams("parallel"),
        name="hyena_dft_b",
    )(spec, fb, kf, fbi)


def _stage_c_kernel(y_ref, fc_ref, gate_ref, z_ref, skip_ref, *rest, stage_a):
    if stage_a:
        fa_ref, z_out, a_out, ys, gs, zs, os, as_ = rest
    else:
        z_out, ys, gs, zs, os = rest
    nb = gate_ref.shape[-2]
    _copy_in(y_ref, ys)
    _copy_in(gate_ref, gs)
    _copy_in(z_ref, zs)
    skip = skip_ref[...]
    for n in range(nb):
        conv = _dot(fc_ref[...], _at_n2(ys, n)[...].astype(BF16))
        _at_n2(os, n)[...] = _at_n2(gs, n)[...] * (conv + skip * _at_n2(zs, n)[...])
    _copy_out(os, z_out)
    if stage_a:
        for n in range(nb):
            _at_n2(as_, n)[...] = _dot(fa_ref[...], _at_n2(os, n)[...].astype(BF16))
        _copy_out(as_, a_out)


def _stage_c(yspec, fc, gate, zin, skip, fa=None, nb=DFT_N2_BLOCK):
    B, r, n2, C = gate.shape
    dat = pl.BlockSpec((2, r, nb, LANES), lambda p, j, c: (p, 0, j, c))
    spc = pl.BlockSpec((1, 2, FFT_N1, nb, LANES), lambda p, j, c: (p, 0, 0, j, c))
    in_specs = [spc, pl.BlockSpec(fc.shape, lambda p, j, c: (0, 0)), dat, dat,
                pl.BlockSpec((1, LANES), lambda p, j, c: (0, c))]
    out_specs = [dat]
    out_shape = [jax.ShapeDtypeStruct(gate.shape, BF16)]
    args = [yspec, fc, gate, zin, skip.reshape(1, C)]
    scratch = [_pitched(2 * FFT_N1, nb)] + [_pitched(2 * r, nb)] * 3
    if fa is not None:
        in_specs.append(pl.BlockSpec(fa.shape, lambda p, j, c: (0, 0)))
        out_specs.append(spc)
        out_shape.append(jax.ShapeDtypeStruct(yspec.shape, BF16))
        args.append(fa)
        scratch.append(_pitched(2 * FFT_N1, nb))
    return pl.pallas_call(
        functools.partial(_stage_c_kernel, stage_a=fa is not None),
        grid=(B // 2, n2 // nb, C // LANES),
        in_specs=in_specs, out_specs=out_specs, out_shape=out_shape,
        scratch_shapes=scratch,
        compiler_params=_cparams("parallel", "parallel", "parallel"),
        name="hyena_dft_c",
    )(*args)


def _hyena(x1, x2, v, skip, kf, consts, nb=DFT_N2_BLOCK):
    fa_data, _, fc, fb, fbi = consts
    B, S, C = v.shape
    split = lambda a: a.reshape(B, S // FFT_N2, FFT_N2, C)
    a0 = _stage_a(split(v), fa_data, nb)
    y0 = _stage_b(a0, fb, kf, fbi, 0)
    z1, a1 = _stage_c(y0, fc, split(x1), split(v), skip[0], fa=fa_data, nb=nb)
    y1 = _stage_b(a1, fb, kf, fbi, 1)
    (out,) = _stage_c(y1, fc, split(x2), z1, skip[1], nb=nb)
    return out.reshape(B, S, C)


def _memkv_kernel(m_ref, g_ref, w_ref, k_out, v_out):
    hm = _rms(m_ref[0], g_ref[...]).astype(BF16)
    kv = _dot(hm, w_ref[...])
    d = k_out.shape[2]
    k_out[0] = kv[:, :d].astype(BF16)
    v_out[0] = kv[:, d:].astype(BF16)


def _memkv(mem, g, w_mkv):
    B, M, D = mem.shape
    dk = w_mkv.shape[1] // 2
    w = w_mkv.astype(BF16)
    return pl.pallas_call(
        _memkv_kernel,
        grid=(B,),
        in_specs=[pl.BlockSpec((1, M, D), lambda b: (b, 0, 0)),
                  pl.BlockSpec((1, D), lambda b: (0, 0)),
                  pl.BlockSpec(w.shape, lambda b: (0, 0))],
        out_specs=[pl.BlockSpec((1, M, dk), lambda b: (b, 0, 0))] * 2,
        out_shape=[jax.ShapeDtypeStruct((B, M, dk), BF16)] * 2,
        compiler_params=_cparams("parallel"),
        name="mem_kv",
    )(mem, g.reshape(1, D), w)


def _route(lt):
    tokens = lt.shape[1]
    ninf = -jnp.inf
    col_max = lambda a: jnp.max(a, axis=0, keepdims=True)
    first = lambda mask, idx, n: jnp.min(jnp.where(mask, idx, float(n)), axis=0, keepdims=True)
    gl = lt[N_EXPERTS:N_EXPERTS + N_GROUPS]
    gi = lax.broadcasted_iota(jnp.int32, gl.shape, 0).astype(F32)
    gmax = col_max(gl)
    g_idx = first(gl == gmax, gi, N_GROUPS)
    p_group = 1.0 / jnp.sum(jnp.exp(gl - gmax), axis=0, keepdims=True)
    ei = lax.broadcasted_iota(jnp.int32, (N_EXPERTS, tokens), 0)
    in_g = (ei // EXPERTS_PER_GROUP).astype(F32) == g_idx
    ei = ei.astype(F32)
    el = jnp.where(in_g, lt[:N_EXPERTS], ninf)
    v1 = col_max(el)
    i1 = first(el == v1, ei, N_EXPERTS)
    el2 = jnp.where(ei == i1, ninf, el)
    v2 = col_max(el2)
    i2 = first(el2 == v2, ei, N_EXPERTS)
    e2 = jnp.exp(v2 - v1)
    p1 = 1.0 / (1.0 + e2)
    p2 = e2 / (1.0 + e2)
    sub = lax.broadcasted_iota(jnp.int32, (SUBLANES, tokens), 0)
    sel = lambda n, val: jnp.where(sub == n, val, 0.0)
    return (sel(ROUTE_ID0, i1) + sel(ROUTE_ID0 + 1, i2)
            + sel(ROUTE_W0, p_group * p1) + sel(ROUTE_W0 + 1, p_group * p2))


def _postmix_kernel(x_ref, a_ref, hy_ref, ag_ref, hg_ref, woa_ref, woh_ref, cg_ref, wmq_ref,
                    mk_ref, mv_ref, wmo_ref, fg_ref, wr_ref, br_ref, x_out, hn_out, route_t_out):
    ra = _rms(a_ref[...].astype(F32), ag_ref[...]).astype(BF16)
    rh = _rms(hy_ref[...].astype(F32), hg_ref[...]).astype(BF16)
    x = x_ref[...] + _dot(ra, woa_ref[...]) + _dot(rh, woh_ref[...])
    q = _dot(_rms(x, cg_ref[...]).astype(BF16), wmq_ref[...])
    dh = q.shape[1] // MEM_HEADS
    outs = []
    for h in range(MEM_HEADS):
        sl = slice(h * dh, (h + 1) * dh)
        s = lax.dot_general(q[:, sl].astype(BF16), mk_ref[0, :, sl], (((1,), (1,)), ((), ())),
                            preferred_element_type=F32) * dh ** -0.5
        p = jnp.exp(s - jnp.max(s, axis=-1, keepdims=True))
        l = jnp.sum(p, axis=-1, keepdims=True)
        outs.append(_dot(p.astype(BF16), mv_ref[0, :, sl]) / l)
    o = jnp.concatenate(outs, axis=1).astype(BF16)
    x = x + _dot(o, wmo_ref[...])
    x_out[...] = x
    hn = _rms(x, fg_ref[...])
    _store_row_tiles(hn_out, hn)
    logits = _dot(hn.astype(BF16), wr_ref[...]) + br_ref[...]
    route_t_out[0] = _route(logits.T)


def _postmix(x2d, a2d, hy2d, seq, ag, hg, w_out, cg, w_mq, mk, mv, w_mo, fg, w_rg, b_rg, w_re, b_re, tm=1024):
    T, D = x2d.shape
    ca = a2d.shape[1]
    woa = w_out[:ca].astype(BF16)
    woh = w_out[ca:].astype(BF16)
    pad = ROUTE_LANES - N_EXPERTS - N_GROUPS
    wr = jnp.concatenate([w_re, w_rg, jnp.zeros((D, pad), F32)], 1).astype(BF16)
    br = jnp.concatenate([b_re, b_rg, jnp.zeros((pad,), F32)]).reshape(1, ROUTE_LANES)
    nseq = seq // tm
    full = lambda a: pl.BlockSpec(a.shape, lambda i: (0,) * a.ndim)
    row = lambda n: pl.BlockSpec((tm, n), lambda i: (i, 0))
    memb = pl.BlockSpec((1,) + mk.shape[1:], lambda i: (i // nseq, 0, 0))
    args = [x2d, a2d, hy2d, ag.reshape(1, -1), hg.reshape(1, -1), woa, woh, cg.reshape(1, D),
            w_mq.astype(BF16), mk, mv, w_mo.astype(BF16), fg.reshape(1, D), wr, br]
    in_specs = [row(D), row(ca), row(hy2d.shape[1])] + [full(a) for a in args[3:9]] + [memb, memb] \
        + [full(a) for a in args[11:]]
    return pl.pallas_call(
        _postmix_kernel,
        grid=(T // tm,),
        in_specs=in_specs,
        out_specs=[row(D), pl.BlockSpec((tm, SUBLANES, LANES), lambda i: (i, 0, 0)),
                   pl.BlockSpec((1, SUBLANES, tm), lambda i: (i, 0, 0))],
        out_shape=[jax.ShapeDtypeStruct((T, D), F32), jax.ShapeDtypeStruct((T, SUBLANES, LANES), F32),
                   jax.ShapeDtypeStruct((T // tm, SUBLANES, tm), F32)],
        compiler_params=_cparams("parallel"),
        name="postmix",
    )(*args)


def _slot_onehots(rt):
    e = lax.broadcasted_iota(jnp.int32, (N_EXPERTS, rt.shape[1]), 0).astype(F32)
    return [e == rt[ROUTE_ID0 + k:ROUTE_ID0 + k + 1, :] for k in range(2)]


def _slot_rows(rows, width):
    sub = lax.broadcasted_iota(jnp.int32, (SUBLANES, width), 0)
    return jnp.where(sub == 0, rows[0], jnp.where(sub == 1, rows[1], 0.0))


def _rank_kernel(rt_ref, rank_out, cnt_out, carry_ref):
    @pl.when(pl.program_id(0) == 0)
    def _():
        carry_ref[...] = jnp.zeros_like(carry_ref)

    tr = rt_ref.shape[2]
    oh = _slot_onehots(rt_ref[0])
    cnt = jnp.where(oh[0] | oh[1], 1.0, 0.0)
    s = lax.broadcasted_iota(jnp.int32, (tr, tr), 0)
    t = lax.broadcasted_iota(jnp.int32, (tr, tr), 1)
    before = jnp.where(s < t, 1.0, 0.0).astype(BF16)
    cum = _dot(cnt.astype(BF16), before) + carry_ref[...]
    ranks = [jnp.sum(jnp.where(m, cum, 0.0), axis=0, keepdims=True) for m in oh]
    rank_out[0] = _slot_rows(ranks, tr)
    carry_ref[...] += jnp.sum(cnt, axis=1, keepdims=True)
    cnt_out[...] = carry_ref[...]


def _pos_kernel(rt_ref, rank_ref, base_ref, pos_out):
    tr = rt_ref.shape[2]
    oh = _slot_onehots(rt_ref[0])
    rank = rank_ref[0]
    pos = [jnp.sum(jnp.where(m, base_ref[...], 0.0), axis=0, keepdims=True) + rank[k:k + 1, :]
           for k, m in enumerate(oh)]
    pos_out[0] = _slot_rows(pos, tr).astype(jnp.int32)


def _dispatch_kernel(base_ref, cnt_ref, cp_ref, pos0_ref, pos1_ref, hn_ref, xs_hbm, zero_ref, sem, *, tt):
    i = pl.program_id(0)
    row_copy = lambda src, dst: pltpu.make_async_copy(
        hn_ref.at[pl.ds(src, 1)], xs_hbm.at[pl.ds(dst, 1)], sem)

    @pl.when(i == 0)
    def _():
        zc = zero_ref.shape[0]
        zero_ref[...] = jnp.zeros_like(zero_ref)
        chunk = lambda c: pltpu.make_async_copy(zero_ref, xs_hbm.at[pl.ds(c * zc, zc)], sem)

        def clear(lo, hi):
            lax.fori_loop(lo, hi, lambda c, carry: (chunk(c).start(), carry)[1], 0)

        def drain(lo, hi):
            lax.fori_loop(lo, hi, lambda c, carry: (chunk(c).wait(), carry)[1], 0)

        last = N_EXPERTS - 1
        spans = [((base_ref[e] + cnt_ref[e]) // zc, (base_ref[e] + cp_ref[e]) // zc)
                 for e in range(N_EXPERTS)]
        spans.append(((base_ref[last] + cp_ref[last]) // zc, xs_hbm.shape[0] // zc))
        for lo, hi in spans:
            clear(lo, hi)
        for lo, hi in spans:
            drain(lo, hi)

    def start(t, carry):
        row_copy(t, pos0_ref[t]).start(priority=0)
        row_copy(t, pos1_ref[t]).start(priority=1)
        return carry

    lax.fori_loop(0, tt, start, 0, unroll=DMA_UNROLL)

    def wait(t, carry):
        row_copy(0, 0).wait()
        row_copy(0, 0).wait()
        return carry

    lax.fori_loop(0, tt, wait, 0, unroll=DMA_UNROLL)


def _ffn_kernel(te_ref, nu_ref, xs_ref, wg_ref, wu_ref, wd_ref, ys_ref):
    used = pl.program_id(0) < nu_ref[0]

    @pl.when(used)
    def _():
        x = _load_row_tiles(xs_ref).astype(BF16)
        a = _dot(x, wg_ref[0].astype(BF16))
        b = _dot(x, wu_ref[0].astype(BF16))
        m = (a * jax.nn.sigmoid(a)) * b
        _store_row_tiles(ys_ref, _dot(m.astype(BF16), wd_ref[0].astype(BF16)))

    @pl.when(jnp.logical_not(used))
    def _():
        ys_ref[...] = jnp.zeros_like(ys_ref)


def _combine_kernel(pos0_ref, pos1_ref, pos0_next_ref, pos1_next_ref, ys_hbm, x_ref, rt_ref, fg_ref,
                    o_ref, buf_ref, sem, *, tc):
    i = pl.program_id(0)
    slot = i % 2
    row_copy = lambda s, k, t, p: pltpu.make_async_copy(
        ys_hbm.at[pl.ds(p, 1)], buf_ref.at[s, k, pl.ds(t, 1)], sem.at[s])

    def fetch(p0_ref, p1_ref, s):
        def start(t, carry):
            row_copy(s, 0, t, p0_ref[t]).start(priority=0)
            row_copy(s, 1, t, p1_ref[t]).start(priority=1)
            return carry

        lax.fori_loop(0, tc, start, 0, unroll=DMA_UNROLL)

    @pl.when(i == 0)
    def _():
        fetch(pos0_ref, pos1_ref, 0)

    @pl.when(i + 1 < pl.num_programs(0))
    def _():
        fetch(pos0_next_ref, pos1_next_ref, 1 - slot)

    rec = rt_ref[0]
    route = jnp.concatenate([rec, jnp.zeros((LANES - SUBLANES, tc), F32)], axis=0).T

    def wait(t, carry):
        row_copy(slot, 0, 0, 0).wait()
        row_copy(slot, 1, 0, 0).wait()
        return carry

    lax.fori_loop(0, tc, wait, 0, unroll=DMA_UNROLL)
    y = (x_ref[...] + route[:, ROUTE_W0:ROUTE_W0 + 1] * _load_row_tiles(buf_ref, (slot, 0))
         + route[:, ROUTE_W0 + 1:ROUTE_W0 + 2] * _load_row_tiles(buf_ref, (slot, 1)))
    o_ref[...] = _rms(y, fg_ref[...])


def _moe(hn, route_t, x2d, w_gate, w_up, w_down, fg, tt=2048, tc=256):
    T, D = x2d.shape
    E = N_EXPERTS
    F = w_gate.shape[-1]
    tmm = MOE_ROW_TILE
    ntr, _, tr = route_t.shape
    row = lambda tm, n: pl.BlockSpec((tm, n), lambda i: (i, 0))
    rec = pl.BlockSpec((1, SUBLANES, tr), lambda i: (i, 0, 0))
    col = pl.BlockSpec((E, 1), lambda i: (0, 0))
    rank, counts = pl.pallas_call(
        _rank_kernel,
        grid=(ntr,),
        in_specs=[rec],
        out_specs=[rec, col],
        out_shape=[jax.ShapeDtypeStruct(route_t.shape, F32), jax.ShapeDtypeStruct((E, 1), F32)],
        scratch_shapes=[pltpu.VMEM((E, 1), F32)],
        compiler_params=_cparams("arbitrary"),
        name="moe_rank",
    )(route_t)

    cnt = counts[:, 0].astype(jnp.int32)
    cp = ((cnt + tmm - 1) // tmm) * tmm
    ends = jnp.cumsum(cp)
    base = ends - cp
    n_used = ends[-1] // tmm
    n_tiles = (2 * T) // tmm + E
    tile_start = jnp.minimum(jnp.arange(n_tiles, dtype=jnp.int32), n_used - 1) * tmm
    tile_expert = jnp.minimum(jnp.sum((tile_start[:, None] >= ends[None, :]).astype(jnp.int32), axis=1), E - 1)

    pos = pl.pallas_call(
        _pos_kernel,
        grid=(ntr,),
        in_specs=[rec, rec, col],
        out_specs=rec,
        out_shape=jax.ShapeDtypeStruct(route_t.shape, jnp.int32),
        compiler_params=_cparams("parallel"),
        name="moe_pos",
    )(route_t, rank, base.astype(F32).reshape(E, 1))
    pos0, pos1 = pos[:, 0, :].reshape(T), pos[:, 1, :].reshape(T)

    xs = pl.pallas_call(
        functools.partial(_dispatch_kernel, tt=tt),
        grid_spec=pltpu.PrefetchScalarGridSpec(
            num_scalar_prefetch=3,
            grid=(T // tt,),
            in_specs=[pl.BlockSpec((tt,), lambda i, b, n, c: (i,), memory_space=pltpu.SMEM),
                      pl.BlockSpec((tt,), lambda i, b, n, c: (i,), memory_space=pltpu.SMEM),
                      pl.BlockSpec((tt, SUBLANES, LANES), lambda i, b, n, c: (i, 0, 0))],
            out_specs=pl.BlockSpec(memory_space=pl.ANY),
            scratch_shapes=[pltpu.VMEM((MOE_CLEAR_ROWS, SUBLANES, LANES), F32),
                            pltpu.SemaphoreType.DMA(())]),
        out_shape=jax.ShapeDtypeStruct((n_tiles * tmm, SUBLANES, LANES), F32),
        compiler_params=_cparams("arbitrary"),
        name="moe_dispatch",
    )(base, cnt, cp, pos0, pos1, hn)

    tile = lambda r, te, nu: (jnp.minimum(r, nu[0] - 1), 0, 0)
    ys = pl.pallas_call(
        _ffn_kernel,
        grid_spec=pltpu.PrefetchScalarGridSpec(
            num_scalar_prefetch=2,
            grid=(n_tiles,),
            in_specs=[pl.BlockSpec((tmm, SUBLANES, LANES), tile),
                      pl.BlockSpec((1, D, F), lambda r, te, nu: (te[r], 0, 0)),
                      pl.BlockSpec((1, D, F), lambda r, te, nu: (te[r], 0, 0)),
                      pl.BlockSpec((1, F, D), lambda r, te, nu: (te[r], 0, 0))],
            out_specs=pl.BlockSpec((tmm, SUBLANES, LANES), lambda r, te, nu: (r, 0, 0))),
        out_shape=jax.ShapeDtypeStruct((n_tiles * tmm, SUBLANES, LANES), F32),
        compiler_params=_cparams("arbitrary"),
        name="moe_ffn",
    )(tile_expert, n_used.reshape(1), xs, w_gate.reshape(E, D, F), w_up.reshape(E, D, F),
      w_down.reshape(E, F, D))

    cur = pl.BlockSpec((tc,), lambda i: (i,), memory_space=pltpu.SMEM)
    nxt = pl.BlockSpec((tc,), lambda i: (jnp.minimum(i + 1, T // tc - 1),), memory_space=pltpu.SMEM)
    return pl.pallas_call(
        functools.partial(_combine_kernel, tc=tc),
        grid=(T // tc,),
        in_specs=[cur, cur, nxt, nxt,
                  pl.BlockSpec(memory_space=pl.ANY),
                  row(tc, D), pl.BlockSpec((1, SUBLANES, tc), lambda i: (i // (tr // tc), 0, i % (tr // tc))),
                  pl.BlockSpec((1, D), lambda i: (0, 0))],
        out_specs=row(tc, D),
        out_shape=jax.ShapeDtypeStruct((T, D), F32),
        scratch_shapes=[pltpu.VMEM((2, 2, tc, SUBLANES, LANES), F32), pltpu.SemaphoreType.DMA((2,))],
        compiler_params=_cparams("arbitrary"),
        name="moe_combine",
    )(pos0, pos1, pos0, pos1, ys, x2d, route_t, fg.reshape(1, D))


def kernel(x, mem, mix_norm_g, w_in, q_norm_g, kv_norm_g, w_uq, w_ukv, hy_conv_w, hy_conv_b, hy_w1, hy_b1, hy_freq, hy_w2, hy_b2, hy_w3, hy_b3, hy_decay, hy_skip, attn_out_g, hy_out_g, w_out, cross_norm_g, mem_norm_g, w_mq, w_mkv, w_mo, ffn_norm_g, w_route_group, b_route_group, w_route_expert, b_route_expert, w_gate, w_up, w_down, final_norm_g):
    B, S, D = x.shape
    depth = w_in.shape[0]
    consts = _dft_constants(S)
    xf = x.reshape(B * S, D)
    for l in range(depth):
        q, k, v, hx1, hx2, hv = _inproj(xf, S, mix_norm_g[l], w_in[l], q_norm_g[l], kv_norm_g[l],
                                        w_uq[l], w_ukv[l], hy_conv_w[l], hy_conv_b[l])
        HP = q.shape[1]
        a_out = _attention(q.reshape(B, S, HP), k.reshape(B, S, HP), v.reshape(B, S, HP))
        filt = _hyena_filter_time(S, hy_w1[l], hy_b1[l], hy_freq[l], hy_w2[l], hy_b2[l], hy_w3[l],
                                  hy_b3[l], hy_decay[l])
        kf = _hyena_filter_spectrum(filt, consts[1], consts[3])
        C = hv.shape[1]
        h_out = _hyena(hx1.reshape(B, S, C), hx2.reshape(B, S, C), hv.reshape(B, S, C),
                       hy_skip[l], kf, consts)
        mk, mv = _memkv(mem, mem_norm_g[l], w_mkv[l])
        x2, hn, route_t = _postmix(xf, a_out.reshape(B * S, -1), h_out.reshape(B * S, C), S,
                                attn_out_g[l], hy_out_g[l], w_out[l], cross_norm_g[l], w_mq[l], mk, mv,
                                w_mo[l], ffn_norm_g[l], w_route_group[l], b_route_group[l],
                                w_route_expert[l], b_route_expert[l])
        assert depth == 1
        xf = _moe(hn, route_t, x2, w_gate[l], w_up[l], w_down[l], final_norm_g)
    return xf.reshape(B, S, D)
```

```python
import functools
import math

import numpy as np
import jax
import jax.numpy as jnp
from jax import lax
from jax.experimental import pallas as pl
from jax.experimental.pallas import tpu as pltpu

F32 = jnp.float32
BF16 = jnp.bfloat16

EPS = 1e-6
MLA_HEADS = 8
MLA_NOPE = 64
MLA_ROPE = 32
MLA_V = 64
ROPE_BASE = 10000.0
HEAD_PAD = 128
HY_ORDER = 2
HY_DIRS = 2
HY_BANDS = 16
MEM_HEADS = 4
N_GROUPS = 4
EXPERTS_PER_GROUP = 8
N_EXPERTS = N_GROUPS * EXPERTS_PER_GROUP
ROUTE_LANES = 128
ROUTE_ID0 = 0
ROUTE_W0 = 2
MOE_ROW_TILE = 512
MOE_CLEAR_ROWS = 64
DMA_UNROLL = 8

FFT_N1 = 64
FFT_N2 = 128
DFT_K1_BLOCK = 8
DFT_N2_BLOCK = 64
DFT_PITCH_PAD = 8

VMEM_LIMIT = 56 * 1024 * 1024


def _cparams(*sem):
    return pltpu.CompilerParams(dimension_semantics=sem, vmem_limit_bytes=VMEM_LIMIT)


def _rms(x, g):
    return x * lax.rsqrt(jnp.mean(x * x, axis=-1, keepdims=True) + EPS) * g


def _dot(a, b):
    return jnp.dot(a, b, preferred_element_type=F32)


SUBLANES = 8
LANES = 128


def _load_row_tiles(ref, lead=()):
    rows = ref.shape[-3]
    flat = ref.reshape(*ref.shape[:-3], rows * SUBLANES, LANES)
    return jnp.concatenate(
        [flat[(*lead, pl.ds(j, rows, stride=SUBLANES), slice(None))] for j in range(SUBLANES)], axis=1)


def _store_row_tiles(ref, val):
    rows = ref.shape[0]
    flat = ref.reshape(rows * SUBLANES, LANES)
    for j in range(SUBLANES):
        flat[pl.ds(j, rows, stride=SUBLANES), :] = val[:, j * LANES:(j + 1) * LANES]


def _inproj_kernel(x_ref, xp_ref, xn_ref, g_ref, wq_ref, wkv_ref, wkra_ref, wkrb_ref, why_ref, qg_ref,
                   kvg_ref, wqa_ref, wqb_ref, wka_ref, wv_ref, tab_ref, cw_ref, cb_ref,
                   q_out, k_out, v_out, x1_out, x2_out, hv_out, *, nseq):
    tm = x_ref.shape[0]
    halo = xp_ref.shape[0]
    hf = _rms(jnp.concatenate([xp_ref[...], x_ref[...], xn_ref[...]], axis=0), g_ref[...])
    h = hf[halo:halo + tm].astype(BF16)
    qn = _rms(_dot(h, wq_ref[...]), qg_ref[...]).astype(BF16)
    kvn = _rms(_dot(h, wkv_ref[...]), kvg_ref[...]).astype(BF16)
    tab = tab_ref[...]
    cq, sq, ck, sk = (tab[:, j * HEAD_PAD:(j + 1) * HEAD_PAD] for j in range(4))
    tile = lambda t: jnp.concatenate([t] * MLA_HEADS, axis=1)
    q = _dot(qn, wqa_ref[...]) * tile(cq) + _dot(qn, wqb_ref[...]) * tile(sq)
    q_out[...] = q.astype(BF16)
    kr = _dot(h, wkra_ref[...]) * ck + _dot(h, wkrb_ref[...]) * sk
    k_out[...] = (_dot(kvn, wka_ref[...]) + tile(kr)).astype(BF16)
    lane = lax.broadcasted_iota(jnp.int32, (1, v_out.shape[1]), 1) % HEAD_PAD
    v_out[...] = (_dot(kvn, wv_ref[...]) + jnp.where(lane == MLA_V, 1.0, 0.0)).astype(BF16)
    hy = _dot(hf.astype(BF16), why_ref[...])
    i = pl.program_id(0) % nseq
    row = lax.broadcasted_iota(jnp.int32, hy.shape, 0)
    outside = ((row == halo - 1) & (i == 0)) | ((row == halo + tm) & (i == nseq - 1))
    hy = jnp.where(outside, 0.0, hy)
    cw = cw_ref[...]
    u = (hy[halo - 1:halo - 1 + tm] * cw[0:1] + hy[halo:halo + tm] * cw[1:2]
         + hy[halo + 1:halo + 1 + tm] * cw[2:3] + cb_ref[...])
    c = x1_out.shape[1]
    x1_out[...] = u[:, :c].astype(x1_out.dtype)
    x2_out[...] = u[:, c:2 * c].astype(x2_out.dtype)
    hv_out[...] = u[:, 2 * c:].astype(hv_out.dtype)


def _inproj(x2d, seq, mix_g, w_in, q_g, kv_g, w_uq, w_ukv, conv_w, conv_b, tm=1024):
    T, D = x2d.shape
    per = tm // SUBLANES
    cb = conv_b.reshape(1, -1)
    q_rank, kv_rank = q_g.shape[0], kv_g.shape[0]
    off_kv = q_rank
    off_kr = off_kv + kv_rank
    off_hy = off_kr + MLA_ROPE
    C = (w_in.shape[1] - off_hy) // 3
    H = MLA_HEADS
    half = MLA_ROPE // 2
    wq = w_in[:, :off_kv].astype(BF16)
    wkv = w_in[:, off_kv:off_kr].astype(BF16)
    wkr = w_in[:, off_kr:off_hy]
    wkr_sw = jnp.concatenate([wkr[:, half:], wkr[:, :half]], axis=1)
    zpad = lambda n: jnp.zeros((D, n), F32)
    wkra = jnp.concatenate([zpad(MLA_NOPE), wkr, zpad(HEAD_PAD - MLA_NOPE - MLA_ROPE)], 1).astype(BF16)
    wkrb = jnp.concatenate([zpad(MLA_NOPE), wkr_sw, zpad(HEAD_PAD - MLA_NOPE - MLA_ROPE)], 1).astype(BF16)
    why = w_in[:, off_hy:].astype(BF16)

    uq = w_uq.reshape(q_rank, H, MLA_NOPE + MLA_ROPE)
    uq_n, uq_r = uq[..., :MLA_NOPE], uq[..., MLA_NOPE:]
    uq_rs = jnp.concatenate([uq_r[..., half:], uq_r[..., :half]], axis=-1)
    zq = lambda n: jnp.zeros((q_rank, H, n), F32)
    wqa = jnp.concatenate([uq_n, uq_r, zq(HEAD_PAD - MLA_NOPE - MLA_ROPE)], -1).reshape(q_rank, H * HEAD_PAD).astype(BF16)
    wqb = jnp.concatenate([zq(MLA_NOPE), uq_rs, zq(HEAD_PAD - MLA_NOPE - MLA_ROPE)], -1).reshape(q_rank, H * HEAD_PAD).astype(BF16)
    ukv = w_ukv.reshape(kv_rank, H, MLA_NOPE + MLA_V)
    zk = lambda n: jnp.zeros((kv_rank, H, n), F32)
    wka = jnp.concatenate([ukv[..., :MLA_NOPE], zk(HEAD_PAD - MLA_NOPE)], -1).reshape(kv_rank, H * HEAD_PAD).astype(BF16)
    wv = jnp.concatenate([ukv[..., MLA_NOPE:], zk(HEAD_PAD - MLA_V)], -1).reshape(kv_rank, H * HEAD_PAD).astype(BF16)

    ang = np.arange(seq)[:, None] * ROPE_BASE ** (-np.arange(half) / half)[None, :]
    cos2 = np.concatenate([np.cos(ang), np.cos(ang)], 1)
    sin2 = np.concatenate([-np.sin(ang), np.sin(ang)], 1)
    zs = lambda n: np.zeros((seq, n))
    scale = (MLA_NOPE + MLA_ROPE) ** -0.5 * math.log2(math.e)
    rest = HEAD_PAD - MLA_NOPE - MLA_ROPE
    cq = scale * np.concatenate([np.ones((seq, MLA_NOPE)), cos2, zs(rest)], 1)
    sq = scale * np.concatenate([zs(MLA_NOPE), sin2, zs(rest)], 1)
    ck = np.concatenate([zs(MLA_NOPE), cos2, zs(rest)], 1)
    sk = np.concatenate([zs(MLA_NOPE), sin2, zs(rest)], 1)
    tab = jnp.asarray(np.concatenate([cq, sq, ck, sk], 1), dtype=F32)

    nseq = seq // tm
    full = lambda a: pl.BlockSpec(a.shape, lambda i: (0,) * a.ndim)
    row = lambda n: pl.BlockSpec((tm, n), lambda i: (i, 0))
    consts = [mix_g.reshape(1, D), wq, wkv, wkra, wkrb, why, q_g.reshape(1, -1), kv_g.reshape(1, -1),
              wqa, wqb, wka, wv]
    HP = H * HEAD_PAD
    return pl.pallas_call(
        functools.partial(_inproj_kernel, nseq=nseq),
        grid=(T // tm,),
        in_specs=[row(D),
                  pl.BlockSpec((SUBLANES, D), lambda i: (jnp.maximum(i * per - 1, 0), 0)),
                  pl.BlockSpec((SUBLANES, D), lambda i: (jnp.minimum((i + 1) * per, T // SUBLANES - 1), 0))]
        + [full(a) for a in consts]
        + [pl.BlockSpec((tm, 4 * HEAD_PAD), lambda i: (i % nseq, 0)), full(conv_w), full(cb)],
        out_specs=[row(HP), row(HP), row(HP), row(C), row(C), row(C)],
        out_shape=[jax.ShapeDtypeStruct((T, HP), BF16)] * 3 + [jax.ShapeDtypeStruct((T, C), BF16)] * 3,
        compiler_params=_cparams("parallel"),
        name="inproj",
    )(x2d, x2d, x2d, *consts, tab, conv_w, cb)


def _attn_kernel(q_ref, k_ref, v_ref, o_ref):
    outs = []
    for h in range(MLA_HEADS):
        sl = slice(h * HEAD_PAD, (h + 1) * HEAD_PAD)
        s = lax.dot_general(q_ref[0, :, sl], k_ref[0, :, sl], (((1,), (1,)), ((), ())),
                            preferred_element_type=F32).astype(BF16)
        p = jnp.exp2(s - jnp.max(s, axis=-1, keepdims=True))
        o = _dot(p, v_ref[0, :, sl])
        outs.append(o[:, :MLA_V] / o[:, MLA_V:MLA_V + 1])
    o_ref[0] = jnp.concatenate(outs, axis=1).astype(o_ref.dtype)


def _attention(q, k, v, tq=512):
    B, S, HP = q.shape
    return pl.pallas_call(
        _attn_kernel,
        grid=(B, S // tq),
        in_specs=[pl.BlockSpec((1, tq, HP), lambda b, i: (b, i, 0)),
                  pl.BlockSpec((1, S, HP), lambda b, i: (b, 0, 0)),
                  pl.BlockSpec((1, S, HP), lambda b, i: (b, 0, 0))],
        out_specs=pl.BlockSpec((1, tq, MLA_HEADS * MLA_V), lambda b, i: (b, i, 0)),
        out_shape=jax.ShapeDtypeStruct((B, S, MLA_HEADS * MLA_V), BF16),
        compiler_params=_cparams("parallel", "arbitrary"),
        name="mla_attention",
    )(q, k, v)


def _dft_constants(seq):
    n = 2 * seq
    n1, n2 = FFT_N1, FFT_N2
    assert n1 * n2 == n
    r1 = np.arange(n1)
    r2 = np.arange(n2)
    blk = lambda z: np.block([[z.real, -z.imag], [z.imag, z.real]])
    w1 = np.exp(-2j * np.pi * np.outer(r1, r1) / n1)
    fa_data = blk(w1[:, :n1 // 2])
    fa_filt = np.concatenate([w1.real, w1.imag], axis=0)
    fc = blk(np.conj(w1).T[:n1 // 2, :])
    w2 = np.exp(-2j * np.pi * np.outer(r2, r2) / n2)
    tw = np.exp(-2j * np.pi * np.outer(r1, r2) / n)
    fb = np.stack([blk(w2 * tw[k][None, :]) for k in range(n1)])
    fbi = np.stack([blk(np.conj(w2).T * np.conj(tw[k])[:, None] / n) for k in range(n1)])
    as_bf = lambda a: jnp.asarray(a, dtype=F32).astype(BF16)
    return as_bf(fa_data), as_bf(fa_filt), as_bf(fc), as_bf(fb), as_bf(fbi)


def _filter_kernel(z_ref, w1_ref, b1_ref, fr_ref, w2_ref, b2_ref, w3_ref, b3_ref, dec_ref, o_ref, *, seq, tr):
    hp = lax.Precision.HIGHEST
    z = z_ref[...]
    fr = fr_ref[...]
    def sin_rows(a):
        half, w = a.shape[0] // 2, a.shape[1]
        s = jnp.sin(jnp.concatenate([a[:half], a[half:]], axis=1))
        return jnp.concatenate([s[:, :w], s[:, w:]], axis=0)

    h = sin_rows(fr[0:1] * (jnp.dot(z, w1_ref[...], precision=hp, preferred_element_type=F32) + b1_ref[...]))
    h = sin_rows(fr[1:2] * (jnp.dot(h, w2_ref[...], precision=hp, preferred_element_type=F32) + b2_ref[...]))
    split = lambda a: (a.astype(BF16), (a - a.astype(BF16).astype(F32)).astype(BF16))
    (hh, hl), (wh, wl) = split(h), split(w3_ref[0])
    h = _dot(hh, wh) + _dot(hh, wl) + _dot(hl, wh) + b3_ref[0]
    h = h * jnp.exp(-z[:, 0:1] * jnp.abs(dec_ref[0]))
    n = pl.program_id(0) * tr + lax.broadcasted_iota(jnp.int32, h.shape, 0)
    o_ref[...] = jnp.where(n == seq, 0.0, h).astype(o_ref.dtype)


def _hyena_filter_time(seq, w1, b1, freq, w2, b2, w3, b3, decay, tr=512):
    n = 2 * seq
    emb, ffn = w1.shape
    C = w3.shape[1] // (HY_ORDER * HY_DIRS)
    off = np.arange(n)
    t = np.where(off < seq, off, n - off).astype(np.float64)
    bands = np.linspace(1e-4, HY_BANDS - 1, HY_BANDS)
    ang = 2.0 * math.pi * t[:, None] * bands[None, :] / seq
    z = np.concatenate([(t / seq)[:, None], np.cos(ang), -np.sin(ang)], axis=-1)
    zl = LANES
    z = jnp.asarray(np.pad(z, ((0, 0), (0, zl - emb))), dtype=F32)
    w1p = jnp.pad(w1, ((0, zl - emb), (0, 0)))
    by_dir = lambda a: jnp.moveaxis(a.reshape(a.shape[0], HY_ORDER, HY_DIRS, C), 2, 0).reshape(
        HY_DIRS, a.shape[0], HY_ORDER * C)
    w3d, b3d, decd = by_dir(w3), by_dir(b3.reshape(1, -1)), by_dir(decay.reshape(1, -1))
    full = lambda a: pl.BlockSpec(a.shape, lambda i: (0,) * a.ndim)
    ndir = lambda a: pl.BlockSpec((1,) + a.shape[1:], lambda i: ((i * tr) // seq, 0, 0))
    consts = [w1p, b1.reshape(1, -1), freq, w2, b2.reshape(1, -1)]
    return pl.pallas_call(
        functools.partial(_filter_kernel, seq=seq, tr=tr),
        grid=(n // tr,),
        in_specs=[pl.BlockSpec((tr, zl), lambda i: (i, 0))] + [full(a) for a in consts]
        + [ndir(w3d), ndir(b3d), ndir(decd)],
        out_specs=pl.BlockSpec((tr, HY_ORDER * C), lambda i: (i, 0)),
        out_shape=jax.ShapeDtypeStruct((n, HY_ORDER * C), BF16),
        compiler_params=_cparams("parallel"),
        name="hyena_filter_mlp",
    )(z, *consts, w3d, b3d, decd)


def _pitched(rows, nb):
    return pltpu.VMEM((rows, nb + DFT_PITCH_PAD, LANES), F32)


def _block_rows(ref):
    return math.prod(ref.shape[:-2]), ref.shape[-2]


def _copy_in(ref, scr):
    rows, nb = _block_rows(ref)
    scr[:, :nb, :] = ref[...].reshape(rows, nb, LANES).astype(scr.dtype)


def _copy_out(scr, ref):
    rows, nb = _block_rows(ref)
    ref[...] = scr[:, :nb, :].reshape(ref.shape).astype(ref.dtype)


def _at_n2(scr, n):
    rows, pitch, _ = scr.shape
    return scr.reshape(rows * pitch, LANES).at[pl.ds(n, rows, stride=pitch), :]


def _outer_dft(mat_ref, x_ref, o_ref, xs, os):
    _copy_in(x_ref, xs)
    for n in range(x_ref.shape[-2]):
        _at_n2(os, n)[...] = _dot(mat_ref[...], _at_n2(xs, n)[...].astype(BF16))
    _copy_out(os, o_ref)


def _filter_stage_a_kernel(x_ref, fa_ref, o_ref, xs, os):
    _outer_dft(fa_ref, x_ref, o_ref, xs, os)


def _filter_stage_b_kernel(x_ref, fb_ref, o_ref):
    for kk in range(x_ref.shape[1]):
        x = jnp.concatenate([x_ref[0, kk], x_ref[1, kk]], axis=0).astype(BF16)
        o_ref[kk] = _dot(fb_ref[kk], x).astype(o_ref.dtype)


def _hyena_filter_spectrum(filt, fa_filt, fb, nb=DFT_N2_BLOCK):
    n, oc = filt.shape
    a = pl.pallas_call(
        _filter_stage_a_kernel,
        grid=(FFT_N2 // nb, oc // LANES),
        in_specs=[pl.BlockSpec((FFT_N1, nb, LANES), lambda j, c: (0, j, c)),
                  pl.BlockSpec(fa_filt.shape, lambda j, c: (0, 0))],
        out_specs=pl.BlockSpec((2, FFT_N1, nb, LANES), lambda j, c: (0, 0, j, c)),
        out_shape=jax.ShapeDtypeStruct((2, FFT_N1, FFT_N2, oc), BF16),
        scratch_shapes=[_pitched(FFT_N1, nb), _pitched(2 * FFT_N1, nb)],
        compiler_params=_cparams("parallel", "parallel"),
        name="hyena_filter_dft_a",
    )(filt.reshape(FFT_N1, FFT_N2, oc), fa_filt)
    return pl.pallas_call(
        _filter_stage_b_kernel,
        grid=(FFT_N1 // DFT_K1_BLOCK,),
        in_specs=[pl.BlockSpec((2, DFT_K1_BLOCK, FFT_N2, oc), lambda k: (0, k, 0, 0)),
                  pl.BlockSpec((DFT_K1_BLOCK, 2 * FFT_N2, 2 * FFT_N2), lambda k: (k, 0, 0))],
        out_specs=pl.BlockSpec((DFT_K1_BLOCK, 2 * FFT_N2, oc), lambda k: (k, 0, 0)),
        out_shape=jax.ShapeDtypeStruct((FFT_N1, 2 * FFT_N2, oc), BF16),
        compiler_params=_cparams("parallel"),
        name="hyena_filter_dft_b",
    )(a, fb)


def _stage_a_kernel(x_ref, fa_ref, o_ref, xs, os):
    _outer_dft(fa_ref, x_ref, o_ref, xs, os)


def _stage_a(x4, fa, nb):
    B, r, n2, C = x4.shape
    return pl.pallas_call(
        _stage_a_kernel,
        grid=(B // 2, n2 // nb, C // LANES),
        in_specs=[pl.BlockSpec((2, r, nb, LANES), lambda p, j, c: (p, 0, j, c)),
                  pl.BlockSpec(fa.shape, lambda p, j, c: (0, 0))],
        out_specs=pl.BlockSpec((1, 2, FFT_N1, nb, LANES), lambda p, j, c: (p, 0, 0, j, c)),
        out_shape=jax.ShapeDtypeStruct((B // 2, 2, FFT_N1, n2, C), BF16),
        scratch_shapes=[_pitched(2 * r, nb), _pitched(2 * FFT_N1, nb)],
        compiler_params=_cparams("parallel", "parallel", "parallel"),
        name="hyena_dft_a",
    )(x4, fa)


def _stage_b_kernel(x_ref, fb_ref, kf_ref, fbi_ref, o_ref):
    npair = x_ref.shape[0]
    n2 = x_ref.shape[3]
    c = x_ref.shape[4]
    for kk in range(x_ref.shape[2]):
        x = jnp.concatenate(
            [jnp.concatenate([x_ref[p, 0, kk], x_ref[p, 1, kk]], axis=0) for p in range(npair)],
            axis=1).astype(BF16)
        g = _dot(fb_ref[kk], x)
        gr, gi = g[:n2], g[n2:]
        kf = kf_ref[kk].astype(F32)
        kr = jnp.concatenate([kf[:n2]] * npair, axis=1)
        ki = jnp.concatenate([kf[n2:]] * npair, axis=1)
        hcat = jnp.concatenate([gr * kr - gi * ki, gr * ki + gi * kr], axis=0).astype(BF16)
        y = _dot(fbi_ref[kk], hcat)
        for p in range(npair):
            o_ref[p, 0, kk] = y[:n2, p * c:(p + 1) * c].astype(o_ref.dtype)
            o_ref[p, 1, kk] = y[n2:, p * c:(p + 1) * c].astype(o_ref.dtype)


def _stage_b(spec, fb, kf, fbi, order, nk=DFT_K1_BLOCK):
    npair, _, _, _, C = spec.shape
    blk = pl.BlockSpec((npair, 2, nk, FFT_N2, C), lambda k: (0, 0, k, 0, 0))
    mat = pl.BlockSpec((nk, 2 * FFT_N2, 2 * FFT_N2), lambda k: (k, 0, 0))
    return pl.pallas_call(
        _stage_b_kernel,
        grid=(FFT_N1 // nk,),
        in_specs=[blk, mat, pl.BlockSpec((nk, 2 * FFT_N2, C), lambda k: (k, 0, order)), mat],
        out_specs=blk,
        out_shape=jax.ShapeDtypeStruct(spec.shape, BF16),
        compiler_params=_cparams("parallel"),
        name="hyena_dft_b",
    )(spec, fb, kf, fbi)


def _stage_c_kernel(y_ref, fc_ref, gate_ref, z_ref, skip_ref, *rest, stage_a):
    if stage_a:
        fa_ref, z_out, a_out, ys, gs, zs, os, as_ = rest
    else:
        z_out, ys, gs, zs, os = rest
    nb = gate_ref.shape[-2]
    _copy_in(y_ref, ys)
    _copy_in(gate_ref, gs)
    _copy_in(z_ref, zs)
    skip = skip_ref[...]
    for n in range(nb):
        conv = _dot(fc_ref[...], _at_n2(ys, n)[...].astype(BF16))
        _at_n2(os, n)[...] = _at_n2(gs, n)[...] * (conv + skip * _at_n2(zs, n)[...])
    _copy_out(os, z_out)
    if stage_a:
        for n in range(nb):
            _at_n2(as_, n)[...] = _dot(fa_ref[...], _at_n2(os, n)[...].astype(BF16))
        _copy_out(as_, a_out)


def _stage_c(yspec, fc, gate, zin, skip, fa=None, nb=DFT_N2_BLOCK):
    B, r, n2, C = gate.shape
    dat = pl.BlockSpec((2, r, nb, LANES), lambda p, j, c: (p, 0, j, c))
    spc = pl.BlockSpec((1, 2, FFT_N1, nb, LANES), lambda p, j, c: (p, 0, 0, j, c))
    in_specs = [spc, pl.BlockSpec(fc.shape, lambda p, j, c: (0, 0)), dat, dat,
                pl.BlockSpec((1, LANES), lambda p, j, c: (0, c))]
    out_specs = [dat]
    out_shape = [jax.ShapeDtypeStruct(gate.shape, BF16)]
    args = [yspec, fc, gate, zin, skip.reshape(1, C)]
    scratch = [_pitched(2 * FFT_N1, nb)] + [_pitched(2 * r, nb)] * 3
    if fa is not None:
        in_specs.append(pl.BlockSpec(fa.shape, lambda p, j, c: (0, 0)))
        out_specs.append(spc)
        out_shape.append(jax.ShapeDtypeStruct(yspec.shape, BF16))
        args.append(fa)
        scratch.append(_pitched(2 * FFT_N1, nb))
    return pl.pallas_call(
        functools.partial(_stage_c_kernel, stage_a=fa is not None),
        grid=(B // 2, n2 // nb, C // LANES),
        in_specs=in_specs, out_specs=out_specs, out_shape=out_shape,
        scratch_shapes=scratch,
        compiler_params=_cparams("parallel", "parallel", "parallel"),
        name="hyena_dft_c",
    )(*args)


def _hyena(x1, x2, v, skip, kf, consts, nb=DFT_N2_BLOCK):
    fa_data, _, fc, fb, fbi = consts
    B, S, C = v.shape
    split = lambda a: a.reshape(B, S // FFT_N2, FFT_N2, C)
    a0 = _stage_a(split(v), fa_data, nb)
    y0 = _stage_b(a0, fb, kf, fbi, 0)
    z1, a1 = _stage_c(y0, fc, split(x1), split(v), skip[0], fa=fa_data, nb=nb)
    y1 = _stage_b(a1, fb, kf, fbi, 1)
    (out,) = _stage_c(y1, fc, split(x2), z1, skip[1], nb=nb)
    return out.reshape(B, S, C)


def _route(lt):
    tokens = lt.shape[1]
    ninf = -jnp.inf
    col_max = lambda a: jnp.max(a, axis=0, keepdims=True)
    first = lambda mask, idx, n: jnp.min(jnp.where(mask, idx, float(n)), axis=0, keepdims=True)
    gl = lt[N_EXPERTS:N_EXPERTS + N_GROUPS]
    gi = lax.broadcasted_iota(jnp.int32, gl.shape, 0).astype(F32)
    gmax = col_max(gl)
    g_idx = first(gl == gmax, gi, N_GROUPS)
    p_group = 1.0 / jnp.sum(jnp.exp(gl - gmax), axis=0, keepdims=True)
    ei = lax.broadcasted_iota(jnp.int32, (N_EXPERTS, tokens), 0)
    in_g = (ei // EXPERTS_PER_GROUP).astype(F32) == g_idx
    ei = ei.astype(F32)
    el = jnp.where(in_g, lt[:N_EXPERTS], ninf)
    v1 = col_max(el)
    i1 = first(el == v1, ei, N_EXPERTS)
    el2 = jnp.where(ei == i1, ninf, el)
    v2 = col_max(el2)
    i2 = first(el2 == v2, ei, N_EXPERTS)
    e2 = jnp.exp(v2 - v1)
    p1 = 1.0 / (1.0 + e2)
    p2 = e2 / (1.0 + e2)
    sub = lax.broadcasted_iota(jnp.int32, (SUBLANES, tokens), 0)
    sel = lambda n, val: jnp.where(sub == n, val, 0.0)
    return (sel(ROUTE_ID0, i1) + sel(ROUTE_ID0 + 1, i2)
            + sel(ROUTE_W0, p_group * p1) + sel(ROUTE_W0 + 1, p_group * p2))


def _postmix_kernel(x_ref, a_ref, hy_ref, ag_ref, hg_ref, woa_ref, woh_ref, cg_ref, wmq_ref,
                    mem_ref, mg_ref, wmkv_ref, wmo_ref, fg_ref, wr_ref, br_ref, x_out, hn_out, route_t_out,
                    mk_ref, mv_ref, *, nseq):
    @pl.when(pl.program_id(0) % nseq == 0)
    def _():
        kv = _dot(_rms(mem_ref[0], mg_ref[...]).astype(BF16), wmkv_ref[...])
        d = mk_ref.shape[2]
        mk_ref[0] = kv[:, :d].astype(BF16)
        mv_ref[0] = kv[:, d:].astype(BF16)

    ra = _rms(a_ref[...].astype(F32), ag_ref[...]).astype(BF16)
    rh = _rms(hy_ref[...].astype(F32), hg_ref[...]).astype(BF16)
    x = x_ref[...] + _dot(ra, woa_ref[...]) + _dot(rh, woh_ref[...])
    q = _dot(_rms(x, cg_ref[...]).astype(BF16), wmq_ref[...])
    dh = q.shape[1] // MEM_HEADS
    outs = []
    for h in range(MEM_HEADS):
        sl = slice(h * dh, (h + 1) * dh)
        s = lax.dot_general(q[:, sl].astype(BF16), mk_ref[0, :, sl], (((1,), (1,)), ((), ())),
                            preferred_element_type=F32) * dh ** -0.5
        p = jnp.exp(s - jnp.max(s, axis=-1, keepdims=True))
        l = jnp.sum(p, axis=-1, keepdims=True)
        outs.append(_dot(p.astype(BF16), mv_ref[0, :, sl]) / l)
    o = jnp.concatenate(outs, axis=1).astype(BF16)
    x = x + _dot(o, wmo_ref[...])
    x_out[...] = x
    hn = _rms(x, fg_ref[...])
    _store_row_tiles(hn_out, hn)
    logits = _dot(hn.astype(BF16), wr_ref[...]) + br_ref[...]
    route_t_out[0] = _route(logits.T)


def _postmix(x2d, a2d, hy2d, seq, ag, hg, w_out, cg, w_mq, mem, mem_g, w_mkv, w_mo, fg, w_rg, b_rg, w_re, b_re,
             tm=1024):
    T, D = x2d.shape
    ca = a2d.shape[1]
    woa = w_out[:ca].astype(BF16)
    woh = w_out[ca:].astype(BF16)
    pad = ROUTE_LANES - N_EXPERTS - N_GROUPS
    wr = jnp.concatenate([w_re, w_rg, jnp.zeros((D, pad), F32)], 1).astype(BF16)
    br = jnp.concatenate([b_re, b_rg, jnp.zeros((pad,), F32)]).reshape(1, ROUTE_LANES)
    nseq = seq // tm
    full = lambda a: pl.BlockSpec(a.shape, lambda i: (0,) * a.ndim)
    row = lambda n: pl.BlockSpec((tm, n), lambda i: (i, 0))
    memb = pl.BlockSpec((1,) + mem.shape[1:], lambda i: (i // nseq, 0, 0))
    args = [x2d, a2d, hy2d, ag.reshape(1, -1), hg.reshape(1, -1), woa, woh, cg.reshape(1, D),
            w_mq.astype(BF16), mem, mem_g.reshape(1, D), w_mkv.astype(BF16), w_mo.astype(BF16),
            fg.reshape(1, D), wr, br]
    in_specs = [row(D), row(ca), row(hy2d.shape[1])] + [full(a) for a in args[3:9]] + [memb] \
        + [full(a) for a in args[10:]]
    dkv = w_mkv.shape[1] // 2
    return pl.pallas_call(
        functools.partial(_postmix_kernel, nseq=nseq),
        grid=(T // tm,),
        scratch_shapes=[pltpu.VMEM((1, mem.shape[1], dkv), BF16)] * 2,
        in_specs=in_specs,
        out_specs=[row(D), pl.BlockSpec((tm, SUBLANES, LANES), lambda i: (i, 0, 0)),
                   pl.BlockSpec((1, SUBLANES, tm), lambda i: (i, 0, 0))],
        out_shape=[jax.ShapeDtypeStruct((T, D), F32), jax.ShapeDtypeStruct((T, SUBLANES, LANES), F32),
                   jax.ShapeDtypeStruct((T // tm, SUBLANES, tm), F32)],
        compiler_params=_cparams("arbitrary"),
        name="postmix",
    )(*args)


def _slot_onehots(rt):
    e = lax.broadcasted_iota(jnp.int32, (N_EXPERTS, rt.shape[1]), 0).astype(F32)
    return [e == rt[ROUTE_ID0 + k:ROUTE_ID0 + k + 1, :] for k in range(2)]


def _slot_rows(rows, width):
    sub = lax.broadcasted_iota(jnp.int32, (SUBLANES, width), 0)
    return jnp.where(sub == 0, rows[0], jnp.where(sub == 1, rows[1], 0.0))


def _rank_kernel(rt_ref, rank_out, cnt_out, carry_ref):
    @pl.when(pl.program_id(0) == 0)
    def _():
        carry_ref[...] = jnp.zeros_like(carry_ref)

    tr = rt_ref.shape[2]
    oh = _slot_onehots(rt_ref[0])
    cnt = jnp.where(oh[0] | oh[1], 1.0, 0.0)
    s = lax.broadcasted_iota(jnp.int32, (tr, tr), 0)
    t = lax.broadcasted_iota(jnp.int32, (tr, tr), 1)
    before = jnp.where(s < t, 1.0, 0.0).astype(BF16)
    cum = _dot(cnt.astype(BF16), before) + carry_ref[...]
    ranks = [jnp.sum(jnp.where(m, cum, 0.0), axis=0, keepdims=True) for m in oh]
    rank_out[0] = _slot_rows(ranks, tr)
    carry_ref[...] += jnp.sum(cnt, axis=1, keepdims=True)
    cnt_out[...] = carry_ref[...]


def _pos_kernel(rt_ref, rank_ref, base_ref, pos_out):
    tr = rt_ref.shape[2]
    oh = _slot_onehots(rt_ref[0])
    rank = rank_ref[0]
    pos = [jnp.sum(jnp.where(m, base_ref[...], 0.0), axis=0, keepdims=True) + rank[k:k + 1, :]
           for k, m in enumerate(oh)]
    pos_out[0] = _slot_rows(pos, tr).astype(jnp.int32)


def _dispatch_kernel(base_ref, cnt_ref, cp_ref, pos0_ref, pos1_ref, hn_ref, xs_hbm, zero_ref, sem, *, tt):
    i = pl.program_id(0)
    row_copy = lambda src, dst: pltpu.make_async_copy(
        hn_ref.at[pl.ds(src, 1)], xs_hbm.at[pl.ds(dst, 1)], sem)

    @pl.when(i == 0)
    def _():
        zc = zero_ref.shape[0]
        zero_ref[...] = jnp.zeros_like(zero_ref)
        chunk = lambda c: pltpu.make_async_copy(zero_ref, xs_hbm.at[pl.ds(c * zc, zc)], sem)

        def clear(lo, hi):
            lax.fori_loop(lo, hi, lambda c, carry: (chunk(c).start(), carry)[1], 0)

        def drain(lo, hi):
            lax.fori_loop(lo, hi, lambda c, carry: (chunk(c).wait(), carry)[1], 0)

        last = N_EXPERTS - 1
        spans = [((base_ref[e] + cnt_ref[e]) // zc, (base_ref[e] + cp_ref[e]) // zc)
                 for e in range(N_EXPERTS)]
        spans.append(((base_ref[last] + cp_ref[last]) // zc, xs_hbm.shape[0] // zc))
        for lo, hi in spans:
            clear(lo, hi)
        for lo, hi in spans:
            drain(lo, hi)

    def start(t, carry):
        row_copy(t, pos0_ref[t]).start(priority=0)
        row_copy(t, pos1_ref[t]).start(priority=1)
        return carry

    lax.fori_loop(0, tt, start, 0, unroll=DMA_UNROLL)

    def wait(t, carry):
        row_copy(0, 0).wait()
        row_copy(0, 0).wait()
        return carry

    lax.fori_loop(0, tt, wait, 0, unroll=DMA_UNROLL)


def _ffn_kernel(te_ref, nu_ref, xs_ref, wg_ref, wu_ref, wd_ref, ys_ref):
    used = pl.program_id(0) < nu_ref[0]

    @pl.when(used)
    def _():
        x = _load_row_tiles(xs_ref).astype(BF16)
        a = _dot(x, wg_ref[0].astype(BF16))
        b = _dot(x, wu_ref[0].astype(BF16))
        m = (a * jax.nn.sigmoid(a)) * b
        _store_row_tiles(ys_ref, _dot(m.astype(BF16), wd_ref[0].astype(BF16)))

    @pl.when(jnp.logical_not(used))
    def _():
        ys_ref[...] = jnp.zeros_like(ys_ref)


def _combine_kernel(pos0_ref, pos1_ref, pos0_next_ref, pos1_next_ref, ys_hbm, x_ref, rt_ref, fg_ref,
                    o_ref, buf_ref, sem, *, tc):
    i = pl.program_id(0)
    slot = i % 2
    row_copy = lambda s, k, t, p: pltpu.make_async_copy(
        ys_hbm.at[pl.ds(p, 1)], buf_ref.at[s, k, pl.ds(t, 1)], sem.at[s])

    def fetch(p0_ref, p1_ref, s):
        def start(t, carry):
            row_copy(s, 0, t, p0_ref[t]).start(priority=0)
            row_copy(s, 1, t, p1_ref[t]).start(priority=1)
            return carry

        lax.fori_loop(0, tc, start, 0, unroll=DMA_UNROLL)

    @pl.when(i == 0)
    def _():
        fetch(pos0_ref, pos1_ref, 0)

    @pl.when(i + 1 < pl.num_programs(0))
    def _():
        fetch(pos0_next_ref, pos1_next_ref, 1 - slot)

    rec = rt_ref[0]
    route = jnp.concatenate([rec, jnp.zeros((LANES - SUBLANES, tc), F32)], axis=0).T

    def wait(t, carry):
        row_copy(slot, 0, 0, 0).wait()
        row_copy(slot, 1, 0, 0).wait()
        return carry

    lax.fori_loop(0, tc, wait, 0, unroll=DMA_UNROLL)
    y = (x_ref[...] + route[:, ROUTE_W0:ROUTE_W0 + 1] * _load_row_tiles(buf_ref, (slot, 0))
         + route[:, ROUTE_W0 + 1:ROUTE_W0 + 2] * _load_row_tiles(buf_ref, (slot, 1)))
    o_ref[...] = _rms(y, fg_ref[...])


def _moe(hn, route_t, x2d, w_gate, w_up, w_down, fg, tt=2048, tc=256):
    T, D = x2d.shape
    E = N_EXPERTS
    F = w_gate.shape[-1]
    tmm = MOE_ROW_TILE
    ntr, _, tr = route_t.shape
    row = lambda tm, n: pl.BlockSpec((tm, n), lambda i: (i, 0))
    rec = pl.BlockSpec((1, SUBLANES, tr), lambda i: (i, 0, 0))
    col = pl.BlockSpec((E, 1), lambda i: (0, 0))
    rank, counts = pl.pallas_call(
        _rank_kernel,
        grid=(ntr,),
        in_specs=[rec],
        out_specs=[rec, col],
        out_shape=[jax.ShapeDtypeStruct(route_t.shape, F32), jax.ShapeDtypeStruct((E, 1), F32)],
        scratch_shapes=[pltpu.VMEM((E, 1), F32)],
        compiler_params=_cparams("arbitrary"),
        name="moe_rank",
    )(route_t)

    cnt = counts[:, 0].astype(jnp.int32)
    cp = ((cnt + tmm - 1) // tmm) * tmm
    ends = jnp.cumsum(cp)
    base = ends - cp
    n_used = ends[-1] // tmm
    n_tiles = (2 * T) // tmm + E
    tile_start = jnp.minimum(jnp.arange(n_tiles, dtype=jnp.int32), n_used - 1) * tmm
    tile_expert = jnp.minimum(jnp.sum((tile_start[:, None] >= ends[None, :]).astype(jnp.int32), axis=1), E - 1)

    pos = pl.pallas_call(
        _pos_kernel,
        grid=(ntr,),
        in_specs=[rec, rec, col],
        out_specs=rec,
        out_shape=jax.ShapeDtypeStruct(route_t.shape, jnp.int32),
        compiler_params=_cparams("parallel"),
        name="moe_pos",
    )(route_t, rank, base.astype(F32).reshape(E, 1))
    pos0, pos1 = pos[:, 0, :].reshape(T), pos[:, 1, :].reshape(T)

    xs = pl.pallas_call(
        functools.partial(_dispatch_kernel, tt=tt),
        grid_spec=pltpu.PrefetchScalarGridSpec(
            num_scalar_prefetch=3,
            grid=(T // tt,),
            in_specs=[pl.BlockSpec((tt,), lambda i, b, n, c: (i,), memory_space=pltpu.SMEM),
                      pl.BlockSpec((tt,), lambda i, b, n, c: (i,), memory_space=pltpu.SMEM),
                      pl.BlockSpec((tt, SUBLANES, LANES), lambda i, b, n, c: (i, 0, 0))],
            out_specs=pl.BlockSpec(memory_space=pl.ANY),
            scratch_shapes=[pltpu.VMEM((MOE_CLEAR_ROWS, SUBLANES, LANES), F32),
                            pltpu.SemaphoreType.DMA(())]),
        out_shape=jax.ShapeDtypeStruct((n_tiles * tmm, SUBLANES, LANES), F32),
        compiler_params=_cparams("arbitrary"),
        name="moe_dispatch",
    )(base, cnt, cp, pos0, pos1, hn)

    tile = lambda r, te, nu: (jnp.minimum(r, nu[0] - 1), 0, 0)
    ys = pl.pallas_call(
        _ffn_kernel,
        grid_spec=pltpu.PrefetchScalarGridSpec(
            num_scalar_prefetch=2,
            grid=(n_tiles,),
            in_specs=[pl.BlockSpec((tmm, SUBLANES, LANES), tile),
                      pl.BlockSpec((1, D, F), lambda r, te, nu: (te[r], 0, 0)),
                      pl.BlockSpec((1, D, F), lambda r, te, nu: (te[r], 0, 0)),
                      pl.BlockSpec((1, F, D), lambda r, te, nu: (te[r], 0, 0))],
            out_specs=pl.BlockSpec((tmm, SUBLANES, LANES), lambda r, te, nu: (r, 0, 0))),
        out_shape=jax.ShapeDtypeStruct((n_tiles * tmm, SUBLANES, LANES), F32),
        compiler_params=_cparams("arbitrary"),
        name="moe_ffn",
    )(tile_expert, n_used.reshape(1), xs, w_gate.reshape(E, D, F), w_up.reshape(E, D, F),
      w_down.reshape(E, F, D))

    cur = pl.BlockSpec((tc,), lambda i: (i,), memory_space=pltpu.SMEM)
    nxt = pl.BlockSpec((tc,), lambda i: (jnp.minimum(i + 1, T // tc - 1),), memory_space=pltpu.SMEM)
    return pl.pallas_call(
        functools.partial(_combine_kernel, tc=tc),
        grid=(T // tc,),
        in_specs=[cur, cur, nxt, nxt,
                  pl.BlockSpec(memory_space=pl.ANY),
                  row(tc, D), pl.BlockSpec((1, SUBLANES, tc), lambda i: (i // (tr // tc), 0, i % (tr // tc))),
                  pl.BlockSpec((1, D), lambda i: (0, 0))],
        out_specs=row(tc, D),
        out_shape=jax.ShapeDtypeStruct((T, D), F32),
        scratch_shapes=[pltpu.VMEM((2, 2, tc, SUBLANES, LANES), F32), pltpu.SemaphoreType.DMA((2,))],
        compiler_params=_cparams("arbitrary"),
        name="moe_combine",
    )(pos0, pos1, pos0, pos1, ys, x2d, route_t, fg.reshape(1, D))


def kernel(x, mem, mix_norm_g, w_in, q_norm_g, kv_norm_g, w_uq, w_ukv, hy_conv_w, hy_conv_b, hy_w1, hy_b1, hy_freq, hy_w2, hy_b2, hy_w3, hy_b3, hy_decay, hy_skip, attn_out_g, hy_out_g, w_out, cross_norm_g, mem_norm_g, w_mq, w_mkv, w_mo, ffn_norm_g, w_route_group, b_route_group, w_route_expert, b_route_expert, w_gate, w_up, w_down, final_norm_g):
    B, S, D = x.shape
    depth = w_in.shape[0]
    consts = _dft_constants(S)
    xf = x.reshape(B * S, D)
    for l in range(depth):
        q, k, v, hx1, hx2, hv = _inproj(xf, S, mix_norm_g[l], w_in[l], q_norm_g[l], kv_norm_g[l],
                                        w_uq[l], w_ukv[l], hy_conv_w[l], hy_conv_b[l])
        HP = q.shape[1]
        a_out = _attention(q.reshape(B, S, HP), k.reshape(B, S, HP), v.reshape(B, S, HP))
        filt = _hyena_filter_time(S, hy_w1[l], hy_b1[l], hy_freq[l], hy_w2[l], hy_b2[l], hy_w3[l],
                                  hy_b3[l], hy_decay[l])
        kf = _hyena_filter_spectrum(filt, consts[1], consts[3])
        C = hv.shape[1]
        h_out = _hyena(hx1.reshape(B, S, C), hx2.reshape(B, S, C), hv.reshape(B, S, C),
                       hy_skip[l], kf, consts)
        x2, hn, route_t = _postmix(xf, a_out.reshape(B * S, -1), h_out.reshape(B * S, C), S,
                                attn_out_g[l], hy_out_g[l], w_out[l], cross_norm_g[l], w_mq[l], mem, mem_norm_g[l], w_mkv[l],
                                w_mo[l], ffn_norm_g[l], w_route_group[l], b_route_group[l],
                                w_route_expert[l], b_route_expert[l])
        assert depth == 1
        xf = _moe(hn, route_t, x2, w_gate[l], w_up[l], w_down[l], final_norm_g)
    return xf.reshape(B, S, D)
```
